```python
import jax
import jax.numpy as jnp
from jax import lax
import numpy as np


D_MODEL = 1024
BATCH = 16
SEQ = 2048
DEPTH = 1

GRID_W = 64
CTX_LEN = 256
NA_HEADS = 8
NA_HEAD_DIM = 64
NA_WIDTH = NA_HEADS * NA_HEAD_DIM
NA_KH = 8
NA_KW = 16
NA_QBW = 16
NA_KBW = NA_QBW + NA_KW
ROPE_AXIS_DIM = NA_HEAD_DIM // 2
ROPE_BASE = 10000.0
HG_HEADS = 4
HG_DK = 128
HG_DV = 128
HG_WIDTH = HG_HEADS * HG_DK
HG_CHUNK = 64
MIX_WIDTH = NA_WIDTH + HG_WIDTH
SEG_WIDTHS = (NA_WIDTH, NA_WIDTH, NA_WIDTH, HG_WIDTH, HG_WIDTH, HG_WIDTH, HG_WIDTH, HG_WIDTH)
IN_COLS = 3 * NA_WIDTH + 5 * HG_WIDTH
N_GROUPS = 4
EXPERTS_PER_GROUP = 4
N_EXPERTS = N_GROUPS * EXPERTS_PER_GROUP
TOP_K = 2
D_EXPERT = 512
EPS = 1e-6
F32 = jnp.float32

kernel_name = 'hybrid_na_hgrn2_hmoe_dit_layer'


def _rms(x, g):
    xf = x.astype(F32)
    y = xf * lax.rsqrt(jnp.mean(xf * xf, axis=-1, keepdims=True) + EPS)
    return (y * g.astype(F32)).astype(x.dtype)


def _seg(w, i):
    start = sum(SEG_WIDTHS[:i])
    return w[:, start:start + SEG_WIDTHS[i]]


def _axial_rope(n_tok):
    t = jnp.arange(n_tok)
    pos = jnp.stack([t // GRID_W, t % GRID_W], axis=-1).astype(F32)
    inv = ROPE_BASE ** (-jnp.arange(0, ROPE_AXIS_DIM, 2, dtype=F32) / ROPE_AXIS_DIM)
    ang = pos[:, :, None] * inv
    return jnp.cos(ang), jnp.sin(ang)


def _rope(x, cos, sin):
    xr = x.astype(F32).reshape(*x.shape[:-1], 2, ROPE_AXIS_DIM)
    half = ROPE_AXIS_DIM // 2
    x1, x2 = xr[..., :half], xr[..., half:]
    cs, sn = cos[None, :, None], sin[None, :, None]
    out = jnp.concatenate([x1 * cs - x2 * sn, x2 * cs + x1 * sn], axis=-1)
    return out.reshape(x.shape).astype(x.dtype)


def _na_column_tables():
    n_cb = GRID_W // NA_QBW
    j = np.arange(n_cb)
    kb_start = np.clip(j * NA_QBW - NA_KW // 2, 0, GRID_W - NA_KBW)
    key_cols = kb_start[:, None] + np.arange(NA_KBW)
    q_cols = j[:, None] * NA_QBW + np.arange(NA_QBW)
    cs = np.clip(q_cols - NA_KW // 2, 0, GRID_W - NA_KW)[..., None]
    kc = key_cols[:, None, :]
    valid = (kc >= cs) & (kc < cs + NA_KW)
    col_off = np.clip(kc - q_cols[..., None] + NA_KW - 1, 0, 2 * NA_KW - 2)
    return key_cols, valid, col_off


def _na_latent(q_rot, k_rot, v, q_raw, kc, vc, rpb):
    B, S, H, Dh = v.shape
    rows = S // GRID_W
    kh = min(NA_KH, rows)
    key_cols, valid, col_off = _na_column_tables()
    n_cb = key_cols.shape[0]
    grid = lambda t: t.reshape(B, rows, GRID_W, H, Dh)
    qg, kg, vg, qrg = grid(q_rot), grid(k_rot), grid(v), grid(q_raw)
    scale = Dh ** -0.5
    mask = jnp.asarray(valid)[None, None, :, :, None, :]

    def one_row(r):
        rs = jnp.clip(r - kh // 2, 0, rows - kh)
        k_blk = lax.dynamic_slice_in_dim(kg, rs, kh, axis=1)[:, :, key_cols]
        v_blk = lax.dynamic_slice_in_dim(vg, rs, kh, axis=1)[:, :, key_cols]
        qb = lax.dynamic_index_in_dim(qg, r, axis=1, keepdims=False).reshape(B, n_cb, NA_QBW, H, Dh)
        qrb = lax.dynamic_index_in_dim(qrg, r, axis=1, keepdims=False).reshape(B, n_cb, NA_QBW, H, Dh)
        s_loc = jnp.einsum('bjqhd,bijkhd->bhjqik', qb, k_blk).astype(F32) * scale
        row_off = rs + jnp.arange(kh) - r + NA_KH - 1
        bias = rpb[:, row_off][:, :, col_off].transpose(0, 2, 3, 1, 4)
        s_loc = jnp.where(mask, s_loc + bias[None].astype(F32), -jnp.inf)
        s_loc = s_loc.reshape(B, H, n_cb, NA_QBW, kh * NA_KBW)
        s_ctx = jnp.einsum('bjqhd,bchd->bhjqc', qrb, kc).astype(F32) * scale
        p = jax.nn.softmax(jnp.concatenate([s_loc, s_ctx], axis=-1), axis=-1).astype(v.dtype)
        p_loc = p[..., :kh * NA_KBW].reshape(B, H, n_cb, NA_QBW, kh, NA_KBW)
        p_ctx = p[..., kh * NA_KBW:]
        o = (jnp.einsum('bhjqik,bijkhd->bjqhd', p_loc, v_blk)
             + jnp.einsum('bhjqc,bchd->bjqhd', p_ctx, vc))
        return o.reshape(B, GRID_W, H, Dh)

    out = lax.map(one_row, jnp.arange(rows))
    return out.transpose(1, 0, 2, 3, 4).reshape(B, S, H * Dh)


def _ctx_attn(q, k, v):
    B, L, H, Dh = q.shape
    s = jnp.einsum('bqhd,bkhd->bhqk', q, k).astype(F32) * (Dh ** -0.5)
    p = jax.nn.softmax(s, axis=-1).astype(v.dtype)
    return jnp.einsum('bhqk,bkhd->bqhd', p, v).reshape(B, L, H * Dh)


def _hg_heads(t):
    B, L, _ = t.shape
    return t.astype(F32).reshape(B, L, HG_HEADS, -1).transpose(0, 2, 1, 3)


def _hg_forget(f_pre, lb):
    f = lb + (1.0 - lb) * jax.nn.sigmoid(f_pre.astype(F32))
    return _hg_heads(1.0 - f), _hg_heads(jnp.log(f))


def _flip(t):
    return jnp.flip(t, axis=2)


def _hg_scan(q, k, v, logf, s0):
    B, H, L, DK = q.shape
    n = L // HG_CHUNK
    chunks = lambda t: jnp.moveaxis(t.reshape(B, H, n, HG_CHUNK, t.shape[-1]), 2, 0)
    incl = jnp.tril(jnp.ones((HG_CHUNK, HG_CHUNK), dtype=bool))[:, :, None]

    def step(S, inp):
        qc, kc, vc, gc = inp
        b = jnp.cumsum(gc, axis=2)
        diff = b[:, :, :, None, :] - b[:, :, None, :, :]
        decay = jnp.exp(jnp.where(incl, diff, -jnp.inf))
        attn = jnp.einsum('bhtd,bhsd,bhtsd->bhts', qc, kc, decay)
        o = (jnp.einsum('bhtd,bhdv->bhtv', qc * jnp.exp(b), S)
             + jnp.einsum('bhts,bhsv->bhtv', attn, vc))
        b_end = b[:, :, -1:, :]
        S = (jnp.exp(b_end[:, :, 0, :])[..., None] * S
             + jnp.einsum('bhsd,bhsv->bhdv', kc * jnp.exp(b_end - b), vc))
        return S, o

    S, o = lax.scan(step, s0, (chunks(q), chunks(k), chunks(v), chunks(logf)))
    return jnp.moveaxis(o, 0, 2).reshape(B, H, L, v.shape[-1]), S


def _hg_final_state(k, v, logf):
    b = jnp.cumsum(logf, axis=2)
    return jnp.einsum('bhsd,bhsv->bhdv', k * jnp.exp(b[:, :, -1:, :] - b), v)


def _hg_output(o, g_pre, gain):
    o = o.transpose(0, 2, 1, 3)
    y = o * lax.rsqrt(jnp.mean(o * o, axis=-1, keepdims=True) + EPS) * gain.astype(F32)
    g = g_pre.astype(F32).reshape(*g_pre.shape[:-1], HG_HEADS, HG_DV)
    B, L = g_pre.shape[:2]
    return (y * jax.nn.silu(g)).reshape(B, L, HG_WIDTH).astype(g_pre.dtype)


def _hier_moe(h, w_grp, b_grp, w_exp, b_exp, w1, w3, w2):
    B, L, D = h.shape
    t = h.reshape(B * L, D)
    g_logits = (t @ w_grp + b_grp).astype(F32)
    g_prob = jax.nn.softmax(g_logits, axis=-1)
    g_sel = jnp.argmax(g_logits, axis=-1)
    g_w = jnp.take_along_axis(g_prob, g_sel[:, None], axis=-1)
    e_logits = (t @ w_exp + b_exp).astype(F32).reshape(-1, N_GROUPS, EXPERTS_PER_GROUP)
    e_in = jnp.take_along_axis(e_logits, g_sel[:, None, None], axis=1)[:, 0]
    top_v, top_i = lax.top_k(e_in, TOP_K)
    w_top = jax.nn.softmax(top_v, axis=-1) * g_w
    e_id = g_sel[:, None] * EXPERTS_PER_GROUP + top_i
    gate = jnp.sum(jax.nn.one_hot(e_id, N_EXPERTS, dtype=F32) * w_top[..., None], axis=1).astype(h.dtype)
    out = jnp.zeros_like(t)
    for e in range(N_EXPERTS):
        y = (jax.nn.silu(t @ w1[e]) * (t @ w3[e])) @ w2[e]
        out = out + gate[:, e:e + 1] * y
    return out.reshape(B, L, D)


def setup_inputs(seed: int = 0) -> dict:
    key = jax.random.key(seed)
    ks = jax.random.split(key, 22)
    nrm = lambda k, shape, s: jax.random.normal(k, shape, F32) * s
    D = D_MODEL
    return {
        'x': nrm(ks[0], (BATCH, SEQ, D), 1.0),
        'c': nrm(ks[1], (BATCH, D), 1.0),
        'ctx': nrm(ks[2], (BATCH, CTX_LEN, D), 1.0),
        'c_ctx': nrm(ks[3], (D,), 1.0),
        'w_mod': nrm(ks[4], (DEPTH, D, 6 * D), 0.5 * D ** -0.5),
        'b_mod': nrm(ks[5], (DEPTH, 6 * D), 0.01),
        'norm_mix': 1.0 + nrm(ks[6], (DEPTH, D), 0.05),
        'norm_ffn': 1.0 + nrm(ks[7], (DEPTH, D), 0.05),
        'w_in': nrm(ks[8], (DEPTH, D, IN_COLS), D ** -0.5),
        'w_out': nrm(ks[9], (DEPTH, MIX_WIDTH, D), MIX_WIDTH ** -0.5),
        'na_rpb': nrm(ks[10], (DEPTH, NA_HEADS, 2 * NA_KH - 1, 2 * NA_KW - 1), 0.5),
        'hg_lb': nrm(ks[11], (DEPTH + 1, 2, HG_WIDTH), 0.5),
        'hg_norm': 1.0 + nrm(ks[12], (DEPTH, HG_DV), 0.05),
        'w_grp': nrm(ks[13], (DEPTH, D, N_GROUPS), D ** -0.5),
        'b_grp': nrm(ks[14], (DEPTH, N_GROUPS), 0.01),
        'w_exp': nrm(ks[15], (DEPTH, D, N_EXPERTS), D ** -0.5),
        'b_exp': nrm(ks[16], (DEPTH, N_EXPERTS), 0.01),
        'w1': nrm(ks[17], (DEPTH, N_EXPERTS, D, D_EXPERT), D ** -0.5),
        'w3': nrm(ks[18], (DEPTH, N_EXPERTS, D, D_EXPERT), D ** -0.5),
        'w2': nrm(ks[19], (DEPTH, N_EXPERTS, D_EXPERT, D), D_EXPERT ** -0.5),
        'norm_final': 1.0 + nrm(ks[20], (D,), 0.05),
    }


def reference(x, c, ctx, c_ctx, w_mod, b_mod, norm_mix, norm_ffn, w_in, w_out, na_rpb,
              hg_lb, hg_norm, w_grp, b_grp, w_exp, b_exp, w1, w3, w2, norm_final):
    B, S, _ = x.shape
    cos, sin = _axial_rope(S)
    lb_all = jnp.cumsum(jax.nn.softmax(hg_lb.astype(F32), axis=0), axis=0)
    splits = [int(s) for s in np.cumsum(SEG_WIDTHS)[:-1]]
    na_heads = lambda t: t.reshape(*t.shape[:-1], NA_HEADS, NA_HEAD_DIM)
    xc = ctx
    for l in range(DEPTH):
        last = l == DEPTH - 1
        mod = jax.nn.silu(c) @ w_mod[l] + b_mod[l]
        sh_a, sc_a, ga_a, sh_f, sc_f, ga_f = jnp.split(mod[:, None, :], 6, axis=-1)
        mod_c = jax.nn.silu(c_ctx) @ w_mod[l] + b_mod[l]
        csh_a, csc_a, cga_a, csh_f, csc_f, cga_f = jnp.split(mod_c, 6)

        h = _rms(x, norm_mix[l]) * (1.0 + sc_a) + sh_a
        hc = _rms(xc, norm_mix[l]) * (1.0 + csc_a) + csh_a
        na_q, na_k, na_v, hg_q, hg_i, hg_ff, hg_fb, hg_g = jnp.split(h @ w_in[l], splits, axis=-1)
        if last:
            c_k, c_v, c_i, c_ff, c_fb = [hc @ _seg(w_in[l], i) for i in (1, 2, 4, 5, 6)]
        else:
            c_q, c_k, c_v, c_hq, c_i, c_ff, c_fb, c_g = jnp.split(hc @ w_in[l], splits, axis=-1)

        q_a = na_heads(na_q)
        na_out = _na_latent(_rope(q_a, cos, sin), _rope(na_heads(na_k), cos, sin), na_heads(na_v),
                            q_a, na_heads(c_k), na_heads(c_v), na_rpb[l])

        lb_f, lb_b = lb_all[l, 0], lb_all[l, 1]
        lq, li = _hg_heads(jax.nn.silu(hg_q)), _hg_heads(hg_i)
        lkf, lgf = _hg_forget(hg_ff, lb_f)
        lkb, lgb = _hg_forget(hg_fb, lb_b)
        ci = _hg_heads(c_i)
        ckf, cgf = _hg_forget(c_ff, lb_f)
        ckb, cgb = _hg_forget(c_fb, lb_b)
        if last:
            s_f = _hg_final_state(ckf, ci, cgf)
            s_b = _hg_final_state(_flip(ckb), _flip(ci), _flip(cgb))
        else:
            cq = _hg_heads(jax.nn.silu(c_hq))
            zero = jnp.zeros((B, HG_HEADS, HG_DK, HG_DV), F32)
            co_f, s_f = _hg_scan(cq, ckf, ci, cgf, zero)
            co_b, s_b = _hg_scan(_flip(cq), _flip(ckb), _flip(ci), _flip(cgb), zero)
        o_f, _ = _hg_scan(lq, lkf, li, lgf, s_f)
        o_b, _ = _hg_scan(_flip(lq), _flip(lkb), _flip(li), _flip(lgb), s_b)
        hg_out = _hg_output(o_f + _flip(o_b), hg_g, hg_norm[l])

        x = x + ga_a * (jnp.concatenate([na_out, hg_out], axis=-1) @ w_out[l])
        if not last:
            ctx_na = _ctx_attn(na_heads(c_q), na_heads(c_k), na_heads(c_v))
            ctx_hg = _hg_output(co_f + _flip(co_b), c_g, hg_norm[l])
            xc = xc + cga_a * (jnp.concatenate([ctx_na, ctx_hg], axis=-1) @ w_out[l])

        h = _rms(x, norm_ffn[l]) * (1.0 + sc_f) + sh_f
        x = x + ga_f * _hier_moe(h, w_grp[l], b_grp[l], w_exp[l], b_exp[l], w1[l], w3[l], w2[l])
        if not last:
            hc = _rms(xc, norm_ffn[l]) * (1.0 + csc_f) + csh_f
            xc = xc + cga_f * _hier_moe(hc, w_grp[l], b_grp[l], w_exp[l], b_exp[l], w1[l], w3[l], w2[l])

    return _rms(x, norm_final)
```

```python
import functools

import numpy as np
import jax
import jax.numpy as jnp
from jax import lax
from jax.experimental import pallas as pl
from jax.experimental.pallas import tpu as pltpu

F32 = jnp.float32
BF16 = jnp.bfloat16
HIGHEST = lax.Precision.HIGHEST

D_MODEL = 1024
GRID_W = 64
NA_HEADS = 8
NA_HEAD_DIM = 64
NA_WIDTH = NA_HEADS * NA_HEAD_DIM
NA_KH = 8
NA_KW = 16
ROPE_AXIS_DIM = NA_HEAD_DIM // 2
ROPE_BASE = 10000.0
HG_HEADS = 4
HG_DK = 128
HG_WIDTH = HG_HEADS * HG_DK
HG_CHUNK = 64
SEG = 512
N_GROUPS = 4
EXPERTS_PER_GROUP = 4
N_EXPERTS = N_GROUPS * EXPERTS_PER_GROUP
D_EXPERT = 512
EPS = 1e-6
NEG = -1e30
LANES = 128
SUBLANES = 8
VMEM_LIMIT = 56 * 1024 * 1024

HG_LEVELS = (32, 16, 8, 4)
HG_DIAG = 4


def _cparams(sem):
    return pltpu.CompilerParams(dimension_semantics=sem, vmem_limit_bytes=VMEM_LIMIT)


def _sigmoid(x):
    return 1.0 / (1.0 + jnp.exp(-x))


def _silu(x):
    return x * _sigmoid(x)


def _mod_kernel(c_ref, w_ref, b_ref, o_ref):
    s = _silu(c_ref[...])
    o_ref[...] = jnp.dot(s, w_ref[...], precision=HIGHEST, preferred_element_type=F32) + b_ref[...]


def _modulation(cc, w_mod, b_mod):
    rows, d = cc.shape
    n = w_mod.shape[1]
    tn = 1024
    return pl.pallas_call(
        _mod_kernel,
        grid=(n // tn,),
        in_specs=[pl.BlockSpec((rows, d), lambda j: (0, 0)),
                  pl.BlockSpec((d, tn), lambda j: (0, j)),
                  pl.BlockSpec((1, tn), lambda j: (0, j))],
        out_specs=pl.BlockSpec((rows, tn), lambda j: (0, j)),
        out_shape=jax.ShapeDtypeStruct((rows, n), F32),
        compiler_params=_cparams(("arbitrary",)),
        name="mod",
    )(cc, w_mod, b_mod.reshape(1, n))


def _norm_mod(x, g, sc, sh):
    y = x * lax.rsqrt(jnp.mean(x * x, axis=-1, keepdims=True) + EPS)
    return (y * g) * (1.0 + sc) + sh


def _rope(a, cos, sin):
    lane = lax.broadcasted_iota(jnp.int32, a.shape, 1)
    first = (lane % ROPE_AXIS_DIM) < (ROPE_AXIS_DIM // 2)
    up = pltpu.roll(a, LANES - ROPE_AXIS_DIM // 2, axis=1)
    dn = pltpu.roll(a, ROPE_AXIS_DIM // 2, axis=1)
    return a * cos + jnp.where(first, up, dn) * sin


def _proj_kernel(x_ref, g_ref, sc_ref, sh_ref, w_ref, cos_ref, sin_ref, o_ref):
    h = _norm_mod(x_ref[0], g_ref[...], sc_ref[0], sh_ref[0]).astype(BF16)
    scale = NA_HEAD_DIM ** -0.5
    for j in range(8):
        acc = jnp.dot(h, w_ref[:, j * SEG:(j + 1) * SEG], preferred_element_type=F32)
        if j <= 1:
            rot = jnp.concatenate(
                [_rope(acc[:, p * LANES:(p + 1) * LANES], cos_ref[:, p * LANES:(p + 1) * LANES],
                       sin_ref[:, p * LANES:(p + 1) * LANES]) for p in range(SEG // LANES)], axis=1)
            if j == 0:
                o_ref[0, :, 0:SEG] = (acc * scale).astype(BF16)
                o_ref[0, :, SEG:2 * SEG] = (rot * scale).astype(BF16)
            else:
                o_ref[0, :, 2 * SEG:3 * SEG] = rot.astype(BF16)
        else:
            o_ref[0, :, (j + 1) * SEG:(j + 2) * SEG] = acc.astype(BF16)


def _project(x, g, sc, sh, w_bf, cos_t, sin_t, tm):
    B, S, D = x.shape
    return pl.pallas_call(
        _proj_kernel,
        grid=(S // tm, B),
        in_specs=[pl.BlockSpec((1, tm, D), lambda s, b: (b, s, 0)),
                  pl.BlockSpec((1, D), lambda s, b: (0, 0)),
                  pl.BlockSpec((1, 1, D), lambda s, b: (b, 0, 0)),
                  pl.BlockSpec((1, 1, D), lambda s, b: (b, 0, 0)),
                  pl.BlockSpec((D, 8 * SEG), lambda s, b: (0, 0)),
                  pl.BlockSpec((tm, SEG), lambda s, b: (s, 0)),
                  pl.BlockSpec((tm, SEG), lambda s, b: (s, 0))],
        out_specs=pl.BlockSpec((1, tm, 9 * SEG), lambda s, b: (b, s, 0)),
        out_shape=jax.ShapeDtypeStruct((B, S, 9 * SEG), BF16),
        compiler_params=_cparams(("arbitrary", "arbitrary")),
        name="proj",
    )(x, g, sc, sh, w_bf, cos_t, sin_t)


def _ctxproj_kernel(x_ref, g_ref, sc_ref, sh_ref, w_ref, o_ref):
    h = _norm_mod(x_ref[0], g_ref[...], sc_ref[...], sh_ref[...]).astype(BF16)
    for j in range(5):
        acc = jnp.dot(h, w_ref[:, j * SEG:(j + 1) * SEG], preferred_element_type=F32)
        o_ref[0, :, j * SEG:(j + 1) * SEG] = acc.astype(BF16)


def _project_ctx(ctx, g, sc, sh, w_bf):
    B, L, D = ctx.shape
    return pl.pallas_call(
        _ctxproj_kernel,
        grid=(B,),
        in_specs=[pl.BlockSpec((1, L, D), lambda b: (b, 0, 0)),
                  pl.BlockSpec((1, D), lambda b: (0, 0)),
                  pl.BlockSpec((1, D), lambda b: (0, 0)),
                  pl.BlockSpec((1, D), lambda b: (0, 0)),
                  pl.BlockSpec((D, 5 * SEG), lambda b: (0, 0))],
        out_specs=pl.BlockSpec((1, L, 5 * SEG), lambda b: (b, 0, 0)),
        out_shape=jax.ShapeDtypeStruct((B, L, 5 * SEG), BF16),
        compiler_params=_cparams(("arbitrary",)),
        name="ctxproj",
    )(ctx, g, sc, sh, w_bf)


def _na_kernel(qraw_ref, qrot_ref, k_ref, v_ref, ck_ref, cv_ref, bias_ref, o_ref, *, rows):
    r = pl.program_id(1)
    rs = jnp.clip(r - NA_KH // 2, 0, rows - NA_KH)
    start = pl.multiple_of(rs * GRID_W, GRID_W)
    nk = NA_KH * GRID_W
    lane = lax.broadcasted_iota(jnp.int32, (GRID_W, LANES), 1)
    nt = (((1,), (1,)), ((), ()))
    for p in range(NA_WIDTH // LANES):
        cols = slice(p * LANES, (p + 1) * LANES)
        qr = qrot_ref[0, :, cols]
        qw = qraw_ref[0, :, cols]
        kw = k_ref[0, pl.ds(start, nk), cols]
        vw = v_ref[0, pl.ds(start, nk), cols]
        ck = ck_ref[0, :, cols]
        cv = cv_ref[0, :, cols]
        o_pair = None
        for hh in range(LANES // NA_HEAD_DIM):
            sel = (lane // NA_HEAD_DIM) == hh
            qm = jnp.where(sel, qr, jnp.zeros_like(qr))
            qwm = jnp.where(sel, qw, jnp.zeros_like(qw))
            s_loc = lax.dot_general(qm, kw, nt, preferred_element_type=F32) + bias_ref[0, 2 * p + hh]
            s_ctx = lax.dot_general(qwm, ck, nt, preferred_element_type=F32)
            m = jnp.maximum(jnp.max(s_loc, axis=-1, keepdims=True), jnp.max(s_ctx, axis=-1, keepdims=True))
            p_loc = jnp.exp(s_loc - m)
            p_ctx = jnp.exp(s_ctx - m)
            l = jnp.sum(p_loc, axis=-1, keepdims=True) + jnp.sum(p_ctx, axis=-1, keepdims=True)
            o = (jnp.dot(p_loc.astype(BF16), vw, preferred_element_type=F32)
                 + jnp.dot(p_ctx.astype(BF16), cv, preferred_element_type=F32)) / l
            o_pair = o if hh == 0 else jnp.where(sel, o, o_pair)
        o_ref[0, :, cols] = o_pair.astype(BF16)


def _na_bias_table(rpb):
    qc = np.arange(GRID_W)[:, None]
    kc = np.arange(GRID_W)[None, :]
    cs = np.clip(qc - NA_KW // 2, 0, GRID_W - NA_KW)
    valid = (kc >= cs) & (kc < cs + NA_KW)
    col_off = np.clip(kc - qc + NA_KW - 1, 0, 2 * NA_KW - 2)
    ro = np.arange(NA_KH)[:, None] + np.arange(NA_KH)[None, :]
    t = rpb.astype(F32)[:, ro][:, :, :, col_off]
    t = jnp.where(jnp.asarray(valid)[None, None, None], t, NEG)
    t = t.transpose(1, 0, 3, 2, 4)
    return t.reshape(NA_KH, NA_HEADS, GRID_W, NA_KH * GRID_W)


def _neighbourhood_attention(proj, cproj, bias_tbl):
    B, S, _ = proj.shape
    L = cproj.shape[1]
    rows = S // GRID_W
    nk = NA_KH * GRID_W

    def bias_map(b, r):
        return (jnp.clip(r - NA_KH // 2, 0, rows - NA_KH) - r + NA_KH - 1, 0, 0, 0)

    return pl.pallas_call(
        functools.partial(_na_kernel, rows=rows),
        grid=(B, rows),
        in_specs=[pl.BlockSpec((1, GRID_W, SEG), lambda b, r: (b, r, 0)),
                  pl.BlockSpec((1, GRID_W, SEG), lambda b, r: (b, r, 1)),
                  pl.BlockSpec((1, S, SEG), lambda b, r: (b, 0, 2)),
                  pl.BlockSpec((1, S, SEG), lambda b, r: (b, 0, 3)),
                  pl.BlockSpec((1, L, SEG), lambda b, r: (b, 0, 0)),
                  pl.BlockSpec((1, L, SEG), lambda b, r: (b, 0, 1)),
                  pl.BlockSpec((1, NA_HEADS, GRID_W, nk), bias_map)],
        out_specs=pl.BlockSpec((1, GRID_W, SEG), lambda b, r: (b, r, 0)),
        out_shape=jax.ShapeDtypeStruct((B, S, NA_WIDTH), BF16),
        compiler_params=_cparams(("arbitrary", "arbitrary")),
        name="na",
    )(proj, proj, proj, proj, cproj, cproj, bias_tbl)


def _split3(g):
    g1 = g.astype(BF16)
    r1 = g - g1.astype(F32)
    g2 = r1.astype(BF16)
    g3 = (r1 - g2.astype(F32)).astype(BF16)
    return g1, g2, g3


def _cumdot(c_bf, g):
    g1, g2, g3 = _split3(g)
    d = lambda a: jnp.dot(c_bf, a, preferred_element_type=F32)
    return (d(g3) + d(g2)) + d(g1)


def _forget(pre, lb):
    f = lb + (1.0 - lb) * _sigmoid(pre.astype(F32))
    return 1.0 - f, jnp.log(f)


def _hg_masks():
    c = HG_CHUNK
    t = np.arange(c)[:, None]
    s = np.arange(c)[None, :]
    lm = []
    for m in HG_LEVELS:
        lm.append(((t // (2 * m)) == (s // (2 * m))) & ((t % (2 * m)) >= m) & ((s % (2 * m)) < m))
    om = [((t - s) == d) & ((t % HG_DIAG) >= d) for d in range(HG_DIAG)]
    lm = np.stack(lm).astype(np.float32)
    om = np.stack(om).astype(np.float32)
    return np.concatenate([lm, om, lm.transpose(0, 2, 1), om.transpose(0, 2, 1)], axis=0)


def _hg_chunk_dir(q, v, pre, lb, cmat, masks_ref, ones_bf, st, *, fwd):
    c = HG_CHUNK
    nl = len(HG_LEVELS)
    k, g = _forget(pre, lb)
    b = _cumdot(cmat, g)
    bend = b[c - 1:c, :] if fwd else b[0:1, :]
    qe = (q * jnp.exp(b)).astype(BF16)
    ke = (k * jnp.exp(bend - b)).astype(BF16)
    nt = (((1,), (1,)), ((), ()))
    moff = 0 if fwd else nl + HG_DIAG
    a = None
    for li, m in enumerate(HG_LEVELS):
        pieces = []
        for j in range(c // SUBLANES):
            blk = (SUBLANES * j) // (2 * m)
            row = 2 * m * blk + (m - 1 if fwd else m)
            pieces.append(jnp.broadcast_to(b[row:row + 1, :], (SUBLANES, b.shape[1])))
        ref = jnp.concatenate(pieces, axis=0)
        e = jnp.exp(-jnp.abs(b - ref))
        r = lax.dot_general((q * e).astype(BF16), (k * e).astype(BF16), nt, preferred_element_type=F32)
        r = r * masks_ref[moff + li]
        a = r if a is None else a + r
    for d in range(HG_DIAG):
        if d == 0:
            x = q * k
        else:
            sh = d if fwd else c - d
            kd = pltpu.roll(k, sh, axis=0)
            w = jnp.minimum(b - pltpu.roll(b, sh, axis=0), 0.0)
            x = q * kd * jnp.exp(w)
        r = jnp.dot(x.astype(BF16), ones_bf, preferred_element_type=F32)
        a = a + r * masks_ref[moff + nl + d]
    o = (lax.dot_general(qe, st.astype(BF16), nt, preferred_element_type=F32)
         + jnp.dot(a.astype(BF16), v, preferred_element_type=F32))
    upd = lax.dot_general(v, ke, (((0,), (0,)), ((), ())), preferred_element_type=F32)
    st_new = jnp.exp(bend) * st + upd
    return o, st_new


def _hgrn_kernel(hq_ref, hi_ref, hff_ref, hfb_ref, hg_ref, ci_ref, cff_ref, cfb_ref, lb_ref, gain_ref,
                 lincl_ref, uincl_ref, ustrict_ref, lstrict_ref, masks_ref, o_ref, acc_ref, st_ref, *, n_chunks):
    c = HG_CHUNK
    ones_bf = jnp.ones((HG_DK, c), BF16)
    tn = (((0,), (0,)), ((), ()))

    for h in range(HG_HEADS):
        hs = slice(h * HG_DK, (h + 1) * HG_DK)
        ci = ci_ref[0, :, hs]
        for di, (pre_ref, cm_ref) in enumerate(((cff_ref, ustrict_ref), (cfb_ref, lstrict_ref))):
            k, g = _forget(pre_ref[0, :, hs], lb_ref[di:di + 1, hs])
            ke = (k * jnp.exp(_cumdot(cm_ref[...], g))).astype(BF16)
            st_ref[di, h] = lax.dot_general(ci, ke, tn, preferred_element_type=F32)

    def step(i, second):
        cf = i
        cb = n_chunks - 1 - i
        rf = pl.ds(pl.multiple_of(cf * c, c), c)
        rb = pl.ds(pl.multiple_of(cb * c, c), c)
        for h in range(HG_HEADS):
            hs = slice(h * HG_DK, (h + 1) * HG_DK)
            for di, rr in enumerate((rf, rb)):
                fwd = di == 0
                q = _silu(hq_ref[0, rr, hs].astype(F32))
                v = hi_ref[0, rr, hs]
                pre = (hff_ref if fwd else hfb_ref)[0, rr, hs]
                cmat = (lincl_ref if fwd else uincl_ref)[...]
                o, st_new = _hg_chunk_dir(q, v, pre, lb_ref[di:di + 1, hs], cmat, masks_ref, ones_bf,
                                          st_ref[di, h], fwd=fwd)
                st_ref[di, h] = st_new
                if not second:
                    acc_ref[rr, hs] = o
                else:
                    ot = acc_ref[rr, hs] + o
                    y = ot * lax.rsqrt(jnp.mean(ot * ot, axis=-1, keepdims=True) + EPS) * gain_ref[:, hs]
                    o_ref[0, rr, hs] = (y * _silu(hg_ref[0, rr, hs].astype(F32))).astype(BF16)

    def first_half(i, carry):
        step(i, False)
        return carry

    def second_half(i, carry):
        step(i, True)
        return carry

    lax.fori_loop(0, n_chunks // 2, first_half, 0)
    lax.fori_loop(n_chunks // 2, n_chunks, second_half, 0)


def _hgrn(proj, cproj, lb, gain):
    B, S, _ = proj.shape
    L = cproj.shape[1]
    c = HG_CHUNK
    tri = np.tril(np.ones((c, c), np.float32))
    tri_l = np.tril(np.ones((L, L), np.float32), -1)
    consts = [jnp.asarray(tri, BF16), jnp.asarray(tri.T, BF16),
              jnp.asarray(tri_l.T, BF16), jnp.asarray(tri_l, BF16), jnp.asarray(_hg_masks())]
    n_masks = 2 * (len(HG_LEVELS) + HG_DIAG)
    seq = lambda j: pl.BlockSpec((1, S, SEG), lambda b: (b, 0, j))
    cseq = lambda j: pl.BlockSpec((1, L, SEG), lambda b: (b, 0, j))
    full2 = lambda n, m: pl.BlockSpec((n, m), lambda b: (0, 0))
    return pl.pallas_call(
        functools.partial(_hgrn_kernel, n_chunks=S // c),
        grid=(B,),
        in_specs=[seq(4), seq(5), seq(6), seq(7), seq(8), cseq(2), cseq(3), cseq(4),
                  full2(2, HG_WIDTH), full2(1, HG_WIDTH),
                  full2(c, c), full2(c, c), full2(L, L), full2(L, L),
                  pl.BlockSpec((n_masks, c, c), lambda b: (0, 0, 0))],
        out_specs=pl.BlockSpec((1, S, HG_WIDTH), lambda b: (b, 0, 0)),
        out_shape=jax.ShapeDtypeStruct((B, S, HG_WIDTH), BF16),
        scratch_shapes=[pltpu.VMEM((S, HG_WIDTH), F32),
                        pltpu.VMEM((2, HG_HEADS, HG_DK, HG_DK), F32)],
        compiler_params=_cparams(("arbitrary",)),
        name="hgrn",
    )(proj, proj, proj, proj, proj, cproj, cproj, cproj, lb, gain, *consts)


def _route(logits):
    lane = lax.broadcasted_iota(jnp.int32, logits.shape, 1).astype(F32)
    big = float(LANES)
    is_g = lane < N_GROUPS
    gl = jnp.where(is_g, logits, -jnp.inf)
    gmax = jnp.max(gl, axis=-1, keepdims=True)
    g_sel = jnp.min(jnp.where(gl == gmax, lane, big), axis=-1, keepdims=True)
    g_w = 1.0 / jnp.sum(jnp.where(is_g, jnp.exp(gl - gmax), 0.0), axis=-1, keepdims=True)
    el_lane = lane - N_GROUPS
    in_grp = (el_lane >= g_sel * EXPERTS_PER_GROUP) & (el_lane < (g_sel + 1) * EXPERTS_PER_GROUP)
    e1 = jnp.where(in_grp, logits, -jnp.inf)
    v1 = jnp.max(e1, axis=-1, keepdims=True)
    i1 = jnp.min(jnp.where(e1 == v1, lane, big), axis=-1, keepdims=True)
    e2 = jnp.where(lane == i1, -jnp.inf, e1)
    v2 = jnp.max(e2, axis=-1, keepdims=True)
    i2 = jnp.min(jnp.where(e2 == v2, lane, big), axis=-1, keepdims=True)
    t = jnp.exp(v2 - v1)
    w1 = g_w / (1.0 + t)
    w2 = g_w * t / (1.0 + t)
    return jnp.where(lane == i1, w1, jnp.where(lane == i2, w2, 0.0))


def _outproj_kernel(x_ref, na_ref, hg_ref, w_ref, ga_ref, g_ref, sc_ref, sh_ref, wr_ref, br_ref,
                    x1_ref, h2_ref, gate_ref):
    mix = (jnp.dot(na_ref[...], w_ref[0:NA_WIDTH, :], preferred_element_type=F32)
           + jnp.dot(hg_ref[...], w_ref[NA_WIDTH:, :], preferred_element_type=F32))
    x1 = x_ref[...] + ga_ref[0] * mix
    x1_ref[...] = x1
    h2 = _norm_mod(x1, g_ref[...], sc_ref[0], sh_ref[0])
    h2_ref[...] = h2.astype(BF16)
    logits = jnp.dot(h2, wr_ref[...], precision=HIGHEST, preferred_element_type=F32) + br_ref[...]
    gate_ref[...] = _route(logits)


def _outproj(x2d, na2d, hg2d, w_out_bf, ga, g, sc, sh, wr, br, tm, S):
    T, D = x2d.shape
    per = S // tm
    tok = lambda w: pl.BlockSpec((tm, w), lambda i: (i, 0))
    bat = pl.BlockSpec((1, 1, D), lambda i: (i // per, 0, 0))
    return pl.pallas_call(
        _outproj_kernel,
        grid=(T // tm,),
        in_specs=[tok(D), tok(NA_WIDTH), tok(HG_WIDTH),
                  pl.BlockSpec((NA_WIDTH + HG_WIDTH, D), lambda i: (0, 0)),
                  bat, pl.BlockSpec((1, D), lambda i: (0, 0)), bat, bat,
                  pl.BlockSpec((D, LANES), lambda i: (0, 0)),
                  pl.BlockSpec((1, LANES), lambda i: (0, 0))],
        out_specs=[tok(D), tok(D), tok(LANES)],
        out_shape=[jax.ShapeDtypeStruct((T, D), F32), jax.ShapeDtypeStruct((T, D), BF16),
                   jax.ShapeDtypeStruct((T, LANES), F32)],
        compiler_params=_cparams(("arbitrary",)),
        name="outproj",
    )(x2d, na2d, hg2d, w_out_bf, ga, g, sc, sh, wr, br)


def _moe_kernel(x1_ref, h2_ref, gate_ref, w1_ref, w3_ref, w2_ref, ga_ref, gf_ref, o_ref, acc_ref):
    e = pl.program_id(1)

    @pl.when(e == 0)
    def _():
        acc_ref[...] = jnp.zeros_like(acc_ref)

    t = h2_ref[...]
    a = jnp.dot(t, w1_ref[0], preferred_element_type=F32)
    b = jnp.dot(t, w3_ref[0], preferred_element_type=F32)
    hid = (_silu(a) * b).astype(BF16)
    gate = gate_ref[...]
    lane = lax.broadcasted_iota(jnp.int32, gate.shape, 1)
    ge = jnp.sum(jnp.where(lane == e + N_GROUPS, gate, 0.0), axis=-1, keepdims=True)
    acc_ref[...] += ge * jnp.dot(hid, w2_ref[0], preferred_element_type=F32)

    @pl.when(e == N_EXPERTS - 1)
    def _():
        x2 = x1_ref[...] + ga_ref[0] * acc_ref[...]
        o_ref[...] = x2 * lax.rsqrt(jnp.mean(x2 * x2, axis=-1, keepdims=True) + EPS) * gf_ref[...]


def _moe(x1, h2, gate, w1_bf, w3_bf, w2_bf, ga, gfin, tm, S):
    T, D = x1.shape
    per = S // tm
    tok = lambda w: pl.BlockSpec((tm, w), lambda i, e: (i, 0))
    return pl.pallas_call(
        _moe_kernel,
        grid=(T // tm, N_EXPERTS),
        in_specs=[tok(D), tok(D), tok(LANES),
                  pl.BlockSpec((1, D, D_EXPERT), lambda i, e: (e, 0, 0)),
                  pl.BlockSpec((1, D, D_EXPERT), lambda i, e: (e, 0, 0)),
                  pl.BlockSpec((1, D_EXPERT, D), lambda i, e: (e, 0, 0)),
                  pl.BlockSpec((1, 1, D), lambda i, e: (i // per, 0, 0)),
                  pl.BlockSpec((1, D), lambda i, e: (0, 0))],
        out_specs=tok(D),
        out_shape=jax.ShapeDtypeStruct((T, D), F32),
        scratch_shapes=[pltpu.VMEM((tm, D), F32)],
        compiler_params=_cparams(("arbitrary", "arbitrary")),
        name="moe",
    )(x1, h2, gate, w1_bf, w3_bf, w2_bf, ga, gfin)


def _rope_tables(S):
    t = jnp.arange(S)
    pos = jnp.stack([t // GRID_W, t % GRID_W], axis=-1).astype(F32)
    inv = ROPE_BASE ** (-jnp.arange(0, ROPE_AXIS_DIM, 2, dtype=F32) / ROPE_AXIS_DIM)
    ang = pos[:, :, None] * inv
    cos, sin = jnp.cos(ang), jnp.sin(ang)
    cos_h = jnp.concatenate([cos, cos], axis=-1).reshape(S, NA_HEAD_DIM)
    sin_h = jnp.concatenate([-sin, sin], axis=-1).reshape(S, NA_HEAD_DIM)
    return jnp.tile(cos_h, (1, NA_HEADS)), jnp.tile(sin_h, (1, NA_HEADS))


def kernel(x, c, ctx, c_ctx, w_mod, b_mod, norm_mix, norm_ffn, w_in, w_out, na_rpb, hg_lb, hg_norm,
           w_grp, b_grp, w_exp, b_exp, w1, w3, w2, norm_final):
    B, S, D = x.shape
    T = B * S
    assert w_mod.shape[0] == 1, "single-layer kernel"

    rows = -(-(B + 1) // SUBLANES) * SUBLANES
    cc = jnp.zeros((rows, D), F32).at[:B].set(c).at[B].set(c_ctx)
    mod = _modulation(cc, w_mod[0], b_mod[0])
    sh_a, sc_a, ga_a, sh_f, sc_f, ga_f = [m.reshape(B, 1, D) for m in jnp.split(mod[:B], 6, axis=-1)]
    csh_a, csc_a = [m.reshape(1, D) for m in jnp.split(mod[B], 6)[:2]]

    w_in_bf = w_in[0].astype(BF16)
    w_ctx_bf = jnp.concatenate([w_in_bf[:, i * SEG:(i + 1) * SEG] for i in (1, 2, 4, 5, 6)], axis=1)
    cos_t, sin_t = _rope_tables(S)
    g_mix = norm_mix[0].reshape(1, D)

    proj = _project(x, g_mix, sc_a, sh_a, w_in_bf, cos_t, sin_t, tm=512)
    cproj = _project_ctx(ctx, g_mix, csc_a, csh_a, w_ctx_bf)

    na_out = _neighbourhood_attention(proj, cproj, _na_bias_table(na_rpb[0]))

    lb = jnp.cumsum(jax.nn.softmax(hg_lb.astype(F32), axis=0), axis=0)[0]
    gain = jnp.tile(hg_norm[0].astype(F32), HG_HEADS).reshape(1, HG_WIDTH)
    hg_out = _hgrn(proj, cproj, lb, gain)

    wr = jnp.zeros((D, LANES), F32).at[:, :N_GROUPS].set(w_grp[0]).at[:, N_GROUPS:N_GROUPS + N_EXPERTS].set(w_exp[0])
    br = jnp.zeros((1, LANES), F32).at[0, :N_GROUPS].set(b_grp[0]).at[0, N_GROUPS:N_GROUPS + N_EXPERTS].set(b_exp[0])
    x1, h2, gate = _outproj(x.reshape(T, D), na_out.reshape(T, NA_WIDTH), hg_out.reshape(T, HG_WIDTH),
                            w_out[0].astype(BF16), ga_a, norm_ffn[0].reshape(1, D), sc_f, sh_f, wr, br,
                            tm=512, S=S)

    out = _moe(x1, h2, gate, w1[0].astype(BF16), w3[0].astype(BF16), w2[0].astype(BF16), ga_f,
               norm_final.reshape(1, D), tm=1024, S=S)
    return out.reshape(B, S, D)
```

```python
import functools

import numpy as np
import jax
import jax.numpy as jnp
from jax import lax
from jax.experimental import pallas as pl
from jax.experimental.pallas import tpu as pltpu

F32 = jnp.float32
BF16 = jnp.bfloat16
HIGHEST = lax.Precision.HIGHEST

D_MODEL = 1024
GRID_W = 64
NA_HEADS = 8
NA_HEAD_DIM = 64
NA_WIDTH = NA_HEADS * NA_HEAD_DIM
NA_KH = 8
NA_KW = 16
ROPE_AXIS_DIM = NA_HEAD_DIM // 2
ROPE_BASE = 10000.0
HG_HEADS = 4
HG_DK = 128
HG_WIDTH = HG_HEADS * HG_DK
HG_CHUNK = 64
SEG = 512
N_GROUPS = 4
EXPERTS_PER_GROUP = 4
N_EXPERTS = N_GROUPS * EXPERTS_PER_GROUP
D_EXPERT = 512
EPS = 1e-6
NEG = -1e30
LANES = 128
SUBLANES = 8
VMEM_LIMIT = 56 * 1024 * 1024

HG_LEVELS = (32, 16, 8, 4)
HG_DIAG = 4


def _cparams(sem):
    return pltpu.CompilerParams(dimension_semantics=sem, vmem_limit_bytes=VMEM_LIMIT)


def _sigmoid(x):
    return 1.0 / (1.0 + jnp.exp(-x))


def _silu(x):
    return x * _sigmoid(x)


def _mod_kernel(c_ref, w_ref, b_ref, o_ref):
    s = _silu(c_ref[...])
    o_ref[...] = jnp.dot(s, w_ref[...], precision=HIGHEST, preferred_element_type=F32) + b_ref[...]


def _modulation(cc, w_mod, b_mod):
    rows, d = cc.shape
    n = w_mod.shape[1]
    tn = 1024
    return pl.pallas_call(
        _mod_kernel,
        grid=(n // tn,),
        in_specs=[pl.BlockSpec((rows, d), lambda j: (0, 0)),
                  pl.BlockSpec((d, tn), lambda j: (0, j)),
                  pl.BlockSpec((1, tn), lambda j: (0, j))],
        out_specs=pl.BlockSpec((rows, tn), lambda j: (0, j)),
        out_shape=jax.ShapeDtypeStruct((rows, n), F32),
        compiler_params=_cparams(("arbitrary",)),
        name="mod",
    )(cc, w_mod, b_mod.reshape(1, n))


def _norm_mod(x, g, sc, sh):
    y = x * lax.rsqrt(jnp.mean(x * x, axis=-1, keepdims=True) + EPS)
    return (y * g) * (1.0 + sc) + sh


def _rope(a, cos, sin):
    lane = lax.broadcasted_iota(jnp.int32, a.shape, 1)
    first = (lane % ROPE_AXIS_DIM) < (ROPE_AXIS_DIM // 2)
    up = pltpu.roll(a, LANES - ROPE_AXIS_DIM // 2, axis=1)
    dn = pltpu.roll(a, ROPE_AXIS_DIM // 2, axis=1)
    return a * cos + jnp.where(first, up, dn) * sin


def _proj_kernel(x_ref, g_ref, sc_ref, sh_ref, w_ref, cos_ref, sin_ref, o_ref):
    h = _norm_mod(x_ref[0], g_ref[...], sc_ref[0], sh_ref[0]).astype(BF16)
    scale = NA_HEAD_DIM ** -0.5
    for j in range(8):
        acc = jnp.dot(h, w_ref[:, j * SEG:(j + 1) * SEG], preferred_element_type=F32)
        if j <= 1:
            rot = jnp.concatenate(
                [_rope(acc[:, p * LANES:(p + 1) * LANES], cos_ref[:, p * LANES:(p + 1) * LANES],
                       sin_ref[:, p * LANES:(p + 1) * LANES]) for p in range(SEG // LANES)], axis=1)
            if j == 0:
                o_ref[0, :, 0:SEG] = (acc * scale).astype(BF16)
                o_ref[0, :, SEG:2 * SEG] = (rot * scale).astype(BF16)
            else:
                o_ref[0, :, 2 * SEG:3 * SEG] = rot.astype(BF16)
        else:
            o_ref[0, :, (j + 1) * SEG:(j + 2) * SEG] = acc.astype(BF16)


def _project(x, g, sc, sh, w_bf, cos_t, sin_t, tm):
    B, S, D = x.shape
    return pl.pallas_call(
        _proj_kernel,
        grid=(S // tm, B),
        in_specs=[pl.BlockSpec((1, tm, D), lambda s, b: (b, s, 0)),
                  pl.BlockSpec((1, D), lambda s, b: (0, 0)),
                  pl.BlockSpec((1, 1, D), lambda s, b: (b, 0, 0)),
                  pl.BlockSpec((1, 1, D), lambda s, b: (b, 0, 0)),
                  pl.BlockSpec((D, 8 * SEG), lambda s, b: (0, 0)),
                  pl.BlockSpec((tm, SEG), lambda s, b: (s, 0)),
                  pl.BlockSpec((tm, SEG), lambda s, b: (s, 0))],
        out_specs=pl.BlockSpec((1, tm, 9 * SEG), lambda s, b: (b, s, 0)),
        out_shape=jax.ShapeDtypeStruct((B, S, 9 * SEG), BF16),
        compiler_params=_cparams(("arbitrary", "arbitrary")),
        name="proj",
    )(x, g, sc, sh, w_bf, cos_t, sin_t)


def _ctxproj_kernel(x_ref, g_ref, sc_ref, sh_ref, w_ref, o_ref):
    h = _norm_mod(x_ref[0], g_ref[...], sc_ref[...], sh_ref[...]).astype(BF16)
    for j in range(5):
        acc = jnp.dot(h, w_ref[:, j * SEG:(j + 1) * SEG], preferred_element_type=F32)
        o_ref[0, :, j * SEG:(j + 1) * SEG] = acc.astype(BF16)


def _project_ctx(ctx, g, sc, sh, w_bf):
    B, L, D = ctx.shape
    return pl.pallas_call(
        _ctxproj_kernel,
        grid=(B,),
        in_specs=[pl.BlockSpec((1, L, D), lambda b: (b, 0, 0)),
                  pl.BlockSpec((1, D), lambda b: (0, 0)),
                  pl.BlockSpec((1, D), lambda b: (0, 0)),
                  pl.BlockSpec((1, D), lambda b: (0, 0)),
                  pl.BlockSpec((D, 5 * SEG), lambda b: (0, 0))],
        out_specs=pl.BlockSpec((1, L, 5 * SEG), lambda b: (b, 0, 0)),
        out_shape=jax.ShapeDtypeStruct((B, L, 5 * SEG), BF16),
        compiler_params=_cparams(("arbitrary",)),
        name="ctxproj",
    )(ctx, g, sc, sh, w_bf)


def _na_kernel(qraw_ref, qrot_ref, k_ref, v_ref, ck_ref, cv_ref, bias_ref, o_ref, vt_ref, cvt_ref, *, rows):
    r = pl.program_id(1)

    @pl.when(r == 0)
    def _():
        def body(i, carry):
            blk = v_ref[0, pl.ds(pl.multiple_of(i * GRID_W, GRID_W), GRID_W), :]
            vt_ref[i] = blk.astype(F32).T.astype(BF16)
            return carry
        lax.fori_loop(0, rows, body, 0)
        cvt_ref[...] = cv_ref[0].astype(F32).T.astype(BF16)

    rs = jnp.clip(r - NA_KH // 2, 0, rows - NA_KH)
    start = pl.multiple_of(rs * GRID_W, GRID_W)
    nk = NA_KH * GRID_W
    lane = lax.broadcasted_iota(jnp.int32, (GRID_W, LANES), 1)
    sel0 = lane < NA_HEAD_DIM
    nt = (((1,), (1,)), ((), ()))
    for p in range(NA_WIDTH // LANES):
        cols = slice(p * LANES, (p + 1) * LANES)
        qr = qrot_ref[0, :, cols]
        qw = qraw_ref[0, :, cols]
        zero = jnp.zeros_like(qr)
        qs_rot = jnp.concatenate([jnp.where(sel0, qr, zero), jnp.where(sel0, zero, qr)], axis=0)
        qs_raw = jnp.concatenate([jnp.where(sel0, qw, zero), jnp.where(sel0, zero, qw)], axis=0)
        s_loc = lax.dot_general(k_ref[0, pl.ds(start, nk), cols], qs_rot, nt,
                                preferred_element_type=F32) + bias_ref[0, p]
        s_ctx = lax.dot_general(ck_ref[0, :, cols], qs_raw, nt, preferred_element_type=F32)
        m = jnp.maximum(jnp.max(s_loc, axis=0, keepdims=True), jnp.max(s_ctx, axis=0, keepdims=True))
        p_loc = jnp.exp(s_loc - m)
        p_ctx = jnp.exp(s_ctx - m)
        l = jnp.sum(p_loc, axis=0, keepdims=True) + jnp.sum(p_ctx, axis=0, keepdims=True)
        p_loc = p_loc.astype(BF16)
        ot = jnp.dot(cvt_ref[cols, :], p_ctx.astype(BF16), preferred_element_type=F32)
        for i in range(NA_KH):
            ot = ot + jnp.dot(vt_ref[rs + i, cols, :], p_loc[i * GRID_W:(i + 1) * GRID_W, :],
                              preferred_element_type=F32)
        res = (ot * (1.0 / l)).T
        o_ref[0, :, cols] = jnp.where(sel0, res[0:GRID_W], res[GRID_W:2 * GRID_W]).astype(BF16)


def _na_bias_table(rpb):
    qc = np.arange(GRID_W)[None, :]
    kc = np.arange(GRID_W)[:, None]
    cs = np.clip(qc - NA_KW // 2, 0, GRID_W - NA_KW)
    valid = (kc >= cs) & (kc < cs + NA_KW)
    col_off = np.clip(kc - qc + NA_KW - 1, 0, 2 * NA_KW - 2)
    ro = np.arange(NA_KH)[:, None] + np.arange(NA_KH)[None, :]
    t = rpb.astype(F32)[:, ro][:, :, :, col_off]
    t = jnp.where(jnp.asarray(valid)[None, None, None], t, NEG)
    n_pairs = NA_HEADS // 2
    t = t.reshape(n_pairs, 2, NA_KH, NA_KH, GRID_W, GRID_W)
    t = t.transpose(2, 0, 3, 4, 1, 5)
    return t.reshape(NA_KH, n_pairs, NA_KH * GRID_W, 2 * GRID_W)


def _neighbourhood_attention(proj, cproj, bias_tbl):
    B, S, _ = proj.shape
    L = cproj.shape[1]
    rows = S // GRID_W
    nk = NA_KH * GRID_W

    def bias_map(b, r):
        return (jnp.clip(r - NA_KH // 2, 0, rows - NA_KH) - r + NA_KH - 1, 0, 0, 0)

    return pl.pallas_call(
        functools.partial(_na_kernel, rows=rows),
        grid=(B, rows),
        in_specs=[pl.BlockSpec((1, GRID_W, SEG), lambda b, r: (b, r, 0)),
                  pl.BlockSpec((1, GRID_W, SEG), lambda b, r: (b, r, 1)),
                  pl.BlockSpec((1, S, SEG), lambda b, r: (b, 0, 2)),
                  pl.BlockSpec((1, S, SEG), lambda b, r: (b, 0, 3)),
                  pl.BlockSpec((1, L, SEG), lambda b, r: (b, 0, 0)),
                  pl.BlockSpec((1, L, SEG), lambda b, r: (b, 0, 1)),
                  pl.BlockSpec((1, NA_HEADS // 2, nk, LANES), bias_map)],
        out_specs=pl.BlockSpec((1, GRID_W, SEG), lambda b, r: (b, r, 0)),
        out_shape=jax.ShapeDtypeStruct((B, S, NA_WIDTH), BF16),
        scratch_shapes=[pltpu.VMEM((rows, NA_WIDTH, GRID_W), BF16),
                        pltpu.VMEM((NA_WIDTH, L), BF16)],
        compiler_params=_cparams(("arbitrary", "arbitrary")),
        name="na",
    )(proj, proj, proj, proj, cproj, cproj, bias_tbl)


def _split3(g):
    g1 = g.astype(BF16)
    r1 = g - g1.astype(F32)
    g2 = r1.astype(BF16)
    g3 = (r1 - g2.astype(F32)).astype(BF16)
    return g1, g2, g3


def _cumdot(c_bf, g):
    g1, g2, g3 = _split3(g)
    d = lambda a: jnp.dot(c_bf, a, preferred_element_type=F32)
    return (d(g3) + d(g2)) + d(g1)


def _forget(pre, lb):
    f = lb + (1.0 - lb) * _sigmoid(pre.astype(F32))
    return 1.0 - f, jnp.log(f)


def _hg_masks():
    c = HG_CHUNK
    t = np.arange(c)[:, None]
    s = np.arange(c)[None, :]
    lm = []
    for m in HG_LEVELS:
        lm.append(((t // (2 * m)) == (s // (2 * m))) & ((t % (2 * m)) >= m) & ((s % (2 * m)) < m))
    om = [((t - s) == d) & ((t % HG_DIAG) >= d) for d in range(HG_DIAG)]
    lm = np.stack(lm).astype(np.float32)
    om = np.stack(om).astype(np.float32)
    return np.concatenate([lm, om, lm.transpose(0, 2, 1), om.transpose(0, 2, 1)], axis=0)


def _hg_chunk_dir(q, v, pre, lb, cmat, masks_ref, ones_bf, st, *, fwd):
    c = HG_CHUNK
    nl = len(HG_LEVELS)
    k, g = _forget(pre, lb)
    b = _cumdot(cmat, g)
    bend = b[c - 1:c, :] if fwd else b[0:1, :]
    qe = (q * jnp.exp(b)).astype(BF16)
    ke = (k * jnp.exp(bend - b)).astype(BF16)
    nt = (((1,), (1,)), ((), ()))
    moff = 0 if fwd else nl + HG_DIAG
    a = None
    for li, m in enumerate(HG_LEVELS):
        pieces = []
        for j in range(c // SUBLANES):
            blk = (SUBLANES * j) // (2 * m)
            row = 2 * m * blk + (m - 1 if fwd else m)
            pieces.append(jnp.broadcast_to(b[row:row + 1, :], (SUBLANES, b.shape[1])))
        ref = jnp.concatenate(pieces, axis=0)
        e = jnp.exp(-jnp.abs(b - ref))
        r = lax.dot_general((q * e).astype(BF16), (k * e).astype(BF16), nt, preferred_element_type=F32)
        r = r * masks_ref[moff + li]
        a = r if a is None else a + r
    for d in range(HG_DIAG):
        if d == 0:
            x = q * k
        else:
            sh = d if fwd else c - d
            kd = pltpu.roll(k, sh, axis=0)
            w = jnp.minimum(b - pltpu.roll(b, sh, axis=0), 0.0)
            x = q * kd * jnp.exp(w)
        r = jnp.dot(x.astype(BF16), ones_bf, preferred_element_type=F32)
        a = a + r * masks_ref[moff + nl + d]
    o = (lax.dot_general(qe, st.astype(BF16), nt, preferred_element_type=F32)
         + jnp.dot(a.astype(BF16), v, preferred_element_type=F32))
    upd = lax.dot_general(v, ke, (((0,), (0,)), ((), ())), preferred_element_type=F32)
    st_new = jnp.exp(bend) * st + upd
    return o, st_new


def _hgrn_kernel(hq_ref, hi_ref, hff_ref, hfb_ref, hg_ref, ci_ref, cff_ref, cfb_ref, lb_ref, gain_ref,
                 lincl_ref, uincl_ref, ustrict_ref, lstrict_ref, masks_ref, o_ref, acc_ref, st_ref, *, n_chunks):
    c = HG_CHUNK
    ones_bf = jnp.ones((HG_DK, c), BF16)
    tn = (((0,), (0,)), ((), ()))

    for h in range(HG_HEADS):
        hs = slice(h * HG_DK, (h + 1) * HG_DK)
        ci = ci_ref[0, :, hs]
        for di, (pre_ref, cm_ref) in enumerate(((cff_ref, ustrict_ref), (cfb_ref, lstrict_ref))):
            k, g = _forget(pre_ref[0, :, hs], lb_ref[di:di + 1, hs])
            ke = (k * jnp.exp(_cumdot(cm_ref[...], g))).astype(BF16)
            st_ref[di, h] = lax.dot_general(ci, ke, tn, preferred_element_type=F32)

    def step(i, second):
        cf = i
        cb = n_chunks - 1 - i
        rf = pl.ds(pl.multiple_of(cf * c, c), c)
        rb = pl.ds(pl.multiple_of(cb * c, c), c)
        for h in range(HG_HEADS):
            hs = slice(h * HG_DK, (h + 1) * HG_DK)
            for di, rr in enumerate((rf, rb)):
                fwd = di == 0
                q = _silu(hq_ref[0, rr, hs].astype(F32))
                v = hi_ref[0, rr, hs]
                pre = (hff_ref if fwd else hfb_ref)[0, rr, hs]
                cmat = (lincl_ref if fwd else uincl_ref)[...]
                o, st_new = _hg_chunk_dir(q, v, pre, lb_ref[di:di + 1, hs], cmat, masks_ref, ones_bf,
                                          st_ref[di, h], fwd=fwd)
                st_ref[di, h] = st_new
                if not second:
                    acc_ref[rr, hs] = o
                else:
                    ot = acc_ref[rr, hs] + o
                    y = ot * lax.rsqrt(jnp.mean(ot * ot, axis=-1, keepdims=True) + EPS) * gain_ref[:, hs]
                    o_ref[0, rr, hs] = (y * _silu(hg_ref[0, rr, hs].astype(F32))).astype(BF16)

    def first_half(i, carry):
        step(i, False)
        return carry

    def second_half(i, carry):
        step(i, True)
        return carry

    lax.fori_loop(0, n_chunks // 2, first_half, 0)
    lax.fori_loop(n_chunks // 2, n_chunks, second_half, 0)


def _hgrn(proj, cproj, lb, gain):
    B, S, _ = proj.shape
    L = cproj.shape[1]
    c = HG_CHUNK
    tri = np.tril(np.ones((c, c), np.float32))
    tri_l = np.tril(np.ones((L, L), np.float32), -1)
    consts = [jnp.asarray(tri, BF16), jnp.asarray(tri.T, BF16),
              jnp.asarray(tri_l.T, BF16), jnp.asarray(tri_l, BF16), jnp.asarray(_hg_masks())]
    n_masks = 2 * (len(HG_LEVELS) + HG_DIAG)
    seq = lambda j: pl.BlockSpec((1, S, SEG), lambda b: (b, 0, j))
    cseq = lambda j: pl.BlockSpec((1, L, SEG), lambda b: (b, 0, j))
    full2 = lambda n, m: pl.BlockSpec((n, m), lambda b: (0, 0))
    return pl.pallas_call(
        functools.partial(_hgrn_kernel, n_chunks=S // c),
        grid=(B,),
        in_specs=[seq(4), seq(5), seq(6), seq(7), seq(8), cseq(2), cseq(3), cseq(4),
                  full2(2, HG_WIDTH), full2(1, HG_WIDTH),
                  full2(c, c), full2(c, c), full2(L, L), full2(L, L),
                  pl.BlockSpec((n_masks, c, c), lambda b: (0, 0, 0))],
        out_specs=pl.BlockSpec((1, S, HG_WIDTH), lambda b: (b, 0, 0)),
        out_shape=jax.ShapeDtypeStruct((B, S, HG_WIDTH), BF16),
        scratch_shapes=[pltpu.VMEM((S, HG_WIDTH), F32),
                        pltpu.VMEM((2, HG_HEADS, HG_DK, HG_DK), F32)],
        compiler_params=_cparams(("arbitrary",)),
        name="hgrn",
    )(proj, proj, proj, proj, proj, cproj, cproj, cproj, lb, gain, *consts)


def _route(logits):
    lane = lax.broadcasted_iota(jnp.int32, logits.shape, 1).astype(F32)
    big = float(LANES)
    is_g = lane < N_GROUPS
    gl = jnp.where(is_g, logits, -jnp.inf)
    gmax = jnp.max(gl, axis=-1, keepdims=True)
    g_sel = jnp.min(jnp.where(gl == gmax, lane, big), axis=-1, keepdims=True)
    g_w = 1.0 / jnp.sum(jnp.where(is_g, jnp.exp(gl - gmax), 0.0), axis=-1, keepdims=True)
    el_lane = lane - N_GROUPS
    in_grp = (el_lane >= g_sel * EXPERTS_PER_GROUP) & (el_lane < (g_sel + 1) * EXPERTS_PER_GROUP)
    e1 = jnp.where(in_grp, logits, -jnp.inf)
    v1 = jnp.max(e1, axis=-1, keepdims=True)
    i1 = jnp.min(jnp.where(e1 == v1, lane, big), axis=-1, keepdims=True)
    e2 = jnp.where(lane == i1, -jnp.inf, e1)
    v2 = jnp.max(e2, axis=-1, keepdims=True)
    i2 = jnp.min(jnp.where(e2 == v2, lane, big), axis=-1, keepdims=True)
    t = jnp.exp(v2 - v1)
    w1 = g_w / (1.0 + t)
    w2 = g_w * t / (1.0 + t)
    return jnp.where(lane == i1, w1, jnp.where(lane == i2, w2, 0.0))


def _outproj_kernel(x_ref, na_ref, hg_ref, w_ref, ga_ref, g_ref, sc_ref, sh_ref, wr_ref, br_ref,
                    x1_ref, h2_ref, gate_ref):
    mix = (jnp.dot(na_ref[...], w_ref[0:NA_WIDTH, :], preferred_element_type=F32)
           + jnp.dot(hg_ref[...], w_ref[NA_WIDTH:, :], preferred_element_type=F32))
    x1 = x_ref[...] + ga_ref[0] * mix
    x1_ref[...] = x1
    h2 = _norm_mod(x1, g_ref[...], sc_ref[0], sh_ref[0])
    h2_ref[...] = h2.astype(BF16)
    logits = jnp.dot(h2, wr_ref[...], precision=HIGHEST, preferred_element_type=F32) + br_ref[...]
    gate_ref[...] = _route(logits)


def _outproj(x2d, na2d, hg2d, w_out_bf, ga, g, sc, sh, wr, br, tm, S):
    T, D = x2d.shape
    per = S // tm
    tok = lambda w: pl.BlockSpec((tm, w), lambda i: (i, 0))
    bat = pl.BlockSpec((1, 1, D), lambda i: (i // per, 0, 0))
    return pl.pallas_call(
        _outproj_kernel,
        grid=(T // tm,),
        in_specs=[tok(D), tok(NA_WIDTH), tok(HG_WIDTH),
                  pl.BlockSpec((NA_WIDTH + HG_WIDTH, D), lambda i: (0, 0)),
                  bat, pl.BlockSpec((1, D), lambda i: (0, 0)), bat, bat,
                  pl.BlockSpec((D, LANES), lambda i: (0, 0)),
                  pl.BlockSpec((1, LANES), lambda i: (0, 0))],
        out_specs=[tok(D), tok(D), tok(LANES)],
        out_shape=[jax.ShapeDtypeStruct((T, D), F32), jax.ShapeDtypeStruct((T, D), BF16),
                   jax.ShapeDtypeStruct((T, LANES), F32)],
        compiler_params=_cparams(("arbitrary",)),
        name="outproj",
    )(x2d, na2d, hg2d, w_out_bf, ga, g, sc, sh, wr, br)


def _moe_kernel(x1_ref, h2_ref, gate_ref, w1_ref, w3_ref, w2_ref, ga_ref, gf_ref, o_ref, acc_ref):
    e = pl.program_id(1)

    @pl.when(e == 0)
    def _():
        acc_ref[...] = jnp.zeros_like(acc_ref)

    t = h2_ref[...]
    a = jnp.dot(t, w1_ref[0], preferred_element_type=F32)
    b = jnp.dot(t, w3_ref[0], preferred_element_type=F32)
    hid = (_silu(a) * b).astype(BF16)
    gate = gate_ref[...]
    lane = lax.broadcasted_iota(jnp.int32, gate.shape, 1)
    ge = jnp.sum(jnp.where(lane == e + N_GROUPS, gate, 0.0), axis=-1, keepdims=True)
    acc_ref[...] += ge * jnp.dot(hid, w2_ref[0], preferred_element_type=F32)

    @pl.when(e == N_EXPERTS - 1)
    def _():
        x2 = x1_ref[...] + ga_ref[0] * acc_ref[...]
        o_ref[...] = x2 * lax.rsqrt(jnp.mean(x2 * x2, axis=-1, keepdims=True) + EPS) * gf_ref[...]


def _moe(x1, h2, gate, w1_bf, w3_bf, w2_bf, ga, gfin, tm, S):
    T, D = x1.shape
    per = S // tm
    tok = lambda w: pl.BlockSpec((tm, w), lambda i, e: (i, 0))
    return pl.pallas_call(
        _moe_kernel,
        grid=(T // tm, N_EXPERTS),
        in_specs=[tok(D), tok(D), tok(LANES),
                  pl.BlockSpec((1, D, D_EXPERT), lambda i, e: (e, 0, 0)),
                  pl.BlockSpec((1, D, D_EXPERT), lambda i, e: (e, 0, 0)),
                  pl.BlockSpec((1, D_EXPERT, D), lambda i, e: (e, 0, 0)),
                  pl.BlockSpec((1, 1, D), lambda i, e: (i // per, 0, 0)),
                  pl.BlockSpec((1, D), lambda i, e: (0, 0))],
        out_specs=tok(D),
        out_shape=jax.ShapeDtypeStruct((T, D), F32),
        scratch_shapes=[pltpu.VMEM((tm, D), F32)],
        compiler_params=_cparams(("arbitrary", "arbitrary")),
        name="moe",
    )(x1, h2, gate, w1_bf, w3_bf, w2_bf, ga, gfin)


def _rope_tables(S):
    t = jnp.arange(S)
    pos = jnp.stack([t // GRID_W, t % GRID_W], axis=-1).astype(F32)
    inv = ROPE_BASE ** (-jnp.arange(0, ROPE_AXIS_DIM, 2, dtype=F32) / ROPE_AXIS_DIM)
    ang = pos[:, :, None] * inv
    cos, sin = jnp.cos(ang), jnp.sin(ang)
    cos_h = jnp.concatenate([cos, cos], axis=-1).reshape(S, NA_HEAD_DIM)
    sin_h = jnp.concatenate([-sin, sin], axis=-1).reshape(S, NA_HEAD_DIM)
    return jnp.tile(cos_h, (1, NA_HEADS)), jnp.tile(sin_h, (1, NA_HEADS))


def kernel(x, c, ctx, c_ctx, w_mod, b_mod, norm_mix, norm_ffn, w_in, w_out, na_rpb, hg_lb, hg_norm,
           w_grp, b_grp, w_exp, b_exp, w1, w3, w2, norm_final):
    B, S, D = x.shape
    T = B * S
    assert w_mod.shape[0] == 1, "single-layer kernel"

    rows = -(-(B + 1) // SUBLANES) * SUBLANES
    cc = jnp.zeros((rows, D), F32).at[:B].set(c).at[B].set(c_ctx)
    mod = _modulation(cc, w_mod[0], b_mod[0])
    sh_a, sc_a, ga_a, sh_f, sc_f, ga_f = [m.reshape(B, 1, D) for m in jnp.split(mod[:B], 6, axis=-1)]
    csh_a, csc_a = [m.reshape(1, D) for m in jnp.split(mod[B], 6)[:2]]

    w_in_bf = w_in[0].astype(BF16)
    w_ctx_bf = jnp.concatenate([w_in_bf[:, i * SEG:(i + 1) * SEG] for i in (1, 2, 4, 5, 6)], axis=1)
    cos_t, sin_t = _rope_tables(S)
    g_mix = norm_mix[0].reshape(1, D)

    proj = _project(x, g_mix, sc_a, sh_a, w_in_bf, cos_t, sin_t, tm=512)
    cproj = _project_ctx(ctx, g_mix, csc_a, csh_a, w_ctx_bf)

    na_out = _neighbourhood_attention(proj, cproj, _na_bias_table(na_rpb[0]))

    lb = jnp.cumsum(jax.nn.softmax(hg_lb.astype(F32), axis=0), axis=0)[0]
    gain = jnp.tile(hg_norm[0].astype(F32), HG_HEADS).reshape(1, HG_WIDTH)
    hg_out = _hgrn(proj, cproj, lb, gain)

    wr = jnp.zeros((D, LANES), F32).at[:, :N_GROUPS].set(w_grp[0]).at[:, N_GROUPS:N_GROUPS + N_EXPERTS].set(w_exp[0])
    br = jnp.zeros((1, LANES), F32).at[0, :N_GROUPS].set(b_grp[0]).at[0, N_GROUPS:N_GROUPS + N_EXPERTS].set(b_exp[0])
    x1, h2, gate = _outproj(x.reshape(T, D), na_out.reshape(T, NA_WIDTH), hg_out.reshape(T, HG_WIDTH),
                            w_out[0].astype(BF16), ga_a, norm_ffn[0].reshape(1, D), sc_f, sh_f, wr, br,
                            tm=512, S=S)

    out = _moe(x1, h2, gate, w1[0].astype(BF16), w3[0].astype(BF16), w2[0].astype(BF16), ga_f,
               norm_final.reshape(1, D), tm=1024, S=S)
    return out.reshape(B, S, D)
```

```python
import functools

import numpy as np
import jax
import jax.numpy as jnp
from jax import lax
from jax.experimental import pallas as pl
from jax.experimental.pallas import tpu as pltpu

F32 = jnp.float32
BF16 = jnp.bfloat16
HIGHEST = lax.Precision.HIGHEST

D_MODEL = 1024
GRID_W = 64
NA_HEADS = 8
NA_HEAD_DIM = 64
NA_WIDTH = NA_HEADS * NA_HEAD_DIM
NA_KH = 8
NA_KW = 16
ROPE_AXIS_DIM = NA_HEAD_DIM // 2
ROPE_BASE = 10000.0
HG_HEADS = 4
HG_DK = 128
HG_WIDTH = HG_HEADS * HG_DK
HG_CHUNK = 64
SEG = 512
N_GROUPS = 4
EXPERTS_PER_GROUP = 4
N_EXPERTS = N_GROUPS * EXPERTS_PER_GROUP
D_EXPERT = 512
EPS = 1e-6
NEG = -1e30
LANES = 128
SUBLANES = 8
VMEM_LIMIT = 56 * 1024 * 1024

HG_LEVELS = (32, 16, 8, 4)
HG_DIAG = 4


def _cparams(sem):
    return pltpu.CompilerParams(dimension_semantics=sem, vmem_limit_bytes=VMEM_LIMIT)


def _sigmoid(x):
    return 1.0 / (1.0 + jnp.exp(-x))


def _silu(x):
    return x * _sigmoid(x)


def _mod_kernel(c_ref, w_ref, b_ref, o_ref):
    s = _silu(c_ref[...])
    o_ref[...] = jnp.dot(s, w_ref[...], precision=HIGHEST, preferred_element_type=F32) + b_ref[...]


def _modulation(cc, w_mod, b_mod):
    rows, d = cc.shape
    n = w_mod.shape[1]
    tn = 1024
    return pl.pallas_call(
        _mod_kernel,
        grid=(n // tn,),
        in_specs=[pl.BlockSpec((rows, d), lambda j: (0, 0)),
                  pl.BlockSpec((d, tn), lambda j: (0, j)),
                  pl.BlockSpec((1, tn), lambda j: (0, j))],
        out_specs=pl.BlockSpec((rows, tn), lambda j: (0, j)),
        out_shape=jax.ShapeDtypeStruct((rows, n), F32),
        compiler_params=_cparams(("arbitrary",)),
        name="mod",
    )(cc, w_mod, b_mod.reshape(1, n))


def _norm_mod(x, g, sc, sh):
    y = x * lax.rsqrt(jnp.mean(x * x, axis=-1, keepdims=True) + EPS)
    return (y * g) * (1.0 + sc) + sh


def _rope(a, cos, sin):
    lane = lax.broadcasted_iota(jnp.int32, a.shape, 1)
    first = (lane % ROPE_AXIS_DIM) < (ROPE_AXIS_DIM // 2)
    up = pltpu.roll(a, LANES - ROPE_AXIS_DIM // 2, axis=1)
    dn = pltpu.roll(a, ROPE_AXIS_DIM // 2, axis=1)
    return a * cos + jnp.where(first, up, dn) * sin


def _proj_kernel(x_ref, g_ref, sc_ref, sh_ref, w_ref, cos_ref, sin_ref, o_ref):
    h = _norm_mod(x_ref[0], g_ref[...], sc_ref[0], sh_ref[0]).astype(BF16)
    scale = NA_HEAD_DIM ** -0.5
    for j in range(8):
        acc = jnp.dot(h, w_ref[:, j * SEG:(j + 1) * SEG], preferred_element_type=F32)
        if j <= 1:
            rot = jnp.concatenate(
                [_rope(acc[:, p * LANES:(p + 1) * LANES], cos_ref[:, p * LANES:(p + 1) * LANES],
                       sin_ref[:, p * LANES:(p + 1) * LANES]) for p in range(SEG // LANES)], axis=1)
            if j == 0:
                o_ref[0, :, 0:SEG] = (acc * scale).astype(BF16)
                o_ref[0, :, SEG:2 * SEG] = (rot * scale).astype(BF16)
            else:
                o_ref[0, :, 2 * SEG:3 * SEG] = rot.astype(BF16)
        else:
            o_ref[0, :, (j + 1) * SEG:(j + 2) * SEG] = acc.astype(BF16)


def _project(x, g, sc, sh, w_bf, cos_t, sin_t, tm):
    B, S, D = x.shape
    return pl.pallas_call(
        _proj_kernel,
        grid=(S // tm, B),
        in_specs=[pl.BlockSpec((1, tm, D), lambda s, b: (b, s, 0)),
                  pl.BlockSpec((1, D), lambda s, b: (0, 0)),
                  pl.BlockSpec((1, 1, D), lambda s, b: (b, 0, 0)),
                  pl.BlockSpec((1, 1, D), lambda s, b: (b, 0, 0)),
                  pl.BlockSpec((D, 8 * SEG), lambda s, b: (0, 0)),
                  pl.BlockSpec((tm, SEG), lambda s, b: (s, 0)),
                  pl.BlockSpec((tm, SEG), lambda s, b: (s, 0))],
        out_specs=pl.BlockSpec((1, tm, 9 * SEG), lambda s, b: (b, s, 0)),
        out_shape=jax.ShapeDtypeStruct((B, S, 9 * SEG), BF16),
        compiler_params=_cparams(("arbitrary", "arbitrary")),
        name="proj",
    )(x, g, sc, sh, w_bf, cos_t, sin_t)


def _ctxproj_kernel(x_ref, g_ref, sc_ref, sh_ref, w_ref, o_ref):
    h = _norm_mod(x_ref[0], g_ref[...], sc_ref[...], sh_ref[...]).astype(BF16)
    for j in range(5):
        acc = jnp.dot(h, w_ref[:, j * SEG:(j + 1) * SEG], preferred_element_type=F32)
        o_ref[0, :, j * SEG:(j + 1) * SEG] = acc.astype(BF16)


def _project_ctx(ctx, g, sc, sh, w_bf):
    B, L, D = ctx.shape
    return pl.pallas_call(
        _ctxproj_kernel,
        grid=(B,),
        in_specs=[pl.BlockSpec((1, L, D), lambda b: (b, 0, 0)),
                  pl.BlockSpec((1, D), lambda b: (0, 0)),
                  pl.BlockSpec((1, D), lambda b: (0, 0)),
                  pl.BlockSpec((1, D), lambda b: (0, 0)),
                  pl.BlockSpec((D, 5 * SEG), lambda b: (0, 0))],
        out_specs=pl.BlockSpec((1, L, 5 * SEG), lambda b: (b, 0, 0)),
        out_shape=jax.ShapeDtypeStruct((B, L, 5 * SEG), BF16),
        compiler_params=_cparams(("arbitrary",)),
        name="ctxproj",
    )(ctx, g, sc, sh, w_bf)


def _na_kernel(qraw_ref, qrot_ref, k_ref, v_ref, ck_ref, cv_ref, bias_ref, o_ref, vt_ref, cvt_ref, *, rows):
    r = pl.program_id(1)

    @pl.when(r == 0)
    def _():
        def body(i, carry):
            blk = v_ref[0, pl.ds(pl.multiple_of(i * GRID_W, GRID_W), GRID_W), :]
            vt_ref[i] = blk.astype(F32).T.astype(BF16)
            return carry
        lax.fori_loop(0, rows, body, 0)
        cvt_ref[...] = cv_ref[0].astype(F32).T.astype(BF16)

    rs = jnp.clip(r - NA_KH // 2, 0, rows - NA_KH)
    start = pl.multiple_of(rs * GRID_W, GRID_W)
    nk = NA_KH * GRID_W
    lane = lax.broadcasted_iota(jnp.int32, (GRID_W, LANES), 1)
    sel0 = lane < NA_HEAD_DIM
    nt = (((1,), (1,)), ((), ()))
    for p in range(NA_WIDTH // LANES):
        cols = slice(p * LANES, (p + 1) * LANES)
        qr = qrot_ref[0, :, cols]
        qw = qraw_ref[0, :, cols]
        zero = jnp.zeros_like(qr)
        qs_rot = jnp.concatenate([jnp.where(sel0, qr, zero), jnp.where(sel0, zero, qr)], axis=0)
        qs_raw = jnp.concatenate([jnp.where(sel0, qw, zero), jnp.where(sel0, zero, qw)], axis=0)
        s_loc = lax.dot_general(k_ref[0, pl.ds(start, nk), cols], qs_rot, nt,
                                preferred_element_type=F32) + bias_ref[0, p]
        s_ctx = lax.dot_general(ck_ref[0, :, cols], qs_raw, nt, preferred_element_type=F32)
        m = jnp.maximum(jnp.max(s_loc, axis=0, keepdims=True), jnp.max(s_ctx, axis=0, keepdims=True))
        p_loc = jnp.exp(s_loc - m)
        p_ctx = jnp.exp(s_ctx - m)
        l = jnp.sum(p_loc, axis=0, keepdims=True) + jnp.sum(p_ctx, axis=0, keepdims=True)
        p_loc = p_loc.astype(BF16)
        ot = jnp.dot(cvt_ref[cols, :], p_ctx.astype(BF16), preferred_element_type=F32)
        for i in range(NA_KH):
            ot = ot + jnp.dot(vt_ref[rs + i, cols, :], p_loc[i * GRID_W:(i + 1) * GRID_W, :],
                              preferred_element_type=F32)
        res = (ot * (1.0 / l)).T
        o_ref[0, :, cols] = jnp.where(sel0, res[0:GRID_W], res[GRID_W:2 * GRID_W]).astype(BF16)


def _na_bias_table(rpb):
    qc = np.arange(GRID_W)[None, :]
    kc = np.arange(GRID_W)[:, None]
    cs = np.clip(qc - NA_KW // 2, 0, GRID_W - NA_KW)
    valid = (kc >= cs) & (kc < cs + NA_KW)
    col_off = np.clip(kc - qc + NA_KW - 1, 0, 2 * NA_KW - 2)
    ro = np.arange(NA_KH)[:, None] + np.arange(NA_KH)[None, :]
    t = rpb.astype(F32)[:, ro][:, :, :, col_off]
    t = jnp.where(jnp.asarray(valid)[None, None, None], t, NEG)
    n_pairs = NA_HEADS // 2
    t = t.reshape(n_pairs, 2, NA_KH, NA_KH, GRID_W, GRID_W)
    t = t.transpose(2, 0, 3, 4, 1, 5)
    return t.reshape(NA_KH, n_pairs, NA_KH * GRID_W, 2 * GRID_W)


def _neighbourhood_attention(proj, cproj, bias_tbl):
    B, S, _ = proj.shape
    L = cproj.shape[1]
    rows = S // GRID_W
    nk = NA_KH * GRID_W

    def bias_map(b, r):
        return (jnp.clip(r - NA_KH // 2, 0, rows - NA_KH) - r + NA_KH - 1, 0, 0, 0)

    return pl.pallas_call(
        functools.partial(_na_kernel, rows=rows),
        grid=(B, rows),
        in_specs=[pl.BlockSpec((1, GRID_W, SEG), lambda b, r: (b, r, 0)),
                  pl.BlockSpec((1, GRID_W, SEG), lambda b, r: (b, r, 1)),
                  pl.BlockSpec((1, S, SEG), lambda b, r: (b, 0, 2)),
                  pl.BlockSpec((1, S, SEG), lambda b, r: (b, 0, 3)),
                  pl.BlockSpec((1, L, SEG), lambda b, r: (b, 0, 0)),
                  pl.BlockSpec((1, L, SEG), lambda b, r: (b, 0, 1)),
                  pl.BlockSpec((1, NA_HEADS // 2, nk, LANES), bias_map)],
        out_specs=pl.BlockSpec((1, GRID_W, SEG), lambda b, r: (b, r, 0)),
        out_shape=jax.ShapeDtypeStruct((B, S, NA_WIDTH), BF16),
        scratch_shapes=[pltpu.VMEM((rows, NA_WIDTH, GRID_W), BF16),
                        pltpu.VMEM((NA_WIDTH, L), BF16)],
        compiler_params=_cparams(("arbitrary", "arbitrary")),
        name="na",
    )(proj, proj, proj, proj, cproj, cproj, bias_tbl)


def _split3(g):
    g1 = g.astype(BF16)
    r1 = g - g1.astype(F32)
    g2 = r1.astype(BF16)
    g3 = (r1 - g2.astype(F32)).astype(BF16)
    return g1, g2, g3


def _cumdot(c_bf, g):
    g1, g2, g3 = _split3(g)
    d = lambda a: jnp.dot(c_bf, a, preferred_element_type=F32)
    return (d(g3) + d(g2)) + d(g1)


def _forget(pre, lb):
    f = lb + (1.0 - lb) * _sigmoid(pre.astype(F32))
    return 1.0 - f, jnp.log(f)


def _hg_masks():
    c = HG_CHUNK
    t = np.arange(c)[:, None]
    s = np.arange(c)[None, :]
    lm = []
    for m in HG_LEVELS:
        lm.append(((t // (2 * m)) == (s // (2 * m))) & ((t % (2 * m)) >= m) & ((s % (2 * m)) < m))
    om = [((t - s) == d) & ((t % HG_DIAG) >= d) for d in range(HG_DIAG)]
    lm = np.stack(lm).astype(np.float32)
    om = np.stack(om).astype(np.float32)
    return np.concatenate([lm, om, lm.transpose(0, 2, 1), om.transpose(0, 2, 1)], axis=0)


def _hg_chunk_dir(q, v, pre, lb, cmat, masks_ref, ones_bf, st, *, fwd):
    c = HG_CHUNK
    nl = len(HG_LEVELS)
    k, g = _forget(pre, lb)
    b = _cumdot(cmat, g)
    bend = b[c - 1:c, :] if fwd else b[0:1, :]
    qe = (q * jnp.exp(b)).astype(BF16)
    ke = (k * jnp.exp(bend - b)).astype(BF16)
    nt = (((1,), (1,)), ((), ()))
    moff = 0 if fwd else nl + HG_DIAG
    a = None
    for li, m in enumerate(HG_LEVELS):
        pieces = []
        for j in range(c // SUBLANES):
            blk = (SUBLANES * j) // (2 * m)
            row = 2 * m * blk + (m - 1 if fwd else m)
            pieces.append(jnp.broadcast_to(b[row:row + 1, :], (SUBLANES, b.shape[1])))
        ref = jnp.concatenate(pieces, axis=0)
        e = jnp.exp(-jnp.abs(b - ref))
        r = lax.dot_general((q * e).astype(BF16), (k * e).astype(BF16), nt, preferred_element_type=F32)
        r = r * masks_ref[moff + li]
        a = r if a is None else a + r
    for d in range(HG_DIAG):
        if d == 0:
            x = q * k
        else:
            sh = d if fwd else c - d
            kd = pltpu.roll(k, sh, axis=0)
            w = jnp.minimum(b - pltpu.roll(b, sh, axis=0), 0.0)
            x = q * kd * jnp.exp(w)
        r = jnp.dot(x.astype(BF16), ones_bf, preferred_element_type=F32)
        a = a + r * masks_ref[moff + nl + d]
    o = (lax.dot_general(qe, st.astype(BF16), nt, preferred_element_type=F32)
         + jnp.dot(a.astype(BF16), v, preferred_element_type=F32))
    upd = lax.dot_general(v, ke, (((0,), (0,)), ((), ())), preferred_element_type=F32)
    st_new = jnp.exp(bend) * st + upd
    return o, st_new


def _hgrn_kernel(hq_ref, hi_ref, hff_ref, hfb_ref, hg_ref, ci_ref, cff_ref, cfb_ref, lb_ref, gain_ref,
                 lincl_ref, uincl_ref, ustrict_ref, lstrict_ref, masks_ref, o_ref, acc_ref, st_ref, *, n_chunks):
    c = HG_CHUNK
    ones_bf = jnp.ones((HG_DK, c), BF16)
    tn = (((0,), (0,)), ((), ()))

    for h in range(HG_HEADS):
        hs = slice(h * HG_DK, (h + 1) * HG_DK)
        ci = ci_ref[0, :, hs]
        for di, (pre_ref, cm_ref) in enumerate(((cff_ref, ustrict_ref), (cfb_ref, lstrict_ref))):
            k, g = _forget(pre_ref[0, :, hs], lb_ref[di:di + 1, hs])
            ke = (k * jnp.exp(_cumdot(cm_ref[...], g))).astype(BF16)
            st_ref[di, h] = lax.dot_general(ci, ke, tn, preferred_element_type=F32)

    def step(i, second):
        cf = i
        cb = n_chunks - 1 - i
        rf = pl.ds(pl.multiple_of(cf * c, c), c)
        rb = pl.ds(pl.multiple_of(cb * c, c), c)
        for h in range(HG_HEADS):
            hs = slice(h * HG_DK, (h + 1) * HG_DK)
            for di, rr in enumerate((rf, rb)):
                fwd = di == 0
                q = _silu(hq_ref[0, rr, hs].astype(F32))
                v = hi_ref[0, rr, hs]
                pre = (hff_ref if fwd else hfb_ref)[0, rr, hs]
                cmat = (lincl_ref if fwd else uincl_ref)[...]
                o, st_new = _hg_chunk_dir(q, v, pre, lb_ref[di:di + 1, hs], cmat, masks_ref, ones_bf,
                                          st_ref[di, h], fwd=fwd)
                st_ref[di, h] = st_new
                if not second:
                    acc_ref[rr, hs] = o
                else:
                    ot = acc_ref[rr, hs] + o
                    y = ot * lax.rsqrt(jnp.mean(ot * ot, axis=-1, keepdims=True) + EPS) * gain_ref[:, hs]
                    o_ref[0, rr, hs] = (y * _silu(hg_ref[0, rr, hs].astype(F32))).astype(BF16)

    def first_half(i, carry):
        step(i, False)
        return carry

    def second_half(i, carry):
        step(i, True)
        return carry

    lax.fori_loop(0, n_chunks // 2, first_half, 0)
    lax.fori_loop(n_chunks // 2, n_chunks, second_half, 0)


def _hgrn(proj, cproj, lb, gain):
    B, S, _ = proj.shape
    L = cproj.shape[1]
    c = HG_CHUNK
    tri = np.tril(np.ones((c, c), np.float32))
    tri_l = np.tril(np.ones((L, L), np.float32), -1)
    consts = [jnp.asarray(tri, BF16), jnp.asarray(tri.T, BF16),
              jnp.asarray(tri_l.T, BF16), jnp.asarray(tri_l, BF16), jnp.asarray(_hg_masks())]
    n_masks = 2 * (len(HG_LEVELS) + HG_DIAG)
    seq = lambda j: pl.BlockSpec((1, S, SEG), lambda b: (b, 0, j))
    cseq = lambda j: pl.BlockSpec((1, L, SEG), lambda b: (b, 0, j))
    full2 = lambda n, m: pl.BlockSpec((n, m), lambda b: (0, 0))
    return pl.pallas_call(
        functools.partial(_hgrn_kernel, n_chunks=S // c),
        grid=(B,),
        in_specs=[seq(4), seq(5), seq(6), seq(7), seq(8), cseq(2), cseq(3), cseq(4),
                  full2(2, HG_WIDTH), full2(1, HG_WIDTH),
                  full2(c, c), full2(c, c), full2(L, L), full2(L, L),
                  pl.BlockSpec((n_masks, c, c), lambda b: (0, 0, 0))],
        out_specs=pl.BlockSpec((1, S, HG_WIDTH), lambda b: (b, 0, 0)),
        out_shape=jax.ShapeDtypeStruct((B, S, HG_WIDTH), BF16),
        scratch_shapes=[pltpu.VMEM((S, HG_WIDTH), F32),
                        pltpu.VMEM((2, HG_HEADS, HG_DK, HG_DK), F32)],
        compiler_params=_cparams(("arbitrary",)),
        name="hgrn",
    )(proj, proj, proj, proj, proj, cproj, cproj, cproj, lb, gain, *consts)


def _route(logits):
    lane = lax.broadcasted_iota(jnp.int32, logits.shape, 1).astype(F32)
    big = float(LANES)
    is_g = lane < N_GROUPS
    gl = jnp.where(is_g, logits, -jnp.inf)
    gmax = jnp.max(gl, axis=-1, keepdims=True)
    g_sel = jnp.min(jnp.where(gl == gmax, lane, big), axis=-1, keepdims=True)
    g_w = 1.0 / jnp.sum(jnp.where(is_g, jnp.exp(gl - gmax), 0.0), axis=-1, keepdims=True)
    el_lane = lane - N_GROUPS
    in_grp = (el_lane >= g_sel * EXPERTS_PER_GROUP) & (el_lane < (g_sel + 1) * EXPERTS_PER_GROUP)
    e1 = jnp.where(in_grp, logits, -jnp.inf)
    v1 = jnp.max(e1, axis=-1, keepdims=True)
    i1 = jnp.min(jnp.where(e1 == v1, lane, big), axis=-1, keepdims=True)
    e2 = jnp.where(lane == i1, -jnp.inf, e1)
    v2 = jnp.max(e2, axis=-1, keepdims=True)
    i2 = jnp.min(jnp.where(e2 == v2, lane, big), axis=-1, keepdims=True)
    t = jnp.exp(v2 - v1)
    w1 = g_w / (1.0 + t)
    w2 = g_w * t / (1.0 + t)
    onehot = jnp.where(lane == g_sel, 1.0, 0.0)
    first = g_sel * EXPERTS_PER_GROUP + N_GROUPS
    gates = jnp.where(lane == i1 - first, w1, jnp.where(lane == i2 - first, w2, 0.0))
    return onehot, gates


def _outproj_kernel(x_ref, na_ref, hg_ref, w_ref, ga_ref, g_ref, sc_ref, sh_ref, wr_ref, br_ref,
                    x1_ref, hx_ref, oh_ref, cnt_ref):
    d = x_ref.shape[1]
    mix = (jnp.dot(na_ref[...], w_ref[0:NA_WIDTH, :], preferred_element_type=F32)
           + jnp.dot(hg_ref[...], w_ref[NA_WIDTH:, :], preferred_element_type=F32))
    x1 = x_ref[...] + ga_ref[0] * mix
    x1_ref[...] = x1
    h2 = _norm_mod(x1, g_ref[...], sc_ref[0], sh_ref[0])
    logits = jnp.dot(h2, wr_ref[...], precision=HIGHEST, preferred_element_type=F32) + br_ref[...]
    onehot, gates = _route(logits)
    hx_ref[:, 0:d] = h2.astype(BF16)
    for j, piece in enumerate(_split3(gates)):
        hx_ref[:, d + j * LANES:d + (j + 1) * LANES] = piece
    oh_ref[...] = onehot
    cnt_ref[0] = jnp.broadcast_to(jnp.sum(onehot, axis=0, keepdims=True), (SUBLANES, LANES))


def _outproj(x2d, na2d, hg2d, w_out_bf, ga, g, sc, sh, wr, br, S):
    T, D = x2d.shape
    tm = MOE_TM
    per = S // tm
    tok = lambda w: pl.BlockSpec((tm, w), lambda i: (i, 0))
    bat = pl.BlockSpec((1, 1, D), lambda i: (i // per, 0, 0))
    return pl.pallas_call(
        _outproj_kernel,
        grid=(T // tm,),
        in_specs=[tok(D), tok(NA_WIDTH), tok(HG_WIDTH),
                  pl.BlockSpec((NA_WIDTH + HG_WIDTH, D), lambda i: (0, 0)),
                  bat, pl.BlockSpec((1, D), lambda i: (0, 0)), bat, bat,
                  pl.BlockSpec((D, LANES), lambda i: (0, 0)),
                  pl.BlockSpec((1, LANES), lambda i: (0, 0))],
        out_specs=[tok(D), tok(MOE_XW), tok(LANES),
                   pl.BlockSpec((1, SUBLANES, LANES), lambda i: (i, 0, 0))],
        out_shape=[jax.ShapeDtypeStruct((T, D), F32), jax.ShapeDtypeStruct((T, MOE_XW), BF16),
                   jax.ShapeDtypeStruct((T, LANES), F32),
                   jax.ShapeDtypeStruct((T // tm, SUBLANES, LANES), F32)],
        compiler_params=_cparams(("arbitrary",)),
        name="outproj",
    )(x2d, na2d, hg2d, w_out_bf, ga, g, sc, sh, wr, br)


MOE_TM = 512
MOE_ALIGN = 16
MOE_LOC = 640
MOE_XW = D_MODEL + 3 * LANES
MOE_TE = 512
MOE_BITS = tuple(1 << b for b in range(9, 3, -1))


def _moe_steps(T):
    worst = T + (T // MOE_TM) * N_GROUPS * (MOE_ALIGN - 1)
    return -(-worst // MOE_TE) + N_GROUPS


def _local_positions(oh_ref, lstrict_ref, cnt_ref, k):
    onehot = oh_ref[...]
    ranks = jnp.dot(lstrict_ref[...], onehot.astype(BF16), preferred_element_type=F32)
    lane = lax.broadcasted_iota(jnp.int32, (1, LANES), 1)
    base = jnp.zeros((1, LANES), F32)
    o = jnp.int32(0)
    for g in range(N_GROUPS):
        base = jnp.where(lane == g, o.astype(F32), base)
        o = o + cnt_ref[k * N_GROUPS + g]
    return jnp.sum(onehot * (ranks + base), axis=-1, keepdims=True)


def _sort_matrix(lpos):
    col = lax.broadcasted_iota(jnp.int32, (MOE_TM, MOE_LOC), 1).astype(F32)
    return jnp.where(lpos == col, 1.0, 0.0).astype(BF16)


def _block_copies(n_rows, src_row, dst_row, bits, make_copy, action):
    for bit in bits:
        part = n_rows & (-2 * bit)

        @pl.when((n_rows & bit) != 0)
        def _(part=part, bit=bit):
            action(make_copy(pl.multiple_of(src_row + part, MOE_ALIGN), pl.multiple_of(dst_row + part, MOE_ALIGN), bit))


def _run_copies(off_ref, cnt_ref, k, make_copy, action):
    o = jnp.int32(0)
    for g in range(N_GROUPS):
        c = cnt_ref[k * N_GROUPS + g]
        _block_copies(c, o, off_ref[k * N_GROUPS + g], MOE_BITS, make_copy, action)
        o = o + c


def _dispatch_kernel(off_ref, cnt_ref, tail_ref, hx_ref, oh_ref, lstrict_ref, hs_ref, buf_ref, zero_ref, sem,
                     *, n_tiles, n_steps):
    k = pl.program_id(0)
    slot = k % 2

    def copies(kk, sl, action):
        def make(src_row, dst_row, n):
            return pltpu.make_async_copy(buf_ref.at[sl, pl.ds(src_row, n)], hs_ref.at[pl.ds(dst_row, n)], sem.at[sl])
        _run_copies(off_ref, cnt_ref, kk, make, action)

    @pl.when(k >= 2)
    def _():
        copies(k - 2, slot, lambda cp: cp.wait())

    pt = _sort_matrix(_local_positions(oh_ref, lstrict_ref, cnt_ref, k))
    srt = lax.dot_general(pt, hx_ref[...], (((0,), (0,)), ((), ())), preferred_element_type=F32)
    buf_ref[slot] = srt.astype(BF16)
    copies(k, slot, lambda cp: cp.start())

    @pl.when(k == n_tiles - 1)
    def _():
        zero_ref[...] = jnp.zeros_like(zero_ref)
        n_used = tail_ref[2 * N_GROUPS]

        def zero_copy(src_row, dst_row, n):
            return pltpu.make_async_copy(zero_ref.at[pl.ds(src_row, n)], hs_ref.at[pl.ds(dst_row, n)], sem.at[2])

        def tile_copy(i):
            return zero_copy(0, pl.multiple_of(i * MOE_TE, MOE_TE), MOE_TE)

        def fills(action):
            for g in range(N_GROUPS):
                _block_copies(tail_ref[N_GROUPS + g], jnp.int32(0), tail_ref[g], MOE_BITS[1:], zero_copy, action)

        fills(lambda cp: cp.start())
        lax.fori_loop(n_used, n_steps, lambda i, c: (tile_copy(i).start(), c)[1], 0)
        if n_tiles >= 2:
            copies(k - 1, 1 - slot, lambda cp: cp.wait())
        copies(k, slot, lambda cp: cp.wait())
        fills(lambda cp: cp.wait())
        lax.fori_loop(n_used, n_steps, lambda i, c: (tile_copy(i).wait(), c)[1], 0)


def _dispatch(off, cnt, tail, hx, onehot, lstrict, n_steps):
    T = hx.shape[0]
    n_tiles = T // MOE_TM
    return pl.pallas_call(
        functools.partial(_dispatch_kernel, n_tiles=n_tiles, n_steps=n_steps),
        grid_spec=pltpu.PrefetchScalarGridSpec(
            num_scalar_prefetch=3,
            grid=(n_tiles,),
            in_specs=[pl.BlockSpec((MOE_TM, MOE_XW), lambda k, *_: (k, 0)),
                      pl.BlockSpec((MOE_TM, LANES), lambda k, *_: (k, 0)),
                      pl.BlockSpec((MOE_TM, MOE_TM), lambda k, *_: (0, 0))],
            out_specs=pl.BlockSpec(memory_space=pl.ANY),
            scratch_shapes=[pltpu.VMEM((2, MOE_LOC, MOE_XW), BF16),
                            pltpu.VMEM((MOE_TE, MOE_XW), BF16),
                            pltpu.SemaphoreType.DMA((3,))]),
        out_shape=jax.ShapeDtypeStruct((n_steps * MOE_TE, MOE_XW), BF16),
        compiler_params=_cparams(("arbitrary",)),
        name="dispatch",
    )(off, cnt, tail, hx, onehot, lstrict)


def _experts_kernel(grp_ref, used_ref, hs_ref, w13_ref, w2_ref, ys_ref):
    i = pl.program_id(0)
    d = ys_ref.shape[1]
    ne = EXPERTS_PER_GROUP

    @pl.when(i < used_ref[0])
    def _():
        t = hs_ref[:, 0:d]
        gates = ((hs_ref[:, d + 2 * LANES:d + 3 * LANES].astype(F32) + hs_ref[:, d + LANES:d + 2 * LANES].astype(F32))
                 + hs_ref[:, d:d + LANES].astype(F32))
        lane = lax.broadcasted_iota(jnp.int32, gates.shape, 1)
        acc = None
        for j in range(ne):
            a = jnp.dot(t, w13_ref[0, :, j * D_EXPERT:(j + 1) * D_EXPERT], preferred_element_type=F32)
            b = jnp.dot(t, w13_ref[0, :, (ne + j) * D_EXPERT:(ne + j + 1) * D_EXPERT], preferred_element_type=F32)
            gj = jnp.sum(jnp.where(lane == j, gates, 0.0), axis=-1, keepdims=True)
            y = gj * jnp.dot((_silu(a) * b).astype(BF16), w2_ref[0, j * D_EXPERT:(j + 1) * D_EXPERT, :],
                             preferred_element_type=F32)
            acc = y if acc is None else acc + y
        ys_ref[...] = acc.astype(BF16)

    @pl.when(i >= used_ref[0])
    def _():
        ys_ref[...] = jnp.zeros_like(ys_ref)


def _experts(grp, used, hs, w13_bf, w2_bf):
    rows, _ = hs.shape
    D = w2_bf.shape[2]
    return pl.pallas_call(
        _experts_kernel,
        grid_spec=pltpu.PrefetchScalarGridSpec(
            num_scalar_prefetch=2,
            grid=(rows // MOE_TE,),
            in_specs=[pl.BlockSpec((MOE_TE, MOE_XW), lambda i, grp, used: (i, 0)),
                      pl.BlockSpec((1, D, 2 * EXPERTS_PER_GROUP * D_EXPERT), lambda i, grp, used: (grp[i], 0, 0)),
                      pl.BlockSpec((1, EXPERTS_PER_GROUP * D_EXPERT, D), lambda i, grp, used: (grp[i], 0, 0))],
            out_specs=pl.BlockSpec((MOE_TE, D), lambda i, grp, used: (i, 0))),
        out_shape=jax.ShapeDtypeStruct((rows, D), BF16),
        compiler_params=_cparams(("arbitrary",)),
        name="experts",
    )(grp, used, hs, w13_bf, w2_bf)


def _combine_kernel(off_ref, cnt_ref, x1_ref, oh_ref, lstrict_ref, ga_ref, gf_ref, ys_ref, o_ref, buf_ref, sem,
                    *, n_tiles):
    k = pl.program_id(0)
    slot = k % 2

    def copies(kk, sl, action):
        def make(loc_row, seg_row, n):
            return pltpu.make_async_copy(ys_ref.at[pl.ds(seg_row, n)], buf_ref.at[sl, pl.ds(loc_row, n)], sem.at[sl])
        _run_copies(off_ref, cnt_ref, kk, make, action)

    @pl.when(k == 0)
    def _():
        buf_ref[...] = jnp.zeros_like(buf_ref)
        copies(k, slot, lambda cp: cp.start())

    @pl.when(k + 1 < n_tiles)
    def _():
        copies(k + 1, 1 - slot, lambda cp: cp.start())

    pt = _sort_matrix(_local_positions(oh_ref, lstrict_ref, cnt_ref, k))
    copies(k, slot, lambda cp: cp.wait())
    y = jnp.dot(pt, buf_ref[slot], preferred_element_type=F32)
    x2 = x1_ref[...] + ga_ref[0] * y
    o_ref[...] = x2 * lax.rsqrt(jnp.mean(x2 * x2, axis=-1, keepdims=True) + EPS) * gf_ref[...]


def _combine(off, cnt, x1, onehot, lstrict, ga, gfin, ys, S):
    T, D = x1.shape
    n_tiles = T // MOE_TM
    per = S // MOE_TM
    return pl.pallas_call(
        functools.partial(_combine_kernel, n_tiles=n_tiles),
        grid_spec=pltpu.PrefetchScalarGridSpec(
            num_scalar_prefetch=2,
            grid=(n_tiles,),
            in_specs=[pl.BlockSpec((MOE_TM, D), lambda k, off, cnt: (k, 0)),
                      pl.BlockSpec((MOE_TM, LANES), lambda k, off, cnt: (k, 0)),
                      pl.BlockSpec((MOE_TM, MOE_TM), lambda k, off, cnt: (0, 0)),
                      pl.BlockSpec((1, 1, D), lambda k, off, cnt: (k // per, 0, 0)),
                      pl.BlockSpec((1, D), lambda k, off, cnt: (0, 0)),
                      pl.BlockSpec(memory_space=pl.ANY)],
            out_specs=pl.BlockSpec((MOE_TM, D), lambda k, off, cnt: (k, 0)),
            scratch_shapes=[pltpu.VMEM((2, MOE_LOC, D), BF16),
                            pltpu.SemaphoreType.DMA((2,))]),
        out_shape=jax.ShapeDtypeStruct((T, D), F32),
        compiler_params=_cparams(("arbitrary",)),
        name="combine",
    )(off, cnt, x1, onehot, lstrict, ga, gfin, ys)


def _moe_schedule(cnt_tiles, n_steps):
    cnt = ((cnt_tiles + (MOE_ALIGN - 1)) // MOE_ALIGN) * MOE_ALIGN
    ends = jnp.cumsum(cnt, axis=0)
    total = ends[-1]
    ntile = (total + MOE_TE - 1) // MOE_TE
    cum = jnp.cumsum(ntile)
    base = (cum - ntile) * MOE_TE
    off = base[None, :] + ends - cnt
    i = jnp.arange(n_steps, dtype=jnp.int32)
    grp = jnp.minimum(jnp.sum(i[:, None] >= cum[None, :], axis=1), N_GROUPS - 1)
    grp = jnp.where(i < cum[-1], grp, grp[jnp.maximum(cum[-1] - 1, 0)])
    tail = jnp.concatenate([base + total, ntile * MOE_TE - total, cum[-1:]])
    as_i32 = lambda a: a.reshape(-1).astype(jnp.int32)
    return as_i32(off), as_i32(cnt), as_i32(tail), as_i32(grp), as_i32(cum[-1:])


def _rope_tables(S):
    t = jnp.arange(S)
    pos = jnp.stack([t // GRID_W, t % GRID_W], axis=-1).astype(F32)
    inv = ROPE_BASE ** (-jnp.arange(0, ROPE_AXIS_DIM, 2, dtype=F32) / ROPE_AXIS_DIM)
    ang = pos[:, :, None] * inv
    cos, sin = jnp.cos(ang), jnp.sin(ang)
    cos_h = jnp.concatenate([cos, cos], axis=-1).reshape(S, NA_HEAD_DIM)
    sin_h = jnp.concatenate([-sin, sin], axis=-1).reshape(S, NA_HEAD_DIM)
    return jnp.tile(cos_h, (1, NA_HEADS)), jnp.tile(sin_h, (1, NA_HEADS))


def kernel(x, c, ctx, c_ctx, w_mod, b_mod, norm_mix, norm_ffn, w_in, w_out, na_rpb, hg_lb, hg_norm,
           w_grp, b_grp, w_exp, b_exp, w1, w3, w2, norm_final):
    B, S, D = x.shape
    T = B * S
    assert w_mod.shape[0] == 1, "single-layer kernel"

    rows = -(-(B + 1) // SUBLANES) * SUBLANES
    cc = jnp.zeros((rows, D), F32).at[:B].set(c).at[B].set(c_ctx)
    mod = _modulation(cc, w_mod[0], b_mod[0])
    sh_a, sc_a, ga_a, sh_f, sc_f, ga_f = [m.reshape(B, 1, D) for m in jnp.split(mod[:B], 6, axis=-1)]
    csh_a, csc_a = [m.reshape(1, D) for m in jnp.split(mod[B], 6)[:2]]

    w_in_bf = w_in[0].astype(BF16)
    w_ctx_bf = jnp.concatenate([w_in_bf[:, i * SEG:(i + 1) * SEG] for i in (1, 2, 4, 5, 6)], axis=1)
    cos_t, sin_t = _rope_tables(S)
    g_mix = norm_mix[0].reshape(1, D)

    proj = _project(x, g_mix, sc_a, sh_a, w_in_bf, cos_t, sin_t, tm=512)
    cproj = _project_ctx(ctx, g_mix, csc_a, csh_a, w_ctx_bf)

    na_out = _neighbourhood_attention(proj, cproj, _na_bias_table(na_rpb[0]))

    lb = jnp.cumsum(jax.nn.softmax(hg_lb.astype(F32), axis=0), axis=0)[0]
    gain = jnp.tile(hg_norm[0].astype(F32), HG_HEADS).reshape(1, HG_WIDTH)
    hg_out = _hgrn(proj, cproj, lb, gain)

    wr = jnp.zeros((D, LANES), F32).at[:, :N_GROUPS].set(w_grp[0]).at[:, N_GROUPS:N_GROUPS + N_EXPERTS].set(w_exp[0])
    br = jnp.zeros((1, LANES), F32).at[0, :N_GROUPS].set(b_grp[0]).at[0, N_GROUPS:N_GROUPS + N_EXPERTS].set(b_exp[0])
    x1, hx, onehot, cnt_tiles = _outproj(x.reshape(T, D), na_out.reshape(T, NA_WIDTH), hg_out.reshape(T, HG_WIDTH),
                                         w_out[0].astype(BF16), ga_a, norm_ffn[0].reshape(1, D), sc_f, sh_f, wr, br,
                                         S=S)

    n_steps = _moe_steps(T)
    off, cnt, tail, grp, used = _moe_schedule(cnt_tiles[:, 0, :N_GROUPS].astype(jnp.int32), n_steps)
    lstrict = jnp.asarray(np.tril(np.ones((MOE_TM, MOE_TM), np.float32), -1), BF16)
    by_group = lambda w: w.reshape(N_GROUPS, EXPERTS_PER_GROUP, *w.shape[1:])
    w13 = jnp.concatenate([by_group(w1[0]), by_group(w3[0])], axis=1).astype(BF16)
    w13 = w13.transpose(0, 2, 1, 3).reshape(N_GROUPS, D, 2 * EXPERTS_PER_GROUP * D_EXPERT)
    w2g = w2[0].astype(BF16).reshape(N_GROUPS, EXPERTS_PER_GROUP * D_EXPERT, D)
    hs = _dispatch(off, cnt, tail, hx, onehot, lstrict, n_steps)
    ys = _experts(grp, used, hs, w13, w2g)
    out = _combine(off, cnt, x1, onehot, lstrict, ga_f, norm_final.reshape(1, D), ys, S)
    return out.reshape(B, S, D)
```

```python
import functools

import numpy as np
import jax
import jax.numpy as jnp
from jax import lax
from jax.experimental import pallas as pl
from jax.experimental.pallas import tpu as pltpu

F32 = jnp.float32
BF16 = jnp.bfloat16
HIGHEST = lax.Precision.HIGHEST

D_MODEL = 1024
GRID_W = 64
NA_HEADS = 8
NA_HEAD_DIM = 64
NA_WIDTH = NA_HEADS * NA_HEAD_DIM
NA_KH = 8
NA_KW = 16
ROPE_AXIS_DIM = NA_HEAD_DIM // 2
ROPE_BASE = 10000.0
HG_HEADS = 4
HG_DK = 128
HG_WIDTH = HG_HEADS * HG_DK
HG_CHUNK = 64
SEG = 512
N_GROUPS = 4
EXPERTS_PER_GROUP = 4
N_EXPERTS = N_GROUPS * EXPERTS_PER_GROUP
D_EXPERT = 512
EPS = 1e-6
NEG = -1e30
LANES = 128
SUBLANES = 8
VMEM_LIMIT = 56 * 1024 * 1024

HG_LEVELS = (32, 16, 8, 4, 2, 1)


def _cparams(sem):
    return pltpu.CompilerParams(dimension_semantics=sem, vmem_limit_bytes=VMEM_LIMIT)


def _sigmoid(x):
    return 1.0 / (1.0 + jnp.exp(-x))


def _silu(x):
    return x * _sigmoid(x)


def _mod_kernel(c_ref, w_ref, b_ref, o_ref):
    s = _silu(c_ref[...])
    o_ref[...] = jnp.dot(s, w_ref[...], precision=HIGHEST, preferred_element_type=F32) + b_ref[...]


def _modulation(cc, w_mod, b_mod):
    rows, d = cc.shape
    n = w_mod.shape[1]
    tn = 1024
    return pl.pallas_call(
        _mod_kernel,
        grid=(n // tn,),
        in_specs=[pl.BlockSpec((rows, d), lambda j: (0, 0)),
                  pl.BlockSpec((d, tn), lambda j: (0, j)),
                  pl.BlockSpec((1, tn), lambda j: (0, j))],
        out_specs=pl.BlockSpec((rows, tn), lambda j: (0, j)),
        out_shape=jax.ShapeDtypeStruct((rows, n), F32),
        compiler_params=_cparams(("arbitrary",)),
        name="mod",
    )(cc, w_mod, b_mod.reshape(1, n))


def _norm_mod(x, g, sc, sh):
    y = x * lax.rsqrt(jnp.mean(x * x, axis=-1, keepdims=True) + EPS)
    return (y * g) * (1.0 + sc) + sh


def _rope(a, cos, sin):
    lane = lax.broadcasted_iota(jnp.int32, a.shape, 1)
    first = (lane % ROPE_AXIS_DIM) < (ROPE_AXIS_DIM // 2)
    up = pltpu.roll(a, LANES - ROPE_AXIS_DIM // 2, axis=1)
    dn = pltpu.roll(a, ROPE_AXIS_DIM // 2, axis=1)
    return a * cos + jnp.where(first, up, dn) * sin


P_QRAW, P_QROT, P_KROT, P_V, P_HQ, P_HI, P_KF, P_GFH, P_GFL, P_KB, P_GBH, P_GBL, P_HG = range(13)
P_SEGS = 13


def _proj_kernel(x_ref, g_ref, sc_ref, sh_ref, w_ref, cos_ref, sin_ref, lb_ref, o_ref):
    h = _norm_mod(x_ref[0], g_ref[...], sc_ref[0], sh_ref[0]).astype(BF16)
    scale = NA_HEAD_DIM ** -0.5

    def put(seg, val):
        o_ref[0, :, seg * SEG:(seg + 1) * SEG] = val.astype(BF16)

    for j in range(8):
        acc = jnp.dot(h, w_ref[:, j * SEG:(j + 1) * SEG], preferred_element_type=F32)
        if j <= 1:
            rot = jnp.concatenate(
                [_rope(acc[:, p * LANES:(p + 1) * LANES], cos_ref[:, p * LANES:(p + 1) * LANES],
                       sin_ref[:, p * LANES:(p + 1) * LANES]) for p in range(SEG // LANES)], axis=1)
            if j == 0:
                put(P_QRAW, acc * scale)
                put(P_QROT, rot * scale)
            else:
                put(P_KROT, rot)
        elif j == 2:
            put(P_V, acc)
        elif j == 3:
            put(P_HQ, _silu(acc))
        elif j == 4:
            put(P_HI, acc)
        elif j in (5, 6):
            lb = lb_ref[j - 5:j - 4, :]
            f = lb + (1.0 - lb) * _sigmoid(acc)
            g2 = jnp.log2(f)
            hi = g2.astype(BF16)
            base = P_KF if j == 5 else P_KB
            put(base, 1.0 - f)
            put(base + 1, hi)
            put(base + 2, g2 - hi.astype(F32))
        else:
            put(P_HG, acc)


def _project(x, g, sc, sh, w_bf, cos_t, sin_t, lb, tm):
    B, S, D = x.shape
    return pl.pallas_call(
        _proj_kernel,
        grid=(S // tm, B),
        in_specs=[pl.BlockSpec((1, tm, D), lambda s, b: (b, s, 0)),
                  pl.BlockSpec((1, D), lambda s, b: (0, 0)),
                  pl.BlockSpec((1, 1, D), lambda s, b: (b, 0, 0)),
                  pl.BlockSpec((1, 1, D), lambda s, b: (b, 0, 0)),
                  pl.BlockSpec((D, 8 * SEG), lambda s, b: (0, 0)),
                  pl.BlockSpec((tm, SEG), lambda s, b: (s, 0)),
                  pl.BlockSpec((tm, SEG), lambda s, b: (s, 0)),
                  pl.BlockSpec((2, HG_WIDTH), lambda s, b: (0, 0))],
        out_specs=pl.BlockSpec((1, tm, P_SEGS * SEG), lambda s, b: (b, s, 0)),
        out_shape=jax.ShapeDtypeStruct((B, S, P_SEGS * SEG), BF16),
        compiler_params=_cparams(("arbitrary", "arbitrary")),
        name="proj",
    )(x, g, sc, sh, w_bf, cos_t, sin_t, lb)


def _ctxproj_kernel(x_ref, g_ref, sc_ref, sh_ref, w_ref, o_ref):
    h = _norm_mod(x_ref[0], g_ref[...], sc_ref[...], sh_ref[...]).astype(BF16)
    for j in range(5):
        acc = jnp.dot(h, w_ref[:, j * SEG:(j + 1) * SEG], preferred_element_type=F32)
        o_ref[0, :, j * SEG:(j + 1) * SEG] = acc.astype(BF16)


def _project_ctx(ctx, g, sc, sh, w_bf):
    B, L, D = ctx.shape
    return pl.pallas_call(
        _ctxproj_kernel,
        grid=(B,),
        in_specs=[pl.BlockSpec((1, L, D), lambda b: (b, 0, 0)),
                  pl.BlockSpec((1, D), lambda b: (0, 0)),
                  pl.BlockSpec((1, D), lambda b: (0, 0)),
                  pl.BlockSpec((1, D), lambda b: (0, 0)),
                  pl.BlockSpec((D, 5 * SEG), lambda b: (0, 0))],
        out_specs=pl.BlockSpec((1, L, 5 * SEG), lambda b: (b, 0, 0)),
        out_shape=jax.ShapeDtypeStruct((B, L, 5 * SEG), BF16),
        compiler_params=_cparams(("arbitrary",)),
        name="ctxproj",
    )(ctx, g, sc, sh, w_bf)


def _na_kernel(qraw_ref, qrot_ref, k_ref, v_ref, ck_ref, cv_ref, bias_ref, o_ref, vt_ref, cvt_ref, *, rows):
    r = pl.program_id(1)

    @pl.when(r == 0)
    def _():
        def body(i, carry):
            blk = v_ref[0, pl.ds(pl.multiple_of(i * GRID_W, GRID_W), GRID_W), :]
            vt_ref[i] = blk.astype(F32).T.astype(BF16)
            return carry
        lax.fori_loop(0, rows, body, 0)
        cvt_ref[...] = cv_ref[0].astype(F32).T.astype(BF16)

    rs = jnp.clip(r - NA_KH // 2, 0, rows - NA_KH)
    start = pl.multiple_of(rs * GRID_W, GRID_W)
    nk = NA_KH * GRID_W
    lane = lax.broadcasted_iota(jnp.int32, (GRID_W, LANES), 1)
    sel0 = lane < NA_HEAD_DIM
    nt = (((1,), (1,)), ((), ()))
    for p in range(NA_WIDTH // LANES):
        cols = slice(p * LANES, (p + 1) * LANES)
        qr = qrot_ref[0, :, cols]
        qw = qraw_ref[0, :, cols]
        zero = jnp.zeros_like(qr)
        qs_rot = jnp.concatenate([jnp.where(sel0, qr, zero), jnp.where(sel0, zero, qr)], axis=0)
        qs_raw = jnp.concatenate([jnp.where(sel0, qw, zero), jnp.where(sel0, zero, qw)], axis=0)
        s_loc = lax.dot_general(k_ref[0, pl.ds(start, nk), cols], qs_rot, nt,
                                preferred_element_type=F32) + bias_ref[0, p]
        s_ctx = lax.dot_general(ck_ref[0, :, cols], qs_raw, nt, preferred_element_type=F32)
        m = jnp.maximum(jnp.max(s_loc, axis=0, keepdims=True), jnp.max(s_ctx, axis=0, keepdims=True))
        p_loc = jnp.exp(s_loc - m)
        p_ctx = jnp.exp(s_ctx - m)
        l = jnp.sum(p_loc, axis=0, keepdims=True) + jnp.sum(p_ctx, axis=0, keepdims=True)
        p_loc = p_loc.astype(BF16)
        ot = jnp.dot(cvt_ref[cols, :], p_ctx.astype(BF16), preferred_element_type=F32)
        for i in range(NA_KH):
            ot = ot + jnp.dot(vt_ref[rs + i, cols, :], p_loc[i * GRID_W:(i + 1) * GRID_W, :],
                              preferred_element_type=F32)
        res = (ot * (1.0 / l)).T
        o_ref[0, :, cols] = jnp.where(sel0, res[0:GRID_W], res[GRID_W:2 * GRID_W]).astype(BF16)


def _na_bias_table(rpb):
    qc = np.arange(GRID_W)[None, :]
    kc = np.arange(GRID_W)[:, None]
    cs = np.clip(qc - NA_KW // 2, 0, GRID_W - NA_KW)
    valid = (kc >= cs) & (kc < cs + NA_KW)
    col_off = np.clip(kc - qc + NA_KW - 1, 0, 2 * NA_KW - 2)
    ro = np.arange(NA_KH)[:, None] + np.arange(NA_KH)[None, :]
    t = rpb.astype(F32)[:, ro][:, :, :, col_off]
    t = jnp.where(jnp.asarray(valid)[None, None, None], t, NEG)
    n_pairs = NA_HEADS // 2
    t = t.reshape(n_pairs, 2, NA_KH, NA_KH, GRID_W, GRID_W)
    t = t.transpose(2, 0, 3, 4, 1, 5)
    return t.reshape(NA_KH, n_pairs, NA_KH * GRID_W, 2 * GRID_W)


def _neighbourhood_attention(proj, cproj, bias_tbl):
    B, S, _ = proj.shape
    L = cproj.shape[1]
    rows = S // GRID_W
    nk = NA_KH * GRID_W

    def bias_map(b, r):
        return (jnp.clip(r - NA_KH // 2, 0, rows - NA_KH) - r + NA_KH - 1, 0, 0, 0)

    return pl.pallas_call(
        functools.partial(_na_kernel, rows=rows),
        grid=(B, rows),
        in_specs=[pl.BlockSpec((1, GRID_W, SEG), lambda b, r: (b, r, 0)),
                  pl.BlockSpec((1, GRID_W, SEG), lambda b, r: (b, r, 1)),
                  pl.BlockSpec((1, S, SEG), lambda b, r: (b, 0, 2)),
                  pl.BlockSpec((1, S, SEG), lambda b, r: (b, 0, 3)),
                  pl.BlockSpec((1, L, SEG), lambda b, r: (b, 0, 0)),
                  pl.BlockSpec((1, L, SEG), lambda b, r: (b, 0, 1)),
                  pl.BlockSpec((1, NA_HEADS // 2, nk, LANES), bias_map)],
        out_specs=pl.BlockSpec((1, GRID_W, SEG), lambda b, r: (b, r, 0)),
        out_shape=jax.ShapeDtypeStruct((B, S, NA_WIDTH), BF16),
        scratch_shapes=[pltpu.VMEM((rows, NA_WIDTH, GRID_W), BF16),
                        pltpu.VMEM((NA_WIDTH, L), BF16)],
        compiler_params=_cparams(("arbitrary", "arbitrary")),
        name="na",
    )(proj, proj, proj, proj, cproj, cproj, bias_tbl)


def _split3(g):
    g1 = g.astype(BF16)
    r1 = g - g1.astype(F32)
    g2 = r1.astype(BF16)
    g3 = (r1 - g2.astype(F32)).astype(BF16)
    return g1, g2, g3


def _cumdot(c_bf, g):
    g1, g2, g3 = _split3(g)
    d = lambda a: jnp.dot(c_bf, a, preferred_element_type=F32)
    return (d(g3) + d(g2)) + d(g1)


def _forget(pre, lb):
    f = lb + (1.0 - lb) * _sigmoid(pre.astype(F32))
    return 1.0 - f, jnp.log(f)


def _hg_consts():
    c = HG_CHUNK
    t = np.arange(c)[:, None]
    u = np.arange(c)[None, :]
    cm, role, masks = [], [], []
    for fwd in (True, False):
        blocks = [(u <= t) if fwd else (u >= t)]
        roles, ms = [], []
        for m in HG_LEVELS:
            blk = t // (2 * m)
            if fwd:
                mid = 2 * m * blk + m - 1
                is_q = (t % (2 * m)) >= m
                expo = np.where(is_q, (u > mid) & (u <= t), (u > t) & (u <= mid))
            else:
                mid = 2 * m * blk + m
                is_q = (t % (2 * m)) < m
                expo = np.where(is_q, (u >= t) & (u < mid), (u >= mid) & (u < t))
            blocks.append(expo)
            roles.append(np.broadcast_to(is_q, (c, 2 * HG_DK)))
            ms.append((blk == blk.T) & is_q & ~is_q.T)
        ms.append(t == u)
        full = np.concatenate(blocks, axis=0).astype(np.float32)
        cm.append(np.concatenate([full, full], axis=1))
        role.append(np.stack(roles).astype(np.float32))
        masks.append(np.stack([np.concatenate([x, x], axis=1) for x in ms]).astype(np.float32))
    ones_bd = np.kron(np.eye(2, dtype=np.float32), np.ones((HG_DK, c), np.float32))
    return (jnp.asarray(np.stack(cm), BF16), jnp.asarray(np.stack(role), BF16),
            jnp.asarray(np.stack(masks)), jnp.asarray(ones_bd, BF16))


def _block_diag(x, zeros):
    return jnp.concatenate([jnp.concatenate([x[:, :HG_DK], zeros], axis=1),
                            jnp.concatenate([zeros, x[:, HG_DK:]], axis=1)], axis=0)


def _hg_pair_dir(q, v, k, gh, gl, cmat_ref, role_ref, masks_ref, ones_ref, st_ref, di, pair):
    c = HG_CHUNK
    nl = len(HG_LEVELS)
    nt = (((1,), (1,)), ((), ()))
    tn = (((0,), (0,)), ((), ()))
    dall = jnp.dot(cmat_ref[di], jnp.concatenate([gh, gl], axis=0), preferred_element_type=F32)
    b = dall[0:c]
    bend = b[c - 1:c] if di == 0 else b[0:1]
    qe = q * jnp.exp2(b).astype(BF16)
    ke = k * jnp.exp2(bend - b).astype(BF16)
    zeros = jnp.zeros((c, HG_DK), BF16)
    a = None
    for li in range(nl):
        e = jnp.exp2(dall[(li + 1) * c:(li + 2) * c]).astype(BF16)
        x = jnp.where(role_ref[di, li] > 0.5, q, k) * e
        r = lax.dot_general(x, _block_diag(x, zeros), nt, preferred_element_type=F32) * masks_ref[di, li]
        a = r if a is None else a + r
    a = a + jnp.dot(q * k, ones_ref[...], preferred_element_type=F32) * masks_ref[di, nl]
    o = jnp.dot(a.astype(BF16), _block_diag(v, zeros), preferred_element_type=F32)
    outs = []
    for hh in range(2):
        hs = slice(hh * HG_DK, (hh + 1) * HG_DK)
        st = st_ref[di, 2 * pair + hh]
        outs.append(o[:, hs] + lax.dot_general(qe[:, hs], st.astype(BF16), nt, preferred_element_type=F32))
        upd = lax.dot_general(v[:, hs], ke[:, hs], tn, preferred_element_type=F32)
        st_ref[di, 2 * pair + hh] = jnp.exp2(bend[:, hs]) * st + upd
    return outs


def _hgrn_kernel(sq_ref, hi_ref, kf_ref, gfh_ref, gfl_ref, kb_ref, gbh_ref, gbl_ref, hg_ref,
                 ci_ref, cff_ref, cfb_ref, lb_ref, gain_ref, cmat_ref, role_ref, masks_ref, ones_ref,
                 ustrict_ref, lstrict_ref, o_ref, acc_ref, st_ref, *, n_chunks):
    c = HG_CHUNK
    tn = (((0,), (0,)), ((), ()))

    for h in range(HG_HEADS):
        hs = slice(h * HG_DK, (h + 1) * HG_DK)
        ci = ci_ref[0, :, hs]
        for di, (pre_ref, cm_ref) in enumerate(((cff_ref, ustrict_ref), (cfb_ref, lstrict_ref))):
            k, g = _forget(pre_ref[0, :, hs], lb_ref[di:di + 1, hs])
            ke = (k * jnp.exp(_cumdot(cm_ref[...], g))).astype(BF16)
            st_ref[di, h] = lax.dot_general(ci, ke, tn, preferred_element_type=F32)

    def step(i, second):
        rows = (pl.ds(pl.multiple_of(i * c, c), c), pl.ds(pl.multiple_of((n_chunks - 1 - i) * c, c), c))
        for pair in range(HG_HEADS // 2):
            cols = slice(2 * pair * HG_DK, 2 * (pair + 1) * HG_DK)
            for di, rr in enumerate(rows):
                k_ref, gh_ref, gl_ref = (kf_ref, gfh_ref, gfl_ref) if di == 0 else (kb_ref, gbh_ref, gbl_ref)
                outs = _hg_pair_dir(sq_ref[0, rr, cols], hi_ref[0, rr, cols], k_ref[0, rr, cols],
                                    gh_ref[0, rr, cols], gl_ref[0, rr, cols],
                                    cmat_ref, role_ref, masks_ref, ones_ref, st_ref, di, pair)
                for hh, o in enumerate(outs):
                    hs = slice((2 * pair + hh) * HG_DK, (2 * pair + hh + 1) * HG_DK)
                    if not second:
                        acc_ref[rr, hs] = o
                    else:
                        ot = acc_ref[rr, hs] + o
                        y = ot * lax.rsqrt(jnp.mean(ot * ot, axis=-1, keepdims=True) + EPS) * gain_ref[:, hs]
                        o_ref[0, rr, hs] = (y * _silu(hg_ref[0, rr, hs].astype(F32))).astype(BF16)

    def first_half(i, carry):
        step(i, False)
        return carry

    def second_half(i, carry):
        step(i, True)
        return carry

    lax.fori_loop(0, n_chunks // 2, first_half, 0)
    lax.fori_loop(n_chunks // 2, n_chunks, second_half, 0)


def _hgrn(proj, cproj, lb, gain):
    B, S, _ = proj.shape
    L = cproj.shape[1]
    c = HG_CHUNK
    nl = len(HG_LEVELS)
    tri_l = np.tril(np.ones((L, L), np.float32), -1)
    consts = list(_hg_consts()) + [jnp.asarray(tri_l.T, BF16), jnp.asarray(tri_l, BF16)]
    seq = lambda j: pl.BlockSpec((1, S, SEG), lambda b: (b, 0, j))
    cseq = lambda j: pl.BlockSpec((1, L, SEG), lambda b: (b, 0, j))
    full2 = lambda n, m: pl.BlockSpec((n, m), lambda b: (0, 0))
    return pl.pallas_call(
        functools.partial(_hgrn_kernel, n_chunks=S // c),
        grid=(B,),
        in_specs=[seq(P_HQ), seq(P_HI), seq(P_KF), seq(P_GFH), seq(P_GFL), seq(P_KB), seq(P_GBH), seq(P_GBL),
                  seq(P_HG), cseq(2), cseq(3), cseq(4),
                  full2(2, HG_WIDTH), full2(1, HG_WIDTH),
                  pl.BlockSpec((2, (nl + 1) * c, 2 * c), lambda b: (0, 0, 0)),
                  pl.BlockSpec((2, nl, c, 2 * HG_DK), lambda b: (0, 0, 0, 0)),
                  pl.BlockSpec((2, nl + 1, c, 2 * c), lambda b: (0, 0, 0, 0)),
                  full2(2 * HG_DK, 2 * c), full2(L, L), full2(L, L)],
        out_specs=pl.BlockSpec((1, S, HG_WIDTH), lambda b: (b, 0, 0)),
        out_shape=jax.ShapeDtypeStruct((B, S, HG_WIDTH), BF16),
        scratch_shapes=[pltpu.VMEM((S, HG_WIDTH), F32),
                        pltpu.VMEM((2, HG_HEADS, HG_DK, HG_DK), F32)],
        compiler_params=_cparams(("arbitrary",)),
        name="hgrn",
    )(*([proj] * 9), cproj, cproj, cproj, lb, gain, *consts)


def _route(logits):
    lane = lax.broadcasted_iota(jnp.int32, logits.shape, 1).astype(F32)
    big = float(LANES)
    is_g = lane < N_GROUPS
    gl = jnp.where(is_g, logits, -jnp.inf)
    gmax = jnp.max(gl, axis=-1, keepdims=True)
    g_sel = jnp.min(jnp.where(gl == gmax, lane, big), axis=-1, keepdims=True)
    g_w = 1.0 / jnp.sum(jnp.where(is_g, jnp.exp(gl - gmax), 0.0), axis=-1, keepdims=True)
    el_lane = lane - N_GROUPS
    in_grp = (el_lane >= g_sel * EXPERTS_PER_GROUP) & (el_lane < (g_sel + 1) * EXPERTS_PER_GROUP)
    e1 = jnp.where(in_grp, logits, -jnp.inf)
    v1 = jnp.max(e1, axis=-1, keepdims=True)
    i1 = jnp.min(jnp.where(e1 == v1, lane, big), axis=-1, keepdims=True)
    e2 = jnp.where(lane == i1, -jnp.inf, e1)
    v2 = jnp.max(e2, axis=-1, keepdims=True)
    i2 = jnp.min(jnp.where(e2 == v2, lane, big), axis=-1, keepdims=True)
    t = jnp.exp(v2 - v1)
    w1 = g_w / (1.0 + t)
    w2 = g_w * t / (1.0 + t)
    onehot = jnp.where(lane == g_sel, 1.0, 0.0)
    first = g_sel * EXPERTS_PER_GROUP + N_GROUPS
    gates = jnp.where(lane == i1 - first, w1, jnp.where(lane == i2 - first, w2, 0.0))
    return onehot, gates


def _outproj_kernel(x_ref, na_ref, hg_ref, w_ref, ga_ref, g_ref, sc_ref, sh_ref, wr_ref, br_ref,
                    x1_ref, hx_ref, oh_ref, cnt_ref):
    d = x_ref.shape[1]
    mix = (jnp.dot(na_ref[...], w_ref[0:NA_WIDTH, :], preferred_element_type=F32)
           + jnp.dot(hg_ref[...], w_ref[NA_WIDTH:, :], preferred_element_type=F32))
    x1 = x_ref[...] + ga_ref[0] * mix
    x1_ref[...] = x1
    h2 = _norm_mod(x1, g_ref[...], sc_ref[0], sh_ref[0])
    logits = jnp.dot(h2, wr_ref[...], precision=HIGHEST, preferred_element_type=F32) + br_ref[...]
    onehot, gates = _route(logits)
    hx_ref[:, 0:d] = h2.astype(BF16)
    for j, piece in enumerate(_split3(gates)):
        hx_ref[:, d + j * LANES:d + (j + 1) * LANES] = piece
    oh_ref[...] = onehot
    cnt_ref[0] = jnp.broadcast_to(jnp.sum(onehot, axis=0, keepdims=True), (SUBLANES, LANES))


def _outproj(x2d, na2d, hg2d, w_out_bf, ga, g, sc, sh, wr, br, S):
    T, D = x2d.shape
    tm = MOE_TM
    per = S // tm
    tok = lambda w: pl.BlockSpec((tm, w), lambda i: (i, 0))
    bat = pl.BlockSpec((1, 1, D), lambda i: (i // per, 0, 0))
    return pl.pallas_call(
        _outproj_kernel,
        grid=(T // tm,),
        in_specs=[tok(D), tok(NA_WIDTH), tok(HG_WIDTH),
                  pl.BlockSpec((NA_WIDTH + HG_WIDTH, D), lambda i: (0, 0)),
                  bat, pl.BlockSpec((1, D), lambda i: (0, 0)), bat, bat,
                  pl.BlockSpec((D, LANES), lambda i: (0, 0)),
                  pl.BlockSpec((1, LANES), lambda i: (0, 0))],
        out_specs=[tok(D), tok(MOE_XW), tok(LANES),
                   pl.BlockSpec((1, SUBLANES, LANES), lambda i: (i, 0, 0))],
        out_shape=[jax.ShapeDtypeStruct((T, D), F32), jax.ShapeDtypeStruct((T, MOE_XW), BF16),
                   jax.ShapeDtypeStruct((T, LANES), F32),
                   jax.ShapeDtypeStruct((T // tm, SUBLANES, LANES), F32)],
        compiler_params=_cparams(("arbitrary",)),
        name="outproj",
    )(x2d, na2d, hg2d, w_out_bf, ga, g, sc, sh, wr, br)


MOE_TM = 512
MOE_ALIGN = 16
MOE_LOC = 640
MOE_XW = D_MODEL + 3 * LANES
MOE_TE = 512
MOE_BITS = tuple(1 << b for b in range(9, 3, -1))


def _moe_steps(T):
    worst = T + (T // MOE_TM) * N_GROUPS * (MOE_ALIGN - 1)
    return -(-worst // MOE_TE) + N_GROUPS


def _local_positions(oh_ref, lstrict_ref, cnt_ref, k):
    onehot = oh_ref[...]
    ranks = jnp.dot(lstrict_ref[...], onehot.astype(BF16), preferred_element_type=F32)
    lane = lax.broadcasted_iota(jnp.int32, (1, LANES), 1)
    base = jnp.zeros((1, LANES), F32)
    o = jnp.int32(0)
    for g in range(N_GROUPS):
        base = jnp.where(lane == g, o.astype(F32), base)
        o = o + cnt_ref[k * N_GROUPS + g]
    return jnp.sum(onehot * (ranks + base), axis=-1, keepdims=True)


def _sort_matrix(lpos):
    col = lax.broadcasted_iota(jnp.int32, (MOE_TM, MOE_LOC), 1).astype(F32)
    return jnp.where(lpos == col, 1.0, 0.0).astype(BF16)


def _block_copies(n_rows, src_row, dst_row, bits, make_copy, action):
    for bit in bits:
        part = n_rows & (-2 * bit)

        @pl.when((n_rows & bit) != 0)
        def _(part=part, bit=bit):
            action(make_copy(pl.multiple_of(src_row + part, MOE_ALIGN), pl.multiple_of(dst_row + part, MOE_ALIGN), bit))


def _run_copies(off_ref, cnt_ref, k, make_copy, action):
    o = jnp.int32(0)
    for g in range(N_GROUPS):
        c = cnt_ref[k * N_GROUPS + g]
        _block_copies(c, o, off_ref[k * N_GROUPS + g], MOE_BITS, make_copy, action)
        o = o + c


def _dispatch_kernel(off_ref, cnt_ref, tail_ref, hx_ref, oh_ref, lstrict_ref, hs_ref, buf_ref, zero_ref, sem,
                     *, n_tiles, n_steps):
    k = pl.program_id(0)
    slot = k % 2

    def copies(kk, sl, action):
        def make(src_row, dst_row, n):
            return pltpu.make_async_copy(buf_ref.at[sl, pl.ds(src_row, n)], hs_ref.at[pl.ds(dst_row, n)], sem.at[sl])
        _run_copies(off_ref, cnt_ref, kk, make, action)

    @pl.when(k >= 2)
    def _():
        copies(k - 2, slot, lambda cp: cp.wait())

    pt = _sort_matrix(_local_positions(oh_ref, lstrict_ref, cnt_ref, k))
    srt = lax.dot_general(pt, hx_ref[...], (((0,), (0,)), ((), ())), preferred_element_type=F32)
    buf_ref[slot] = srt.astype(BF16)
    copies(k, slot, lambda cp: cp.start())

    @pl.when(k == n_tiles - 1)
    def _():
        zero_ref[...] = jnp.zeros_like(zero_ref)
        n_used = tail_ref[2 * N_GROUPS]

        def zero_copy(src_row, dst_row, n):
            return pltpu.make_async_copy(zero_ref.at[pl.ds(src_row, n)], hs_ref.at[pl.ds(dst_row, n)], sem.at[2])

        def tile_copy(i):
            return zero_copy(0, pl.multiple_of(i * MOE_TE, MOE_TE), MOE_TE)

        def fills(action):
            for g in range(N_GROUPS):
                _block_copies(tail_ref[N_GROUPS + g], jnp.int32(0), tail_ref[g], MOE_BITS[1:], zero_copy, action)

        fills(lambda cp: cp.start())
        lax.fori_loop(n_used, n_steps, lambda i, c: (tile_copy(i).start(), c)[1], 0)
        if n_tiles >= 2:
            copies(k - 1, 1 - slot, lambda cp: cp.wait())
        copies(k, slot, lambda cp: cp.wait())
        fills(lambda cp: cp.wait())
        lax.fori_loop(n_used, n_steps, lambda i, c: (tile_copy(i).wait(), c)[1], 0)


def _dispatch(off, cnt, tail, hx, onehot, lstrict, n_steps):
    T = hx.shape[0]
    n_tiles = T // MOE_TM
    return pl.pallas_call(
        functools.partial(_dispatch_kernel, n_tiles=n_tiles, n_steps=n_steps),
        grid_spec=pltpu.PrefetchScalarGridSpec(
            num_scalar_prefetch=3,
            grid=(n_tiles,),
            in_specs=[pl.BlockSpec((MOE_TM, MOE_XW), lambda k, *_: (k, 0)),
                      pl.BlockSpec((MOE_TM, LANES), lambda k, *_: (k, 0)),
                      pl.BlockSpec((MOE_TM, MOE_TM), lambda k, *_: (0, 0))],
            out_specs=pl.BlockSpec(memory_space=pl.ANY),
            scratch_shapes=[pltpu.VMEM((2, MOE_LOC, MOE_XW), BF16),
                            pltpu.VMEM((MOE_TE, MOE_XW), BF16),
                            pltpu.SemaphoreType.DMA((3,))]),
        out_shape=jax.ShapeDtypeStruct((n_steps * MOE_TE, MOE_XW), BF16),
        compiler_params=_cparams(("arbitrary",)),
        name="dispatch",
    )(off, cnt, tail, hx, onehot, lstrict)


def _experts_kernel(grp_ref, used_ref, hs_ref, w1_ref, w3_ref, w2_ref, ys_ref):
    i = pl.program_id(0)
    d = ys_ref.shape[1]
    ne = EXPERTS_PER_GROUP

    @pl.when(i < used_ref[0])
    def _():
        t = hs_ref[:, 0:d]
        gates = ((hs_ref[:, d + 2 * LANES:d + 3 * LANES].astype(F32) + hs_ref[:, d + LANES:d + 2 * LANES].astype(F32))
                 + hs_ref[:, d:d + LANES].astype(F32))
        lane = lax.broadcasted_iota(jnp.int32, gates.shape, 1)
        acc = None
        for j in range(ne):
            a = jnp.dot(t, w1_ref[0, j], preferred_element_type=F32)
            b = jnp.dot(t, w3_ref[0, j], preferred_element_type=F32)
            gj = jnp.sum(jnp.where(lane == j, gates, 0.0), axis=-1, keepdims=True)
            y = gj * jnp.dot((_silu(a) * b).astype(BF16), w2_ref[0, j], preferred_element_type=F32)
            acc = y if acc is None else acc + y
        ys_ref[...] = acc.astype(BF16)

    @pl.when(i >= used_ref[0])
    def _():
        ys_ref[...] = jnp.zeros_like(ys_ref)


def _experts(grp, used, hs, w1_bf, w3_bf, w2_bf):
    rows, _ = hs.shape
    D = w2_bf.shape[3]
    wspec = lambda w: pl.BlockSpec((1,) + w.shape[1:], lambda i, grp, used: (grp[i], 0, 0, 0))
    return pl.pallas_call(
        _experts_kernel,
        grid_spec=pltpu.PrefetchScalarGridSpec(
            num_scalar_prefetch=2,
            grid=(rows // MOE_TE,),
            in_specs=[pl.BlockSpec((MOE_TE, MOE_XW), lambda i, grp, used: (i, 0)),
                      wspec(w1_bf), wspec(w3_bf), wspec(w2_bf)],
            out_specs=pl.BlockSpec((MOE_TE, D), lambda i, grp, used: (i, 0))),
        out_shape=jax.ShapeDtypeStruct((rows, D), BF16),
        compiler_params=_cparams(("arbitrary",)),
        name="experts",
    )(grp, used, hs, w1_bf, w3_bf, w2_bf)


def _combine_kernel(off_ref, cnt_ref, x1_ref, oh_ref, lstrict_ref, ga_ref, gf_ref, ys_ref, o_ref, buf_ref, sem,
                    *, n_tiles):
    k = pl.program_id(0)
    slot = k % 2

    def copies(kk, sl, action):
        def make(loc_row, seg_row, n):
            return pltpu.make_async_copy(ys_ref.at[pl.ds(seg_row, n)], buf_ref.at[sl, pl.ds(loc_row, n)], sem.at[sl])
        _run_copies(off_ref, cnt_ref, kk, make, action)

    @pl.when(k == 0)
    def _():
        buf_ref[...] = jnp.zeros_like(buf_ref)
        copies(k, slot, lambda cp: cp.start())

    @pl.when(k + 1 < n_tiles)
    def _():
        copies(k + 1, 1 - slot, lambda cp: cp.start())

    pt = _sort_matrix(_local_positions(oh_ref, lstrict_ref, cnt_ref, k))
    copies(k, slot, lambda cp: cp.wait())
    y = jnp.dot(pt, buf_ref[slot], preferred_element_type=F32)
    x2 = x1_ref[...] + ga_ref[0] * y
    o_ref[...] = x2 * lax.rsqrt(jnp.mean(x2 * x2, axis=-1, keepdims=True) + EPS) * gf_ref[...]


def _combine(off, cnt, x1, onehot, lstrict, ga, gfin, ys, S):
    T, D = x1.shape
    n_tiles = T // MOE_TM
    per = S // MOE_TM
    return pl.pallas_call(
        functools.partial(_combine_kernel, n_tiles=n_tiles),
        grid_spec=pltpu.PrefetchScalarGridSpec(
            num_scalar_prefetch=2,
            grid=(n_tiles,),
            in_specs=[pl.BlockSpec((MOE_TM, D), lambda k, off, cnt: (k, 0)),
                      pl.BlockSpec((MOE_TM, LANES), lambda k, off, cnt: (k, 0)),
                      pl.BlockSpec((MOE_TM, MOE_TM), lambda k, off, cnt: (0, 0)),
                      pl.BlockSpec((1, 1, D), lambda k, off, cnt: (k // per, 0, 0)),
                      pl.BlockSpec((1, D), lambda k, off, cnt: (0, 0)),
                      pl.BlockSpec(memory_space=pl.ANY)],
            out_specs=pl.BlockSpec((MOE_TM, D), lambda k, off, cnt: (k, 0)),
            scratch_shapes=[pltpu.VMEM((2, MOE_LOC, D), BF16),
                            pltpu.SemaphoreType.DMA((2,))]),
        out_shape=jax.ShapeDtypeStruct((T, D), F32),
        compiler_params=_cparams(("arbitrary",)),
        name="combine",
    )(off, cnt, x1, onehot, lstrict, ga, gfin, ys)


def _moe_schedule(cnt_tiles, n_steps):
    cnt = ((cnt_tiles + (MOE_ALIGN - 1)) // MOE_ALIGN) * MOE_ALIGN
    ends = jnp.cumsum(cnt, axis=0)
    total = ends[-1]
    ntile = (total + MOE_TE - 1) // MOE_TE
    cum = jnp.cumsum(ntile)
    base = (cum - ntile) * MOE_TE
    off = base[None, :] + ends - cnt
    i = jnp.arange(n_steps, dtype=jnp.int32)
    grp = jnp.minimum(jnp.sum(i[:, None] >= cum[None, :], axis=1), N_GROUPS - 1)
    grp = jnp.where(i < cum[-1], grp, grp[jnp.maximum(cum[-1] - 1, 0)])
    tail = jnp.concatenate([base + total, ntile * MOE_TE - total, cum[-1:]])
    as_i32 = lambda a: a.reshape(-1).astype(jnp.int32)
    return as_i32(off), as_i32(cnt), as_i32(tail), as_i32(grp), as_i32(cum[-1:])


def _rope_tables(S):
    t = jnp.arange(S)
    pos = jnp.stack([t // GRID_W, t % GRID_W], axis=-1).astype(F32)
    inv = ROPE_BASE ** (-jnp.arange(0, ROPE_AXIS_DIM, 2, dtype=F32) / ROPE_AXIS_DIM)
    ang = pos[:, :, None] * inv
    cos, sin = jnp.cos(ang), jnp.sin(ang)
    cos_h = jnp.concatenate([cos, cos], axis=-1).reshape(S, NA_HEAD_DIM)
    sin_h = jnp.concatenate([-sin, sin], axis=-1).reshape(S, NA_HEAD_DIM)
    return jnp.tile(cos_h, (1, NA_HEADS)), jnp.tile(sin_h, (1, NA_HEADS))


def kernel(x, c, ctx, c_ctx, w_mod, b_mod, norm_mix, norm_ffn, w_in, w_out, na_rpb, hg_lb, hg_norm,
           w_grp, b_grp, w_exp, b_exp, w1, w3, w2, norm_final):
    B, S, D = x.shape
    T = B * S
    assert w_mod.shape[0] == 1, "single-layer kernel"

    rows = -(-(B + 1) // SUBLANES) * SUBLANES
    cc = jnp.zeros((rows, D), F32).at[:B].set(c).at[B].set(c_ctx)
    mod = _modulation(cc, w_mod[0], b_mod[0])
    sh_a, sc_a, ga_a, sh_f, sc_f, ga_f = [m.reshape(B, 1, D) for m in jnp.split(mod[:B], 6, axis=-1)]
    csh_a, csc_a = [m.reshape(1, D) for m in jnp.split(mod[B], 6)[:2]]

    w_in_bf = w_in[0].astype(BF16)
    w_ctx_bf = jnp.concatenate([w_in_bf[:, i * SEG:(i + 1) * SEG] for i in (1, 2, 4, 5, 6)], axis=1)
    cos_t, sin_t = _rope_tables(S)
    g_mix = norm_mix[0].reshape(1, D)

    lb = jnp.cumsum(jax.nn.softmax(hg_lb.astype(F32), axis=0), axis=0)[0]
    proj = _project(x, g_mix, sc_a, sh_a, w_in_bf, cos_t, sin_t, lb, tm=512)
    cproj = _project_ctx(ctx, g_mix, csc_a, csh_a, w_ctx_bf)

    na_out = _neighbourhood_attention(proj, cproj, _na_bias_table(na_rpb[0]))

    gain = jnp.tile(hg_norm[0].astype(F32), HG_HEADS).reshape(1, HG_WIDTH)
    hg_out = _hgrn(proj, cproj, lb, gain)

    wr = jnp.zeros((D, LANES), F32).at[:, :N_GROUPS].set(w_grp[0]).at[:, N_GROUPS:N_GROUPS + N_EXPERTS].set(w_exp[0])
    br = jnp.zeros((1, LANES), F32).at[0, :N_GROUPS].set(b_grp[0]).at[0, N_GROUPS:N_GROUPS + N_EXPERTS].set(b_exp[0])
    x1, hx, onehot, cnt_tiles = _outproj(x.reshape(T, D), na_out.reshape(T, NA_WIDTH), hg_out.reshape(T, HG_WIDTH),
                                         w_out[0].astype(BF16), ga_a, norm_ffn[0].reshape(1, D), sc_f, sh_f, wr, br,
                                         S=S)

    n_steps = _moe_steps(T)
    off, cnt, tail, grp, used = _moe_schedule(cnt_tiles[:, 0, :N_GROUPS].astype(jnp.int32), n_steps)
    lstrict = jnp.asarray(np.tril(np.ones((MOE_TM, MOE_TM), np.float32), -1), BF16)
    by_group = lambda w: w[0].astype(BF16).reshape(N_GROUPS, EXPERTS_PER_GROUP, *w.shape[2:])
    hs = _dispatch(off, cnt, tail, hx, onehot, lstrict, n_steps)
    ys = _experts(grp, used, hs, by_group(w1), by_group(w3), by_group(w2))
    out = _combine(off, cnt, x1, onehot, lstrict, ga_f, norm_final.reshape(1, D), ys, S)
    return out.reshape(B, S, D)
```

```python
import functools

import numpy as np
import jax
import jax.numpy as jnp
from jax import lax
from jax.experimental import pallas as pl
from jax.experimental.pallas import tpu as pltpu

F32 = jnp.float32
BF16 = jnp.bfloat16
HIGHEST = lax.Precision.HIGHEST

D_MODEL = 1024
GRID_W = 64
NA_HEADS = 8
NA_HEAD_DIM = 64
NA_WIDTH = NA_HEADS * NA_HEAD_DIM
NA_KH = 8
NA_KW = 16
ROPE_AXIS_DIM = NA_HEAD_DIM // 2
ROPE_BASE = 10000.0
HG_HEADS = 4
HG_DK = 128
HG_WIDTH = HG_HEADS * HG_DK
HG_CHUNK = 64
SEG = 512
N_GROUPS = 4
EXPERTS_PER_GROUP = 4
N_EXPERTS = N_GROUPS * EXPERTS_PER_GROUP
D_EXPERT = 512
EPS = 1e-6
NEG = -1e30
LOG2E = 1.4426950408889634
LANES = 128
SUBLANES = 8
VMEM_LIMIT = 56 * 1024 * 1024

HG_LEVELS = (32, 16, 8, 4, 2, 1)


def _cparams(sem):
    return pltpu.CompilerParams(dimension_semantics=sem, vmem_limit_bytes=VMEM_LIMIT)


def _sigmoid(x):
    return 1.0 / (1.0 + jnp.exp(-x))


def _silu(x):
    return x * _sigmoid(x)


def _mod_kernel(c_ref, w_ref, b_ref, o_ref):
    s = _silu(c_ref[...])
    o_ref[...] = jnp.dot(s, w_ref[...], precision=HIGHEST, preferred_element_type=F32) + b_ref[...]


def _modulation(cc, w_mod, b_mod):
    rows, d = cc.shape
    n = w_mod.shape[1]
    tn = 1024
    return pl.pallas_call(
        _mod_kernel,
        grid=(n // tn,),
        in_specs=[pl.BlockSpec((rows, d), lambda j: (0, 0)),
                  pl.BlockSpec((d, tn), lambda j: (0, j)),
                  pl.BlockSpec((1, tn), lambda j: (0, j))],
        out_specs=pl.BlockSpec((rows, tn), lambda j: (0, j)),
        out_shape=jax.ShapeDtypeStruct((rows, n), F32),
        compiler_params=_cparams(("arbitrary",)),
        name="mod",
    )(cc, w_mod, b_mod.reshape(1, n))


def _norm_mod(x, g, sc, sh):
    y = x * lax.rsqrt(jnp.mean(x * x, axis=-1, keepdims=True) + EPS)
    return (y * g) * (1.0 + sc) + sh


def _rope(a, cos, sin):
    lane = lax.broadcasted_iota(jnp.int32, a.shape, 1)
    first = (lane % ROPE_AXIS_DIM) < (ROPE_AXIS_DIM // 2)
    up = pltpu.roll(a, LANES - ROPE_AXIS_DIM // 2, axis=1)
    dn = pltpu.roll(a, ROPE_AXIS_DIM // 2, axis=1)
    return a * cos + jnp.where(first, up, dn) * sin


P_QRAW, P_QROT, P_KROT, P_V, P_HQ, P_HI, P_KF, P_GFH, P_GFL, P_KB, P_GBH, P_GBL, P_HG = range(13)
P_SEGS = 13


def _proj_kernel(x_ref, g_ref, sc_ref, sh_ref, w_ref, cos_ref, sin_ref, lb_ref, o_ref):
    h = _norm_mod(x_ref[0], g_ref[...], sc_ref[0], sh_ref[0]).astype(BF16)
    scale = NA_HEAD_DIM ** -0.5 * LOG2E

    def put(seg, val):
        o_ref[0, :, seg * SEG:(seg + 1) * SEG] = val.astype(BF16)

    for j in range(8):
        acc = jnp.dot(h, w_ref[:, j * SEG:(j + 1) * SEG], preferred_element_type=F32)
        if j <= 1:
            rot = jnp.concatenate(
                [_rope(acc[:, p * LANES:(p + 1) * LANES], cos_ref[:, p * LANES:(p + 1) * LANES],
                       sin_ref[:, p * LANES:(p + 1) * LANES]) for p in range(SEG // LANES)], axis=1)
            if j == 0:
                put(P_QRAW, acc * scale)
                put(P_QROT, rot * scale)
            else:
                put(P_KROT, rot)
        elif j == 2:
            put(P_V, acc)
        elif j == 3:
            put(P_HQ, _silu(acc))
        elif j == 4:
            put(P_HI, acc)
        elif j in (5, 6):
            lb = lb_ref[j - 5:j - 4, :]
            f = lb + (1.0 - lb) * _sigmoid(acc)
            g2 = jnp.log2(f)
            hi = g2.astype(BF16)
            base = P_KF if j == 5 else P_KB
            put(base, 1.0 - f)
            put(base + 1, hi)
            put(base + 2, g2 - hi.astype(F32))
        else:
            put(P_HG, acc)


def _project(x, g, sc, sh, w_bf, cos_t, sin_t, lb, tm):
    B, S, D = x.shape
    return pl.pallas_call(
        _proj_kernel,
        grid=(S // tm, B),
        in_specs=[pl.BlockSpec((1, tm, D), lambda s, b: (b, s, 0)),
                  pl.BlockSpec((1, D), lambda s, b: (0, 0)),
                  pl.BlockSpec((1, 1, D), lambda s, b: (b, 0, 0)),
                  pl.BlockSpec((1, 1, D), lambda s, b: (b, 0, 0)),
                  pl.BlockSpec((D, 8 * SEG), lambda s, b: (0, 0)),
                  pl.BlockSpec((tm, SEG), lambda s, b: (s, 0)),
                  pl.BlockSpec((tm, SEG), lambda s, b: (s, 0)),
                  pl.BlockSpec((2, HG_WIDTH), lambda s, b: (0, 0))],
        out_specs=pl.BlockSpec((1, tm, P_SEGS * SEG), lambda s, b: (b, s, 0)),
        out_shape=jax.ShapeDtypeStruct((B, S, P_SEGS * SEG), BF16),
        compiler_params=_cparams(("arbitrary", "arbitrary")),
        name="proj",
    )(x, g, sc, sh, w_bf, cos_t, sin_t, lb)


def _ctxproj_kernel(x_ref, g_ref, sc_ref, sh_ref, w_ref, o_ref):
    h = _norm_mod(x_ref[0], g_ref[...], sc_ref[...], sh_ref[...]).astype(BF16)
    for j in range(5):
        acc = jnp.dot(h, w_ref[:, j * SEG:(j + 1) * SEG], preferred_element_type=F32)
        o_ref[0, :, j * SEG:(j + 1) * SEG] = acc.astype(BF16)


def _project_ctx(ctx, g, sc, sh, w_bf):
    B, L, D = ctx.shape
    return pl.pallas_call(
        _ctxproj_kernel,
        grid=(B,),
        in_specs=[pl.BlockSpec((1, L, D), lambda b: (b, 0, 0)),
                  pl.BlockSpec((1, D), lambda b: (0, 0)),
                  pl.BlockSpec((1, D), lambda b: (0, 0)),
                  pl.BlockSpec((1, D), lambda b: (0, 0)),
                  pl.BlockSpec((D, 5 * SEG), lambda b: (0, 0))],
        out_specs=pl.BlockSpec((1, L, 5 * SEG), lambda b: (b, 0, 0)),
        out_shape=jax.ShapeDtypeStruct((B, L, 5 * SEG), BF16),
        compiler_params=_cparams(("arbitrary",)),
        name="ctxproj",
    )(ctx, g, sc, sh, w_bf)


def _na_kernel(qraw_ref, qrot_ref, k_ref, v_ref, ck_ref, cv_ref, bias_ref, o_ref, vt_ref, cvt_ref, *, rows):
    r = pl.program_id(1)

    @pl.when(r == 0)
    def _():
        def body(i, carry):
            blk = v_ref[0, pl.ds(pl.multiple_of(i * GRID_W, GRID_W), GRID_W), :]
            vt_ref[i] = blk.astype(F32).T.astype(BF16)
            return carry
        lax.fori_loop(0, rows, body, 0)
        cvt_ref[...] = cv_ref[0].astype(F32).T.astype(BF16)

    rs = jnp.clip(r - NA_KH // 2, 0, rows - NA_KH)
    start = pl.multiple_of(rs * GRID_W, GRID_W)
    bias_row = pl.multiple_of((rs - r + NA_KH - 1) * GRID_W, GRID_W)
    nk = NA_KH * GRID_W
    lane = lax.broadcasted_iota(jnp.int32, (GRID_W, LANES), 1)
    sel0 = lane < NA_HEAD_DIM
    nt = (((1,), (1,)), ((), ()))
    for p in range(NA_WIDTH // LANES):
        cols = slice(p * LANES, (p + 1) * LANES)
        qr = qrot_ref[0, :, cols]
        qw = qraw_ref[0, :, cols]
        zero = jnp.zeros_like(qr)
        qs_rot = jnp.concatenate([jnp.where(sel0, qr, zero), jnp.where(sel0, zero, qr)], axis=0)
        qs_raw = jnp.concatenate([jnp.where(sel0, qw, zero), jnp.where(sel0, zero, qw)], axis=0)
        s_loc = lax.dot_general(k_ref[0, pl.ds(start, nk), cols], qs_rot, nt,
                                preferred_element_type=F32) + bias_ref[p, pl.ds(bias_row, nk), :]
        s_ctx = lax.dot_general(ck_ref[0, :, cols], qs_raw, nt, preferred_element_type=F32)
        m = jnp.maximum(jnp.max(s_loc, axis=0, keepdims=True), jnp.max(s_ctx, axis=0, keepdims=True))
        p_loc = jnp.exp2(s_loc - m)
        p_ctx = jnp.exp2(s_ctx - m)
        l = jnp.sum(p_loc, axis=0, keepdims=True) + jnp.sum(p_ctx, axis=0, keepdims=True)
        p_loc = p_loc.astype(BF16)
        ot = jnp.dot(cvt_ref[cols, :], p_ctx.astype(BF16), preferred_element_type=F32)
        for i in range(NA_KH):
            ot = ot + jnp.dot(vt_ref[rs + i, cols, :], p_loc[i * GRID_W:(i + 1) * GRID_W, :],
                              preferred_element_type=F32)
        res = (ot * (1.0 / l)).T
        o_ref[0, :, cols] = jnp.where(sel0, res[0:GRID_W], res[GRID_W:2 * GRID_W]).astype(BF16)


def _na_bias_table(rpb):
    qc = np.arange(GRID_W)[None, :]
    kc = np.arange(GRID_W)[:, None]
    cs = np.clip(qc - NA_KW // 2, 0, GRID_W - NA_KW)
    valid = (kc >= cs) & (kc < cs + NA_KW)
    col_off = np.clip(kc - qc + NA_KW - 1, 0, 2 * NA_KW - 2)
    t = (rpb.astype(F32) * LOG2E)[:, :, col_off]
    t = jnp.where(jnp.asarray(valid)[None, None], t, NEG)
    n_pairs = NA_HEADS // 2
    n_ro = 2 * NA_KH - 1
    t = t.reshape(n_pairs, 2, n_ro, GRID_W, GRID_W).transpose(0, 2, 3, 1, 4)
    return t.reshape(n_pairs, n_ro * GRID_W, 2 * GRID_W)


def _neighbourhood_attention(proj, cproj, bias_tbl):
    B, S, _ = proj.shape
    L = cproj.shape[1]
    rows = S // GRID_W
    return pl.pallas_call(
        functools.partial(_na_kernel, rows=rows),
        grid=(B, rows),
        in_specs=[pl.BlockSpec((1, GRID_W, SEG), lambda b, r: (b, r, 0)),
                  pl.BlockSpec((1, GRID_W, SEG), lambda b, r: (b, r, 1)),
                  pl.BlockSpec((1, S, SEG), lambda b, r: (b, 0, 2)),
                  pl.BlockSpec((1, S, SEG), lambda b, r: (b, 0, 3)),
                  pl.BlockSpec((1, L, SEG), lambda b, r: (b, 0, 0)),
                  pl.BlockSpec((1, L, SEG), lambda b, r: (b, 0, 1)),
                  pl.BlockSpec(bias_tbl.shape, lambda b, r: (0, 0, 0))],
        out_specs=pl.BlockSpec((1, GRID_W, SEG), lambda b, r: (b, r, 0)),
        out_shape=jax.ShapeDtypeStruct((B, S, NA_WIDTH), BF16),
        scratch_shapes=[pltpu.VMEM((rows, NA_WIDTH, GRID_W), BF16),
                        pltpu.VMEM((NA_WIDTH, L), BF16)],
        compiler_params=_cparams(("arbitrary", "arbitrary")),
        name="na",
    )(proj, proj, proj, proj, cproj, cproj, bias_tbl)


def _split3(g):
    g1 = g.astype(BF16)
    r1 = g - g1.astype(F32)
    g2 = r1.astype(BF16)
    g3 = (r1 - g2.astype(F32)).astype(BF16)
    return g1, g2, g3


def _cumdot(c_bf, g):
    g1, g2, g3 = _split3(g)
    d = lambda a: jnp.dot(c_bf, a, preferred_element_type=F32)
    return (d(g3) + d(g2)) + d(g1)


def _forget(pre, lb):
    f = lb + (1.0 - lb) * _sigmoid(pre.astype(F32))
    return 1.0 - f, jnp.log(f)


def _hg_consts():
    c = HG_CHUNK
    t = np.arange(c)[:, None]
    u = np.arange(c)[None, :]
    cm, role, masks = [], [], []
    for fwd in (True, False):
        blocks = [(u <= t) if fwd else (u >= t)]
        roles, ms = [], []
        for m in HG_LEVELS:
            blk = t // (2 * m)
            if fwd:
                mid = 2 * m * blk + m - 1
                is_q = (t % (2 * m)) >= m
                expo = np.where(is_q, (u > mid) & (u <= t), (u > t) & (u <= mid))
            else:
                mid = 2 * m * blk + m
                is_q = (t % (2 * m)) < m
                expo = np.where(is_q, (u >= t) & (u < mid), (u >= mid) & (u < t))
            blocks.append(expo)
            roles.append(np.broadcast_to(is_q, (c, 2 * HG_DK)))
            ms.append((blk == blk.T) & is_q & ~is_q.T)
        ms.append(t == u)
        full = np.concatenate(blocks, axis=0).astype(np.float32)
        cm.append(np.concatenate([full, full], axis=1))
        role.append(np.stack(roles).astype(np.float32))
        masks.append(np.stack([np.concatenate([x, x], axis=1) for x in ms]).astype(np.float32))
    ones_bd = np.kron(np.eye(2, dtype=np.float32), np.ones((HG_DK, c), np.float32))
    return (jnp.asarray(np.stack(cm), BF16), jnp.asarray(np.stack(role), BF16),
            jnp.asarray(np.stack(masks)), jnp.asarray(ones_bd, BF16))


def _block_diag(x, zeros):
    return jnp.concatenate([jnp.concatenate([x[:, :HG_DK], zeros], axis=1),
                            jnp.concatenate([zeros, x[:, HG_DK:]], axis=1)], axis=0)


def _hg_pair_dir(q, v, k, gh, gl, cmat_ref, role_ref, masks_ref, ones_ref, st_ref, di, pair):
    c = HG_CHUNK
    nl = len(HG_LEVELS)
    nt = (((1,), (1,)), ((), ()))
    tn = (((0,), (0,)), ((), ()))
    dall = jnp.dot(cmat_ref[di], jnp.concatenate([gh, gl], axis=0), preferred_element_type=F32)
    b = dall[0:c]
    bend = b[c - 1:c] if di == 0 else b[0:1]
    qe = q * jnp.exp2(b).astype(BF16)
    ke = k * jnp.exp2(bend - b).astype(BF16)
    zeros = jnp.zeros((c, HG_DK), BF16)
    a = None
    for li in range(nl):
        e = jnp.exp2(dall[(li + 1) * c:(li + 2) * c]).astype(BF16)
        x = jnp.where(role_ref[di, li] > 0.5, q, k) * e
        r = lax.dot_general(x, _block_diag(x, zeros), nt, preferred_element_type=F32) * masks_ref[di, li]
        a = r if a is None else a + r
    a = a + jnp.dot(q * k, ones_ref[...], preferred_element_type=F32) * masks_ref[di, nl]
    o = jnp.dot(a.astype(BF16), _block_diag(v, zeros), preferred_element_type=F32)
    outs = []
    for hh in range(2):
        hs = slice(hh * HG_DK, (hh + 1) * HG_DK)
        st = st_ref[di, 2 * pair + hh]
        outs.append(o[:, hs] + lax.dot_general(qe[:, hs], st.astype(BF16), nt, preferred_element_type=F32))
        upd = lax.dot_general(v[:, hs], ke[:, hs], tn, preferred_element_type=F32)
        st_ref[di, 2 * pair + hh] = jnp.exp2(bend[:, hs]) * st + upd
    return outs


def _hgrn_kernel(sq_ref, hi_ref, kf_ref, gfh_ref, gfl_ref, kb_ref, gbh_ref, gbl_ref, hg_ref,
                 ci_ref, cff_ref, cfb_ref, lb_ref, gain_ref, cmat_ref, role_ref, masks_ref, ones_ref,
                 ustrict_ref, lstrict_ref, o_ref, acc_ref, st_ref, *, n_chunks):
    c = HG_CHUNK
    tn = (((0,), (0,)), ((), ()))

    for h in range(HG_HEADS):
        hs = slice(h * HG_DK, (h + 1) * HG_DK)
        ci = ci_ref[0, :, hs]
        for di, (pre_ref, cm_ref) in enumerate(((cff_ref, ustrict_ref), (cfb_ref, lstrict_ref))):
            k, g = _forget(pre_ref[0, :, hs], lb_ref[di:di + 1, hs])
            ke = (k * jnp.exp(_cumdot(cm_ref[...], g))).astype(BF16)
            st_ref[di, h] = lax.dot_general(ci, ke, tn, preferred_element_type=F32)

    def step(i, second):
        rows = (pl.ds(pl.multiple_of(i * c, c), c), pl.ds(pl.multiple_of((n_chunks - 1 - i) * c, c), c))
        for pair in range(HG_HEADS // 2):
            cols = slice(2 * pair * HG_DK, 2 * (pair + 1) * HG_DK)
            for di, rr in enumerate(rows):
                k_ref, gh_ref, gl_ref = (kf_ref, gfh_ref, gfl_ref) if di == 0 else (kb_ref, gbh_ref, gbl_ref)
                outs = _hg_pair_dir(sq_ref[0, rr, cols], hi_ref[0, rr, cols], k_ref[0, rr, cols],
                                    gh_ref[0, rr, cols], gl_ref[0, rr, cols],
                                    cmat_ref, role_ref, masks_ref, ones_ref, st_ref, di, pair)
                for hh, o in enumerate(outs):
                    hs = slice((2 * pair + hh) * HG_DK, (2 * pair + hh + 1) * HG_DK)
                    if not second:
                        acc_ref[rr, hs] = o
                    else:
                        ot = acc_ref[rr, hs] + o
                        y = ot * lax.rsqrt(jnp.mean(ot * ot, axis=-1, keepdims=True) + EPS) * gain_ref[:, hs]
                        o_ref[0, rr, hs] = (y * _silu(hg_ref[0, rr, hs].astype(F32))).astype(BF16)

    def first_half(i, carry):
        step(i, False)
        return carry

    def second_half(i, carry):
        step(i, True)
        return carry

    lax.fori_loop(0, n_chunks // 2, first_half, 0)
    lax.fori_loop(n_chunks // 2, n_chunks, second_half, 0)


def _hgrn(proj, cproj, lb, gain):
    B, S, _ = proj.shape
    L = cproj.shape[1]
    c = HG_CHUNK
    nl = len(HG_LEVELS)
    tri_l = np.tril(np.ones((L, L), np.float32), -1)
    consts = list(_hg_consts()) + [jnp.asarray(tri_l.T, BF16), jnp.asarray(tri_l, BF16)]
    seq = lambda j: pl.BlockSpec((1, S, SEG), lambda b: (b, 0, j))
    cseq = lambda j: pl.BlockSpec((1, L, SEG), lambda b: (b, 0, j))
    full2 = lambda n, m: pl.BlockSpec((n, m), lambda b: (0, 0))
    return pl.pallas_call(
        functools.partial(_hgrn_kernel, n_chunks=S // c),
        grid=(B,),
        in_specs=[seq(P_HQ), seq(P_HI), seq(P_KF), seq(P_GFH), seq(P_GFL), seq(P_KB), seq(P_GBH), seq(P_GBL),
                  seq(P_HG), cseq(2), cseq(3), cseq(4),
                  full2(2, HG_WIDTH), full2(1, HG_WIDTH),
                  pl.BlockSpec((2, (nl + 1) * c, 2 * c), lambda b: (0, 0, 0)),
                  pl.BlockSpec((2, nl, c, 2 * HG_DK), lambda b: (0, 0, 0, 0)),
                  pl.BlockSpec((2, nl + 1, c, 2 * c), lambda b: (0, 0, 0, 0)),
                  full2(2 * HG_DK, 2 * c), full2(L, L), full2(L, L)],
        out_specs=pl.BlockSpec((1, S, HG_WIDTH), lambda b: (b, 0, 0)),
        out_shape=jax.ShapeDtypeStruct((B, S, HG_WIDTH), BF16),
        scratch_shapes=[pltpu.VMEM((S, HG_WIDTH), F32),
                        pltpu.VMEM((2, HG_HEADS, HG_DK, HG_DK), F32)],
        compiler_params=_cparams(("arbitrary",)),
        name="hgrn",
    )(*([proj] * 9), cproj, cproj, cproj, lb, gain, *consts)


ROUTE_E_ROW = 8


def _route_t(lt):
    tm = lt.shape[1]
    row8 = lax.broadcasted_iota(jnp.int32, (SUBLANES, tm), 0).astype(F32)
    gl = jnp.where(row8 < N_GROUPS, lt[0:SUBLANES], -jnp.inf)
    gmax = jnp.max(gl, axis=0, keepdims=True)
    g_sel = jnp.min(jnp.where(gl == gmax, row8, float(SUBLANES)), axis=0, keepdims=True)
    g_w = 1.0 / jnp.sum(jnp.exp(gl - gmax), axis=0, keepdims=True)
    row16 = lax.broadcasted_iota(jnp.int32, (N_EXPERTS, tm), 0).astype(F32)
    first = g_sel * EXPERTS_PER_GROUP
    in_grp = (row16 >= first) & (row16 < first + EXPERTS_PER_GROUP)
    e1 = jnp.where(in_grp, lt[ROUTE_E_ROW:ROUTE_E_ROW + N_EXPERTS], -jnp.inf)
    v1 = jnp.max(e1, axis=0, keepdims=True)
    i1 = jnp.min(jnp.where(e1 == v1, row16, float(N_EXPERTS)), axis=0, keepdims=True)
    e2 = jnp.where(row16 == i1, -jnp.inf, e1)
    v2 = jnp.max(e2, axis=0, keepdims=True)
    i2 = jnp.min(jnp.where(e2 == v2, row16, float(N_EXPERTS)), axis=0, keepdims=True)
    t = jnp.exp(v2 - v1)
    w1 = g_w / (1.0 + t)
    w2 = g_w * t / (1.0 + t)
    onehot = jnp.where(row8 == g_sel, 1.0, 0.0)
    gates = jnp.where(row8 == i1 - first, w1, jnp.where(row8 == i2 - first, w2, 0.0))
    return onehot, gates


def _outproj_kernel(x_ref, na_ref, hg_ref, w_ref, ga_ref, g_ref, sc_ref, sh_ref, wrh_ref, wrl_ref, br_ref,
                    x1_ref, hx_ref, oh_ref, cnt_ref):
    tm, d = x_ref.shape
    nt = (((1,), (1,)), ((), ()))
    mix = (jnp.dot(na_ref[...], w_ref[0:NA_WIDTH, :], preferred_element_type=F32)
           + jnp.dot(hg_ref[...], w_ref[NA_WIDTH:, :], preferred_element_type=F32))
    x1 = x_ref[...] + ga_ref[0] * mix
    x1_ref[...] = x1
    h2 = _norm_mod(x1, g_ref[...], sc_ref[0], sh_ref[0])
    h_hi = h2.astype(BF16)
    h_lo = (h2 - h_hi.astype(F32)).astype(BF16)
    hx_ref[:, 0:d] = h_hi
    lt = ((lax.dot_general(wrl_ref[...], h_hi, nt, preferred_element_type=F32)
           + lax.dot_general(wrh_ref[...], h_lo, nt, preferred_element_type=F32))
          + lax.dot_general(wrh_ref[...], h_hi, nt, preferred_element_type=F32)) + br_ref[...]
    onehot_t, gates_t = _route_t(lt)
    tok = jnp.concatenate([onehot_t, gates_t, jnp.zeros((LANES - 2 * SUBLANES, tm), F32)], axis=0).T
    lane = lax.broadcasted_iota(jnp.int32, tok.shape, 1)
    onehot = jnp.where(lane < ROUTE_E_ROW, tok, 0.0)
    gates = jnp.where(lane >= ROUTE_E_ROW, tok, 0.0)
    for j, piece in enumerate(_split3(gates)):
        hx_ref[:, d + j * LANES:d + (j + 1) * LANES] = piece
    oh_ref[...] = onehot
    cnt_ref[0] = jnp.broadcast_to(jnp.sum(onehot, axis=0, keepdims=True), (SUBLANES, LANES))


def _outproj(x2d, na2d, hg2d, w_out_bf, ga, g, sc, sh, wr_hi, wr_lo, br, S):
    T, D = x2d.shape
    tm = MOE_TM
    per = S // tm
    tok = lambda w: pl.BlockSpec((tm, w), lambda i: (i, 0))
    bat = pl.BlockSpec((1, 1, D), lambda i: (i // per, 0, 0))
    return pl.pallas_call(
        _outproj_kernel,
        grid=(T // tm,),
        in_specs=[tok(D), tok(NA_WIDTH), tok(HG_WIDTH),
                  pl.BlockSpec((NA_WIDTH + HG_WIDTH, D), lambda i: (0, 0)),
                  bat, pl.BlockSpec((1, D), lambda i: (0, 0)), bat, bat,
                  pl.BlockSpec((LANES, D), lambda i: (0, 0)),
                  pl.BlockSpec((LANES, D), lambda i: (0, 0)),
                  pl.BlockSpec((LANES, 1), lambda i: (0, 0))],
        out_specs=[tok(D), tok(MOE_XW), tok(LANES),
                   pl.BlockSpec((1, SUBLANES, LANES), lambda i: (i, 0, 0))],
        out_shape=[jax.ShapeDtypeStruct((T, D), F32), jax.ShapeDtypeStruct((T, MOE_XW), BF16),
                   jax.ShapeDtypeStruct((T, LANES), F32),
                   jax.ShapeDtypeStruct((T // tm, SUBLANES, LANES), F32)],
        compiler_params=_cparams(("arbitrary",)),
        name="outproj",
    )(x2d, na2d, hg2d, w_out_bf, ga, g, sc, sh, wr_hi, wr_lo, br)


MOE_TM = 512
MOE_ALIGN = 16
MOE_LOC = 640
MOE_XW = D_MODEL + 3 * LANES
MOE_TE = 512
MOE_BITS = tuple(1 << b for b in range(9, 3, -1))


def _moe_steps(T):
    worst = T + (T // MOE_TM) * N_GROUPS * (MOE_ALIGN - 1)
    return -(-worst // MOE_TE) + N_GROUPS


def _local_positions(oh_ref, lstrict_ref, cnt_ref, k):
    onehot = oh_ref[...]
    ranks = jnp.dot(lstrict_ref[...], onehot.astype(BF16), preferred_element_type=F32)
    lane = lax.broadcasted_iota(jnp.int32, (1, LANES), 1)
    base = jnp.zeros((1, LANES), F32)
    o = jnp.int32(0)
    for g in range(N_GROUPS):
        base = jnp.where(lane == g, o.astype(F32), base)
        o = o + cnt_ref[k * N_GROUPS + g]
    return jnp.sum(onehot * (ranks + base), axis=-1, keepdims=True)


def _sort_matrix(lpos):
    col = lax.broadcasted_iota(jnp.int32, (MOE_TM, MOE_LOC), 1).astype(F32)
    return jnp.where(lpos == col, 1.0, 0.0).astype(BF16)


def _block_copies(n_rows, src_row, dst_row, bits, make_copy, action):
    for bit in bits:
        part = n_rows & (-2 * bit)

        @pl.when((n_rows & bit) != 0)
        def _(part=part, bit=bit):
            action(make_copy(pl.multiple_of(src_row + part, MOE_ALIGN), pl.multiple_of(dst_row + part, MOE_ALIGN), bit))


def _run_copies(off_ref, cnt_ref, k, make_copy, action):
    o = jnp.int32(0)
    for g in range(N_GROUPS):
        c = cnt_ref[k * N_GROUPS + g]
        _block_copies(c, o, off_ref[k * N_GROUPS + g], MOE_BITS, make_copy, action)
        o = o + c


def _dispatch_kernel(off_ref, cnt_ref, tail_ref, hx_ref, oh_ref, lstrict_ref, hs_ref, buf_ref, zero_ref, sem,
                     *, n_tiles, n_steps):
    k = pl.program_id(0)
    slot = k % 2

    def copies(kk, sl, action):
        def make(src_row, dst_row, n):
            return pltpu.make_async_copy(buf_ref.at[sl, pl.ds(src_row, n)], hs_ref.at[pl.ds(dst_row, n)], sem.at[sl])
        _run_copies(off_ref, cnt_ref, kk, make, action)

    @pl.when(k >= 2)
    def _():
        copies(k - 2, slot, lambda cp: cp.wait())

    pt = _sort_matrix(_local_positions(oh_ref, lstrict_ref, cnt_ref, k))
    srt = lax.dot_general(pt, hx_ref[...], (((0,), (0,)), ((), ())), preferred_element_type=F32)
    buf_ref[slot] = srt.astype(BF16)
    copies(k, slot, lambda cp: cp.start())

    @pl.when(k == n_tiles - 1)
    def _():
        zero_ref[...] = jnp.zeros_like(zero_ref)
        n_used = tail_ref[2 * N_GROUPS]

        def zero_copy(src_row, dst_row, n):
            return pltpu.make_async_copy(zero_ref.at[pl.ds(src_row, n)], hs_ref.at[pl.ds(dst_row, n)], sem.at[2])

        def tile_copy(i):
            return zero_copy(0, pl.multiple_of(i * MOE_TE, MOE_TE), MOE_TE)

        def fills(action):
            for g in range(N_GROUPS):
                _block_copies(tail_ref[N_GROUPS + g], jnp.int32(0), tail_ref[g], MOE_BITS[1:], zero_copy, action)

        fills(lambda cp: cp.start())
        lax.fori_loop(n_used, n_steps, lambda i, c: (tile_copy(i).start(), c)[1], 0)
        if n_tiles >= 2:
            copies(k - 1, 1 - slot, lambda cp: cp.wait())
        copies(k, slot, lambda cp: cp.wait())
        fills(lambda cp: cp.wait())
        lax.fori_loop(n_used, n_steps, lambda i, c: (tile_copy(i).wait(), c)[1], 0)


def _dispatch(off, cnt, tail, hx, onehot, lstrict, n_steps):
    T = hx.shape[0]
    n_tiles = T // MOE_TM
    return pl.pallas_call(
        functools.partial(_dispatch_kernel, n_tiles=n_tiles, n_steps=n_steps),
        grid_spec=pltpu.PrefetchScalarGridSpec(
            num_scalar_prefetch=3,
            grid=(n_tiles,),
            in_specs=[pl.BlockSpec((MOE_TM, MOE_XW), lambda k, *_: (k, 0)),
                      pl.BlockSpec((MOE_TM, LANES), lambda k, *_: (k, 0)),
                      pl.BlockSpec((MOE_TM, MOE_TM), lambda k, *_: (0, 0))],
            out_specs=pl.BlockSpec(memory_space=pl.ANY),
            scratch_shapes=[pltpu.VMEM((2, MOE_LOC, MOE_XW), BF16),
                            pltpu.VMEM((MOE_TE, MOE_XW), BF16),
                            pltpu.SemaphoreType.DMA((3,))]),
        out_shape=jax.ShapeDtypeStruct((n_steps * MOE_TE, MOE_XW), BF16),
        compiler_params=_cparams(("arbitrary",)),
        name="dispatch",
    )(off, cnt, tail, hx, onehot, lstrict)


def _experts_kernel(grp_ref, used_ref, hs_ref, w1_ref, w3_ref, w2_ref, ys_ref):
    i = pl.program_id(0)
    d = ys_ref.shape[1]
    ne = EXPERTS_PER_GROUP

    @pl.when(i < used_ref[0])
    def _():
        t = hs_ref[:, 0:d]
        gates = ((hs_ref[:, d + 2 * LANES:d + 3 * LANES].astype(F32) + hs_ref[:, d + LANES:d + 2 * LANES].astype(F32))
                 + hs_ref[:, d:d + LANES].astype(F32))
        lane = lax.broadcasted_iota(jnp.int32, gates.shape, 1)
        acc = None
        for j in range(ne):
            a = jnp.dot(t, w1_ref[0, j], preferred_element_type=F32)
            b = jnp.dot(t, w3_ref[0, j], preferred_element_type=F32)
            gj = jnp.sum(jnp.where(lane == ROUTE_E_ROW + j, gates, 0.0), axis=-1, keepdims=True)
            y = gj * jnp.dot((_silu(a) * b).astype(BF16), w2_ref[0, j], preferred_element_type=F32)
            acc = y if acc is None else acc + y
        ys_ref[...] = acc.astype(BF16)

    @pl.when(i >= used_ref[0])
    def _():
        ys_ref[...] = jnp.zeros_like(ys_ref)


def _experts(grp, used, hs, w1_bf, w3_bf, w2_bf):
    rows, _ = hs.shape
    D = w2_bf.shape[3]
    wspec = lambda w: pl.BlockSpec((1,) + w.shape[1:], lambda i, grp, used: (grp[i], 0, 0, 0))
    return pl.pallas_call(
        _experts_kernel,
        grid_spec=pltpu.PrefetchScalarGridSpec(
            num_scalar_prefetch=2,
            grid=(rows // MOE_TE,),
            in_specs=[pl.BlockSpec((MOE_TE, MOE_XW), lambda i, grp, used: (i, 0)),
                      wspec(w1_bf), wspec(w3_bf), wspec(w2_bf)],
            out_specs=pl.BlockSpec((MOE_TE, D), lambda i, grp, used: (i, 0))),
        out_shape=jax.ShapeDtypeStruct((rows, D), BF16),
        compiler_params=_cparams(("arbitrary",)),
        name="experts",
    )(grp, used, hs, w1_bf, w3_bf, w2_bf)


def _combine_kernel(off_ref, cnt_ref, x1_ref, oh_ref, lstrict_ref, ga_ref, gf_ref, ys_ref, o_ref, buf_ref, sem,
                    *, n_tiles):
    k = pl.program_id(0)
    slot = k % 2

    def copies(kk, sl, action):
        def make(loc_row, seg_row, n):
            return pltpu.make_async_copy(ys_ref.at[pl.ds(seg_row, n)], buf_ref.at[sl, pl.ds(loc_row, n)], sem.at[sl])
        _run_copies(off_ref, cnt_ref, kk, make, action)

    @pl.when(k == 0)
    def _():
        buf_ref[...] = jnp.zeros_like(buf_ref)
        copies(k, slot, lambda cp: cp.start())

    @pl.when(k + 1 < n_tiles)
    def _():
        copies(k + 1, 1 - slot, lambda cp: cp.start())

    pt = _sort_matrix(_local_positions(oh_ref, lstrict_ref, cnt_ref, k))
    copies(k, slot, lambda cp: cp.wait())
    y = jnp.dot(pt, buf_ref[slot], preferred_element_type=F32)
    x2 = x1_ref[...] + ga_ref[0] * y
    o_ref[...] = x2 * lax.rsqrt(jnp.mean(x2 * x2, axis=-1, keepdims=True) + EPS) * gf_ref[...]


def _combine(off, cnt, x1, onehot, lstrict, ga, gfin, ys, S):
    T, D = x1.shape
    n_tiles = T // MOE_TM
    per = S // MOE_TM
    return pl.pallas_call(
        functools.partial(_combine_kernel, n_tiles=n_tiles),
        grid_spec=pltpu.PrefetchScalarGridSpec(
            num_scalar_prefetch=2,
            grid=(n_tiles,),
            in_specs=[pl.BlockSpec((MOE_TM, D), lambda k, off, cnt: (k, 0)),
                      pl.BlockSpec((MOE_TM, LANES), lambda k, off, cnt: (k, 0)),
                      pl.BlockSpec((MOE_TM, MOE_TM), lambda k, off, cnt: (0, 0)),
                      pl.BlockSpec((1, 1, D), lambda k, off, cnt: (k // per, 0, 0)),
                      pl.BlockSpec((1, D), lambda k, off, cnt: (0, 0)),
                      pl.BlockSpec(memory_space=pl.ANY)],
            out_specs=pl.BlockSpec((MOE_TM, D), lambda k, off, cnt: (k, 0)),
            scratch_shapes=[pltpu.VMEM((2, MOE_LOC, D), BF16),
                            pltpu.SemaphoreType.DMA((2,))]),
        out_shape=jax.ShapeDtypeStruct((T, D), F32),
        compiler_params=_cparams(("arbitrary",)),
        name="combine",
    )(off, cnt, x1, onehot, lstrict, ga, gfin, ys)


def _moe_schedule(cnt_tiles, n_steps):
    cnt = ((cnt_tiles + (MOE_ALIGN - 1)) // MOE_ALIGN) * MOE_ALIGN
    ends = jnp.cumsum(cnt, axis=0)
    total = ends[-1]
    ntile = (total + MOE_TE - 1) // MOE_TE
    cum = jnp.cumsum(ntile)
    base = (cum - ntile) * MOE_TE
    off = base[None, :] + ends - cnt
    i = jnp.arange(n_steps, dtype=jnp.int32)
    grp = jnp.minimum(jnp.sum(i[:, None] >= cum[None, :], axis=1), N_GROUPS - 1)
    grp = jnp.where(i < cum[-1], grp, grp[jnp.maximum(cum[-1] - 1, 0)])
    tail = jnp.concatenate([base + total, ntile * MOE_TE - total, cum[-1:]])
    as_i32 = lambda a: a.reshape(-1).astype(jnp.int32)
    return as_i32(off), as_i32(cnt), as_i32(tail), as_i32(grp), as_i32(cum[-1:])


def _rope_tables(S):
    t = jnp.arange(S)
    pos = jnp.stack([t // GRID_W, t % GRID_W], axis=-1).astype(F32)
    inv = ROPE_BASE ** (-jnp.arange(0, ROPE_AXIS_DIM, 2, dtype=F32) / ROPE_AXIS_DIM)
    ang = pos[:, :, None] * inv
    cos, sin = jnp.cos(ang), jnp.sin(ang)
    cos_h = jnp.concatenate([cos, cos], axis=-1).reshape(S, NA_HEAD_DIM)
    sin_h = jnp.concatenate([-sin, sin], axis=-1).reshape(S, NA_HEAD_DIM)
    return jnp.tile(cos_h, (1, NA_HEADS)), jnp.tile(sin_h, (1, NA_HEADS))


def kernel(x, c, ctx, c_ctx, w_mod, b_mod, norm_mix, norm_ffn, w_in, w_out, na_rpb, hg_lb, hg_norm,
           w_grp, b_grp, w_exp, b_exp, w1, w3, w2, norm_final):
    B, S, D = x.shape
    T = B * S
    assert w_mod.shape[0] == 1, "single-layer kernel"

    rows = -(-(B + 1) // SUBLANES) * SUBLANES
    cc = jnp.zeros((rows, D), F32).at[:B].set(c).at[B].set(c_ctx)
    mod = _modulation(cc, w_mod[0], b_mod[0])
    sh_a, sc_a, ga_a, sh_f, sc_f, ga_f = [m.reshape(B, 1, D) for m in jnp.split(mod[:B], 6, axis=-1)]
    csh_a, csc_a = [m.reshape(1, D) for m in jnp.split(mod[B], 6)[:2]]

    w_in_bf = w_in[0].astype(BF16)
    w_ctx_bf = jnp.concatenate([w_in_bf[:, i * SEG:(i + 1) * SEG] for i in (1, 2, 4, 5, 6)], axis=1)
    cos_t, sin_t = _rope_tables(S)
    g_mix = norm_mix[0].reshape(1, D)

    lb = jnp.cumsum(jax.nn.softmax(hg_lb.astype(F32), axis=0), axis=0)[0]
    proj = _project(x, g_mix, sc_a, sh_a, w_in_bf, cos_t, sin_t, lb, tm=512)
    cproj = _project_ctx(ctx, g_mix, csc_a, csh_a, w_ctx_bf)

    na_out = _neighbourhood_attention(proj, cproj, _na_bias_table(na_rpb[0]))

    gain = jnp.tile(hg_norm[0].astype(F32), HG_HEADS).reshape(1, HG_WIDTH)
    hg_out = _hgrn(proj, cproj, lb, gain)

    e_rows = slice(ROUTE_E_ROW, ROUTE_E_ROW + N_EXPERTS)
    wr = jnp.zeros((LANES, D), F32).at[:N_GROUPS].set(w_grp[0].T).at[e_rows].set(w_exp[0].T)
    br = jnp.zeros((LANES, 1), F32).at[:N_GROUPS, 0].set(b_grp[0]).at[e_rows, 0].set(b_exp[0])
    hi_f32 = lax.bitcast_convert_type(lax.bitcast_convert_type(wr, jnp.uint32) & jnp.uint32(0xFFFF0000), F32)
    wr_hi = hi_f32.astype(BF16)
    wr_lo = (wr - hi_f32).astype(BF16)
    x1, hx, onehot, cnt_tiles = _outproj(x.reshape(T, D), na_out.reshape(T, NA_WIDTH), hg_out.reshape(T, HG_WIDTH),
                                         w_out[0].astype(BF16), ga_a, norm_ffn[0].reshape(1, D), sc_f, sh_f,
                                         wr_hi, wr_lo, br, S=S)

    n_steps = _moe_steps(T)
    off, cnt, tail, grp, used = _moe_schedule(cnt_tiles[:, 0, :N_GROUPS].astype(jnp.int32), n_steps)
    lstrict = jnp.asarray(np.tril(np.ones((MOE_TM, MOE_TM), np.float32), -1), BF16)
    by_group = lambda w: w[0].astype(BF16).reshape(N_GROUPS, EXPERTS_PER_GROUP, *w.shape[2:])
    hs = _dispatch(off, cnt, tail, hx, onehot, lstrict, n_steps)
    ys = _experts(grp, used, hs, by_group(w1), by_group(w3), by_group(w2))
    out = _combine(off, cnt, x1, onehot, lstrict, ga_f, norm_final.reshape(1, D), ys, S)
    return out.reshape(B, S, D)
```

```python
import functools

import numpy as np
import jax
import jax.numpy as jnp
from jax import lax
from jax.experimental import pallas as pl
from jax.experimental.pallas import tpu as pltpu

F32 = jnp.float32
BF16 = jnp.bfloat16
HIGHEST = lax.Precision.HIGHEST

D_MODEL = 1024
GRID_W = 64
NA_HEADS = 8
NA_HEAD_DIM = 64
NA_WIDTH = NA_HEADS * NA_HEAD_DIM
NA_KH = 8
NA_KW = 16
ROPE_AXIS_DIM = NA_HEAD_DIM // 2
ROPE_BASE = 10000.0
HG_HEADS = 4
HG_DK = 128
HG_WIDTH = HG_HEADS * HG_DK
HG_CHUNK = 64
SEG = 512
N_GROUPS = 4
EXPERTS_PER_GROUP = 4
N_EXPERTS = N_GROUPS * EXPERTS_PER_GROUP
D_EXPERT = 512
EPS = 1e-6
NEG = -1e30
LOG2E = 1.4426950408889634
LANES = 128
SUBLANES = 8
VMEM_LIMIT = 56 * 1024 * 1024

HG_LEVELS = (32, 16, 8, 4, 2, 1)


def _cparams(sem):
    return pltpu.CompilerParams(dimension_semantics=sem, vmem_limit_bytes=VMEM_LIMIT)


def _sigmoid(x):
    return 1.0 / (1.0 + jnp.exp(-x))


def _silu(x):
    return x * _sigmoid(x)


def _mod_kernel(c_ref, w_ref, b_ref, o_ref):
    s = _silu(c_ref[...])
    o_ref[...] = jnp.dot(s, w_ref[...], precision=HIGHEST, preferred_element_type=F32) + b_ref[...]


def _modulation(cc, w_mod, b_mod):
    rows, d = cc.shape
    n = w_mod.shape[1]
    tn = 1024
    return pl.pallas_call(
        _mod_kernel,
        grid=(n // tn,),
        in_specs=[pl.BlockSpec((rows, d), lambda j: (0, 0)),
                  pl.BlockSpec((d, tn), lambda j: (0, j)),
                  pl.BlockSpec((1, tn), lambda j: (0, j))],
        out_specs=pl.BlockSpec((rows, tn), lambda j: (0, j)),
        out_shape=jax.ShapeDtypeStruct((rows, n), F32),
        compiler_params=_cparams(("arbitrary",)),
        name="mod",
    )(cc, w_mod, b_mod.reshape(1, n))


def _norm_mod(x, g, sc, sh):
    y = x * lax.rsqrt(jnp.mean(x * x, axis=-1, keepdims=True) + EPS)
    return (y * g) * (1.0 + sc) + sh


def _rope(a, cos, sin):
    lane = lax.broadcasted_iota(jnp.int32, a.shape, 1)
    first = (lane % ROPE_AXIS_DIM) < (ROPE_AXIS_DIM // 2)
    up = pltpu.roll(a, LANES - ROPE_AXIS_DIM // 2, axis=1)
    dn = pltpu.roll(a, ROPE_AXIS_DIM // 2, axis=1)
    return a * cos + jnp.where(first, up, dn) * sin


P_QRAW, P_QROT, P_KROT, P_V, P_HQ, P_HI, P_KF, P_GFH, P_GFL, P_KB, P_GBH, P_GBL, P_HG = range(13)
P_SEGS = 13


def _proj_kernel(x_ref, g_ref, sc_ref, sh_ref, w_ref, cos_ref, sin_ref, lb_ref, o_ref):
    h = _norm_mod(x_ref[0], g_ref[...], sc_ref[0], sh_ref[0]).astype(BF16)
    scale = NA_HEAD_DIM ** -0.5 * LOG2E

    def put(seg, val):
        o_ref[0, :, seg * SEG:(seg + 1) * SEG] = val.astype(BF16)

    for j in range(8):
        acc = jnp.dot(h, w_ref[:, j * SEG:(j + 1) * SEG], preferred_element_type=F32)
        if j <= 1:
            rot = jnp.concatenate(
                [_rope(acc[:, p * LANES:(p + 1) * LANES], cos_ref[:, p * LANES:(p + 1) * LANES],
                       sin_ref[:, p * LANES:(p + 1) * LANES]) for p in range(SEG // LANES)], axis=1)
            if j == 0:
                put(P_QRAW, acc * scale)
                put(P_QROT, rot * scale)
            else:
                put(P_KROT, rot)
        elif j == 2:
            put(P_V, acc)
        elif j == 3:
            put(P_HQ, _silu(acc))
        elif j == 4:
            put(P_HI, acc)
        elif j in (5, 6):
            lb = lb_ref[j - 5:j - 4, :]
            f = lb + (1.0 - lb) * _sigmoid(acc)
            g2 = jnp.log2(f)
            hi = g2.astype(BF16)
            base = P_KF if j == 5 else P_KB
            put(base, 1.0 - f)
            put(base + 1, hi)
            put(base + 2, g2 - hi.astype(F32))
        else:
            put(P_HG, acc)


def _project(x, g, sc, sh, w_bf, cos_t, sin_t, lb, tm):
    B, S, D = x.shape
    return pl.pallas_call(
        _proj_kernel,
        grid=(S // tm, B),
        in_specs=[pl.BlockSpec((1, tm, D), lambda s, b: (b, s, 0)),
                  pl.BlockSpec((1, D), lambda s, b: (0, 0)),
                  pl.BlockSpec((1, 1, D), lambda s, b: (b, 0, 0)),
                  pl.BlockSpec((1, 1, D), lambda s, b: (b, 0, 0)),
                  pl.BlockSpec((D, 8 * SEG), lambda s, b: (0, 0)),
                  pl.BlockSpec((tm, SEG), lambda s, b: (s, 0)),
                  pl.BlockSpec((tm, SEG), lambda s, b: (s, 0)),
                  pl.BlockSpec((2, HG_WIDTH), lambda s, b: (0, 0))],
        out_specs=pl.BlockSpec((1, tm, P_SEGS * SEG), lambda s, b: (b, s, 0)),
        out_shape=jax.ShapeDtypeStruct((B, S, P_SEGS * SEG), BF16),
        compiler_params=_cparams(("arbitrary", "arbitrary")),
        name="proj",
    )(x, g, sc, sh, w_bf, cos_t, sin_t, lb)


def _ctxproj_kernel(x_ref, g_ref, sc_ref, sh_ref, w_ref, o_ref):
    h = _norm_mod(x_ref[0], g_ref[...], sc_ref[...], sh_ref[...]).astype(BF16)
    for j in range(5):
        acc = jnp.dot(h, w_ref[:, j * SEG:(j + 1) * SEG], preferred_element_type=F32)
        o_ref[0, :, j * SEG:(j + 1) * SEG] = acc.astype(BF16)


def _project_ctx(ctx, g, sc, sh, w_bf):
    B, L, D = ctx.shape
    return pl.pallas_call(
        _ctxproj_kernel,
        grid=(B,),
        in_specs=[pl.BlockSpec((1, L, D), lambda b: (b, 0, 0)),
                  pl.BlockSpec((1, D), lambda b: (0, 0)),
                  pl.BlockSpec((1, D), lambda b: (0, 0)),
                  pl.BlockSpec((1, D), lambda b: (0, 0)),
                  pl.BlockSpec((D, 5 * SEG), lambda b: (0, 0))],
        out_specs=pl.BlockSpec((1, L, 5 * SEG), lambda b: (b, 0, 0)),
        out_shape=jax.ShapeDtypeStruct((B, L, 5 * SEG), BF16),
        compiler_params=_cparams(("arbitrary",)),
        name="ctxproj",
    )(ctx, g, sc, sh, w_bf)


def _na_kernel(qraw_ref, qrot_ref, k_ref, v_ref, ck_ref, cv_ref, bias_ref, o_ref, *, rows):
    r = pl.program_id(1)
    rs = jnp.clip(r - NA_KH // 2, 0, rows - NA_KH)
    start = pl.multiple_of(rs * GRID_W, GRID_W)
    bias_row = pl.multiple_of((rs - r + NA_KH - 1) * GRID_W, GRID_W)
    nk = NA_KH * GRID_W
    lane = lax.broadcasted_iota(jnp.int32, (GRID_W, LANES), 1)
    sel0 = lane < NA_HEAD_DIM
    nt = (((1,), (1,)), ((), ()))
    n_pairs = NA_WIDTH // LANES
    ones = jnp.ones((nk + ck_ref.shape[1], LANES), BF16)

    def scores(p):
        cols = slice(p * LANES, (p + 1) * LANES)
        qr = qrot_ref[0, :, cols]
        qw = qraw_ref[0, :, cols]
        zero = jnp.zeros_like(qr)
        qs_rot = jnp.concatenate([jnp.where(sel0, qr, zero), jnp.where(sel0, zero, qr)], axis=0)
        qs_raw = jnp.concatenate([jnp.where(sel0, qw, zero), jnp.where(sel0, zero, qw)], axis=0)
        s_loc = lax.dot_general(k_ref[0, pl.ds(start, nk), cols], qs_rot, nt,
                                preferred_element_type=F32) + bias_ref[p, pl.ds(bias_row, nk), :]
        s_ctx = lax.dot_general(ck_ref[0, :, cols], qs_raw, nt, preferred_element_type=F32)
        return s_loc, s_ctx

    def softmax(s_loc, s_ctx):
        m = jnp.maximum(jnp.max(s_loc, axis=0, keepdims=True), jnp.max(s_ctx, axis=0, keepdims=True))
        return jnp.concatenate([jnp.exp2(s_loc - m).T, jnp.exp2(s_ctx - m).T], axis=1).astype(BF16)

    def values(p, probs):
        cols = slice(p * LANES, (p + 1) * LANES)
        vals = jnp.concatenate([v_ref[0, pl.ds(start, nk), cols], cv_ref[0, :, cols]], axis=0)
        oe = jnp.dot(probs, jnp.concatenate([vals, ones], axis=1), preferred_element_type=F32)
        res = oe[:, :LANES] / oe[:, LANES:]
        o_ref[0, :, cols] = jnp.where(sel0, res[0:GRID_W], res[GRID_W:2 * GRID_W]).astype(BF16)

    s_all = [scores(p) for p in range(n_pairs)]
    probs = [softmax(*s_all[p]) for p in range(n_pairs)]
    for p in range(n_pairs):
        values(p, probs[p])


def _na_bias_table(rpb):
    qc = np.arange(GRID_W)[None, :]
    kc = np.arange(GRID_W)[:, None]
    cs = np.clip(qc - NA_KW // 2, 0, GRID_W - NA_KW)
    valid = (kc >= cs) & (kc < cs + NA_KW)
    col_off = np.clip(kc - qc + NA_KW - 1, 0, 2 * NA_KW - 2)
    t = (rpb.astype(F32) * LOG2E)[:, :, col_off]
    t = jnp.where(jnp.asarray(valid)[None, None], t, NEG)
    n_pairs = NA_HEADS // 2
    n_ro = 2 * NA_KH - 1
    t = t.reshape(n_pairs, 2, n_ro, GRID_W, GRID_W).transpose(0, 2, 3, 1, 4)
    return t.reshape(n_pairs, n_ro * GRID_W, 2 * GRID_W)


def _neighbourhood_attention(proj, cproj, bias_tbl):
    B, S, _ = proj.shape
    L = cproj.shape[1]
    rows = S // GRID_W
    return pl.pallas_call(
        functools.partial(_na_kernel, rows=rows),
        grid=(B, rows),
        in_specs=[pl.BlockSpec((1, GRID_W, SEG), lambda b, r: (b, r, 0)),
                  pl.BlockSpec((1, GRID_W, SEG), lambda b, r: (b, r, 1)),
                  pl.BlockSpec((1, S, SEG), lambda b, r: (b, 0, 2)),
                  pl.BlockSpec((1, S, SEG), lambda b, r: (b, 0, 3)),
                  pl.BlockSpec((1, L, SEG), lambda b, r: (b, 0, 0)),
                  pl.BlockSpec((1, L, SEG), lambda b, r: (b, 0, 1)),
                  pl.BlockSpec(bias_tbl.shape, lambda b, r: (0, 0, 0))],
        out_specs=pl.BlockSpec((1, GRID_W, SEG), lambda b, r: (b, r, 0)),
        out_shape=jax.ShapeDtypeStruct((B, S, NA_WIDTH), BF16),
        compiler_params=_cparams(("arbitrary", "arbitrary")),
        name="na",
    )(proj, proj, proj, proj, cproj, cproj, bias_tbl)


def _split3(g):
    g1 = g.astype(BF16)
    r1 = g - g1.astype(F32)
    g2 = r1.astype(BF16)
    g3 = (r1 - g2.astype(F32)).astype(BF16)
    return g1, g2, g3


def _cumdot(c_bf, g):
    g1, g2, g3 = _split3(g)
    d = lambda a: jnp.dot(c_bf, a, preferred_element_type=F32)
    return (d(g3) + d(g2)) + d(g1)


def _forget(pre, lb):
    f = lb + (1.0 - lb) * _sigmoid(pre.astype(F32))
    return 1.0 - f, jnp.log(f)


def _hg_consts():
    c = HG_CHUNK
    t = np.arange(c)[:, None]
    u = np.arange(c)[None, :]
    cm, role, masks = [], [], []
    for fwd in (True, False):
        blocks = [(u <= t) if fwd else (u >= t)]
        roles, ms = [], []
        for m in HG_LEVELS:
            blk = t // (2 * m)
            if fwd:
                mid = 2 * m * blk + m - 1
                is_q = (t % (2 * m)) >= m
                expo = np.where(is_q, (u > mid) & (u <= t), (u > t) & (u <= mid))
            else:
                mid = 2 * m * blk + m
                is_q = (t % (2 * m)) < m
                expo = np.where(is_q, (u >= t) & (u < mid), (u >= mid) & (u < t))
            blocks.append(expo)
            roles.append(np.broadcast_to(is_q, (c, 2 * HG_DK)))
            ms.append((blk == blk.T) & is_q & ~is_q.T)
        ms.append(t == u)
        full = np.concatenate(blocks, axis=0).astype(np.float32)
        cm.append(np.concatenate([full, full], axis=1))
        role.append(np.stack(roles).astype(np.float32))
        masks.append(np.stack([np.concatenate([x, x], axis=1) for x in ms]).astype(np.float32))
    ones_bd = np.kron(np.eye(2, dtype=np.float32), np.ones((HG_DK, c), np.float32))
    return (jnp.asarray(np.stack(cm), BF16), jnp.asarray(np.stack(role), BF16),
            jnp.asarray(np.stack(masks)), jnp.asarray(ones_bd, BF16))


def _block_diag(x, zeros):
    return jnp.concatenate([jnp.concatenate([x[:, :HG_DK], zeros], axis=1),
                            jnp.concatenate([zeros, x[:, HG_DK:]], axis=1)], axis=0)


def _hg_exponents(gh, gl, cmat_ref, di):
    return jnp.dot(cmat_ref[di], jnp.concatenate([gh, gl], axis=0), preferred_element_type=F32)


def _hg_intra(q, k, dall, role_ref, masks_ref, ones_ref, di):
    c = HG_CHUNK
    nl = len(HG_LEVELS)
    nt = (((1,), (1,)), ((), ()))
    b = dall[0:c]
    bend = b[c - 1:c] if di == 0 else b[0:1]
    qe = q * jnp.exp2(b).astype(BF16)
    ke = k * jnp.exp2(bend - b).astype(BF16)
    zeros = jnp.zeros((c, HG_DK), BF16)
    a = None
    for li in range(nl):
        e = jnp.exp2(dall[(li + 1) * c:(li + 2) * c]).astype(BF16)
        x = jnp.where(role_ref[di, li] > 0.5, q, k) * e
        r = lax.dot_general(x, _block_diag(x, zeros), nt, preferred_element_type=F32) * masks_ref[di, li]
        a = r if a is None else a + r
    a = a + jnp.dot(q * k, ones_ref[...], preferred_element_type=F32) * masks_ref[di, nl]
    return a.astype(BF16), qe, ke, bend


def _hg_outputs(a, qe, ke, bend, v, st_ref, di, pair):
    nt = (((1,), (1,)), ((), ()))
    tn = (((0,), (0,)), ((), ()))
    o = jnp.dot(a, _block_diag(v, jnp.zeros((HG_CHUNK, HG_DK), BF16)), preferred_element_type=F32)
    outs = []
    for hh in range(2):
        hs = slice(hh * HG_DK, (hh + 1) * HG_DK)
        st = st_ref[di, 2 * pair + hh]
        outs.append(o[:, hs] + lax.dot_general(qe[:, hs], st.astype(BF16), nt, preferred_element_type=F32))
        upd = lax.dot_general(v[:, hs], ke[:, hs], tn, preferred_element_type=F32)
        st_ref[di, 2 * pair + hh] = jnp.exp2(bend[:, hs]) * st + upd
    return outs


def _hgrn_kernel(sq_ref, hi_ref, kf_ref, gfh_ref, gfl_ref, kb_ref, gbh_ref, gbl_ref, hg_ref,
                 ci_ref, cff_ref, cfb_ref, lb_ref, gain_ref, cmat_ref, role_ref, masks_ref, ones_ref,
                 ustrict_ref, lstrict_ref, o_ref, acc_ref, st_ref, *, n_chunks):
    c = HG_CHUNK
    tn = (((0,), (0,)), ((), ()))

    for h in range(HG_HEADS):
        hs = slice(h * HG_DK, (h + 1) * HG_DK)
        ci = ci_ref[0, :, hs]
        for di, (pre_ref, cm_ref) in enumerate(((cff_ref, ustrict_ref), (cfb_ref, lstrict_ref))):
            k, g = _forget(pre_ref[0, :, hs], lb_ref[di:di + 1, hs])
            ke = (k * jnp.exp(_cumdot(cm_ref[...], g))).astype(BF16)
            st_ref[di, h] = lax.dot_general(ci, ke, tn, preferred_element_type=F32)

    def step(i, second):
        rows = (pl.ds(pl.multiple_of(i * c, c), c), pl.ds(pl.multiple_of((n_chunks - 1 - i) * c, c), c))
        units = [(pair, di) for pair in range(HG_HEADS // 2) for di in range(2)]

        def window(u):
            pair, di = units[u]
            return rows[di], slice(2 * pair * HG_DK, 2 * (pair + 1) * HG_DK)

        def exponents(u):
            rr, cols = window(u)
            gh_ref, gl_ref = (gfh_ref, gfl_ref) if units[u][1] == 0 else (gbh_ref, gbl_ref)
            return _hg_exponents(gh_ref[0, rr, cols], gl_ref[0, rr, cols], cmat_ref, units[u][1])

        def intra(u, dall):
            rr, cols = window(u)
            k_ref = kf_ref if units[u][1] == 0 else kb_ref
            return _hg_intra(sq_ref[0, rr, cols], k_ref[0, rr, cols], dall, role_ref, masks_ref, ones_ref,
                             units[u][1])

        def finish(u, parts):
            pair, di = units[u]
            rr, cols = window(u)
            outs = _hg_outputs(*parts, hi_ref[0, rr, cols], st_ref, di, pair)
            for hh, o in enumerate(outs):
                hs = slice((2 * pair + hh) * HG_DK, (2 * pair + hh + 1) * HG_DK)
                if not second:
                    acc_ref[rr, hs] = o
                else:
                    ot = acc_ref[rr, hs] + o
                    y = ot * lax.rsqrt(jnp.mean(ot * ot, axis=-1, keepdims=True) + EPS) * gain_ref[:, hs]
                    o_ref[0, rr, hs] = (y * _silu(hg_ref[0, rr, hs].astype(F32))).astype(BF16)

        n = len(units)
        dalls = [exponents(u) for u in range(n)]
        parts = [intra(u, dalls[u]) for u in range(n)]
        for u in range(n):
            finish(u, parts[u])

    def first_half(i, carry):
        step(i, False)
        return carry

    def second_half(i, carry):
        step(i, True)
        return carry

    lax.fori_loop(0, n_chunks // 2, first_half, 0)
    lax.fori_loop(n_chunks // 2, n_chunks, second_half, 0)


def _hgrn(proj, cproj, lb, gain):
    B, S, _ = proj.shape
    L = cproj.shape[1]
    c = HG_CHUNK
    nl = len(HG_LEVELS)
    tri_l = np.tril(np.ones((L, L), np.float32), -1)
    consts = list(_hg_consts()) + [jnp.asarray(tri_l.T, BF16), jnp.asarray(tri_l, BF16)]
    seq = lambda j: pl.BlockSpec((1, S, SEG), lambda b: (b, 0, j))
    cseq = lambda j: pl.BlockSpec((1, L, SEG), lambda b: (b, 0, j))
    full2 = lambda n, m: pl.BlockSpec((n, m), lambda b: (0, 0))
    return pl.pallas_call(
        functools.partial(_hgrn_kernel, n_chunks=S // c),
        grid=(B,),
        in_specs=[seq(P_HQ), seq(P_HI), seq(P_KF), seq(P_GFH), seq(P_GFL), seq(P_KB), seq(P_GBH), seq(P_GBL),
                  seq(P_HG), cseq(2), cseq(3), cseq(4),
                  full2(2, HG_WIDTH), full2(1, HG_WIDTH),
                  pl.BlockSpec((2, (nl + 1) * c, 2 * c), lambda b: (0, 0, 0)),
                  pl.BlockSpec((2, nl, c, 2 * HG_DK), lambda b: (0, 0, 0, 0)),
                  pl.BlockSpec((2, nl + 1, c, 2 * c), lambda b: (0, 0, 0, 0)),
                  full2(2 * HG_DK, 2 * c), full2(L, L), full2(L, L)],
        out_specs=pl.BlockSpec((1, S, HG_WIDTH), lambda b: (b, 0, 0)),
        out_shape=jax.ShapeDtypeStruct((B, S, HG_WIDTH), BF16),
        scratch_shapes=[pltpu.VMEM((S, HG_WIDTH), F32),
                        pltpu.VMEM((2, HG_HEADS, HG_DK, HG_DK), F32)],
        compiler_params=_cparams(("arbitrary",)),
        name="hgrn",
    )(*([proj] * 9), cproj, cproj, cproj, lb, gain, *consts)


ROUTE_E_ROW = 8


def _route_t(lt):
    tm = lt.shape[1]
    row8 = lax.broadcasted_iota(jnp.int32, (SUBLANES, tm), 0).astype(F32)
    gl = jnp.where(row8 < N_GROUPS, lt[0:SUBLANES], -jnp.inf)
    gmax = jnp.max(gl, axis=0, keepdims=True)
    g_sel = jnp.min(jnp.where(gl == gmax, row8, float(SUBLANES)), axis=0, keepdims=True)
    g_w = 1.0 / jnp.sum(jnp.exp(gl - gmax), axis=0, keepdims=True)
    row16 = lax.broadcasted_iota(jnp.int32, (N_EXPERTS, tm), 0).astype(F32)
    first = g_sel * EXPERTS_PER_GROUP
    in_grp = (row16 >= first) & (row16 < first + EXPERTS_PER_GROUP)
    e1 = jnp.where(in_grp, lt[ROUTE_E_ROW:ROUTE_E_ROW + N_EXPERTS], -jnp.inf)
    v1 = jnp.max(e1, axis=0, keepdims=True)
    i1 = jnp.min(jnp.where(e1 == v1, row16, float(N_EXPERTS)), axis=0, keepdims=True)
    e2 = jnp.where(row16 == i1, -jnp.inf, e1)
    v2 = jnp.max(e2, axis=0, keepdims=True)
    i2 = jnp.min(jnp.where(e2 == v2, row16, float(N_EXPERTS)), axis=0, keepdims=True)
    t = jnp.exp(v2 - v1)
    w1 = g_w / (1.0 + t)
    w2 = g_w * t / (1.0 + t)
    onehot = jnp.where(row8 == g_sel, 1.0, 0.0)
    gates = jnp.where(row8 == i1 - first, w1, jnp.where(row8 == i2 - first, w2, 0.0))
    return onehot, gates


def _outproj_kernel(x_ref, na_ref, hg_ref, w_ref, ga_ref, g_ref, sc_ref, sh_ref, wrh_ref, wrl_ref, br_ref,
                    x1_ref, hx_ref, oh_ref, cnt_ref):
    tm, d = x_ref.shape
    nt = (((1,), (1,)), ((), ()))
    mix = (jnp.dot(na_ref[...], w_ref[0:NA_WIDTH, :], preferred_element_type=F32)
           + jnp.dot(hg_ref[...], w_ref[NA_WIDTH:, :], preferred_element_type=F32))
    x1 = x_ref[...] + ga_ref[0] * mix
    x1_ref[...] = x1
    h2 = _norm_mod(x1, g_ref[...], sc_ref[0], sh_ref[0])
    h_hi = h2.astype(BF16)
    h_lo = (h2 - h_hi.astype(F32)).astype(BF16)
    hx_ref[:, 0:d] = h_hi
    lt = ((lax.dot_general(wrl_ref[...], h_hi, nt, preferred_element_type=F32)
           + lax.dot_general(wrh_ref[...], h_lo, nt, preferred_element_type=F32))
          + lax.dot_general(wrh_ref[...], h_hi, nt, preferred_element_type=F32)) + br_ref[...]
    onehot_t, gates_t = _route_t(lt)
    tok = jnp.concatenate([onehot_t, gates_t, jnp.zeros((LANES - 2 * SUBLANES, tm), F32)], axis=0).T
    lane = lax.broadcasted_iota(jnp.int32, tok.shape, 1)
    onehot = jnp.where(lane < ROUTE_E_ROW, tok, 0.0)
    gates = jnp.where(lane >= ROUTE_E_ROW, tok, 0.0)
    for j, piece in enumerate(_split3(gates)):
        hx_ref[:, d + j * LANES:d + (j + 1) * LANES] = piece
    oh_ref[...] = onehot
    cnt_ref[0] = jnp.broadcast_to(jnp.sum(onehot, axis=0, keepdims=True), (SUBLANES, LANES))


def _outproj(x2d, na2d, hg2d, w_out_bf, ga, g, sc, sh, wr_hi, wr_lo, br, S):
    T, D = x2d.shape
    tm = MOE_TM
    per = S // tm
    tok = lambda w: pl.BlockSpec((tm, w), lambda i: (i, 0))
    bat = pl.BlockSpec((1, 1, D), lambda i: (i // per, 0, 0))
    return pl.pallas_call(
        _outproj_kernel,
        grid=(T // tm,),
        in_specs=[tok(D), tok(NA_WIDTH), tok(HG_WIDTH),
                  pl.BlockSpec((NA_WIDTH + HG_WIDTH, D), lambda i: (0, 0)),
                  bat, pl.BlockSpec((1, D), lambda i: (0, 0)), bat, bat,
                  pl.BlockSpec((LANES, D), lambda i: (0, 0)),
                  pl.BlockSpec((LANES, D), lambda i: (0, 0)),
                  pl.BlockSpec((LANES, 1), lambda i: (0, 0))],
        out_specs=[tok(D), tok(MOE_XW), tok(LANES),
                   pl.BlockSpec((1, SUBLANES, LANES), lambda i: (i, 0, 0))],
        out_shape=[jax.ShapeDtypeStruct((T, D), F32), jax.ShapeDtypeStruct((T, MOE_XW), BF16),
                   jax.ShapeDtypeStruct((T, LANES), F32),
                   jax.ShapeDtypeStruct((T // tm, SUBLANES, LANES), F32)],
        compiler_params=_cparams(("arbitrary",)),
        name="outproj",
    )(x2d, na2d, hg2d, w_out_bf, ga, g, sc, sh, wr_hi, wr_lo, br)


MOE_TM = 512
MOE_ALIGN = 16
MOE_LOC = 640
MOE_XW = D_MODEL + 3 * LANES
MOE_TE = 512
MOE_BITS = tuple(1 << b for b in range(9, 3, -1))


def _moe_steps(T):
    worst = T + (T // MOE_TM) * N_GROUPS * (MOE_ALIGN - 1)
    return -(-worst // MOE_TE) + N_GROUPS


def _local_positions(oh_ref, lstrict_ref, cnt_ref, k):
    onehot = oh_ref[...]
    ranks = jnp.dot(lstrict_ref[...], onehot.astype(BF16), preferred_element_type=F32)
    lane = lax.broadcasted_iota(jnp.int32, (1, LANES), 1)
    base = jnp.zeros((1, LANES), F32)
    o = jnp.int32(0)
    for g in range(N_GROUPS):
        base = jnp.where(lane == g, o.astype(F32), base)
        o = o + cnt_ref[k * N_GROUPS + g]
    return jnp.sum(onehot * (ranks + base), axis=-1, keepdims=True)


def _sort_matrix(lpos):
    col = lax.broadcasted_iota(jnp.int32, (MOE_TM, MOE_LOC), 1).astype(F32)
    return jnp.where(lpos == col, 1.0, 0.0).astype(BF16)


def _block_copies(n_rows, src_row, dst_row, bits, make_copy, action):
    for bit in bits:
        part = n_rows & (-2 * bit)

        @pl.when((n_rows & bit) != 0)
        def _(part=part, bit=bit):
            action(make_copy(pl.multiple_of(src_row + part, MOE_ALIGN), pl.multiple_of(dst_row + part, MOE_ALIGN), bit))


def _run_copies(off_ref, cnt_ref, k, make_copy, action):
    o = jnp.int32(0)
    for g in range(N_GROUPS):
        c = cnt_ref[k * N_GROUPS + g]
        _block_copies(c, o, off_ref[k * N_GROUPS + g], MOE_BITS, make_copy, action)
        o = o + c


def _dispatch_kernel(off_ref, cnt_ref, tail_ref, hx_ref, oh_ref, lstrict_ref, hs_ref, buf_ref, zero_ref, sem,
                     *, n_tiles, n_steps):
    k = pl.program_id(0)
    slot = k % 2

    def copies(kk, sl, action):
        def make(src_row, dst_row, n):
            return pltpu.make_async_copy(buf_ref.at[sl, pl.ds(src_row, n)], hs_ref.at[pl.ds(dst_row, n)], sem.at[sl])
        _run_copies(off_ref, cnt_ref, kk, make, action)

    @pl.when(k >= 2)
    def _():
        copies(k - 2, slot, lambda cp: cp.wait())

    pt = _sort_matrix(_local_positions(oh_ref, lstrict_ref, cnt_ref, k))
    srt = lax.dot_general(pt, hx_ref[...], (((0,), (0,)), ((), ())), preferred_element_type=F32)
    buf_ref[slot] = srt.astype(BF16)
    copies(k, slot, lambda cp: cp.start())

    @pl.when(k == n_tiles - 1)
    def _():
        zero_ref[...] = jnp.zeros_like(zero_ref)
        n_used = tail_ref[2 * N_GROUPS]

        def zero_copy(src_row, dst_row, n):
            return pltpu.make_async_copy(zero_ref.at[pl.ds(src_row, n)], hs_ref.at[pl.ds(dst_row, n)], sem.at[2])

        def tile_copy(i):
            return zero_copy(0, pl.multiple_of(i * MOE_TE, MOE_TE), MOE_TE)

        def fills(action):
            for g in range(N_GROUPS):
                _block_copies(tail_ref[N_GROUPS + g], jnp.int32(0), tail_ref[g], MOE_BITS[1:], zero_copy, action)

        fills(lambda cp: cp.start())
        lax.fori_loop(n_used, n_steps, lambda i, c: (tile_copy(i).start(), c)[1], 0)
        if n_tiles >= 2:
            copies(k - 1, 1 - slot, lambda cp: cp.wait())
        copies(k, slot, lambda cp: cp.wait())
        fills(lambda cp: cp.wait())
        lax.fori_loop(n_used, n_steps, lambda i, c: (tile_copy(i).wait(), c)[1], 0)


def _dispatch(off, cnt, tail, hx, onehot, lstrict, n_steps):
    T = hx.shape[0]
    n_tiles = T // MOE_TM
    return pl.pallas_call(
        functools.partial(_dispatch_kernel, n_tiles=n_tiles, n_steps=n_steps),
        grid_spec=pltpu.PrefetchScalarGridSpec(
            num_scalar_prefetch=3,
            grid=(n_tiles,),
            in_specs=[pl.BlockSpec((MOE_TM, MOE_XW), lambda k, *_: (k, 0)),
                      pl.BlockSpec((MOE_TM, LANES), lambda k, *_: (k, 0)),
                      pl.BlockSpec((MOE_TM, MOE_TM), lambda k, *_: (0, 0))],
            out_specs=pl.BlockSpec(memory_space=pl.ANY),
            scratch_shapes=[pltpu.VMEM((2, MOE_LOC, MOE_XW), BF16),
                            pltpu.VMEM((MOE_TE, MOE_XW), BF16),
                            pltpu.SemaphoreType.DMA((3,))]),
        out_shape=jax.ShapeDtypeStruct((n_steps * MOE_TE, MOE_XW), BF16),
        compiler_params=_cparams(("arbitrary",)),
        name="dispatch",
    )(off, cnt, tail, hx, onehot, lstrict)


def _experts_kernel(grp_ref, used_ref, hs_ref, w1_ref, w3_ref, w2_ref, ys_ref):
    i = pl.program_id(0)
    d = ys_ref.shape[1]
    ne = EXPERTS_PER_GROUP

    @pl.when(i < used_ref[0])
    def _():
        t = hs_ref[:, 0:d]
        gates = ((hs_ref[:, d + 2 * LANES:d + 3 * LANES].astype(F32) + hs_ref[:, d + LANES:d + 2 * LANES].astype(F32))
                 + hs_ref[:, d:d + LANES].astype(F32))
        lane = lax.broadcasted_iota(jnp.int32, gates.shape, 1)
        acc = None
        for j in range(ne):
            a = jnp.dot(t, w1_ref[0, j], preferred_element_type=F32)
            b = jnp.dot(t, w3_ref[0, j], preferred_element_type=F32)
            gj = jnp.sum(jnp.where(lane == ROUTE_E_ROW + j, gates, 0.0), axis=-1, keepdims=True)
            y = gj * jnp.dot((_silu(a) * b).astype(BF16), w2_ref[0, j], preferred_element_type=F32)
            acc = y if acc is None else acc + y
        ys_ref[...] = acc.astype(BF16)

    @pl.when(i >= used_ref[0])
    def _():
        ys_ref[...] = jnp.zeros_like(ys_ref)


def _experts(grp, used, hs, w1_bf, w3_bf, w2_bf):
    rows, _ = hs.shape
    D = w2_bf.shape[3]
    wspec = lambda w: pl.BlockSpec((1,) + w.shape[1:], lambda i, grp, used: (grp[i], 0, 0, 0))
    return pl.pallas_call(
        _experts_kernel,
        grid_spec=pltpu.PrefetchScalarGridSpec(
            num_scalar_prefetch=2,
            grid=(rows // MOE_TE,),
            in_specs=[pl.BlockSpec((MOE_TE, MOE_XW), lambda i, grp, used: (i, 0)),
                      wspec(w1_bf), wspec(w3_bf), wspec(w2_bf)],
            out_specs=pl.BlockSpec((MOE_TE, D), lambda i, grp, used: (i, 0))),
        out_shape=jax.ShapeDtypeStruct((rows, D), BF16),
        compiler_params=_cparams(("arbitrary",)),
        name="experts",
    )(grp, used, hs, w1_bf, w3_bf, w2_bf)


def _combine_kernel(off_ref, cnt_ref, x1_ref, oh_ref, lstrict_ref, ga_ref, gf_ref, ys_ref, o_ref, buf_ref, sem,
                    *, n_tiles):
    k = pl.program_id(0)
    slot = k % 2

    def copies(kk, sl, action):
        def make(loc_row, seg_row, n):
            return pltpu.make_async_copy(ys_ref.at[pl.ds(seg_row, n)], buf_ref.at[sl, pl.ds(loc_row, n)], sem.at[sl])
        _run_copies(off_ref, cnt_ref, kk, make, action)

    @pl.when(k == 0)
    def _():
        buf_ref[...] = jnp.zeros_like(buf_ref)
        copies(k, slot, lambda cp: cp.start())

    @pl.when(k + 1 < n_tiles)
    def _():
        copies(k + 1, 1 - slot, lambda cp: cp.start())

    pt = _sort_matrix(_local_positions(oh_ref, lstrict_ref, cnt_ref, k))
    copies(k, slot, lambda cp: cp.wait())
    y = jnp.dot(pt, buf_ref[slot], preferred_element_type=F32)
    x2 = x1_ref[...] + ga_ref[0] * y
    o_ref[...] = x2 * lax.rsqrt(jnp.mean(x2 * x2, axis=-1, keepdims=True) + EPS) * gf_ref[...]


def _combine(off, cnt, x1, onehot, lstrict, ga, gfin, ys, S):
    T, D = x1.shape
    n_tiles = T // MOE_TM
    per = S // MOE_TM
    return pl.pallas_call(
        functools.partial(_combine_kernel, n_tiles=n_tiles),
        grid_spec=pltpu.PrefetchScalarGridSpec(
            num_scalar_prefetch=2,
            grid=(n_tiles,),
            in_specs=[pl.BlockSpec((MOE_TM, D), lambda k, off, cnt: (k, 0)),
                      pl.BlockSpec((MOE_TM, LANES), lambda k, off, cnt: (k, 0)),
                      pl.BlockSpec((MOE_TM, MOE_TM), lambda k, off, cnt: (0, 0)),
                      pl.BlockSpec((1, 1, D), lambda k, off, cnt: (k // per, 0, 0)),
                      pl.BlockSpec((1, D), lambda k, off, cnt: (0, 0)),
                      pl.BlockSpec(memory_space=pl.ANY)],
            out_specs=pl.BlockSpec((MOE_TM, D), lambda k, off, cnt: (k, 0)),
            scratch_shapes=[pltpu.VMEM((2, MOE_LOC, D), BF16),
                            pltpu.SemaphoreType.DMA((2,))]),
        out_shape=jax.ShapeDtypeStruct((T, D), F32),
        compiler_params=_cparams(("arbitrary",)),
        name="combine",
    )(off, cnt, x1, onehot, lstrict, ga, gfin, ys)


def _moe_schedule(cnt_tiles, n_steps):
    cnt = ((cnt_tiles + (MOE_ALIGN - 1)) // MOE_ALIGN) * MOE_ALIGN
    ends = jnp.cumsum(cnt, axis=0)
    total = ends[-1]
    ntile = (total + MOE_TE - 1) // MOE_TE
    cum = jnp.cumsum(ntile)
    base = (cum - ntile) * MOE_TE
    off = base[None, :] + ends - cnt
    i = jnp.arange(n_steps, dtype=jnp.int32)
    grp = jnp.minimum(jnp.sum(i[:, None] >= cum[None, :], axis=1), N_GROUPS - 1)
    grp = jnp.where(i < cum[-1], grp, grp[jnp.maximum(cum[-1] - 1, 0)])
    tail = jnp.concatenate([base + total, ntile * MOE_TE - total, cum[-1:]])
    as_i32 = lambda a: a.reshape(-1).astype(jnp.int32)
    return as_i32(off), as_i32(cnt), as_i32(tail), as_i32(grp), as_i32(cum[-1:])


def _rope_tables(S):
    t = jnp.arange(S)
    pos = jnp.stack([t // GRID_W, t % GRID_W], axis=-1).astype(F32)
    inv = ROPE_BASE ** (-jnp.arange(0, ROPE_AXIS_DIM, 2, dtype=F32) / ROPE_AXIS_DIM)
    ang = pos[:, :, None] * inv
    cos, sin = jnp.cos(ang), jnp.sin(ang)
    cos_h = jnp.concatenate([cos, cos], axis=-1).reshape(S, NA_HEAD_DIM)
    sin_h = jnp.concatenate([-sin, sin], axis=-1).reshape(S, NA_HEAD_DIM)
    return jnp.tile(cos_h, (1, NA_HEADS)), jnp.tile(sin_h, (1, NA_HEADS))


def kernel(x, c, ctx, c_ctx, w_mod, b_mod, norm_mix, norm_ffn, w_in, w_out, na_rpb, hg_lb, hg_norm,
           w_grp, b_grp, w_exp, b_exp, w1, w3, w2, norm_final):
    B, S, D = x.shape
    T = B * S
    assert w_mod.shape[0] == 1, "single-layer kernel"

    rows = -(-(B + 1) // SUBLANES) * SUBLANES
    cc = jnp.zeros((rows, D), F32).at[:B].set(c).at[B].set(c_ctx)
    mod = _modulation(cc, w_mod[0], b_mod[0])
    sh_a, sc_a, ga_a, sh_f, sc_f, ga_f = [m.reshape(B, 1, D) for m in jnp.split(mod[:B], 6, axis=-1)]
    csh_a, csc_a = [m.reshape(1, D) for m in jnp.split(mod[B], 6)[:2]]

    w_in_bf = w_in[0].astype(BF16)
    w_ctx_bf = jnp.concatenate([w_in_bf[:, i * SEG:(i + 1) * SEG] for i in (1, 2, 4, 5, 6)], axis=1)
    cos_t, sin_t = _rope_tables(S)
    g_mix = norm_mix[0].reshape(1, D)

    lb = jnp.cumsum(jax.nn.softmax(hg_lb.astype(F32), axis=0), axis=0)[0]
    proj = _project(x, g_mix, sc_a, sh_a, w_in_bf, cos_t, sin_t, lb, tm=512)
    cproj = _project_ctx(ctx, g_mix, csc_a, csh_a, w_ctx_bf)

    na_out = _neighbourhood_attention(proj, cproj, _na_bias_table(na_rpb[0]))

    gain = jnp.tile(hg_norm[0].astype(F32), HG_HEADS).reshape(1, HG_WIDTH)
    hg_out = _hgrn(proj, cproj, lb, gain)

    e_rows = slice(ROUTE_E_ROW, ROUTE_E_ROW + N_EXPERTS)
    wr = jnp.zeros((LANES, D), F32).at[:N_GROUPS].set(w_grp[0].T).at[e_rows].set(w_exp[0].T)
    br = jnp.zeros((LANES, 1), F32).at[:N_GROUPS, 0].set(b_grp[0]).at[e_rows, 0].set(b_exp[0])
    hi_f32 = lax.bitcast_convert_type(lax.bitcast_convert_type(wr, jnp.uint32) & jnp.uint32(0xFFFF0000), F32)
    wr_hi = hi_f32.astype(BF16)
    wr_lo = (wr - hi_f32).astype(BF16)
    x1, hx, onehot, cnt_tiles = _outproj(x.reshape(T, D), na_out.reshape(T, NA_WIDTH), hg_out.reshape(T, HG_WIDTH),
                                         w_out[0].astype(BF16), ga_a, norm_ffn[0].reshape(1, D), sc_f, sh_f,
                                         wr_hi, wr_lo, br, S=S)

    n_steps = _moe_steps(T)
    off, cnt, tail, grp, used = _moe_schedule(cnt_tiles[:, 0, :N_GROUPS].astype(jnp.int32), n_steps)
    lstrict = jnp.asarray(np.tril(np.ones((MOE_TM, MOE_TM), np.float32), -1), BF16)
    by_group = lambda w: w[0].astype(BF16).reshape(N_GROUPS, EXPERTS_PER_GROUP, *w.shape[2:])
    hs = _dispatch(off, cnt, tail, hx, onehot, lstrict, n_steps)
    ys = _experts(grp, used, hs, by_group(w1), by_group(w3), by_group(w2))
    out = _combine(off, cnt, x1, onehot, lstrict, ga_f, norm_final.reshape(1, D), ys, S)
    return out.reshape(B, S, D)
```

```python
import functools

import numpy as np
import jax
import jax.numpy as jnp
from jax import lax
from jax.experimental import pallas as pl
from jax.experimental.pallas import tpu as pltpu

F32 = jnp.float32
BF16 = jnp.bfloat16
HIGHEST = lax.Precision.HIGHEST

D_MODEL = 1024
GRID_W = 64
NA_HEADS = 8
NA_HEAD_DIM = 64
NA_WIDTH = NA_HEADS * NA_HEAD_DIM
NA_KH = 8
NA_KW = 16
ROPE_AXIS_DIM = NA_HEAD_DIM // 2
ROPE_BASE = 10000.0
HG_HEADS = 4
HG_DK = 128
HG_WIDTH = HG_HEADS * HG_DK
HG_CHUNK = 64
SEG = 512
N_GROUPS = 4
EXPERTS_PER_GROUP = 4
N_EXPERTS = N_GROUPS * EXPERTS_PER_GROUP
D_EXPERT = 512
EPS = 1e-6
NEG = -1e30
LOG2E = 1.4426950408889634
LANES = 128
SUBLANES = 8
VMEM_LIMIT = 56 * 1024 * 1024

HG_LEVELS = (32, 16, 8, 4, 2, 1)


def _cparams(sem):
    return pltpu.CompilerParams(dimension_semantics=sem, vmem_limit_bytes=VMEM_LIMIT)


def _sigmoid(x):
    return 1.0 / (1.0 + jnp.exp(-x))


def _silu(x):
    return x * _sigmoid(x)


def _mod_kernel(c_ref, w_ref, b_ref, o_ref):
    s = _silu(c_ref[...])
    o_ref[...] = jnp.dot(s, w_ref[...], precision=HIGHEST, preferred_element_type=F32) + b_ref[...]


def _modulation(cc, w_mod, b_mod):
    rows, d = cc.shape
    n = w_mod.shape[1]
    tn = 1024
    return pl.pallas_call(
        _mod_kernel,
        grid=(n // tn,),
        in_specs=[pl.BlockSpec((rows, d), lambda j: (0, 0)),
                  pl.BlockSpec((d, tn), lambda j: (0, j)),
                  pl.BlockSpec((1, tn), lambda j: (0, j))],
        out_specs=pl.BlockSpec((rows, tn), lambda j: (0, j)),
        out_shape=jax.ShapeDtypeStruct((rows, n), F32),
        compiler_params=_cparams(("arbitrary",)),
        name="mod",
    )(cc, w_mod, b_mod.reshape(1, n))


def _norm_mod(x, g, sc, sh):
    y = x * lax.rsqrt(jnp.mean(x * x, axis=-1, keepdims=True) + EPS)
    return (y * g) * (1.0 + sc) + sh


def _rope(a, cos, sin):
    lane = lax.broadcasted_iota(jnp.int32, a.shape, 1)
    first = (lane % ROPE_AXIS_DIM) < (ROPE_AXIS_DIM // 2)
    up = pltpu.roll(a, LANES - ROPE_AXIS_DIM // 2, axis=1)
    dn = pltpu.roll(a, ROPE_AXIS_DIM // 2, axis=1)
    return a * cos + jnp.where(first, up, dn) * sin


P_QRAW, P_QROT, P_KROT, P_V, P_HQ, P_HI, P_KF, P_GFH, P_GFL, P_KB, P_GBH, P_GBL, P_HG = range(13)
P_SEGS = 13


def _proj_kernel(x_ref, g_ref, sc_ref, sh_ref, w_ref, cos_ref, sin_ref, lb_ref, o_ref):
    h = _norm_mod(x_ref[0], g_ref[...], sc_ref[0], sh_ref[0]).astype(BF16)
    scale = NA_HEAD_DIM ** -0.5 * LOG2E

    def put(seg, val):
        o_ref[0, :, seg * SEG:(seg + 1) * SEG] = val.astype(BF16)

    for j in range(8):
        acc = jnp.dot(h, w_ref[:, j * SEG:(j + 1) * SEG], preferred_element_type=F32)
        if j <= 1:
            rot = jnp.concatenate(
                [_rope(acc[:, p * LANES:(p + 1) * LANES], cos_ref[:, p * LANES:(p + 1) * LANES],
                       sin_ref[:, p * LANES:(p + 1) * LANES]) for p in range(SEG // LANES)], axis=1)
            if j == 0:
                put(P_QRAW, acc * scale)
                put(P_QROT, rot * scale)
            else:
                put(P_KROT, rot)
        elif j == 2:
            put(P_V, acc)
        elif j == 3:
            put(P_HQ, _silu(acc))
        elif j == 4:
            put(P_HI, acc)
        elif j in (5, 6):
            lb = lb_ref[j - 5:j - 4, :]
            f = lb + (1.0 - lb) * _sigmoid(acc)
            g2 = jnp.log2(f)
            hi = g2.astype(BF16)
            base = P_KF if j == 5 else P_KB
            put(base, 1.0 - f)
            put(base + 1, hi)
            put(base + 2, g2 - hi.astype(F32))
        else:
            put(P_HG, acc)


def _project(x, g, sc, sh, w_bf, cos_t, sin_t, lb, tm):
    B, S, D = x.shape
    return pl.pallas_call(
        _proj_kernel,
        grid=(S // tm, B),
        in_specs=[pl.BlockSpec((1, tm, D), lambda s, b: (b, s, 0)),
                  pl.BlockSpec((1, D), lambda s, b: (0, 0)),
                  pl.BlockSpec((1, 1, D), lambda s, b: (b, 0, 0)),
                  pl.BlockSpec((1, 1, D), lambda s, b: (b, 0, 0)),
                  pl.BlockSpec((D, 8 * SEG), lambda s, b: (0, 0)),
                  pl.BlockSpec((tm, SEG), lambda s, b: (s, 0)),
                  pl.BlockSpec((tm, SEG), lambda s, b: (s, 0)),
                  pl.BlockSpec((2, HG_WIDTH), lambda s, b: (0, 0))],
        out_specs=pl.BlockSpec((1, tm, P_SEGS * SEG), lambda s, b: (b, s, 0)),
        out_shape=jax.ShapeDtypeStruct((B, S, P_SEGS * SEG), BF16),
        compiler_params=_cparams(("arbitrary", "arbitrary")),
        name="proj",
    )(x, g, sc, sh, w_bf, cos_t, sin_t, lb)


def _ctxproj_kernel(x_ref, g_ref, sc_ref, sh_ref, w_ref, o_ref):
    h = _norm_mod(x_ref[0], g_ref[...], sc_ref[...], sh_ref[...]).astype(BF16)
    for j in range(5):
        acc = jnp.dot(h, w_ref[:, j * SEG:(j + 1) * SEG], preferred_element_type=F32)
        o_ref[0, :, j * SEG:(j + 1) * SEG] = acc.astype(BF16)


def _project_ctx(ctx, g, sc, sh, w_bf):
    B, L, D = ctx.shape
    return pl.pallas_call(
        _ctxproj_kernel,
        grid=(B,),
        in_specs=[pl.BlockSpec((1, L, D), lambda b: (b, 0, 0)),
                  pl.BlockSpec((1, D), lambda b: (0, 0)),
                  pl.BlockSpec((1, D), lambda b: (0, 0)),
                  pl.BlockSpec((1, D), lambda b: (0, 0)),
                  pl.BlockSpec((D, 5 * SEG), lambda b: (0, 0))],
        out_specs=pl.BlockSpec((1, L, 5 * SEG), lambda b: (b, 0, 0)),
        out_shape=jax.ShapeDtypeStruct((B, L, 5 * SEG), BF16),
        compiler_params=_cparams(("arbitrary",)),
        name="ctxproj",
    )(ctx, g, sc, sh, w_bf)


NA_ROWS_PER_STEP = 4


def _na_kernel(qraw_ref, qrot_ref, k_ref, v_ref, ck_ref, cv_ref, bias_ref, o_ref, *, rows):
    nk = NA_KH * GRID_W
    lane = lax.broadcasted_iota(jnp.int32, (GRID_W, LANES), 1)
    sel0 = lane < NA_HEAD_DIM
    nt = (((1,), (1,)), ((), ()))
    n_pairs = NA_WIDTH // LANES
    ones = jnp.ones((nk + ck_ref.shape[1], LANES), BF16)
    units = [(j, p) for j in range(NA_ROWS_PER_STEP) for p in range(n_pairs)]

    def window(j):
        r = pl.program_id(1) * NA_ROWS_PER_STEP + j
        rs = jnp.clip(r - NA_KH // 2, 0, rows - NA_KH)
        return pl.multiple_of(rs * GRID_W, GRID_W), pl.multiple_of((rs - r + NA_KH - 1) * GRID_W, GRID_W)

    def query_slots(ref, j, cols):
        q = ref[0, j * GRID_W:(j + 1) * GRID_W, cols]
        zero = jnp.zeros_like(q)
        return jnp.concatenate([jnp.where(sel0, q, zero), jnp.where(sel0, zero, q)], axis=0)

    def scores(j, p):
        cols = slice(p * LANES, (p + 1) * LANES)
        start, bias_row = window(j)
        s_loc = lax.dot_general(k_ref[0, pl.ds(start, nk), cols], query_slots(qrot_ref, j, cols), nt,
                                preferred_element_type=F32) + bias_ref[p, pl.ds(bias_row, nk), :]
        s_ctx = lax.dot_general(ck_ref[0, :, cols], query_slots(qraw_ref, j, cols), nt,
                                preferred_element_type=F32)
        return s_loc, s_ctx

    def softmax(s_loc, s_ctx):
        m = jnp.maximum(jnp.max(s_loc, axis=0, keepdims=True), jnp.max(s_ctx, axis=0, keepdims=True))
        return jnp.concatenate([jnp.exp2(s_loc - m).T, jnp.exp2(s_ctx - m).T], axis=1).astype(BF16)

    def values(j, p, probs):
        cols = slice(p * LANES, (p + 1) * LANES)
        start, _ = window(j)
        vals = jnp.concatenate([v_ref[0, pl.ds(start, nk), cols], cv_ref[0, :, cols]], axis=0)
        oe = jnp.dot(probs, jnp.concatenate([vals, ones], axis=1), preferred_element_type=F32)
        res = oe[:, :LANES] / oe[:, LANES:]
        o_ref[0, j * GRID_W:(j + 1) * GRID_W, cols] = jnp.where(sel0, res[0:GRID_W],
                                                                res[GRID_W:2 * GRID_W]).astype(BF16)

    s_all = [scores(j, p) for j, p in units]
    probs = [softmax(*s) for s in s_all]
    for (j, p), pr in zip(units, probs):
        values(j, p, pr)


def _na_bias_table(rpb):
    qc = np.arange(GRID_W)[None, :]
    kc = np.arange(GRID_W)[:, None]
    cs = np.clip(qc - NA_KW // 2, 0, GRID_W - NA_KW)
    valid = (kc >= cs) & (kc < cs + NA_KW)
    col_off = np.clip(kc - qc + NA_KW - 1, 0, 2 * NA_KW - 2)
    t = (rpb.astype(F32) * LOG2E)[:, :, col_off]
    t = jnp.where(jnp.asarray(valid)[None, None], t, NEG)
    n_pairs = NA_HEADS // 2
    n_ro = 2 * NA_KH - 1
    t = t.reshape(n_pairs, 2, n_ro, GRID_W, GRID_W).transpose(0, 2, 3, 1, 4)
    return t.reshape(n_pairs, n_ro * GRID_W, 2 * GRID_W)


def _neighbourhood_attention(proj, cproj, bias_tbl):
    B, S, _ = proj.shape
    L = cproj.shape[1]
    rows = S // GRID_W
    tq = NA_ROWS_PER_STEP * GRID_W
    return pl.pallas_call(
        functools.partial(_na_kernel, rows=rows),
        grid=(B, rows // NA_ROWS_PER_STEP),
        in_specs=[pl.BlockSpec((1, tq, SEG), lambda b, r: (b, r, P_QRAW)),
                  pl.BlockSpec((1, tq, SEG), lambda b, r: (b, r, P_QROT)),
                  pl.BlockSpec((1, S, SEG), lambda b, r: (b, 0, P_KROT)),
                  pl.BlockSpec((1, S, SEG), lambda b, r: (b, 0, P_V)),
                  pl.BlockSpec((1, L, SEG), lambda b, r: (b, 0, 0)),
                  pl.BlockSpec((1, L, SEG), lambda b, r: (b, 0, 1)),
                  pl.BlockSpec(bias_tbl.shape, lambda b, r: (0, 0, 0))],
        out_specs=pl.BlockSpec((1, tq, SEG), lambda b, r: (b, r, 0)),
        out_shape=jax.ShapeDtypeStruct((B, S, NA_WIDTH), BF16),
        compiler_params=_cparams(("arbitrary", "arbitrary")),
        name="na",
    )(proj, proj, proj, proj, cproj, cproj, bias_tbl)


def _split3(g):
    g1 = g.astype(BF16)
    r1 = g - g1.astype(F32)
    g2 = r1.astype(BF16)
    g3 = (r1 - g2.astype(F32)).astype(BF16)
    return g1, g2, g3


def _cumdot(c_bf, g):
    g1, g2, g3 = _split3(g)
    d = lambda a: jnp.dot(c_bf, a, preferred_element_type=F32)
    return (d(g3) + d(g2)) + d(g1)


def _forget(pre, lb):
    f = lb + (1.0 - lb) * _sigmoid(pre.astype(F32))
    return 1.0 - f, jnp.log(f)


def _hg_consts():
    c = HG_CHUNK
    t = np.arange(c)[:, None]
    u = np.arange(c)[None, :]
    cm, role, masks = [], [], []
    for fwd in (True, False):
        blocks = [(u <= t) if fwd else (u >= t)]
        roles, ms = [], []
        for m in HG_LEVELS:
            blk = t // (2 * m)
            if fwd:
                mid = 2 * m * blk + m - 1
                is_q = (t % (2 * m)) >= m
                expo = np.where(is_q, (u > mid) & (u <= t), (u > t) & (u <= mid))
            else:
                mid = 2 * m * blk + m
                is_q = (t % (2 * m)) < m
                expo = np.where(is_q, (u >= t) & (u < mid), (u >= mid) & (u < t))
            blocks.append(expo)
            roles.append(np.broadcast_to(is_q, (c, 2 * HG_DK)))
            ms.append((blk == blk.T) & is_q & ~is_q.T)
        ms.append(t == u)
        full = np.concatenate(blocks, axis=0).astype(np.float32)
        cm.append(np.concatenate([full, full], axis=1))
        role.append(np.stack(roles).astype(np.float32))
        masks.append(np.stack([np.concatenate([x, x], axis=1) for x in ms]).astype(np.float32))
    ones_bd = np.kron(np.eye(2, dtype=np.float32), np.ones((HG_DK, c), np.float32))
    return (jnp.asarray(np.stack(cm), BF16), jnp.asarray(np.stack(role), BF16),
            jnp.asarray(np.stack(masks)), jnp.asarray(ones_bd, BF16))


def _block_diag(x, zeros):
    return jnp.concatenate([jnp.concatenate([x[:, :HG_DK], zeros], axis=1),
                            jnp.concatenate([zeros, x[:, HG_DK:]], axis=1)], axis=0)


def _hg_exponents(gh, gl, cmat_ref, di):
    return jnp.dot(cmat_ref[di], jnp.concatenate([gh, gl], axis=0), preferred_element_type=F32)


def _hg_intra(q, k, dall, role_ref, masks_ref, ones_ref, di):
    c = HG_CHUNK
    nl = len(HG_LEVELS)
    nt = (((1,), (1,)), ((), ()))
    b = dall[0:c]
    bend = b[c - 1:c] if di == 0 else b[0:1]
    qe = q * jnp.exp2(b).astype(BF16)
    ke = k * jnp.exp2(bend - b).astype(BF16)
    zeros = jnp.zeros((c, HG_DK), BF16)
    a = None
    for li in range(nl):
        e = jnp.exp2(dall[(li + 1) * c:(li + 2) * c]).astype(BF16)
        x = jnp.where(role_ref[di, li] > 0.5, q, k) * e
        r = lax.dot_general(x, _block_diag(x, zeros), nt, preferred_element_type=F32) * masks_ref[di, li]
        a = r if a is None else a + r
    a = a + jnp.dot(q * k, ones_ref[...], preferred_element_type=F32) * masks_ref[di, nl]
    return a.astype(BF16), qe, ke, bend


def _hg_outputs(a, qe, ke, bend, v, st_ref, di, pair):
    nt = (((1,), (1,)), ((), ()))
    tn = (((0,), (0,)), ((), ()))
    o = jnp.dot(a, _block_diag(v, jnp.zeros((HG_CHUNK, HG_DK), BF16)), preferred_element_type=F32)
    outs = []
    for hh in range(2):
        hs = slice(hh * HG_DK, (hh + 1) * HG_DK)
        st = st_ref[di, 2 * pair + hh]
        outs.append(o[:, hs] + lax.dot_general(qe[:, hs], st.astype(BF16), nt, preferred_element_type=F32))
        upd = lax.dot_general(v[:, hs], ke[:, hs], tn, preferred_element_type=F32)
        st_ref[di, 2 * pair + hh] = jnp.exp2(bend[:, hs]) * st + upd
    return outs


def _hgrn_kernel(sq_ref, hi_ref, kf_ref, gfh_ref, gfl_ref, kb_ref, gbh_ref, gbl_ref, hg_ref,
                 ci_ref, cff_ref, cfb_ref, lb_ref, gain_ref, cmat_ref, role_ref, masks_ref, ones_ref,
                 ustrict_ref, lstrict_ref, o_ref, acc_ref, st_ref, *, n_chunks):
    c = HG_CHUNK
    tn = (((0,), (0,)), ((), ()))

    for h in range(HG_HEADS):
        hs = slice(h * HG_DK, (h + 1) * HG_DK)
        ci = ci_ref[0, :, hs]
        for di, (pre_ref, cm_ref) in enumerate(((cff_ref, ustrict_ref), (cfb_ref, lstrict_ref))):
            k, g = _forget(pre_ref[0, :, hs], lb_ref[di:di + 1, hs])
            ke = (k * jnp.exp(_cumdot(cm_ref[...], g))).astype(BF16)
            st_ref[di, h] = lax.dot_general(ci, ke, tn, preferred_element_type=F32)

    def step(i, second):
        rows = (pl.ds(pl.multiple_of(i * c, c), c), pl.ds(pl.multiple_of((n_chunks - 1 - i) * c, c), c))
        units = [(pair, di) for pair in range(HG_HEADS // 2) for di in range(2)]

        def window(u):
            pair, di = units[u]
            return rows[di], slice(2 * pair * HG_DK, 2 * (pair + 1) * HG_DK)

        def exponents(u):
            rr, cols = window(u)
            gh_ref, gl_ref = (gfh_ref, gfl_ref) if units[u][1] == 0 else (gbh_ref, gbl_ref)
            return _hg_exponents(gh_ref[0, rr, cols], gl_ref[0, rr, cols], cmat_ref, units[u][1])

        def intra(u, dall):
            rr, cols = window(u)
            k_ref = kf_ref if units[u][1] == 0 else kb_ref
            return _hg_intra(sq_ref[0, rr, cols], k_ref[0, rr, cols], dall, role_ref, masks_ref, ones_ref,
                             units[u][1])

        def finish(u, parts):
            pair, di = units[u]
            rr, cols = window(u)
            outs = _hg_outputs(*parts, hi_ref[0, rr, cols], st_ref, di, pair)
            for hh, o in enumerate(outs):
                hs = slice((2 * pair + hh) * HG_DK, (2 * pair + hh + 1) * HG_DK)
                if not second:
                    acc_ref[rr, hs] = o
                else:
                    ot = acc_ref[rr, hs] + o
                    y = ot * lax.rsqrt(jnp.mean(ot * ot, axis=-1, keepdims=True) + EPS) * gain_ref[:, hs]
                    o_ref[0, rr, hs] = (y * _silu(hg_ref[0, rr, hs].astype(F32))).astype(BF16)

        n = len(units)
        dalls = [exponents(u) for u in range(n)]
        parts = [intra(u, dalls[u]) for u in range(n)]
        for u in range(n):
            finish(u, parts[u])

    def first_half(i, carry):
        step(i, False)
        return carry

    def second_half(i, carry):
        step(i, True)
        return carry

    lax.fori_loop(0, n_chunks // 2, first_half, 0)
    lax.fori_loop(n_chunks // 2, n_chunks, second_half, 0)


def _hgrn(proj, cproj, lb, gain):
    B, S, _ = proj.shape
    L = cproj.shape[1]
    c = HG_CHUNK
    nl = len(HG_LEVELS)
    tri_l = np.tril(np.ones((L, L), np.float32), -1)
    consts = list(_hg_consts()) + [jnp.asarray(tri_l.T, BF16), jnp.asarray(tri_l, BF16)]
    seq = lambda j: pl.BlockSpec((1, S, SEG), lambda b: (b, 0, j))
    cseq = lambda j: pl.BlockSpec((1, L, SEG), lambda b: (b, 0, j))
    full2 = lambda n, m: pl.BlockSpec((n, m), lambda b: (0, 0))
    return pl.pallas_call(
        functools.partial(_hgrn_kernel, n_chunks=S // c),
        grid=(B,),
        in_specs=[seq(P_HQ), seq(P_HI), seq(P_KF), seq(P_GFH), seq(P_GFL), seq(P_KB), seq(P_GBH), seq(P_GBL),
                  seq(P_HG), cseq(2), cseq(3), cseq(4),
                  full2(2, HG_WIDTH), full2(1, HG_WIDTH),
                  pl.BlockSpec((2, (nl + 1) * c, 2 * c), lambda b: (0, 0, 0)),
                  pl.BlockSpec((2, nl, c, 2 * HG_DK), lambda b: (0, 0, 0, 0)),
                  pl.BlockSpec((2, nl + 1, c, 2 * c), lambda b: (0, 0, 0, 0)),
                  full2(2 * HG_DK, 2 * c), full2(L, L), full2(L, L)],
        out_specs=pl.BlockSpec((1, S, HG_WIDTH), lambda b: (b, 0, 0)),
        out_shape=jax.ShapeDtypeStruct((B, S, HG_WIDTH), BF16),
        scratch_shapes=[pltpu.VMEM((S, HG_WIDTH), F32),
                        pltpu.VMEM((2, HG_HEADS, HG_DK, HG_DK), F32)],
        compiler_params=_cparams(("arbitrary",)),
        name="hgrn",
    )(*([proj] * 9), cproj, cproj, cproj, lb, gain, *consts)


ROUTE_E_ROW = 8


def _route_t(lt):
    tm = lt.shape[1]
    row8 = lax.broadcasted_iota(jnp.int32, (SUBLANES, tm), 0).astype(F32)
    gl = jnp.where(row8 < N_GROUPS, lt[0:SUBLANES], -jnp.inf)
    gmax = jnp.max(gl, axis=0, keepdims=True)
    g_sel = jnp.min(jnp.where(gl == gmax, row8, float(SUBLANES)), axis=0, keepdims=True)
    g_w = 1.0 / jnp.sum(jnp.exp(gl - gmax), axis=0, keepdims=True)
    row16 = lax.broadcasted_iota(jnp.int32, (N_EXPERTS, tm), 0).astype(F32)
    first = g_sel * EXPERTS_PER_GROUP
    in_grp = (row16 >= first) & (row16 < first + EXPERTS_PER_GROUP)
    e1 = jnp.where(in_grp, lt[ROUTE_E_ROW:ROUTE_E_ROW + N_EXPERTS], -jnp.inf)
    v1 = jnp.max(e1, axis=0, keepdims=True)
    i1 = jnp.min(jnp.where(e1 == v1, row16, float(N_EXPERTS)), axis=0, keepdims=True)
    e2 = jnp.where(row16 == i1, -jnp.inf, e1)
    v2 = jnp.max(e2, axis=0, keepdims=True)
    i2 = jnp.min(jnp.where(e2 == v2, row16, float(N_EXPERTS)), axis=0, keepdims=True)
    t = jnp.exp(v2 - v1)
    w1 = g_w / (1.0 + t)
    w2 = g_w * t / (1.0 + t)
    onehot = jnp.where(row8 == g_sel, 1.0, 0.0)
    gates = jnp.where(row8 == i1 - first, w1, jnp.where(row8 == i2 - first, w2, 0.0))
    return onehot, gates


def _outproj_kernel(x_ref, na_ref, hg_ref, w_ref, ga_ref, g_ref, sc_ref, sh_ref, wrh_ref, wrl_ref, br_ref,
                    x1_ref, hx_ref, oh_ref, cnt_ref):
    tm, d = x_ref.shape
    nt = (((1,), (1,)), ((), ()))
    mix = (jnp.dot(na_ref[...], w_ref[0:NA_WIDTH, :], preferred_element_type=F32)
           + jnp.dot(hg_ref[...], w_ref[NA_WIDTH:, :], preferred_element_type=F32))
    x1 = x_ref[...] + ga_ref[0] * mix
    x1_ref[...] = x1
    h2 = _norm_mod(x1, g_ref[...], sc_ref[0], sh_ref[0])
    h_hi = h2.astype(BF16)
    h_lo = (h2 - h_hi.astype(F32)).astype(BF16)
    hx_ref[:, 0:d] = h_hi
    lt = ((lax.dot_general(wrl_ref[...], h_hi, nt, preferred_element_type=F32)
           + lax.dot_general(wrh_ref[...], h_lo, nt, preferred_element_type=F32))
          + lax.dot_general(wrh_ref[...], h_hi, nt, preferred_element_type=F32)) + br_ref[...]
    onehot_t, gates_t = _route_t(lt)
    tok = jnp.concatenate([onehot_t, gates_t, jnp.zeros((LANES - 2 * SUBLANES, tm), F32)], axis=0).T
    lane = lax.broadcasted_iota(jnp.int32, tok.shape, 1)
    onehot = jnp.where(lane < ROUTE_E_ROW, tok, 0.0)
    gates = jnp.where(lane >= ROUTE_E_ROW, tok, 0.0)
    for j, piece in enumerate(_split3(gates)):
        hx_ref[:, d + j * LANES:d + (j + 1) * LANES] = piece
    oh_ref[...] = onehot
    cnt_ref[0] = jnp.broadcast_to(jnp.sum(onehot, axis=0, keepdims=True), (SUBLANES, LANES))


def _outproj(x2d, na2d, hg2d, w_out_bf, ga, g, sc, sh, wr_hi, wr_lo, br, S):
    T, D = x2d.shape
    tm = MOE_TM
    per = S // tm
    tok = lambda w: pl.BlockSpec((tm, w), lambda i: (i, 0))
    bat = pl.BlockSpec((1, 1, D), lambda i: (i // per, 0, 0))
    return pl.pallas_call(
        _outproj_kernel,
        grid=(T // tm,),
        in_specs=[tok(D), tok(NA_WIDTH), tok(HG_WIDTH),
                  pl.BlockSpec((NA_WIDTH + HG_WIDTH, D), lambda i: (0, 0)),
                  bat, pl.BlockSpec((1, D), lambda i: (0, 0)), bat, bat,
                  pl.BlockSpec((LANES, D), lambda i: (0, 0)),
                  pl.BlockSpec((LANES, D), lambda i: (0, 0)),
                  pl.BlockSpec((LANES, 1), lambda i: (0, 0))],
        out_specs=[tok(D), tok(MOE_XW), tok(LANES),
                   pl.BlockSpec((1, SUBLANES, LANES), lambda i: (i, 0, 0))],
        out_shape=[jax.ShapeDtypeStruct((T, D), F32), jax.ShapeDtypeStruct((T, MOE_XW), BF16),
                   jax.ShapeDtypeStruct((T, LANES), F32),
                   jax.ShapeDtypeStruct((T // tm, SUBLANES, LANES), F32)],
        compiler_params=_cparams(("arbitrary",)),
        name="outproj",
    )(x2d, na2d, hg2d, w_out_bf, ga, g, sc, sh, wr_hi, wr_lo, br)


MOE_TM = 512
MOE_ALIGN = 16
MOE_LOC = 640
MOE_XW = D_MODEL + 3 * LANES
MOE_TE = 512
MOE_BITS = tuple(1 << b for b in range(9, 3, -1))


def _moe_steps(T):
    worst = T + (T // MOE_TM) * N_GROUPS * (MOE_ALIGN - 1)
    return -(-worst // MOE_TE) + N_GROUPS


def _local_positions(oh_ref, lstrict_ref, cnt_ref, k):
    onehot = oh_ref[...]
    ranks = jnp.dot(lstrict_ref[...], onehot.astype(BF16), preferred_element_type=F32)
    lane = lax.broadcasted_iota(jnp.int32, (1, LANES), 1)
    base = jnp.zeros((1, LANES), F32)
    o = jnp.int32(0)
    for g in range(N_GROUPS):
        base = jnp.where(lane == g, o.astype(F32), base)
        o = o + cnt_ref[k * N_GROUPS + g]
    return jnp.sum(onehot * (ranks + base), axis=-1, keepdims=True)


def _sort_matrix(lpos):
    col = lax.broadcasted_iota(jnp.int32, (MOE_TM, MOE_LOC), 1).astype(F32)
    return jnp.where(lpos == col, 1.0, 0.0).astype(BF16)


def _block_copies(n_rows, src_row, dst_row, bits, make_copy, action):
    for bit in bits:
        part = n_rows & (-2 * bit)

        @pl.when((n_rows & bit) != 0)
        def _(part=part, bit=bit):
            action(make_copy(pl.multiple_of(src_row + part, MOE_ALIGN), pl.multiple_of(dst_row + part, MOE_ALIGN), bit))


def _run_copies(off_ref, cnt_ref, k, make_copy, action):
    o = jnp.int32(0)
    for g in range(N_GROUPS):
        c = cnt_ref[k * N_GROUPS + g]
        _block_copies(c, o, off_ref[k * N_GROUPS + g], MOE_BITS, make_copy, action)
        o = o + c


def _dispatch_kernel(off_ref, cnt_ref, tail_ref, hx_ref, oh_ref, lstrict_ref, hs_ref, buf_ref, zero_ref, sem,
                     *, n_tiles, n_steps):
    k = pl.program_id(0)
    slot = k % 2

    def copies(kk, sl, action):
        def make(src_row, dst_row, n):
            return pltpu.make_async_copy(buf_ref.at[sl, pl.ds(src_row, n)], hs_ref.at[pl.ds(dst_row, n)], sem.at[sl])
        _run_copies(off_ref, cnt_ref, kk, make, action)

    @pl.when(k >= 2)
    def _():
        copies(k - 2, slot, lambda cp: cp.wait())

    pt = _sort_matrix(_local_positions(oh_ref, lstrict_ref, cnt_ref, k))
    srt = lax.dot_general(pt, hx_ref[...], (((0,), (0,)), ((), ())), preferred_element_type=F32)
    buf_ref[slot] = srt.astype(BF16)
    copies(k, slot, lambda cp: cp.start())

    @pl.when(k == n_tiles - 1)
    def _():
        zero_ref[...] = jnp.zeros_like(zero_ref)
        n_used = tail_ref[2 * N_GROUPS]

        def zero_copy(src_row, dst_row, n):
            return pltpu.make_async_copy(zero_ref.at[pl.ds(src_row, n)], hs_ref.at[pl.ds(dst_row, n)], sem.at[2])

        def tile_copy(i):
            return zero_copy(0, pl.multiple_of(i * MOE_TE, MOE_TE), MOE_TE)

        def fills(action):
            for g in range(N_GROUPS):
                _block_copies(tail_ref[N_GROUPS + g], jnp.int32(0), tail_ref[g], MOE_BITS[1:], zero_copy, action)

        fills(lambda cp: cp.start())
        lax.fori_loop(n_used, n_steps, lambda i, c: (tile_copy(i).start(), c)[1], 0)
        if n_tiles >= 2:
            copies(k - 1, 1 - slot, lambda cp: cp.wait())
        copies(k, slot, lambda cp: cp.wait())
        fills(lambda cp: cp.wait())
        lax.fori_loop(n_used, n_steps, lambda i, c: (tile_copy(i).wait(), c)[1], 0)


def _dispatch(off, cnt, tail, hx, onehot, lstrict, n_steps):
    T = hx.shape[0]
    n_tiles = T // MOE_TM
    return pl.pallas_call(
        functools.partial(_dispatch_kernel, n_tiles=n_tiles, n_steps=n_steps),
        grid_spec=pltpu.PrefetchScalarGridSpec(
            num_scalar_prefetch=3,
            grid=(n_tiles,),
            in_specs=[pl.BlockSpec((MOE_TM, MOE_XW), lambda k, *_: (k, 0)),
                      pl.BlockSpec((MOE_TM, LANES), lambda k, *_: (k, 0)),
                      pl.BlockSpec((MOE_TM, MOE_TM), lambda k, *_: (0, 0))],
            out_specs=pl.BlockSpec(memory_space=pl.ANY),
            scratch_shapes=[pltpu.VMEM((2, MOE_LOC, MOE_XW), BF16),
                            pltpu.VMEM((MOE_TE, MOE_XW), BF16),
                            pltpu.SemaphoreType.DMA((3,))]),
        out_shape=jax.ShapeDtypeStruct((n_steps * MOE_TE, MOE_XW), BF16),
        compiler_params=_cparams(("arbitrary",)),
        name="dispatch",
    )(off, cnt, tail, hx, onehot, lstrict)


def _experts_kernel(grp_ref, used_ref, hs_ref, w1_ref, w3_ref, w2_ref, ys_ref):
    i = pl.program_id(0)
    d = ys_ref.shape[1]
    ne = EXPERTS_PER_GROUP

    @pl.when(i < used_ref[0])
    def _():
        t = hs_ref[:, 0:d]
        gates = ((hs_ref[:, d + 2 * LANES:d + 3 * LANES].astype(F32) + hs_ref[:, d + LANES:d + 2 * LANES].astype(F32))
                 + hs_ref[:, d:d + LANES].astype(F32))
        lane = lax.broadcasted_iota(jnp.int32, gates.shape, 1)
        acc = None
        for j in range(ne):
            a = jnp.dot(t, w1_ref[0, j], preferred_element_type=F32)
            b = jnp.dot(t, w3_ref[0, j], preferred_element_type=F32)
            gj = jnp.sum(jnp.where(lane == ROUTE_E_ROW + j, gates, 0.0), axis=-1, keepdims=True)
            y = gj * jnp.dot((_silu(a) * b).astype(BF16), w2_ref[0, j], preferred_element_type=F32)
            acc = y if acc is None else acc + y
        ys_ref[...] = acc.astype(BF16)

    @pl.when(i >= used_ref[0])
    def _():
        ys_ref[...] = jnp.zeros_like(ys_ref)


def _experts(grp, used, hs, w1_bf, w3_bf, w2_bf):
    rows, _ = hs.shape
    D = w2_bf.shape[3]
    wspec = lambda w: pl.BlockSpec((1,) + w.shape[1:], lambda i, grp, used: (grp[i], 0, 0, 0))
    return pl.pallas_call(
        _experts_kernel,
        grid_spec=pltpu.PrefetchScalarGridSpec(
            num_scalar_prefetch=2,
            grid=(rows // MOE_TE,),
            in_specs=[pl.BlockSpec((MOE_TE, MOE_XW), lambda i, grp, used: (i, 0)),
                      wspec(w1_bf), wspec(w3_bf), wspec(w2_bf)],
            out_specs=pl.BlockSpec((MOE_TE, D), lambda i, grp, used: (i, 0))),
        out_shape=jax.ShapeDtypeStruct((rows, D), BF16),
        compiler_params=_cparams(("arbitrary",)),
        name="experts",
    )(grp, used, hs, w1_bf, w3_bf, w2_bf)


def _combine_kernel(off_ref, cnt_ref, x1_ref, oh_ref, lstrict_ref, ga_ref, gf_ref, ys_ref, o_ref, buf_ref, sem,
                    *, n_tiles):
    k = pl.program_id(0)
    slot = k % 2

    def copies(kk, sl, action):
        def make(loc_row, seg_row, n):
            return pltpu.make_async_copy(ys_ref.at[pl.ds(seg_row, n)], buf_ref.at[sl, pl.ds(loc_row, n)], sem.at[sl])
        _run_copies(off_ref, cnt_ref, kk, make, action)

    @pl.when(k == 0)
    def _():
        buf_ref[...] = jnp.zeros_like(buf_ref)
        copies(k, slot, lambda cp: cp.start())

    @pl.when(k + 1 < n_tiles)
    def _():
        copies(k + 1, 1 - slot, lambda cp: cp.start())

    pt = _sort_matrix(_local_positions(oh_ref, lstrict_ref, cnt_ref, k))
    copies(k, slot, lambda cp: cp.wait())
    y = jnp.dot(pt, buf_ref[slot], preferred_element_type=F32)
    x2 = x1_ref[...] + ga_ref[0] * y
    o_ref[...] = x2 * lax.rsqrt(jnp.mean(x2 * x2, axis=-1, keepdims=True) + EPS) * gf_ref[...]


def _combine(off, cnt, x1, onehot, lstrict, ga, gfin, ys, S):
    T, D = x1.shape
    n_tiles = T // MOE_TM
    per = S // MOE_TM
    return pl.pallas_call(
        functools.partial(_combine_kernel, n_tiles=n_tiles),
        grid_spec=pltpu.PrefetchScalarGridSpec(
            num_scalar_prefetch=2,
            grid=(n_tiles,),
            in_specs=[pl.BlockSpec((MOE_TM, D), lambda k, off, cnt: (k, 0)),
                      pl.BlockSpec((MOE_TM, LANES), lambda k, off, cnt: (k, 0)),
                      pl.BlockSpec((MOE_TM, MOE_TM), lambda k, off, cnt: (0, 0)),
                      pl.BlockSpec((1, 1, D), lambda k, off, cnt: (k // per, 0, 0)),
                      pl.BlockSpec((1, D), lambda k, off, cnt: (0, 0)),
                      pl.BlockSpec(memory_space=pl.ANY)],
            out_specs=pl.BlockSpec((MOE_TM, D), lambda k, off, cnt: (k, 0)),
            scratch_shapes=[pltpu.VMEM((2, MOE_LOC, D), BF16),
                            pltpu.SemaphoreType.DMA((2,))]),
        out_shape=jax.ShapeDtypeStruct((T, D), F32),
        compiler_params=_cparams(("arbitrary",)),
        name="combine",
    )(off, cnt, x1, onehot, lstrict, ga, gfin, ys)


def _moe_schedule(cnt_tiles, n_steps):
    cnt = ((cnt_tiles + (MOE_ALIGN - 1)) // MOE_ALIGN) * MOE_ALIGN
    ends = jnp.cumsum(cnt, axis=0)
    total = ends[-1]
    ntile = (total + MOE_TE - 1) // MOE_TE
    cum = jnp.cumsum(ntile)
    base = (cum - ntile) * MOE_TE
    off = base[None, :] + ends - cnt
    i = jnp.arange(n_steps, dtype=jnp.int32)
    grp = jnp.minimum(jnp.sum(i[:, None] >= cum[None, :], axis=1), N_GROUPS - 1)
    grp = jnp.where(i < cum[-1], grp, grp[jnp.maximum(cum[-1] - 1, 0)])
    tail = jnp.concatenate([base + total, ntile * MOE_TE - total, cum[-1:]])
    as_i32 = lambda a: a.reshape(-1).astype(jnp.int32)
    return as_i32(off), as_i32(cnt), as_i32(tail), as_i32(grp), as_i32(cum[-1:])


def _rope_tables(S):
    t = jnp.arange(S)
    pos = jnp.stack([t // GRID_W, t % GRID_W], axis=-1).astype(F32)
    inv = ROPE_BASE ** (-jnp.arange(0, ROPE_AXIS_DIM, 2, dtype=F32) / ROPE_AXIS_DIM)
    ang = pos[:, :, None] * inv
    cos, sin = jnp.cos(ang), jnp.sin(ang)
    cos_h = jnp.concatenate([cos, cos], axis=-1).reshape(S, NA_HEAD_DIM)
    sin_h = jnp.concatenate([-sin, sin], axis=-1).reshape(S, NA_HEAD_DIM)
    return jnp.tile(cos_h, (1, NA_HEADS)), jnp.tile(sin_h, (1, NA_HEADS))


def kernel(x, c, ctx, c_ctx, w_mod, b_mod, norm_mix, norm_ffn, w_in, w_out, na_rpb, hg_lb, hg_norm,
           w_grp, b_grp, w_exp, b_exp, w1, w3, w2, norm_final):
    B, S, D = x.shape
    T = B * S
    assert w_mod.shape[0] == 1, "single-layer kernel"

    rows = -(-(B + 1) // SUBLANES) * SUBLANES
    cc = jnp.zeros((rows, D), F32).at[:B].set(c).at[B].set(c_ctx)
    mod = _modulation(cc, w_mod[0], b_mod[0])
    sh_a, sc_a, ga_a, sh_f, sc_f, ga_f = [m.reshape(B, 1, D) for m in jnp.split(mod[:B], 6, axis=-1)]
    csh_a, csc_a = [m.reshape(1, D) for m in jnp.split(mod[B], 6)[:2]]

    w_in_bf = w_in[0].astype(BF16)
    w_ctx_bf = jnp.concatenate([w_in_bf[:, i * SEG:(i + 1) * SEG] for i in (1, 2, 4, 5, 6)], axis=1)
    cos_t, sin_t = _rope_tables(S)
    g_mix = norm_mix[0].reshape(1, D)

    lb = jnp.cumsum(jax.nn.softmax(hg_lb.astype(F32), axis=0), axis=0)[0]
    proj = _project(x, g_mix, sc_a, sh_a, w_in_bf, cos_t, sin_t, lb, tm=512)
    cproj = _project_ctx(ctx, g_mix, csc_a, csh_a, w_ctx_bf)

    na_out = _neighbourhood_attention(proj, cproj, _na_bias_table(na_rpb[0]))

    gain = jnp.tile(hg_norm[0].astype(F32), HG_HEADS).reshape(1, HG_WIDTH)
    hg_out = _hgrn(proj, cproj, lb, gain)

    e_rows = slice(ROUTE_E_ROW, ROUTE_E_ROW + N_EXPERTS)
    wr = jnp.zeros((LANES, D), F32).at[:N_GROUPS].set(w_grp[0].T).at[e_rows].set(w_exp[0].T)
    br = jnp.zeros((LANES, 1), F32).at[:N_GROUPS, 0].set(b_grp[0]).at[e_rows, 0].set(b_exp[0])
    hi_f32 = lax.bitcast_convert_type(lax.bitcast_convert_type(wr, jnp.uint32) & jnp.uint32(0xFFFF0000), F32)
    wr_hi = hi_f32.astype(BF16)
    wr_lo = (wr - hi_f32).astype(BF16)
    x1, hx, onehot, cnt_tiles = _outproj(x.reshape(T, D), na_out.reshape(T, NA_WIDTH), hg_out.reshape(T, HG_WIDTH),
                                         w_out[0].astype(BF16), ga_a, norm_ffn[0].reshape(1, D), sc_f, sh_f,
                                         wr_hi, wr_lo, br, S=S)

    n_steps = _moe_steps(T)
    off, cnt, tail, grp, used = _moe_schedule(cnt_tiles[:, 0, :N_GROUPS].astype(jnp.int32), n_steps)
    lstrict = jnp.asarray(np.tril(np.ones((MOE_TM, MOE_TM), np.float32), -1), BF16)
    by_group = lambda w: w[0].astype(BF16).reshape(N_GROUPS, EXPERTS_PER_GROUP, *w.shape[2:])
    hs = _dispatch(off, cnt, tail, hx, onehot, lstrict, n_steps)
    ys = _experts(grp, used, hs, by_group(w1), by_group(w3), by_group(w2))
    out = _combine(off, cnt, x1, onehot, lstrict, ga_f, norm_final.reshape(1, D), ys, S)
    return out.reshape(B, S, D)
```

```python
import functools

import numpy as np
import jax
import jax.numpy as jnp
from jax import lax
from jax.experimental import pallas as pl
from jax.experimental.pallas import tpu as pltpu

F32 = jnp.float32
BF16 = jnp.bfloat16
HIGHEST = lax.Precision.HIGHEST

D_MODEL = 1024
GRID_W = 64
NA_HEADS = 8
NA_HEAD_DIM = 64
NA_WIDTH = NA_HEADS * NA_HEAD_DIM
NA_KH = 8
NA_KW = 16
ROPE_AXIS_DIM = NA_HEAD_DIM // 2
ROPE_BASE = 10000.0
HG_HEADS = 4
HG_DK = 128
HG_WIDTH = HG_HEADS * HG_DK
HG_CHUNK = 64
SEG = 512
N_GROUPS = 4
EXPERTS_PER_GROUP = 4
N_EXPERTS = N_GROUPS * EXPERTS_PER_GROUP
D_EXPERT = 512
EPS = 1e-6
NEG = -1e30
LOG2E = 1.4426950408889634
LANES = 128
SUBLANES = 8
VMEM_LIMIT = 56 * 1024 * 1024

HG_LEVELS = (32, 16, 8, 4, 2, 1)
HG_CHUNKS_PER_STEP = 4


def _cparams(sem):
    return pltpu.CompilerParams(dimension_semantics=sem, vmem_limit_bytes=VMEM_LIMIT)


def _sigmoid(x):
    return 1.0 / (1.0 + jnp.exp(-x))


def _silu(x):
    return x * _sigmoid(x)


def _mod_kernel(c_ref, w_ref, b_ref, o_ref):
    s = _silu(c_ref[...])
    o_ref[...] = jnp.dot(s, w_ref[...], precision=HIGHEST, preferred_element_type=F32) + b_ref[...]


def _modulation(cc, w_mod, b_mod):
    rows, d = cc.shape
    n = w_mod.shape[1]
    tn = 1024
    return pl.pallas_call(
        _mod_kernel,
        grid=(n // tn,),
        in_specs=[pl.BlockSpec((rows, d), lambda j: (0, 0)),
                  pl.BlockSpec((d, tn), lambda j: (0, j)),
                  pl.BlockSpec((1, tn), lambda j: (0, j))],
        out_specs=pl.BlockSpec((rows, tn), lambda j: (0, j)),
        out_shape=jax.ShapeDtypeStruct((rows, n), F32),
        compiler_params=_cparams(("arbitrary",)),
        name="mod",
    )(cc, w_mod, b_mod.reshape(1, n))


def _norm_mod(x, g, sc, sh):
    y = x * lax.rsqrt(jnp.mean(x * x, axis=-1, keepdims=True) + EPS)
    return (y * g) * (1.0 + sc) + sh


def _rope(a, cos, sin):
    lane = lax.broadcasted_iota(jnp.int32, a.shape, 1)
    first = (lane % ROPE_AXIS_DIM) < (ROPE_AXIS_DIM // 2)
    up = pltpu.roll(a, LANES - ROPE_AXIS_DIM // 2, axis=1)
    dn = pltpu.roll(a, ROPE_AXIS_DIM // 2, axis=1)
    return a * cos + jnp.where(first, up, dn) * sin


P_QRAW, P_QROT, P_KROT, P_V, P_HQ, P_HI, P_KF, P_GFH, P_GFL, P_KB, P_GBH, P_GBL, P_HG = range(13)
P_SEGS = 13


def _proj_kernel(x_ref, g_ref, sc_ref, sh_ref, w_ref, cos_ref, sin_ref, lb_ref, o_ref):
    h = _norm_mod(x_ref[0], g_ref[...], sc_ref[0], sh_ref[0]).astype(BF16)
    scale = NA_HEAD_DIM ** -0.5 * LOG2E

    def put(seg, val):
        o_ref[0, :, seg * SEG:(seg + 1) * SEG] = val.astype(BF16)

    for j in range(8):
        acc = jnp.dot(h, w_ref[:, j * SEG:(j + 1) * SEG], preferred_element_type=F32)
        if j <= 1:
            rot = jnp.concatenate(
                [_rope(acc[:, p * LANES:(p + 1) * LANES], cos_ref[:, p * LANES:(p + 1) * LANES],
                       sin_ref[:, p * LANES:(p + 1) * LANES]) for p in range(SEG // LANES)], axis=1)
            if j == 0:
                put(P_QRAW, acc * scale)
                put(P_QROT, rot * scale)
            else:
                put(P_KROT, rot)
        elif j == 2:
            put(P_V, acc)
        elif j == 3:
            put(P_HQ, _silu(acc))
        elif j == 4:
            put(P_HI, acc)
        elif j in (5, 6):
            lb = lb_ref[j - 5:j - 4, :]
            f = lb + (1.0 - lb) * _sigmoid(acc)
            g2 = jnp.log2(f)
            hi = g2.astype(BF16)
            base = P_KF if j == 5 else P_KB
            put(base, 1.0 - f)
            put(base + 1, hi)
            put(base + 2, g2 - hi.astype(F32))
        else:
            put(P_HG, acc)


def _project(x, g, sc, sh, w_bf, cos_t, sin_t, lb, tm):
    B, S, D = x.shape
    return pl.pallas_call(
        _proj_kernel,
        grid=(S // tm, B),
        in_specs=[pl.BlockSpec((1, tm, D), lambda s, b: (b, s, 0)),
                  pl.BlockSpec((1, D), lambda s, b: (0, 0)),
                  pl.BlockSpec((1, 1, D), lambda s, b: (b, 0, 0)),
                  pl.BlockSpec((1, 1, D), lambda s, b: (b, 0, 0)),
                  pl.BlockSpec((D, 8 * SEG), lambda s, b: (0, 0)),
                  pl.BlockSpec((tm, SEG), lambda s, b: (s, 0)),
                  pl.BlockSpec((tm, SEG), lambda s, b: (s, 0)),
                  pl.BlockSpec((2, HG_WIDTH), lambda s, b: (0, 0))],
        out_specs=pl.BlockSpec((1, tm, P_SEGS * SEG), lambda s, b: (b, s, 0)),
        out_shape=jax.ShapeDtypeStruct((B, S, P_SEGS * SEG), BF16),
        compiler_params=_cparams(("arbitrary", "arbitrary")),
        name="proj",
    )(x, g, sc, sh, w_bf, cos_t, sin_t, lb)


def _ctxproj_kernel(x_ref, g_ref, sc_ref, sh_ref, w_ref, o_ref):
    h = _norm_mod(x_ref[0], g_ref[...], sc_ref[...], sh_ref[...]).astype(BF16)
    for j in range(5):
        acc = jnp.dot(h, w_ref[:, j * SEG:(j + 1) * SEG], preferred_element_type=F32)
        o_ref[0, :, j * SEG:(j + 1) * SEG] = acc.astype(BF16)


def _project_ctx(ctx, g, sc, sh, w_bf):
    B, L, D = ctx.shape
    return pl.pallas_call(
        _ctxproj_kernel,
        grid=(B,),
        in_specs=[pl.BlockSpec((1, L, D), lambda b: (b, 0, 0)),
                  pl.BlockSpec((1, D), lambda b: (0, 0)),
                  pl.BlockSpec((1, D), lambda b: (0, 0)),
                  pl.BlockSpec((1, D), lambda b: (0, 0)),
                  pl.BlockSpec((D, 5 * SEG), lambda b: (0, 0))],
        out_specs=pl.BlockSpec((1, L, 5 * SEG), lambda b: (b, 0, 0)),
        out_shape=jax.ShapeDtypeStruct((B, L, 5 * SEG), BF16),
        compiler_params=_cparams(("arbitrary",)),
        name="ctxproj",
    )(ctx, g, sc, sh, w_bf)


NA_ROWS_PER_STEP = 4


def _na_kernel(qraw_ref, qrot_ref, k_ref, v_ref, ck_ref, cv_ref, bias_ref, o_ref, *, rows):
    nk = NA_KH * GRID_W
    lane = lax.broadcasted_iota(jnp.int32, (GRID_W, LANES), 1)
    sel0 = lane < NA_HEAD_DIM
    nt = (((1,), (1,)), ((), ()))
    n_pairs = NA_WIDTH // LANES
    ones = jnp.ones((nk + ck_ref.shape[1], LANES), BF16)
    units = [(j, p) for j in range(NA_ROWS_PER_STEP) for p in range(n_pairs)]

    def window(j):
        r = pl.program_id(1) * NA_ROWS_PER_STEP + j
        rs = jnp.clip(r - NA_KH // 2, 0, rows - NA_KH)
        return pl.multiple_of(rs * GRID_W, GRID_W), pl.multiple_of((rs - r + NA_KH - 1) * GRID_W, GRID_W)

    def query_slots(ref, j, cols):
        q = ref[0, j * GRID_W:(j + 1) * GRID_W, cols]
        zero = jnp.zeros_like(q)
        return jnp.concatenate([jnp.where(sel0, q, zero), jnp.where(sel0, zero, q)], axis=0)

    def scores(j, p):
        cols = slice(p * LANES, (p + 1) * LANES)
        start, bias_row = window(j)
        s_loc = lax.dot_general(k_ref[0, pl.ds(start, nk), cols], query_slots(qrot_ref, j, cols), nt,
                                preferred_element_type=F32) + bias_ref[p, pl.ds(bias_row, nk), :]
        s_ctx = lax.dot_general(ck_ref[0, :, cols], query_slots(qraw_ref, j, cols), nt,
                                preferred_element_type=F32)
        return s_loc, s_ctx

    def softmax(s_loc, s_ctx):
        m = jnp.maximum(jnp.max(s_loc, axis=0, keepdims=True), jnp.max(s_ctx, axis=0, keepdims=True))
        return jnp.concatenate([jnp.exp2(s_loc - m).T, jnp.exp2(s_ctx - m).T], axis=1).astype(BF16)

    def values(j, p, probs):
        cols = slice(p * LANES, (p + 1) * LANES)
        start, _ = window(j)
        vals = jnp.concatenate([v_ref[0, pl.ds(start, nk), cols], cv_ref[0, :, cols]], axis=0)
        oe = jnp.dot(probs, jnp.concatenate([vals, ones], axis=1), preferred_element_type=F32)
        res = oe[:, :LANES] / oe[:, LANES:]
        o_ref[0, j * GRID_W:(j + 1) * GRID_W, cols] = jnp.where(sel0, res[0:GRID_W],
                                                                res[GRID_W:2 * GRID_W]).astype(BF16)

    s_all = [scores(j, p) for j, p in units]
    probs = [softmax(*s) for s in s_all]
    for (j, p), pr in zip(units, probs):
        values(j, p, pr)


def _na_bias_table(rpb):
    qc = np.arange(GRID_W)[None, :]
    kc = np.arange(GRID_W)[:, None]
    cs = np.clip(qc - NA_KW // 2, 0, GRID_W - NA_KW)
    valid = (kc >= cs) & (kc < cs + NA_KW)
    col_off = np.clip(kc - qc + NA_KW - 1, 0, 2 * NA_KW - 2)
    t = (rpb.astype(F32) * LOG2E)[:, :, col_off]
    t = jnp.where(jnp.asarray(valid)[None, None], t, NEG)
    n_pairs = NA_HEADS // 2
    n_ro = 2 * NA_KH - 1
    t = t.reshape(n_pairs, 2, n_ro, GRID_W, GRID_W).transpose(0, 2, 3, 1, 4)
    return t.reshape(n_pairs, n_ro * GRID_W, 2 * GRID_W)


def _neighbourhood_attention(proj, cproj, bias_tbl):
    B, S, _ = proj.shape
    L = cproj.shape[1]
    rows = S // GRID_W
    tq = NA_ROWS_PER_STEP * GRID_W
    return pl.pallas_call(
        functools.partial(_na_kernel, rows=rows),
        grid=(B, rows // NA_ROWS_PER_STEP),
        in_specs=[pl.BlockSpec((1, tq, SEG), lambda b, r: (b, r, P_QRAW)),
                  pl.BlockSpec((1, tq, SEG), lambda b, r: (b, r, P_QROT)),
                  pl.BlockSpec((1, S, SEG), lambda b, r: (b, 0, P_KROT)),
                  pl.BlockSpec((1, S, SEG), lambda b, r: (b, 0, P_V)),
                  pl.BlockSpec((1, L, SEG), lambda b, r: (b, 0, 0)),
                  pl.BlockSpec((1, L, SEG), lambda b, r: (b, 0, 1)),
                  pl.BlockSpec(bias_tbl.shape, lambda b, r: (0, 0, 0))],
        out_specs=pl.BlockSpec((1, tq, SEG), lambda b, r: (b, r, 0)),
        out_shape=jax.ShapeDtypeStruct((B, S, NA_WIDTH), BF16),
        compiler_params=_cparams(("arbitrary", "arbitrary")),
        name="na",
    )(proj, proj, proj, proj, cproj, cproj, bias_tbl)


def _split3(g):
    g1 = g.astype(BF16)
    r1 = g - g1.astype(F32)
    g2 = r1.astype(BF16)
    g3 = (r1 - g2.astype(F32)).astype(BF16)
    return g1, g2, g3


def _cumdot(c_bf, g):
    g1, g2, g3 = _split3(g)
    d = lambda a: jnp.dot(c_bf, a, preferred_element_type=F32)
    return (d(g3) + d(g2)) + d(g1)


def _forget(pre, lb):
    f = lb + (1.0 - lb) * _sigmoid(pre.astype(F32))
    return 1.0 - f, jnp.log(f)


def _hg_consts():
    c = HG_CHUNK
    t = np.arange(c)[:, None]
    u = np.arange(c)[None, :]
    cm, role, masks = [], [], []
    for fwd in (True, False):
        blocks = [(u <= t) if fwd else (u >= t)]
        roles, ms = [], []
        for m in HG_LEVELS:
            blk = t // (2 * m)
            if fwd:
                mid = 2 * m * blk + m - 1
                is_q = (t % (2 * m)) >= m
                expo = np.where(is_q, (u > mid) & (u <= t), (u > t) & (u <= mid))
            else:
                mid = 2 * m * blk + m
                is_q = (t % (2 * m)) < m
                expo = np.where(is_q, (u >= t) & (u < mid), (u >= mid) & (u < t))
            blocks.append(expo)
            roles.append(np.broadcast_to(is_q, (c, 2 * HG_DK)))
            ms.append((blk == blk.T) & is_q & ~is_q.T)
        ms.append(t == u)
        full = np.concatenate(blocks, axis=0).astype(np.float32)
        cm.append(np.concatenate([full, full], axis=1))
        role.append(np.stack(roles).astype(np.float32))
        masks.append(np.stack([np.concatenate([x, x], axis=1) for x in ms]).astype(np.float32))
    ones_bd = np.kron(np.eye(2, dtype=np.float32), np.ones((HG_DK, c), np.float32))
    return (jnp.asarray(np.stack(cm), BF16), jnp.asarray(np.stack(role), BF16),
            jnp.asarray(np.stack(masks)), jnp.asarray(ones_bd, BF16))


def _block_diag(x, zeros):
    return jnp.concatenate([jnp.concatenate([x[:, :HG_DK], zeros], axis=1),
                            jnp.concatenate([zeros, x[:, HG_DK:]], axis=1)], axis=0)


def _hg_exponents(gh, gl, cmat_ref, di):
    return jnp.dot(cmat_ref[di], jnp.concatenate([gh, gl], axis=0), preferred_element_type=F32)


def _hg_intra(q, k, dall, role_ref, masks_ref, ones_ref, di):
    c = HG_CHUNK
    nl = len(HG_LEVELS)
    nt = (((1,), (1,)), ((), ()))
    b = dall[0:c]
    bend = b[c - 1:c] if di == 0 else b[0:1]
    qe = q * jnp.exp2(b).astype(BF16)
    ke = k * jnp.exp2(bend - b).astype(BF16)
    zeros = jnp.zeros((c, HG_DK), BF16)
    a = None
    for li in range(nl):
        e = jnp.exp2(dall[(li + 1) * c:(li + 2) * c]).astype(BF16)
        x = jnp.where(role_ref[di, li] > 0.5, q, k) * e
        r = lax.dot_general(x, _block_diag(x, zeros), nt, preferred_element_type=F32) * masks_ref[di, li]
        a = r if a is None else a + r
    a = a + jnp.dot(q * k, ones_ref[...], preferred_element_type=F32) * masks_ref[di, nl]
    return a.astype(BF16), qe, ke, bend


def _hg_outputs(a, qe, ke, bend, v, st_ref, di, pair):
    nt = (((1,), (1,)), ((), ()))
    tn = (((0,), (0,)), ((), ()))
    o = jnp.dot(a, _block_diag(v, jnp.zeros((HG_CHUNK, HG_DK), BF16)), preferred_element_type=F32)
    outs = []
    for hh in range(2):
        hs = slice(hh * HG_DK, (hh + 1) * HG_DK)
        st = st_ref[di, 2 * pair + hh]
        outs.append(o[:, hs] + lax.dot_general(qe[:, hs], st.astype(BF16), nt, preferred_element_type=F32))
        upd = lax.dot_general(v[:, hs], ke[:, hs], tn, preferred_element_type=F32)
        st_ref[di, 2 * pair + hh] = jnp.exp2(bend[:, hs]) * st + upd
    return outs


def _hgrn_kernel(sq_ref, hi_ref, kf_ref, gfh_ref, gfl_ref, kb_ref, gbh_ref, gbl_ref, hg_ref,
                 ci_ref, cff_ref, cfb_ref, lb_ref, gain_ref, cmat_ref, role_ref, masks_ref, ones_ref,
                 ustrict_ref, lstrict_ref, o_ref, acc_ref, st_ref, *, n_chunks):
    c = HG_CHUNK
    tn = (((0,), (0,)), ((), ()))

    for h in range(HG_HEADS):
        hs = slice(h * HG_DK, (h + 1) * HG_DK)
        ci = ci_ref[0, :, hs]
        for di, (pre_ref, cm_ref) in enumerate(((cff_ref, ustrict_ref), (cfb_ref, lstrict_ref))):
            k, g = _forget(pre_ref[0, :, hs], lb_ref[di:di + 1, hs])
            ke = (k * jnp.exp(_cumdot(cm_ref[...], g))).astype(BF16)
            st_ref[di, h] = lax.dot_general(ci, ke, tn, preferred_element_type=F32)

    def step(i, second):
        units = [(sub, pair, di) for sub in range(HG_CHUNKS_PER_STEP)
                 for pair in range(HG_HEADS // 2) for di in range(2)]

        def window(u):
            sub, pair, di = units[u]
            ci = i * HG_CHUNKS_PER_STEP + sub
            first = ci * c if di == 0 else (n_chunks - 1 - ci) * c
            return pl.ds(pl.multiple_of(first, c), c), slice(2 * pair * HG_DK, 2 * (pair + 1) * HG_DK)

        def exponents(u):
            rr, cols = window(u)
            gh_ref, gl_ref = (gfh_ref, gfl_ref) if units[u][2] == 0 else (gbh_ref, gbl_ref)
            return _hg_exponents(gh_ref[0, rr, cols], gl_ref[0, rr, cols], cmat_ref, units[u][2])

        def intra(u, dall):
            rr, cols = window(u)
            k_ref = kf_ref if units[u][2] == 0 else kb_ref
            return _hg_intra(sq_ref[0, rr, cols], k_ref[0, rr, cols], dall, role_ref, masks_ref, ones_ref,
                             units[u][2])

        def finish(u, parts):
            _, pair, di = units[u]
            rr, cols = window(u)
            outs = _hg_outputs(*parts, hi_ref[0, rr, cols], st_ref, di, pair)
            for hh, o in enumerate(outs):
                hs = slice((2 * pair + hh) * HG_DK, (2 * pair + hh + 1) * HG_DK)
                if not second:
                    acc_ref[rr, hs] = o
                else:
                    ot = acc_ref[rr, hs] + o
                    y = ot * lax.rsqrt(jnp.mean(ot * ot, axis=-1, keepdims=True) + EPS) * gain_ref[:, hs]
                    o_ref[0, rr, hs] = (y * _silu(hg_ref[0, rr, hs].astype(F32))).astype(BF16)

        n = len(units)
        dalls = [exponents(u) for u in range(n)]
        parts = [intra(u, dalls[u]) for u in range(n)]
        for u in range(n):
            finish(u, parts[u])

    def first_half(i, carry):
        step(i, False)
        return carry

    def second_half(i, carry):
        step(i, True)
        return carry

    n_steps = n_chunks // HG_CHUNKS_PER_STEP
    lax.fori_loop(0, n_steps // 2, first_half, 0)
    lax.fori_loop(n_steps // 2, n_steps, second_half, 0)


def _hgrn(proj, cproj, lb, gain):
    B, S, _ = proj.shape
    L = cproj.shape[1]
    c = HG_CHUNK
    nl = len(HG_LEVELS)
    tri_l = np.tril(np.ones((L, L), np.float32), -1)
    consts = list(_hg_consts()) + [jnp.asarray(tri_l.T, BF16), jnp.asarray(tri_l, BF16)]
    seq = lambda j: pl.BlockSpec((1, S, SEG), lambda b: (b, 0, j))
    cseq = lambda j: pl.BlockSpec((1, L, SEG), lambda b: (b, 0, j))
    full2 = lambda n, m: pl.BlockSpec((n, m), lambda b: (0, 0))
    return pl.pallas_call(
        functools.partial(_hgrn_kernel, n_chunks=S // c),
        grid=(B,),
        in_specs=[seq(P_HQ), seq(P_HI), seq(P_KF), seq(P_GFH), seq(P_GFL), seq(P_KB), seq(P_GBH), seq(P_GBL),
                  seq(P_HG), cseq(2), cseq(3), cseq(4),
                  full2(2, HG_WIDTH), full2(1, HG_WIDTH),
                  pl.BlockSpec(consts[0].shape, lambda b: (0, 0, 0)),
                  pl.BlockSpec((2, nl, c, 2 * HG_DK), lambda b: (0, 0, 0, 0)),
                  pl.BlockSpec((2, nl + 1, c, 2 * c), lambda b: (0, 0, 0, 0)),
                  full2(2 * HG_DK, 2 * c), full2(L, L), full2(L, L)],
        out_specs=pl.BlockSpec((1, S, HG_WIDTH), lambda b: (b, 0, 0)),
        out_shape=jax.ShapeDtypeStruct((B, S, HG_WIDTH), BF16),
        scratch_shapes=[pltpu.VMEM((S, HG_WIDTH), F32),
                        pltpu.VMEM((2, HG_HEADS, HG_DK, HG_DK), F32)],
        compiler_params=_cparams(("arbitrary",)),
        name="hgrn",
    )(*([proj] * 9), cproj, cproj, cproj, lb, gain, *consts)


ROUTE_E_ROW = 8


def _route_t(lt):
    tm = lt.shape[1]
    row8 = lax.broadcasted_iota(jnp.int32, (SUBLANES, tm), 0).astype(F32)
    gl = jnp.where(row8 < N_GROUPS, lt[0:SUBLANES], -jnp.inf)
    gmax = jnp.max(gl, axis=0, keepdims=True)
    g_sel = jnp.min(jnp.where(gl == gmax, row8, float(SUBLANES)), axis=0, keepdims=True)
    g_w = 1.0 / jnp.sum(jnp.exp(gl - gmax), axis=0, keepdims=True)
    row16 = lax.broadcasted_iota(jnp.int32, (N_EXPERTS, tm), 0).astype(F32)
    first = g_sel * EXPERTS_PER_GROUP
    in_grp = (row16 >= first) & (row16 < first + EXPERTS_PER_GROUP)
    e1 = jnp.where(in_grp, lt[ROUTE_E_ROW:ROUTE_E_ROW + N_EXPERTS], -jnp.inf)
    v1 = jnp.max(e1, axis=0, keepdims=True)
    i1 = jnp.min(jnp.where(e1 == v1, row16, float(N_EXPERTS)), axis=0, keepdims=True)
    e2 = jnp.where(row16 == i1, -jnp.inf, e1)
    v2 = jnp.max(e2, axis=0, keepdims=True)
    i2 = jnp.min(jnp.where(e2 == v2, row16, float(N_EXPERTS)), axis=0, keepdims=True)
    t = jnp.exp(v2 - v1)
    w1 = g_w / (1.0 + t)
    w2 = g_w * t / (1.0 + t)
    onehot = jnp.where(row8 == g_sel, 1.0, 0.0)
    gates = jnp.where(row8 == i1 - first, w1, jnp.where(row8 == i2 - first, w2, 0.0))
    return onehot, gates


def _outproj_kernel(x_ref, na_ref, hg_ref, w_ref, ga_ref, g_ref, sc_ref, sh_ref, wrh_ref, wrl_ref, br_ref,
                    x1_ref, hx_ref, oh_ref, cnt_ref):
    tm, d = x_ref.shape
    nt = (((1,), (1,)), ((), ()))
    mix = (jnp.dot(na_ref[...], w_ref[0:NA_WIDTH, :], preferred_element_type=F32)
           + jnp.dot(hg_ref[...], w_ref[NA_WIDTH:, :], preferred_element_type=F32))
    x1 = x_ref[...] + ga_ref[0] * mix
    x1_ref[...] = x1
    h2 = _norm_mod(x1, g_ref[...], sc_ref[0], sh_ref[0])
    h_hi = h2.astype(BF16)
    h_lo = (h2 - h_hi.astype(F32)).astype(BF16)
    hx_ref[:, 0:d] = h_hi
    lt = ((lax.dot_general(wrl_ref[...], h_hi, nt, preferred_element_type=F32)
           + lax.dot_general(wrh_ref[...], h_lo, nt, preferred_element_type=F32))
          + lax.dot_general(wrh_ref[...], h_hi, nt, preferred_element_type=F32)) + br_ref[...]
    onehot_t, gates_t = _route_t(lt)
    tok = jnp.concatenate([onehot_t, gates_t, jnp.zeros((LANES - 2 * SUBLANES, tm), F32)], axis=0).T
    lane = lax.broadcasted_iota(jnp.int32, tok.shape, 1)
    onehot = jnp.where(lane < ROUTE_E_ROW, tok, 0.0)
    gates = jnp.where(lane >= ROUTE_E_ROW, tok, 0.0)
    for j, piece in enumerate(_split3(gates)):
        hx_ref[:, d + j * LANES:d + (j + 1) * LANES] = piece
    oh_ref[...] = onehot
    cnt_ref[0] = jnp.broadcast_to(jnp.sum(onehot, axis=0, keepdims=True), (SUBLANES, LANES))


def _outproj(x2d, na2d, hg2d, w_out_bf, ga, g, sc, sh, wr_hi, wr_lo, br, S):
    T, D = x2d.shape
    tm = MOE_TM
    per = S // tm
    tok = lambda w: pl.BlockSpec((tm, w), lambda i: (i, 0))
    bat = pl.BlockSpec((1, 1, D), lambda i: (i // per, 0, 0))
    return pl.pallas_call(
        _outproj_kernel,
        grid=(T // tm,),
        in_specs=[tok(D), tok(NA_WIDTH), tok(HG_WIDTH),
                  pl.BlockSpec((NA_WIDTH + HG_WIDTH, D), lambda i: (0, 0)),
                  bat, pl.BlockSpec((1, D), lambda i: (0, 0)), bat, bat,
                  pl.BlockSpec((LANES, D), lambda i: (0, 0)),
                  pl.BlockSpec((LANES, D), lambda i: (0, 0)),
                  pl.BlockSpec((LANES, 1), lambda i: (0, 0))],
        out_specs=[tok(D), tok(MOE_XW), tok(LANES),
                   pl.BlockSpec((1, SUBLANES, LANES), lambda i: (i, 0, 0))],
        out_shape=[jax.ShapeDtypeStruct((T, D), F32), jax.ShapeDtypeStruct((T, MOE_XW), BF16),
                   jax.ShapeDtypeStruct((T, LANES), F32),
                   jax.ShapeDtypeStruct((T // tm, SUBLANES, LANES), F32)],
        compiler_params=_cparams(("arbitrary",)),
        name="outproj",
    )(x2d, na2d, hg2d, w_out_bf, ga, g, sc, sh, wr_hi, wr_lo, br)


MOE_TM = 512
MOE_ALIGN = 16
MOE_LOC = 640
MOE_XW = D_MODEL + 3 * LANES
MOE_TE = 512
MOE_BITS = tuple(1 << b for b in range(9, 3, -1))


def _moe_steps(T):
    worst = T + (T // MOE_TM) * N_GROUPS * (MOE_ALIGN - 1)
    return -(-worst // MOE_TE) + N_GROUPS


def _local_positions(oh_ref, lstrict_ref, cnt_ref, k):
    onehot = oh_ref[...]
    ranks = jnp.dot(lstrict_ref[...], onehot.astype(BF16), preferred_element_type=F32)
    lane = lax.broadcasted_iota(jnp.int32, (1, LANES), 1)
    base = jnp.zeros((1, LANES), F32)
    o = jnp.int32(0)
    for g in range(N_GROUPS):
        base = jnp.where(lane == g, o.astype(F32), base)
        o = o + cnt_ref[k * N_GROUPS + g]
    return jnp.sum(onehot * (ranks + base), axis=-1, keepdims=True)


def _sort_matrix(lpos):
    col = lax.broadcasted_iota(jnp.int32, (MOE_TM, MOE_LOC), 1).astype(F32)
    return jnp.where(lpos == col, 1.0, 0.0).astype(BF16)


def _block_copies(n_rows, src_row, dst_row, bits, make_copy, action):
    for bit in bits:
        part = n_rows & (-2 * bit)

        @pl.when((n_rows & bit) != 0)
        def _(part=part, bit=bit):
            action(make_copy(pl.multiple_of(src_row + part, MOE_ALIGN), pl.multiple_of(dst_row + part, MOE_ALIGN), bit))


def _run_copies(off_ref, cnt_ref, k, make_copy, action):
    o = jnp.int32(0)
    for g in range(N_GROUPS):
        c = cnt_ref[k * N_GROUPS + g]
        _block_copies(c, o, off_ref[k * N_GROUPS + g], MOE_BITS, make_copy, action)
        o = o + c


def _dispatch_kernel(off_ref, cnt_ref, tail_ref, hx_ref, oh_ref, lstrict_ref, hs_ref, buf_ref, zero_ref, sem,
                     *, n_tiles, n_steps):
    k = pl.program_id(0)
    slot = k % 2

    def copies(kk, sl, action):
        def make(src_row, dst_row, n):
            return pltpu.make_async_copy(buf_ref.at[sl, pl.ds(src_row, n)], hs_ref.at[pl.ds(dst_row, n)], sem.at[sl])
        _run_copies(off_ref, cnt_ref, kk, make, action)

    @pl.when(k >= 2)
    def _():
        copies(k - 2, slot, lambda cp: cp.wait())

    pt = _sort_matrix(_local_positions(oh_ref, lstrict_ref, cnt_ref, k))
    srt = lax.dot_general(pt, hx_ref[...], (((0,), (0,)), ((), ())), preferred_element_type=F32)
    buf_ref[slot] = srt.astype(BF16)
    copies(k, slot, lambda cp: cp.start())

    @pl.when(k == n_tiles - 1)
    def _():
        zero_ref[...] = jnp.zeros_like(zero_ref)
        n_used = tail_ref[2 * N_GROUPS]

        def zero_copy(src_row, dst_row, n):
            return pltpu.make_async_copy(zero_ref.at[pl.ds(src_row, n)], hs_ref.at[pl.ds(dst_row, n)], sem.at[2])

        def tile_copy(i):
            return zero_copy(0, pl.multiple_of(i * MOE_TE, MOE_TE), MOE_TE)

        def fills(action):
            for g in range(N_GROUPS):
                _block_copies(tail_ref[N_GROUPS + g], jnp.int32(0), tail_ref[g], MOE_BITS[1:], zero_copy, action)

        fills(lambda cp: cp.start())
        lax.fori_loop(n_used, n_steps, lambda i, c: (tile_copy(i).start(), c)[1], 0)
        if n_tiles >= 2:
            copies(k - 1, 1 - slot, lambda cp: cp.wait())
        copies(k, slot, lambda cp: cp.wait())
        fills(lambda cp: cp.wait())
        lax.fori_loop(n_used, n_steps, lambda i, c: (tile_copy(i).wait(), c)[1], 0)


def _dispatch(off, cnt, tail, hx, onehot, lstrict, n_steps):
    T = hx.shape[0]
    n_tiles = T // MOE_TM
    return pl.pallas_call(
        functools.partial(_dispatch_kernel, n_tiles=n_tiles, n_steps=n_steps),
        grid_spec=pltpu.PrefetchScalarGridSpec(
            num_scalar_prefetch=3,
            grid=(n_tiles,),
            in_specs=[pl.BlockSpec((MOE_TM, MOE_XW), lambda k, *_: (k, 0)),
                      pl.BlockSpec((MOE_TM, LANES), lambda k, *_: (k, 0)),
                      pl.BlockSpec((MOE_TM, MOE_TM), lambda k, *_: (0, 0))],
            out_specs=pl.BlockSpec(memory_space=pl.ANY),
            scratch_shapes=[pltpu.VMEM((2, MOE_LOC, MOE_XW), BF16),
                            pltpu.VMEM((MOE_TE, MOE_XW), BF16),
                            pltpu.SemaphoreType.DMA((3,))]),
        out_shape=jax.ShapeDtypeStruct((n_steps * MOE_TE, MOE_XW), BF16),
        compiler_params=_cparams(("arbitrary",)),
        name="dispatch",
    )(off, cnt, tail, hx, onehot, lstrict)


def _experts_kernel(grp_ref, used_ref, hs_ref, w1_ref, w3_ref, w2_ref, ys_ref):
    i = pl.program_id(0)
    d = ys_ref.shape[1]
    ne = EXPERTS_PER_GROUP

    @pl.when(i < used_ref[0])
    def _():
        t = hs_ref[:, 0:d]
        gates = ((hs_ref[:, d + 2 * LANES:d + 3 * LANES].astype(F32) + hs_ref[:, d + LANES:d + 2 * LANES].astype(F32))
                 + hs_ref[:, d:d + LANES].astype(F32))
        lane = lax.broadcasted_iota(jnp.int32, gates.shape, 1)
        acc = None
        for j in range(ne):
            a = jnp.dot(t, w1_ref[0, j], preferred_element_type=F32)
            b = jnp.dot(t, w3_ref[0, j], preferred_element_type=F32)
            gj = jnp.sum(jnp.where(lane == ROUTE_E_ROW + j, gates, 0.0), axis=-1, keepdims=True)
            y = gj * jnp.dot((_silu(a) * b).astype(BF16), w2_ref[0, j], preferred_element_type=F32)
            acc = y if acc is None else acc + y
        ys_ref[...] = acc.astype(BF16)

    @pl.when(i >= used_ref[0])
    def _():
        ys_ref[...] = jnp.zeros_like(ys_ref)


def _experts(grp, used, hs, w1_bf, w3_bf, w2_bf):
    rows, _ = hs.shape
    D = w2_bf.shape[3]
    wspec = lambda w: pl.BlockSpec((1,) + w.shape[1:], lambda i, grp, used: (grp[i], 0, 0, 0))
    return pl.pallas_call(
        _experts_kernel,
        grid_spec=pltpu.PrefetchScalarGridSpec(
            num_scalar_prefetch=2,
            grid=(rows // MOE_TE,),
            in_specs=[pl.BlockSpec((MOE_TE, MOE_XW), lambda i, grp, used: (i, 0)),
                      wspec(w1_bf), wspec(w3_bf), wspec(w2_bf)],
            out_specs=pl.BlockSpec((MOE_TE, D), lambda i, grp, used: (i, 0))),
        out_shape=jax.ShapeDtypeStruct((rows, D), BF16),
        compiler_params=_cparams(("arbitrary",)),
        name="experts",
    )(grp, used, hs, w1_bf, w3_bf, w2_bf)


def _combine_kernel(off_ref, cnt_ref, x1_ref, oh_ref, lstrict_ref, ga_ref, gf_ref, ys_ref, o_ref, buf_ref, sem,
                    *, n_tiles):
    k = pl.program_id(0)
    slot = k % 2

    def copies(kk, sl, action):
        def make(loc_row, seg_row, n):
            return pltpu.make_async_copy(ys_ref.at[pl.ds(seg_row, n)], buf_ref.at[sl, pl.ds(loc_row, n)], sem.at[sl])
        _run_copies(off_ref, cnt_ref, kk, make, action)

    @pl.when(k == 0)
    def _():
        buf_ref[...] = jnp.zeros_like(buf_ref)
        copies(k, slot, lambda cp: cp.start())

    @pl.when(k + 1 < n_tiles)
    def _():
        copies(k + 1, 1 - slot, lambda cp: cp.start())

    pt = _sort_matrix(_local_positions(oh_ref, lstrict_ref, cnt_ref, k))
    copies(k, slot, lambda cp: cp.wait())
    y = jnp.dot(pt, buf_ref[slot], preferred_element_type=F32)
    x2 = x1_ref[...] + ga_ref[0] * y
    o_ref[...] = x2 * lax.rsqrt(jnp.mean(x2 * x2, axis=-1, keepdims=True) + EPS) * gf_ref[...]


def _combine(off, cnt, x1, onehot, lstrict, ga, gfin, ys, S):
    T, D = x1.shape
    n_tiles = T // MOE_TM
    per = S // MOE_TM
    return pl.pallas_call(
        functools.partial(_combine_kernel, n_tiles=n_tiles),
        grid_spec=pltpu.PrefetchScalarGridSpec(
            num_scalar_prefetch=2,
            grid=(n_tiles,),
            in_specs=[pl.BlockSpec((MOE_TM, D), lambda k, off, cnt: (k, 0)),
                      pl.BlockSpec((MOE_TM, LANES), lambda k, off, cnt: (k, 0)),
                      pl.BlockSpec((MOE_TM, MOE_TM), lambda k, off, cnt: (0, 0)),
                      pl.BlockSpec((1, 1, D), lambda k, off, cnt: (k // per, 0, 0)),
                      pl.BlockSpec((1, D), lambda k, off, cnt: (0, 0)),
                      pl.BlockSpec(memory_space=pl.ANY)],
            out_specs=pl.BlockSpec((MOE_TM, D), lambda k, off, cnt: (k, 0)),
            scratch_shapes=[pltpu.VMEM((2, MOE_LOC, D), BF16),
                            pltpu.SemaphoreType.DMA((2,))]),
        out_shape=jax.ShapeDtypeStruct((T, D), F32),
        compiler_params=_cparams(("arbitrary",)),
        name="combine",
    )(off, cnt, x1, onehot, lstrict, ga, gfin, ys)


def _moe_schedule(cnt_tiles, n_steps):
    cnt = ((cnt_tiles + (MOE_ALIGN - 1)) // MOE_ALIGN) * MOE_ALIGN
    ends = jnp.cumsum(cnt, axis=0)
    total = ends[-1]
    ntile = (total + MOE_TE - 1) // MOE_TE
    cum = jnp.cumsum(ntile)
    base = (cum - ntile) * MOE_TE
    off = base[None, :] + ends - cnt
    i = jnp.arange(n_steps, dtype=jnp.int32)
    grp = jnp.minimum(jnp.sum(i[:, None] >= cum[None, :], axis=1), N_GROUPS - 1)
    grp = jnp.where(i < cum[-1], grp, grp[jnp.maximum(cum[-1] - 1, 0)])
    tail = jnp.concatenate([base + total, ntile * MOE_TE - total, cum[-1:]])
    as_i32 = lambda a: a.reshape(-1).astype(jnp.int32)
    return as_i32(off), as_i32(cnt), as_i32(tail), as_i32(grp), as_i32(cum[-1:])


def _rope_tables(S):
    t = jnp.arange(S)
    pos = jnp.stack([t // GRID_W, t % GRID_W], axis=-1).astype(F32)
    inv = ROPE_BASE ** (-jnp.arange(0, ROPE_AXIS_DIM, 2, dtype=F32) / ROPE_AXIS_DIM)
    ang = pos[:, :, None] * inv
    cos, sin = jnp.cos(ang), jnp.sin(ang)
    cos_h = jnp.concatenate([cos, cos], axis=-1).reshape(S, NA_HEAD_DIM)
    sin_h = jnp.concatenate([-sin, sin], axis=-1).reshape(S, NA_HEAD_DIM)
    return jnp.tile(cos_h, (1, NA_HEADS)), jnp.tile(sin_h, (1, NA_HEADS))


def kernel(x, c, ctx, c_ctx, w_mod, b_mod, norm_mix, norm_ffn, w_in, w_out, na_rpb, hg_lb, hg_norm,
           w_grp, b_grp, w_exp, b_exp, w1, w3, w2, norm_final):
    B, S, D = x.shape
    T = B * S
    assert w_mod.shape[0] == 1, "single-layer kernel"

    rows = -(-(B + 1) // SUBLANES) * SUBLANES
    cc = jnp.zeros((rows, D), F32).at[:B].set(c).at[B].set(c_ctx)
    mod = _modulation(cc, w_mod[0], b_mod[0])
    sh_a, sc_a, ga_a, sh_f, sc_f, ga_f = [m.reshape(B, 1, D) for m in jnp.split(mod[:B], 6, axis=-1)]
    csh_a, csc_a = [m.reshape(1, D) for m in jnp.split(mod[B], 6)[:2]]

    w_in_bf = w_in[0].astype(BF16)
    w_ctx_bf = jnp.concatenate([w_in_bf[:, i * SEG:(i + 1) * SEG] for i in (1, 2, 4, 5, 6)], axis=1)
    cos_t, sin_t = _rope_tables(S)
    g_mix = norm_mix[0].reshape(1, D)

    lb = jnp.cumsum(jax.nn.softmax(hg_lb.astype(F32), axis=0), axis=0)[0]
    proj = _project(x, g_mix, sc_a, sh_a, w_in_bf, cos_t, sin_t, lb, tm=512)
    cproj = _project_ctx(ctx, g_mix, csc_a, csh_a, w_ctx_bf)

    na_out = _neighbourhood_attention(proj, cproj, _na_bias_table(na_rpb[0]))

    gain = jnp.tile(hg_norm[0].astype(F32), HG_HEADS).reshape(1, HG_WIDTH)
    hg_out = _hgrn(proj, cproj, lb, gain)

    e_rows = slice(ROUTE_E_ROW, ROUTE_E_ROW + N_EXPERTS)
    wr = jnp.zeros((LANES, D), F32).at[:N_GROUPS].set(w_grp[0].T).at[e_rows].set(w_exp[0].T)
    br = jnp.zeros((LANES, 1), F32).at[:N_GROUPS, 0].set(b_grp[0]).at[e_rows, 0].set(b_exp[0])
    hi_f32 = lax.bitcast_convert_type(lax.bitcast_convert_type(wr, jnp.uint32) & jnp.uint32(0xFFFF0000), F32)
    wr_hi = hi_f32.astype(BF16)
    wr_lo = (wr - hi_f32).astype(BF16)
    x1, hx, onehot, cnt_tiles = _outproj(x.reshape(T, D), na_out.reshape(T, NA_WIDTH), hg_out.reshape(T, HG_WIDTH),
                                         w_out[0].astype(BF16), ga_a, norm_ffn[0].reshape(1, D), sc_f, sh_f,
                                         wr_hi, wr_lo, br, S=S)

    n_steps = _moe_steps(T)
    off, cnt, tail, grp, used = _moe_schedule(cnt_tiles[:, 0, :N_GROUPS].astype(jnp.int32), n_steps)
    lstrict = jnp.asarray(np.tril(np.ones((MOE_TM, MOE_TM), np.float32), -1), BF16)
    by_group = lambda w: w[0].astype(BF16).reshape(N_GROUPS, EXPERTS_PER_GROUP, *w.shape[2:])
    hs = _dispatch(off, cnt, tail, hx, onehot, lstrict, n_steps)
    ys = _experts(grp, used, hs, by_group(w1), by_group(w3), by_group(w2))
    out = _combine(off, cnt, x1, onehot, lstrict, ga_f, norm_final.reshape(1, D), ys, S)
    return out.reshape(B, S, D)
```

```python
import functools

import numpy as np
import jax
import jax.numpy as jnp
from jax import lax
from jax.experimental import pallas as pl
from jax.experimental.pallas import tpu as pltpu

F32 = jnp.float32
BF16 = jnp.bfloat16
HIGHEST = lax.Precision.HIGHEST

D_MODEL = 1024
GRID_W = 64
NA_HEADS = 8
NA_HEAD_DIM = 64
NA_WIDTH = NA_HEADS * NA_HEAD_DIM
NA_KH = 8
NA_KW = 16
ROPE_AXIS_DIM = NA_HEAD_DIM // 2
ROPE_BASE = 10000.0
HG_HEADS = 4
HG_DK = 128
HG_WIDTH = HG_HEADS * HG_DK
HG_CHUNK = 64
SEG = 512
N_GROUPS = 4
EXPERTS_PER_GROUP = 4
N_EXPERTS = N_GROUPS * EXPERTS_PER_GROUP
D_EXPERT = 512
EPS = 1e-6
NEG = -1e30
LOG2E = 1.4426950408889634
LANES = 128
SUBLANES = 8
VMEM_LIMIT = 56 * 1024 * 1024

HG_LEVELS = (32, 16, 8, 4, 2, 1)
HG_CHUNKS_PER_STEP = 4


def _cparams(sem):
    return pltpu.CompilerParams(dimension_semantics=sem, vmem_limit_bytes=VMEM_LIMIT)


def _sigmoid(x):
    return 1.0 / (1.0 + jnp.exp(-x))


def _silu(x):
    return x * _sigmoid(x)


def _mod_kernel(c_ref, w_ref, b_ref, o_ref):
    s = _silu(c_ref[...])
    o_ref[...] = jnp.dot(s, w_ref[...], precision=HIGHEST, preferred_element_type=F32) + b_ref[...]


def _modulation(cc, w_mod, b_mod):
    rows, d = cc.shape
    n = w_mod.shape[1]
    tn = 1024
    return pl.pallas_call(
        _mod_kernel,
        grid=(n // tn,),
        in_specs=[pl.BlockSpec((rows, d), lambda j: (0, 0)),
                  pl.BlockSpec((d, tn), lambda j: (0, j)),
                  pl.BlockSpec((1, tn), lambda j: (0, j))],
        out_specs=pl.BlockSpec((rows, tn), lambda j: (0, j)),
        out_shape=jax.ShapeDtypeStruct((rows, n), F32),
        compiler_params=_cparams(("arbitrary",)),
        name="mod",
    )(cc, w_mod, b_mod.reshape(1, n))


def _norm_mod(x, g, sc, sh):
    y = x * lax.rsqrt(jnp.mean(x * x, axis=-1, keepdims=True) + EPS)
    return (y * g) * (1.0 + sc) + sh


def _rope(a, cos, sin):
    lane = lax.broadcasted_iota(jnp.int32, a.shape, 1)
    first = (lane % ROPE_AXIS_DIM) < (ROPE_AXIS_DIM // 2)
    up = pltpu.roll(a, LANES - ROPE_AXIS_DIM // 2, axis=1)
    dn = pltpu.roll(a, ROPE_AXIS_DIM // 2, axis=1)
    return a * cos + jnp.where(first, up, dn) * sin


def _forget_gate(pre, lb):
    f = lb + (1.0 - lb) * _sigmoid(pre)
    g2 = jnp.log2(f)
    hi = g2.astype(BF16)
    return 1.0 - f, hi, g2 - hi.astype(F32)


P_QRAW, P_QROT, P_KROT, P_V, P_HQ, P_HI, P_KF, P_GFH, P_GFL, P_KB, P_GBH, P_GBL, P_HG = range(13)
P_SEGS = 13


def _proj_kernel(x_ref, g_ref, sc_ref, sh_ref, w_ref, cos_ref, sin_ref, lb_ref, o_ref):
    h = _norm_mod(x_ref[0], g_ref[...], sc_ref[0], sh_ref[0]).astype(BF16)
    scale = NA_HEAD_DIM ** -0.5 * LOG2E

    def put(seg, val):
        o_ref[0, :, seg * SEG:(seg + 1) * SEG] = val.astype(BF16)

    for j in range(8):
        acc = jnp.dot(h, w_ref[:, j * SEG:(j + 1) * SEG], preferred_element_type=F32)
        if j <= 1:
            rot = jnp.concatenate(
                [_rope(acc[:, p * LANES:(p + 1) * LANES], cos_ref[:, p * LANES:(p + 1) * LANES],
                       sin_ref[:, p * LANES:(p + 1) * LANES]) for p in range(SEG // LANES)], axis=1)
            if j == 0:
                put(P_QRAW, acc * scale)
                put(P_QROT, rot * scale)
            else:
                put(P_KROT, rot)
        elif j == 2:
            put(P_V, acc)
        elif j == 3:
            put(P_HQ, _silu(acc))
        elif j == 4:
            put(P_HI, acc)
        elif j in (5, 6):
            base = P_KF if j == 5 else P_KB
            for i, val in enumerate(_forget_gate(acc, lb_ref[j - 5:j - 4, :])):
                put(base + i, val)
        else:
            put(P_HG, acc)


def _project(x, g, sc, sh, w_bf, cos_t, sin_t, lb, tm):
    B, S, D = x.shape
    return pl.pallas_call(
        _proj_kernel,
        grid=(S // tm, B),
        in_specs=[pl.BlockSpec((1, tm, D), lambda s, b: (b, s, 0)),
                  pl.BlockSpec((1, D), lambda s, b: (0, 0)),
                  pl.BlockSpec((1, 1, D), lambda s, b: (b, 0, 0)),
                  pl.BlockSpec((1, 1, D), lambda s, b: (b, 0, 0)),
                  pl.BlockSpec((D, 8 * SEG), lambda s, b: (0, 0)),
                  pl.BlockSpec((tm, SEG), lambda s, b: (s, 0)),
                  pl.BlockSpec((tm, SEG), lambda s, b: (s, 0)),
                  pl.BlockSpec((2, HG_WIDTH), lambda s, b: (0, 0))],
        out_specs=pl.BlockSpec((1, tm, P_SEGS * SEG), lambda s, b: (b, s, 0)),
        out_shape=jax.ShapeDtypeStruct((B, S, P_SEGS * SEG), BF16),
        compiler_params=_cparams(("arbitrary", "arbitrary")),
        name="proj",
    )(x, g, sc, sh, w_bf, cos_t, sin_t, lb)


C_K, C_V, C_I, C_KF, C_GFH, C_GFL, C_KB, C_GBH, C_GBL = range(9)
C_SEGS = 9


def _ctxproj_kernel(x_ref, g_ref, sc_ref, sh_ref, w_ref, lb_ref, o_ref):
    h = _norm_mod(x_ref[0], g_ref[...], sc_ref[...], sh_ref[...]).astype(BF16)

    def put(seg, val):
        o_ref[0, :, seg * SEG:(seg + 1) * SEG] = val.astype(BF16)

    for j in range(5):
        acc = jnp.dot(h, w_ref[:, j * SEG:(j + 1) * SEG], preferred_element_type=F32)
        if j < 3:
            put(j, acc)
        else:
            for i, val in enumerate(_forget_gate(acc, lb_ref[j - 3:j - 2, :])):
                put(C_KF + 3 * (j - 3) + i, val)


def _project_ctx(ctx, g, sc, sh, w_bf, lb):
    B, L, D = ctx.shape
    return pl.pallas_call(
        _ctxproj_kernel,
        grid=(B,),
        in_specs=[pl.BlockSpec((1, L, D), lambda b: (b, 0, 0)),
                  pl.BlockSpec((1, D), lambda b: (0, 0)),
                  pl.BlockSpec((1, D), lambda b: (0, 0)),
                  pl.BlockSpec((1, D), lambda b: (0, 0)),
                  pl.BlockSpec((D, 5 * SEG), lambda b: (0, 0)),
                  pl.BlockSpec((2, HG_WIDTH), lambda b: (0, 0))],
        out_specs=pl.BlockSpec((1, L, C_SEGS * SEG), lambda b: (b, 0, 0)),
        out_shape=jax.ShapeDtypeStruct((B, L, C_SEGS * SEG), BF16),
        compiler_params=_cparams(("arbitrary",)),
        name="ctxproj",
    )(ctx, g, sc, sh, w_bf, lb)


NA_ROWS_PER_STEP = 8


def _na_kernel(qraw_ref, qrot_ref, k_ref, v_ref, ck_ref, cv_ref, bias_ref, o_ref, *, rows):
    nk = NA_KH * GRID_W
    lane = lax.broadcasted_iota(jnp.int32, (GRID_W, LANES), 1)
    sel0 = lane < NA_HEAD_DIM
    nt = (((1,), (1,)), ((), ()))
    n_pairs = NA_WIDTH // LANES
    ones = jnp.ones((nk + ck_ref.shape[1], LANES), BF16)
    units = [(j, p) for j in range(NA_ROWS_PER_STEP) for p in range(n_pairs)]

    def window(j):
        r = pl.program_id(1) * NA_ROWS_PER_STEP + j
        rs = jnp.clip(r - NA_KH // 2, 0, rows - NA_KH)
        return pl.multiple_of(rs * GRID_W, GRID_W), pl.multiple_of((rs - r + NA_KH - 1) * GRID_W, GRID_W)

    def query_slots(ref, j, cols):
        q = ref[0, j * GRID_W:(j + 1) * GRID_W, cols]
        zero = jnp.zeros_like(q)
        return jnp.concatenate([jnp.where(sel0, q, zero), jnp.where(sel0, zero, q)], axis=0)

    def scores(j, p):
        cols = slice(p * LANES, (p + 1) * LANES)
        start, bias_row = window(j)
        s_loc = lax.dot_general(k_ref[0, pl.ds(start, nk), cols], query_slots(qrot_ref, j, cols), nt,
                                preferred_element_type=F32) + bias_ref[p, pl.ds(bias_row, nk), :]
        s_ctx = lax.dot_general(ck_ref[0, :, cols], query_slots(qraw_ref, j, cols), nt,
                                preferred_element_type=F32)
        return s_loc, s_ctx

    def softmax(s_loc, s_ctx):
        m = jnp.maximum(jnp.max(s_loc, axis=0, keepdims=True), jnp.max(s_ctx, axis=0, keepdims=True))
        return jnp.concatenate([jnp.exp2(s_loc - m).T, jnp.exp2(s_ctx - m).T], axis=1).astype(BF16)

    def values(j, p, probs):
        cols = slice(p * LANES, (p + 1) * LANES)
        start, _ = window(j)
        vals = jnp.concatenate([v_ref[0, pl.ds(start, nk), cols], cv_ref[0, :, cols]], axis=0)
        oe = jnp.dot(probs, jnp.concatenate([vals, ones], axis=1), preferred_element_type=F32)
        res = oe[:, :LANES] / oe[:, LANES:]
        o_ref[0, j * GRID_W:(j + 1) * GRID_W, cols] = jnp.where(sel0, res[0:GRID_W],
                                                                res[GRID_W:2 * GRID_W]).astype(BF16)

    s_all = [scores(j, p) for j, p in units]
    probs = [softmax(*s) for s in s_all]
    for (j, p), pr in zip(units, probs):
        values(j, p, pr)


def _na_bias_table(rpb):
    qc = np.arange(GRID_W)[None, :]
    kc = np.arange(GRID_W)[:, None]
    cs = np.clip(qc - NA_KW // 2, 0, GRID_W - NA_KW)
    valid = (kc >= cs) & (kc < cs + NA_KW)
    col_off = np.clip(kc - qc + NA_KW - 1, 0, 2 * NA_KW - 2)
    t = (rpb.astype(F32) * LOG2E)[:, :, col_off]
    t = jnp.where(jnp.asarray(valid)[None, None], t, NEG)
    n_pairs = NA_HEADS // 2
    n_ro = 2 * NA_KH - 1
    t = t.reshape(n_pairs, 2, n_ro, GRID_W, GRID_W).transpose(0, 2, 3, 1, 4)
    return t.reshape(n_pairs, n_ro * GRID_W, 2 * GRID_W)


def _neighbourhood_attention(proj, cproj, bias_tbl):
    B, S, _ = proj.shape
    L = cproj.shape[1]
    rows = S // GRID_W
    tq = NA_ROWS_PER_STEP * GRID_W
    return pl.pallas_call(
        functools.partial(_na_kernel, rows=rows),
        grid=(B, rows // NA_ROWS_PER_STEP),
        in_specs=[pl.BlockSpec((1, tq, SEG), lambda b, r: (b, r, P_QRAW)),
                  pl.BlockSpec((1, tq, SEG), lambda b, r: (b, r, P_QROT)),
                  pl.BlockSpec((1, S, SEG), lambda b, r: (b, 0, P_KROT)),
                  pl.BlockSpec((1, S, SEG), lambda b, r: (b, 0, P_V)),
                  pl.BlockSpec((1, L, SEG), lambda b, r: (b, 0, C_K)),
                  pl.BlockSpec((1, L, SEG), lambda b, r: (b, 0, C_V)),
                  pl.BlockSpec(bias_tbl.shape, lambda b, r: (0, 0, 0))],
        out_specs=pl.BlockSpec((1, tq, SEG), lambda b, r: (b, r, 0)),
        out_shape=jax.ShapeDtypeStruct((B, S, NA_WIDTH), BF16),
        compiler_params=_cparams(("arbitrary", "arbitrary")),
        name="na",
    )(proj, proj, proj, proj, cproj, cproj, bias_tbl)


def _split3(g):
    g1 = g.astype(BF16)
    r1 = g - g1.astype(F32)
    g2 = r1.astype(BF16)
    g3 = (r1 - g2.astype(F32)).astype(BF16)
    return g1, g2, g3


def _hg_consts():
    c = HG_CHUNK
    t = np.arange(c)[:, None]
    u = np.arange(c)[None, :]
    cm, role, masks = [], [], []
    for fwd in (True, False):
        blocks = [(u <= t) if fwd else (u >= t)]
        roles, ms = [], []
        for m in HG_LEVELS:
            blk = t // (2 * m)
            if fwd:
                mid = 2 * m * blk + m - 1
                is_q = (t % (2 * m)) >= m
                expo = np.where(is_q, (u > mid) & (u <= t), (u > t) & (u <= mid))
            else:
                mid = 2 * m * blk + m
                is_q = (t % (2 * m)) < m
                expo = np.where(is_q, (u >= t) & (u < mid), (u >= mid) & (u < t))
            blocks.append(expo)
            roles.append(np.broadcast_to(is_q, (c, 2 * HG_DK)))
            ms.append((blk == blk.T) & is_q & ~is_q.T)
        ms.append(t == u)
        full = np.concatenate(blocks, axis=0).astype(np.float32)
        cm.append(np.concatenate([full, full], axis=1))
        role.append(np.stack(roles).astype(np.float32))
        masks.append(np.stack([np.concatenate([x, x], axis=1) for x in ms]).astype(np.float32))
    ones_bd = np.kron(np.eye(2, dtype=np.float32), np.ones((HG_DK, c), np.float32))
    return (jnp.asarray(np.stack(cm), BF16), jnp.asarray(np.stack(role), BF16),
            jnp.asarray(np.stack(masks)), jnp.asarray(ones_bd, BF16))


def _block_diag(x, zeros):
    return jnp.concatenate([jnp.concatenate([x[:, :HG_DK], zeros], axis=1),
                            jnp.concatenate([zeros, x[:, HG_DK:]], axis=1)], axis=0)


def _hg_exponents(gh, gl, cmat_ref, di):
    return jnp.dot(cmat_ref[di], jnp.concatenate([gh, gl], axis=0), preferred_element_type=F32)


def _hg_intra(q, k, dall, role_ref, masks_ref, ones_ref, di):
    c = HG_CHUNK
    nl = len(HG_LEVELS)
    nt = (((1,), (1,)), ((), ()))
    b = dall[0:c]
    bend = b[c - 1:c] if di == 0 else b[0:1]
    qe = q * jnp.exp2(b).astype(BF16)
    ke = k * jnp.exp2(bend - b).astype(BF16)
    zeros = jnp.zeros((c, HG_DK), BF16)
    a = None
    for li in range(nl):
        e = jnp.exp2(dall[(li + 1) * c:(li + 2) * c]).astype(BF16)
        x = jnp.where(role_ref[di, li] > 0.5, q, k) * e
        r = lax.dot_general(x, _block_diag(x, zeros), nt, preferred_element_type=F32) * masks_ref[di, li]
        a = r if a is None else a + r
    a = a + jnp.dot(q * k, ones_ref[...], preferred_element_type=F32) * masks_ref[di, nl]
    return a.astype(BF16), qe, ke, bend


def _hg_outputs(a, qe, ke, bend, v, st_ref, di, pair):
    nt = (((1,), (1,)), ((), ()))
    tn = (((0,), (0,)), ((), ()))
    o = jnp.dot(a, _block_diag(v, jnp.zeros((HG_CHUNK, HG_DK), BF16)), preferred_element_type=F32)
    outs = []
    for hh in range(2):
        hs = slice(hh * HG_DK, (hh + 1) * HG_DK)
        st = st_ref[di, 2 * pair + hh]
        outs.append(o[:, hs] + lax.dot_general(qe[:, hs], st.astype(BF16), nt, preferred_element_type=F32))
        upd = lax.dot_general(v[:, hs], ke[:, hs], tn, preferred_element_type=F32)
        st_ref[di, 2 * pair + hh] = jnp.exp2(bend[:, hs]) * st + upd
    return outs


def _hgrn_kernel(sq_ref, hi_ref, kf_ref, gfh_ref, gfl_ref, kb_ref, gbh_ref, gbl_ref, hg_ref,
                 ci_ref, ckf_ref, cgfh_ref, cgfl_ref, ckb_ref, cgbh_ref, cgbl_ref, gain_ref,
                 cmat_ref, role_ref, masks_ref, ones_ref, after_ref, before_ref, o_ref, acc_ref, st_ref, *, n_chunks):
    c = HG_CHUNK
    tn = (((0,), (0,)), ((), ()))

    ctx_dirs = ((ckf_ref, cgfh_ref, cgfl_ref, after_ref), (ckb_ref, cgbh_ref, cgbl_ref, before_ref))
    for di, (k_ref, gh_ref, gl_ref, cm_ref) in enumerate(ctx_dirs):
        decay = jnp.dot(cm_ref[...], jnp.concatenate([gh_ref[0], gl_ref[0]], axis=0), preferred_element_type=F32)
        ke = k_ref[0] * jnp.exp2(decay).astype(BF16)
        for h in range(HG_HEADS):
            hs = slice(h * HG_DK, (h + 1) * HG_DK)
            st_ref[di, h] = lax.dot_general(ci_ref[0, :, hs], ke[:, hs], tn, preferred_element_type=F32)

    def step(i, second):
        units = [(sub, pair, di) for sub in range(HG_CHUNKS_PER_STEP)
                 for pair in range(HG_HEADS // 2) for di in range(2)]

        def window(u):
            sub, pair, di = units[u]
            ci = i * HG_CHUNKS_PER_STEP + sub
            first = ci * c if di == 0 else (n_chunks - 1 - ci) * c
            return pl.ds(pl.multiple_of(first, c), c), slice(2 * pair * HG_DK, 2 * (pair + 1) * HG_DK)

        def exponents(u):
            rr, cols = window(u)
            gh_ref, gl_ref = (gfh_ref, gfl_ref) if units[u][2] == 0 else (gbh_ref, gbl_ref)
            return _hg_exponents(gh_ref[0, rr, cols], gl_ref[0, rr, cols], cmat_ref, units[u][2])

        def intra(u, dall):
            rr, cols = window(u)
            k_ref = kf_ref if units[u][2] == 0 else kb_ref
            return _hg_intra(sq_ref[0, rr, cols], k_ref[0, rr, cols], dall, role_ref, masks_ref, ones_ref,
                             units[u][2])

        def finish(u, parts):
            _, pair, di = units[u]
            rr, cols = window(u)
            outs = _hg_outputs(*parts, hi_ref[0, rr, cols], st_ref, di, pair)
            for hh, o in enumerate(outs):
                hs = slice((2 * pair + hh) * HG_DK, (2 * pair + hh + 1) * HG_DK)
                if not second:
                    acc_ref[rr, hs] = o
                else:
                    ot = acc_ref[rr, hs] + o
                    y = ot * lax.rsqrt(jnp.mean(ot * ot, axis=-1, keepdims=True) + EPS) * gain_ref[:, hs]
                    o_ref[0, rr, hs] = (y * _silu(hg_ref[0, rr, hs].astype(F32))).astype(BF16)

        n = len(units)
        dalls = [exponents(u) for u in range(n)]
        parts = [intra(u, dalls[u]) for u in range(n)]
        for u in range(n):
            finish(u, parts[u])

    def first_half(i, carry):
        step(i, False)
        return carry

    def second_half(i, carry):
        step(i, True)
        return carry

    n_steps = n_chunks // HG_CHUNKS_PER_STEP
    lax.fori_loop(0, n_steps // 2, first_half, 0)
    lax.fori_loop(n_steps // 2, n_steps, second_half, 0)


def _hgrn(proj, cproj, gain):
    B, S, _ = proj.shape
    L = cproj.shape[1]
    c = HG_CHUNK
    nl = len(HG_LEVELS)
    before = np.tril(np.ones((L, L), np.float32), -1)
    stacked = lambda m: jnp.asarray(np.concatenate([m, m], axis=1), BF16)
    consts = list(_hg_consts()) + [stacked(before.T), stacked(before)]
    seq = lambda j: pl.BlockSpec((1, S, SEG), lambda b: (b, 0, j))
    cseq = lambda j: pl.BlockSpec((1, L, SEG), lambda b: (b, 0, j))
    full2 = lambda n, m: pl.BlockSpec((n, m), lambda b: (0, 0))
    return pl.pallas_call(
        functools.partial(_hgrn_kernel, n_chunks=S // c),
        grid=(B,),
        in_specs=[seq(P_HQ), seq(P_HI), seq(P_KF), seq(P_GFH), seq(P_GFL), seq(P_KB), seq(P_GBH), seq(P_GBL),
                  seq(P_HG), cseq(C_I), cseq(C_KF), cseq(C_GFH), cseq(C_GFL), cseq(C_KB), cseq(C_GBH), cseq(C_GBL),
                  full2(1, HG_WIDTH),
                  pl.BlockSpec(consts[0].shape, lambda b: (0, 0, 0)),
                  pl.BlockSpec((2, nl, c, 2 * HG_DK), lambda b: (0, 0, 0, 0)),
                  pl.BlockSpec((2, nl + 1, c, 2 * c), lambda b: (0, 0, 0, 0)),
                  full2(2 * HG_DK, 2 * c), full2(L, 2 * L), full2(L, 2 * L)],
        out_specs=pl.BlockSpec((1, S, HG_WIDTH), lambda b: (b, 0, 0)),
        out_shape=jax.ShapeDtypeStruct((B, S, HG_WIDTH), BF16),
        scratch_shapes=[pltpu.VMEM((S, HG_WIDTH), F32),
                        pltpu.VMEM((2, HG_HEADS, HG_DK, HG_DK), F32)],
        compiler_params=_cparams(("arbitrary",)),
        name="hgrn",
    )(*([proj] * 9), *([cproj] * 7), gain, *consts)


ROUTE_E_ROW = 8


def _route_t(lt):
    tm = lt.shape[1]
    row8 = lax.broadcasted_iota(jnp.int32, (SUBLANES, tm), 0).astype(F32)
    gl = jnp.where(row8 < N_GROUPS, lt[0:SUBLANES], -jnp.inf)
    gmax = jnp.max(gl, axis=0, keepdims=True)
    g_sel = jnp.min(jnp.where(gl == gmax, row8, float(SUBLANES)), axis=0, keepdims=True)
    g_w = 1.0 / jnp.sum(jnp.exp(gl - gmax), axis=0, keepdims=True)
    row16 = lax.broadcasted_iota(jnp.int32, (N_EXPERTS, tm), 0).astype(F32)
    first = g_sel * EXPERTS_PER_GROUP
    in_grp = (row16 >= first) & (row16 < first + EXPERTS_PER_GROUP)
    e1 = jnp.where(in_grp, lt[ROUTE_E_ROW:ROUTE_E_ROW + N_EXPERTS], -jnp.inf)
    v1 = jnp.max(e1, axis=0, keepdims=True)
    i1 = jnp.min(jnp.where(e1 == v1, row16, float(N_EXPERTS)), axis=0, keepdims=True)
    e2 = jnp.where(row16 == i1, -jnp.inf, e1)
    v2 = jnp.max(e2, axis=0, keepdims=True)
    i2 = jnp.min(jnp.where(e2 == v2, row16, float(N_EXPERTS)), axis=0, keepdims=True)
    t = jnp.exp(v2 - v1)
    w1 = g_w / (1.0 + t)
    w2 = g_w * t / (1.0 + t)
    onehot = jnp.where(row8 == g_sel, 1.0, 0.0)
    gates = jnp.where(row8 == i1 - first, w1, jnp.where(row8 == i2 - first, w2, 0.0))
    return onehot, gates


def _outproj_kernel(x_ref, na_ref, hg_ref, w_ref, ga_ref, g_ref, sc_ref, sh_ref, wrh_ref, wrl_ref, br_ref,
                    x1_ref, hx_ref, oh_ref, cnt_ref):
    tm, d = x_ref.shape
    nt = (((1,), (1,)), ((), ()))
    mix = (jnp.dot(na_ref[...], w_ref[0:NA_WIDTH, :], preferred_element_type=F32)
           + jnp.dot(hg_ref[...], w_ref[NA_WIDTH:, :], preferred_element_type=F32))
    x1 = x_ref[...] + ga_ref[0] * mix
    x1_ref[...] = x1
    h2 = _norm_mod(x1, g_ref[...], sc_ref[0], sh_ref[0])
    h_hi = h2.astype(BF16)
    h_lo = (h2 - h_hi.astype(F32)).astype(BF16)
    hx_ref[:, 0:d] = h_hi
    lt = ((lax.dot_general(wrl_ref[...], h_hi, nt, preferred_element_type=F32)
           + lax.dot_general(wrh_ref[...], h_lo, nt, preferred_element_type=F32))
          + lax.dot_general(wrh_ref[...], h_hi, nt, preferred_element_type=F32)) + br_ref[...]
    onehot_t, gates_t = _route_t(lt)
    tok = jnp.concatenate([onehot_t, gates_t, jnp.zeros((LANES - 2 * SUBLANES, tm), F32)], axis=0).T
    lane = lax.broadcasted_iota(jnp.int32, tok.shape, 1)
    onehot = jnp.where(lane < ROUTE_E_ROW, tok, 0.0)
    gates = jnp.where(lane >= ROUTE_E_ROW, tok, 0.0)
    for j, piece in enumerate(_split3(gates)):
        hx_ref[:, d + j * LANES:d + (j + 1) * LANES] = piece
    oh_ref[...] = onehot
    cnt_ref[0] = jnp.broadcast_to(jnp.sum(onehot, axis=0, keepdims=True), (SUBLANES, LANES))


def _outproj(x2d, na2d, hg2d, w_out_bf, ga, g, sc, sh, wr_hi, wr_lo, br, S):
    T, D = x2d.shape
    tm = MOE_TM
    per = S // tm
    tok = lambda w: pl.BlockSpec((tm, w), lambda i: (i, 0))
    bat = pl.BlockSpec((1, 1, D), lambda i: (i // per, 0, 0))
    return pl.pallas_call(
        _outproj_kernel,
        grid=(T // tm,),
        in_specs=[tok(D), tok(NA_WIDTH), tok(HG_WIDTH),
                  pl.BlockSpec((NA_WIDTH + HG_WIDTH, D), lambda i: (0, 0)),
                  bat, pl.BlockSpec((1, D), lambda i: (0, 0)), bat, bat,
                  pl.BlockSpec((LANES, D), lambda i: (0, 0)),
                  pl.BlockSpec((LANES, D), lambda i: (0, 0)),
                  pl.BlockSpec((LANES, 1), lambda i: (0, 0))],
        out_specs=[tok(D), tok(MOE_XW), tok(LANES),
                   pl.BlockSpec((1, SUBLANES, LANES), lambda i: (i, 0, 0))],
        out_shape=[jax.ShapeDtypeStruct((T, D), F32), jax.ShapeDtypeStruct((T, MOE_XW), BF16),
                   jax.ShapeDtypeStruct((T, LANES), F32),
                   jax.ShapeDtypeStruct((T // tm, SUBLANES, LANES), F32)],
        compiler_params=_cparams(("arbitrary",)),
        name="outproj",
    )(x2d, na2d, hg2d, w_out_bf, ga, g, sc, sh, wr_hi, wr_lo, br)


MOE_TM = 512
MOE_ALIGN = 16
MOE_LOC = 640
MOE_XW = D_MODEL + 3 * LANES
MOE_TE = 512
MOE_BITS = tuple(1 << b for b in range(9, 3, -1))


def _moe_steps(T):
    worst = T + (T // MOE_TM) * N_GROUPS * (MOE_ALIGN - 1)
    return -(-worst // MOE_TE) + N_GROUPS


def _local_positions(oh_ref, lstrict_ref, cnt_ref, k):
    onehot = oh_ref[...]
    ranks = jnp.dot(lstrict_ref[...], onehot.astype(BF16), preferred_element_type=F32)
    lane = lax.broadcasted_iota(jnp.int32, (1, LANES), 1)
    base = jnp.zeros((1, LANES), F32)
    o = jnp.int32(0)
    for g in range(N_GROUPS):
        base = jnp.where(lane == g, o.astype(F32), base)
        o = o + cnt_ref[k * N_GROUPS + g]
    return jnp.sum(onehot * (ranks + base), axis=-1, keepdims=True)


def _sort_matrix(lpos):
    col = lax.broadcasted_iota(jnp.int32, (MOE_TM, MOE_LOC), 1).astype(F32)
    return jnp.where(lpos == col, 1.0, 0.0).astype(BF16)


def _block_copies(n_rows, src_row, dst_row, bits, make_copy, action):
    for bit in bits:
        part = n_rows & (-2 * bit)

        @pl.when((n_rows & bit) != 0)
        def _(part=part, bit=bit):
            action(make_copy(pl.multiple_of(src_row + part, MOE_ALIGN), pl.multiple_of(dst_row + part, MOE_ALIGN), bit))


def _run_copies(off_ref, cnt_ref, k, make_copy, action):
    o = jnp.int32(0)
    for g in range(N_GROUPS):
        c = cnt_ref[k * N_GROUPS + g]
        _block_copies(c, o, off_ref[k * N_GROUPS + g], MOE_BITS, make_copy, action)
        o = o + c


def _dispatch_kernel(off_ref, cnt_ref, tail_ref, hx_ref, oh_ref, lstrict_ref, hs_ref, buf_ref, zero_ref, sem,
                     *, n_tiles, n_steps):
    k = pl.program_id(0)
    slot = k % 2

    def copies(kk, sl, action):
        def make(src_row, dst_row, n):
            return pltpu.make_async_copy(buf_ref.at[sl, pl.ds(src_row, n)], hs_ref.at[pl.ds(dst_row, n)], sem.at[sl])
        _run_copies(off_ref, cnt_ref, kk, make, action)

    @pl.when(k >= 2)
    def _():
        copies(k - 2, slot, lambda cp: cp.wait())

    pt = _sort_matrix(_local_positions(oh_ref, lstrict_ref, cnt_ref, k))
    srt = lax.dot_general(pt, hx_ref[...], (((0,), (0,)), ((), ())), preferred_element_type=F32)
    buf_ref[slot] = srt.astype(BF16)
    copies(k, slot, lambda cp: cp.start())

    @pl.when(k == n_tiles - 1)
    def _():
        zero_ref[...] = jnp.zeros_like(zero_ref)
        n_used = tail_ref[2 * N_GROUPS]

        def zero_copy(src_row, dst_row, n):
            return pltpu.make_async_copy(zero_ref.at[pl.ds(src_row, n)], hs_ref.at[pl.ds(dst_row, n)], sem.at[2])

        def tile_copy(i):
            return zero_copy(0, pl.multiple_of(i * MOE_TE, MOE_TE), MOE_TE)

        def fills(action):
            for g in range(N_GROUPS):
                _block_copies(tail_ref[N_GROUPS + g], jnp.int32(0), tail_ref[g], MOE_BITS[1:], zero_copy, action)

        fills(lambda cp: cp.start())
        lax.fori_loop(n_used, n_steps, lambda i, c: (tile_copy(i).start(), c)[1], 0)
        if n_tiles >= 2:
            copies(k - 1, 1 - slot, lambda cp: cp.wait())
        copies(k, slot, lambda cp: cp.wait())
        fills(lambda cp: cp.wait())
        lax.fori_loop(n_used, n_steps, lambda i, c: (tile_copy(i).wait(), c)[1], 0)


def _dispatch(off, cnt, tail, hx, onehot, lstrict, n_steps):
    T = hx.shape[0]
    n_tiles = T // MOE_TM
    return pl.pallas_call(
        functools.partial(_dispatch_kernel, n_tiles=n_tiles, n_steps=n_steps),
        grid_spec=pltpu.PrefetchScalarGridSpec(
            num_scalar_prefetch=3,
            grid=(n_tiles,),
            in_specs=[pl.BlockSpec((MOE_TM, MOE_XW), lambda k, *_: (k, 0)),
                      pl.BlockSpec((MOE_TM, LANES), lambda k, *_: (k, 0)),
                      pl.BlockSpec((MOE_TM, MOE_TM), lambda k, *_: (0, 0))],
            out_specs=pl.BlockSpec(memory_space=pl.ANY),
            scratch_shapes=[pltpu.VMEM((2, MOE_LOC, MOE_XW), BF16),
                            pltpu.VMEM((MOE_TE, MOE_XW), BF16),
                            pltpu.SemaphoreType.DMA((3,))]),
        out_shape=jax.ShapeDtypeStruct((n_steps * MOE_TE, MOE_XW), BF16),
        compiler_params=_cparams(("arbitrary",)),
        name="dispatch",
    )(off, cnt, tail, hx, onehot, lstrict)


def _experts_kernel(grp_ref, used_ref, hs_ref, w1_ref, w3_ref, w2_ref, ys_ref):
    i = pl.program_id(0)
    d = ys_ref.shape[1]
    ne = EXPERTS_PER_GROUP

    @pl.when(i < used_ref[0])
    def _():
        t = hs_ref[:, 0:d]
        gates = ((hs_ref[:, d + 2 * LANES:d + 3 * LANES].astype(F32) + hs_ref[:, d + LANES:d + 2 * LANES].astype(F32))
                 + hs_ref[:, d:d + LANES].astype(F32))
        lane = lax.broadcasted_iota(jnp.int32, gates.shape, 1)
        acc = None
        for j in range(ne):
            a = jnp.dot(t, w1_ref[0, j], preferred_element_type=F32)
            b = jnp.dot(t, w3_ref[0, j], preferred_element_type=F32)
            gj = jnp.sum(jnp.where(lane == ROUTE_E_ROW + j, gates, 0.0), axis=-1, keepdims=True)
            y = gj * jnp.dot((_silu(a) * b).astype(BF16), w2_ref[0, j], preferred_element_type=F32)
            acc = y if acc is None else acc + y
        ys_ref[...] = acc.astype(BF16)

    @pl.when(i >= used_ref[0])
    def _():
        ys_ref[...] = jnp.zeros_like(ys_ref)


def _experts(grp, used, hs, w1_bf, w3_bf, w2_bf):
    rows, _ = hs.shape
    D = w2_bf.shape[3]
    wspec = lambda w: pl.BlockSpec((1,) + w.shape[1:], lambda i, grp, used: (grp[i], 0, 0, 0))
    return pl.pallas_call(
        _experts_kernel,
        grid_spec=pltpu.PrefetchScalarGridSpec(
            num_scalar_prefetch=2,
            grid=(rows // MOE_TE,),
            in_specs=[pl.BlockSpec((MOE_TE, MOE_XW), lambda i, grp, used: (i, 0)),
                      wspec(w1_bf), wspec(w3_bf), wspec(w2_bf)],
            out_specs=pl.BlockSpec((MOE_TE, D), lambda i, grp, used: (i, 0))),
        out_shape=jax.ShapeDtypeStruct((rows, D), BF16),
        compiler_params=_cparams(("arbitrary",)),
        name="experts",
    )(grp, used, hs, w1_bf, w3_bf, w2_bf)


def _combine_kernel(off_ref, cnt_ref, x1_ref, oh_ref, lstrict_ref, ga_ref, gf_ref, ys_ref, o_ref, buf_ref, sem,
                    *, n_tiles):
    k = pl.program_id(0)
    slot = k % 2

    def copies(kk, sl, action):
        def make(loc_row, seg_row, n):
            return pltpu.make_async_copy(ys_ref.at[pl.ds(seg_row, n)], buf_ref.at[sl, pl.ds(loc_row, n)], sem.at[sl])
        _run_copies(off_ref, cnt_ref, kk, make, action)

    @pl.when(k == 0)
    def _():
        buf_ref[...] = jnp.zeros_like(buf_ref)
        copies(k, slot, lambda cp: cp.start())

    @pl.when(k + 1 < n_tiles)
    def _():
        copies(k + 1, 1 - slot, lambda cp: cp.start())

    pt = _sort_matrix(_local_positions(oh_ref, lstrict_ref, cnt_ref, k))
    copies(k, slot, lambda cp: cp.wait())
    y = jnp.dot(pt, buf_ref[slot], preferred_element_type=F32)
    x2 = x1_ref[...] + ga_ref[0] * y
    o_ref[...] = x2 * lax.rsqrt(jnp.mean(x2 * x2, axis=-1, keepdims=True) + EPS) * gf_ref[...]


def _combine(off, cnt, x1, onehot, lstrict, ga, gfin, ys, S):
    T, D = x1.shape
    n_tiles = T // MOE_TM
    per = S // MOE_TM
    return pl.pallas_call(
        functools.partial(_combine_kernel, n_tiles=n_tiles),
        grid_spec=pltpu.PrefetchScalarGridSpec(
            num_scalar_prefetch=2,
            grid=(n_tiles,),
            in_specs=[pl.BlockSpec((MOE_TM, D), lambda k, off, cnt: (k, 0)),
                      pl.BlockSpec((MOE_TM, LANES), lambda k, off, cnt: (k, 0)),
                      pl.BlockSpec((MOE_TM, MOE_TM), lambda k, off, cnt: (0, 0)),
                      pl.BlockSpec((1, 1, D), lambda k, off, cnt: (k // per, 0, 0)),
                      pl.BlockSpec((1, D), lambda k, off, cnt: (0, 0)),
                      pl.BlockSpec(memory_space=pl.ANY)],
            out_specs=pl.BlockSpec((MOE_TM, D), lambda k, off, cnt: (k, 0)),
            scratch_shapes=[pltpu.VMEM((2, MOE_LOC, D), BF16),
                            pltpu.SemaphoreType.DMA((2,))]),
        out_shape=jax.ShapeDtypeStruct((T, D), F32),
        compiler_params=_cparams(("arbitrary",)),
        name="combine",
    )(off, cnt, x1, onehot, lstrict, ga, gfin, ys)


def _moe_schedule(cnt_tiles, n_steps):
    cnt = ((cnt_tiles + (MOE_ALIGN - 1)) // MOE_ALIGN) * MOE_ALIGN
    ends = jnp.cumsum(cnt, axis=0)
    total = ends[-1]
    ntile = (total + MOE_TE - 1) // MOE_TE
    cum = jnp.cumsum(ntile)
    base = (cum - ntile) * MOE_TE
    off = base[None, :] + ends - cnt
    i = jnp.arange(n_steps, dtype=jnp.int32)
    grp = jnp.minimum(jnp.sum(i[:, None] >= cum[None, :], axis=1), N_GROUPS - 1)
    grp = jnp.where(i < cum[-1], grp, grp[jnp.maximum(cum[-1] - 1, 0)])
    tail = jnp.concatenate([base + total, ntile * MOE_TE - total, cum[-1:]])
    as_i32 = lambda a: a.reshape(-1).astype(jnp.int32)
    return as_i32(off), as_i32(cnt), as_i32(tail), as_i32(grp), as_i32(cum[-1:])


def _rope_tables(S):
    t = jnp.arange(S)
    pos = jnp.stack([t // GRID_W, t % GRID_W], axis=-1).astype(F32)
    inv = ROPE_BASE ** (-jnp.arange(0, ROPE_AXIS_DIM, 2, dtype=F32) / ROPE_AXIS_DIM)
    ang = pos[:, :, None] * inv
    cos, sin = jnp.cos(ang), jnp.sin(ang)
    cos_h = jnp.concatenate([cos, cos], axis=-1).reshape(S, NA_HEAD_DIM)
    sin_h = jnp.concatenate([-sin, sin], axis=-1).reshape(S, NA_HEAD_DIM)
    return jnp.tile(cos_h, (1, NA_HEADS)), jnp.tile(sin_h, (1, NA_HEADS))


def kernel(x, c, ctx, c_ctx, w_mod, b_mod, norm_mix, norm_ffn, w_in, w_out, na_rpb, hg_lb, hg_norm,
           w_grp, b_grp, w_exp, b_exp, w1, w3, w2, norm_final):
    B, S, D = x.shape
    T = B * S
    assert w_mod.shape[0] == 1, "single-layer kernel"

    rows = -(-(B + 1) // SUBLANES) * SUBLANES
    cc = jnp.zeros((rows, D), F32).at[:B].set(c).at[B].set(c_ctx)
    mod = _modulation(cc, w_mod[0], b_mod[0])
    sh_a, sc_a, ga_a, sh_f, sc_f, ga_f = [m.reshape(B, 1, D) for m in jnp.split(mod[:B], 6, axis=-1)]
    csh_a, csc_a = [m.reshape(1, D) for m in jnp.split(mod[B], 6)[:2]]

    w_in_bf = w_in[0].astype(BF16)
    w_ctx_bf = jnp.concatenate([w_in_bf[:, i * SEG:(i + 1) * SEG] for i in (1, 2, 4, 5, 6)], axis=1)
    cos_t, sin_t = _rope_tables(S)
    g_mix = norm_mix[0].reshape(1, D)

    lb = jnp.cumsum(jax.nn.softmax(hg_lb.astype(F32), axis=0), axis=0)[0]
    proj = _project(x, g_mix, sc_a, sh_a, w_in_bf, cos_t, sin_t, lb, tm=512)
    cproj = _project_ctx(ctx, g_mix, csc_a, csh_a, w_ctx_bf, lb)

    na_out = _neighbourhood_attention(proj, cproj, _na_bias_table(na_rpb[0]))

    gain = jnp.tile(hg_norm[0].astype(F32), HG_HEADS).reshape(1, HG_WIDTH)
    hg_out = _hgrn(proj, cproj, gain)

    e_rows = slice(ROUTE_E_ROW, ROUTE_E_ROW + N_EXPERTS)
    wr = jnp.zeros((LANES, D), F32).at[:N_GROUPS].set(w_grp[0].T).at[e_rows].set(w_exp[0].T)
    br = jnp.zeros((LANES, 1), F32).at[:N_GROUPS, 0].set(b_grp[0]).at[e_rows, 0].set(b_exp[0])
    hi_f32 = lax.bitcast_convert_type(lax.bitcast_convert_type(wr, jnp.uint32) & jnp.uint32(0xFFFF0000), F32)
    wr_hi = hi_f32.astype(BF16)
    wr_lo = (wr - hi_f32).astype(BF16)
    x1, hx, onehot, cnt_tiles = _outproj(x.reshape(T, D), na_out.reshape(T, NA_WIDTH), hg_out.reshape(T, HG_WIDTH),
                                         w_out[0].astype(BF16), ga_a, norm_ffn[0].reshape(1, D), sc_f, sh_f,
                                         wr_hi, wr_lo, br, S=S)

    n_steps = _moe_steps(T)
    off, cnt, tail, grp, used = _moe_schedule(cnt_tiles[:, 0, :N_GROUPS].astype(jnp.int32), n_steps)
    lstrict = jnp.asarray(np.tril(np.ones((MOE_TM, MOE_TM), np.float32), -1), BF16)
    by_group = lambda w: w[0].astype(BF16).reshape(N_GROUPS, EXPERTS_PER_GROUP, *w.shape[2:])
    hs = _dispatch(off, cnt, tail, hx, onehot, lstrict, n_steps)
    ys = _experts(grp, used, hs, by_group(w1), by_group(w3), by_group(w2))
    out = _combine(off, cnt, x1, onehot, lstrict, ga_f, norm_final.reshape(1, D), ys, S)
    return out.reshape(B, S, D)
```

```python
import functools

import numpy as np
import jax
import jax.numpy as jnp
from jax import lax
from jax.experimental import pallas as pl
from jax.experimental.pallas import tpu as pltpu

F32 = jnp.float32
BF16 = jnp.bfloat16
HIGHEST = lax.Precision.HIGHEST

D_MODEL = 1024
GRID_W = 64
NA_HEADS = 8
NA_HEAD_DIM = 64
NA_WIDTH = NA_HEADS * NA_HEAD_DIM
NA_KH = 8
NA_KW = 16
ROPE_AXIS_DIM = NA_HEAD_DIM // 2
ROPE_BASE = 10000.0
HG_HEADS = 4
HG_DK = 128
HG_WIDTH = HG_HEADS * HG_DK
HG_CHUNK = 64
SEG = 512
N_GROUPS = 4
EXPERTS_PER_GROUP = 4
N_EXPERTS = N_GROUPS * EXPERTS_PER_GROUP
D_EXPERT = 512
EPS = 1e-6
NEG = -1e30
LOG2E = 1.4426950408889634
LANES = 128
SUBLANES = 8
VMEM_LIMIT = 56 * 1024 * 1024

HG_LEVELS = (32, 16, 8, 4, 2, 1)
HG_CHUNKS_PER_STEP = 4


def _cparams(sem):
    return pltpu.CompilerParams(dimension_semantics=sem, vmem_limit_bytes=VMEM_LIMIT)


def _sigmoid(x):
    return 1.0 / (1.0 + jnp.exp(-x))


def _silu(x):
    return x * _sigmoid(x)


def _mod_kernel(c_ref, w_ref, b_ref, o_ref):
    s = _silu(c_ref[...])
    o_ref[...] = jnp.dot(s, w_ref[...], precision=HIGHEST, preferred_element_type=F32) + b_ref[...]


def _modulation(cc, w_mod, b_mod):
    rows, d = cc.shape
    n = w_mod.shape[1]
    tn = 1024
    return pl.pallas_call(
        _mod_kernel,
        grid=(n // tn,),
        in_specs=[pl.BlockSpec((rows, d), lambda j: (0, 0)),
                  pl.BlockSpec((d, tn), lambda j: (0, j)),
                  pl.BlockSpec((1, tn), lambda j: (0, j))],
        out_specs=pl.BlockSpec((rows, tn), lambda j: (0, j)),
        out_shape=jax.ShapeDtypeStruct((rows, n), F32),
        compiler_params=_cparams(("arbitrary",)),
        name="mod",
    )(cc, w_mod, b_mod.reshape(1, n))


def _norm_mod(x, g, sc, sh):
    y = x * lax.rsqrt(jnp.mean(x * x, axis=-1, keepdims=True) + EPS)
    return (y * g) * (1.0 + sc) + sh


def _rope(a, cos, sin):
    lane = lax.broadcasted_iota(jnp.int32, a.shape, 1)
    first = (lane % ROPE_AXIS_DIM) < (ROPE_AXIS_DIM // 2)
    up = pltpu.roll(a, LANES - ROPE_AXIS_DIM // 2, axis=1)
    dn = pltpu.roll(a, ROPE_AXIS_DIM // 2, axis=1)
    return a * cos + jnp.where(first, up, dn) * sin


def _forget_gate(pre, lb):
    f = lb + (1.0 - lb) * _sigmoid(pre)
    g2 = jnp.log2(f)
    hi = g2.astype(BF16)
    return 1.0 - f, hi, g2 - hi.astype(F32)


P_QRAW, P_QROT, P_KROT, P_V, P_HQ, P_HI, P_KF, P_GFH, P_GFL, P_KB, P_GBH, P_GBL, P_HG = range(13)
P_SEGS = 13


def _proj_kernel(x_ref, g_ref, sc_ref, sh_ref, w_ref, cos_ref, sin_ref, lb_ref, o_ref):
    h = _norm_mod(x_ref[0], g_ref[...], sc_ref[0], sh_ref[0]).astype(BF16)
    scale = NA_HEAD_DIM ** -0.5 * LOG2E

    def put(seg, val):
        o_ref[0, :, seg * SEG:(seg + 1) * SEG] = val.astype(BF16)

    for j in range(8):
        acc = jnp.dot(h, w_ref[:, j * SEG:(j + 1) * SEG], preferred_element_type=F32)
        if j <= 1:
            rot = jnp.concatenate(
                [_rope(acc[:, p * LANES:(p + 1) * LANES], cos_ref[:, p * LANES:(p + 1) * LANES],
                       sin_ref[:, p * LANES:(p + 1) * LANES]) for p in range(SEG // LANES)], axis=1)
            if j == 0:
                put(P_QRAW, acc * scale)
                put(P_QROT, rot * scale)
            else:
                put(P_KROT, rot)
        elif j == 2:
            put(P_V, acc)
        elif j == 3:
            put(P_HQ, _silu(acc))
        elif j == 4:
            put(P_HI, acc)
        elif j in (5, 6):
            base = P_KF if j == 5 else P_KB
            for i, val in enumerate(_forget_gate(acc, lb_ref[j - 5:j - 4, :])):
                put(base + i, val)
        else:
            put(P_HG, acc)


def _project(x, g, sc, sh, w_bf, cos_t, sin_t, lb, tm):
    B, S, D = x.shape
    return pl.pallas_call(
        _proj_kernel,
        grid=(S // tm, B),
        in_specs=[pl.BlockSpec((1, tm, D), lambda s, b: (b, s, 0)),
                  pl.BlockSpec((1, D), lambda s, b: (0, 0)),
                  pl.BlockSpec((1, 1, D), lambda s, b: (b, 0, 0)),
                  pl.BlockSpec((1, 1, D), lambda s, b: (b, 0, 0)),
                  pl.BlockSpec((D, 8 * SEG), lambda s, b: (0, 0)),
                  pl.BlockSpec((tm, SEG), lambda s, b: (s, 0)),
                  pl.BlockSpec((tm, SEG), lambda s, b: (s, 0)),
                  pl.BlockSpec((2, HG_WIDTH), lambda s, b: (0, 0))],
        out_specs=pl.BlockSpec((1, tm, P_SEGS * SEG), lambda s, b: (b, s, 0)),
        out_shape=jax.ShapeDtypeStruct((B, S, P_SEGS * SEG), BF16),
        compiler_params=_cparams(("arbitrary", "arbitrary")),
        name="proj",
    )(x, g, sc, sh, w_bf, cos_t, sin_t, lb)


C_K, C_V, C_I, C_KF, C_GFH, C_GFL, C_KB, C_GBH, C_GBL = range(9)
C_SEGS = 9


def _ctxproj_kernel(x_ref, g_ref, sc_ref, sh_ref, w_ref, lb_ref, o_ref):
    h = _norm_mod(x_ref[0], g_ref[...], sc_ref[...], sh_ref[...]).astype(BF16)

    def put(seg, val):
        o_ref[0, :, seg * SEG:(seg + 1) * SEG] = val.astype(BF16)

    for j in range(5):
        acc = jnp.dot(h, w_ref[:, j * SEG:(j + 1) * SEG], preferred_element_type=F32)
        if j < 3:
            put(j, acc)
        else:
            for i, val in enumerate(_forget_gate(acc, lb_ref[j - 3:j - 2, :])):
                put(C_KF + 3 * (j - 3) + i, val)


def _project_ctx(ctx, g, sc, sh, w_bf, lb):
    B, L, D = ctx.shape
    return pl.pallas_call(
        _ctxproj_kernel,
        grid=(B,),
        in_specs=[pl.BlockSpec((1, L, D), lambda b: (b, 0, 0)),
                  pl.BlockSpec((1, D), lambda b: (0, 0)),
                  pl.BlockSpec((1, D), lambda b: (0, 0)),
                  pl.BlockSpec((1, D), lambda b: (0, 0)),
                  pl.BlockSpec((D, 5 * SEG), lambda b: (0, 0)),
                  pl.BlockSpec((2, HG_WIDTH), lambda b: (0, 0))],
        out_specs=pl.BlockSpec((1, L, C_SEGS * SEG), lambda b: (b, 0, 0)),
        out_shape=jax.ShapeDtypeStruct((B, L, C_SEGS * SEG), BF16),
        compiler_params=_cparams(("arbitrary",)),
        name="ctxproj",
    )(ctx, g, sc, sh, w_bf, lb)


NA_ROWS_PER_STEP = 8


def _na_kernel(qraw_ref, qrot_ref, k_ref, v_ref, ck_ref, cv_ref, bias_ref, o_ref, *, rows):
    nk = NA_KH * GRID_W
    lane = lax.broadcasted_iota(jnp.int32, (GRID_W, LANES), 1)
    sel0 = lane < NA_HEAD_DIM
    nt = (((1,), (1,)), ((), ()))
    n_pairs = NA_WIDTH // LANES
    ones = jnp.ones((nk + ck_ref.shape[1], LANES), BF16)
    units = [(j, p) for j in range(NA_ROWS_PER_STEP) for p in range(n_pairs)]

    def window(j):
        r = pl.program_id(1) * NA_ROWS_PER_STEP + j
        rs = jnp.clip(r - NA_KH // 2, 0, rows - NA_KH)
        return pl.multiple_of(rs * GRID_W, GRID_W), pl.multiple_of((rs - r + NA_KH - 1) * GRID_W, GRID_W)

    def query_slots(ref, j, cols):
        q = ref[0, j * GRID_W:(j + 1) * GRID_W, cols]
        zero = jnp.zeros_like(q)
        return jnp.concatenate([jnp.where(sel0, q, zero), jnp.where(sel0, zero, q)], axis=0)

    def scores(j, p):
        cols = slice(p * LANES, (p + 1) * LANES)
        start, bias_row = window(j)
        s_loc = lax.dot_general(k_ref[0, pl.ds(start, nk), cols], query_slots(qrot_ref, j, cols), nt,
                                preferred_element_type=F32) + bias_ref[p, pl.ds(bias_row, nk), :]
        s_ctx = lax.dot_general(ck_ref[0, :, cols], query_slots(qraw_ref, j, cols), nt,
                                preferred_element_type=F32)
        return s_loc, s_ctx

    def softmax(s_loc, s_ctx):
        m = jnp.maximum(jnp.max(s_loc, axis=0, keepdims=True), jnp.max(s_ctx, axis=0, keepdims=True))
        return jnp.concatenate([jnp.exp2(s_loc - m).T, jnp.exp2(s_ctx - m).T], axis=1).astype(BF16)

    def values(j, p, probs):
        cols = slice(p * LANES, (p + 1) * LANES)
        start, _ = window(j)
        vals = jnp.concatenate([v_ref[0, pl.ds(start, nk), cols], cv_ref[0, :, cols]], axis=0)
        oe = jnp.dot(probs, jnp.concatenate([vals, ones], axis=1), preferred_element_type=F32)
        res = oe[:, :LANES] / oe[:, LANES:]
        o_ref[0, j * GRID_W:(j + 1) * GRID_W, cols] = jnp.where(sel0, res[0:GRID_W],
                                                                res[GRID_W:2 * GRID_W]).astype(BF16)

    s_all = [scores(j, p) for j, p in units]
    probs = [softmax(*s) for s in s_all]
    for (j, p), pr in zip(units, probs):
        values(j, p, pr)


def _na_bias_table(rpb):
    qc = np.arange(GRID_W)[None, :]
    kc = np.arange(GRID_W)[:, None]
    cs = np.clip(qc - NA_KW // 2, 0, GRID_W - NA_KW)
    valid = (kc >= cs) & (kc < cs + NA_KW)
    col_off = np.clip(kc - qc + NA_KW - 1, 0, 2 * NA_KW - 2)
    t = (rpb.astype(F32) * LOG2E)[:, :, col_off]
    t = jnp.where(jnp.asarray(valid)[None, None], t, NEG)
    n_pairs = NA_HEADS // 2
    n_ro = 2 * NA_KH - 1
    t = t.reshape(n_pairs, 2, n_ro, GRID_W, GRID_W).transpose(0, 2, 3, 1, 4)
    return t.reshape(n_pairs, n_ro * GRID_W, 2 * GRID_W)


def _neighbourhood_attention(proj, cproj, bias_tbl):
    B, S, _ = proj.shape
    L = cproj.shape[1]
    rows = S // GRID_W
    tq = NA_ROWS_PER_STEP * GRID_W
    return pl.pallas_call(
        functools.partial(_na_kernel, rows=rows),
        grid=(B, rows // NA_ROWS_PER_STEP),
        in_specs=[pl.BlockSpec((1, tq, SEG), lambda b, r: (b, r, P_QRAW)),
                  pl.BlockSpec((1, tq, SEG), lambda b, r: (b, r, P_QROT)),
                  pl.BlockSpec((1, S, SEG), lambda b, r: (b, 0, P_KROT)),
                  pl.BlockSpec((1, S, SEG), lambda b, r: (b, 0, P_V)),
                  pl.BlockSpec((1, L, SEG), lambda b, r: (b, 0, C_K)),
                  pl.BlockSpec((1, L, SEG), lambda b, r: (b, 0, C_V)),
                  pl.BlockSpec(bias_tbl.shape, lambda b, r: (0, 0, 0))],
        out_specs=pl.BlockSpec((1, tq, SEG), lambda b, r: (b, r, 0)),
        out_shape=jax.ShapeDtypeStruct((B, S, NA_WIDTH), BF16),
        compiler_params=_cparams(("arbitrary", "arbitrary")),
        name="na",
    )(proj, proj, proj, proj, cproj, cproj, bias_tbl)


def _split3(g):
    g1 = g.astype(BF16)
    r1 = g - g1.astype(F32)
    g2 = r1.astype(BF16)
    g3 = (r1 - g2.astype(F32)).astype(BF16)
    return g1, g2, g3


def _hg_consts():
    c = HG_CHUNK
    t = np.arange(c)[:, None]
    u = np.arange(c)[None, :]
    cm, role, masks = [], [], []
    for fwd in (True, False):
        blocks = [(u <= t) if fwd else (u >= t)]
        roles, ms = [], []
        for m in HG_LEVELS:
            blk = t // (2 * m)
            if fwd:
                mid = 2 * m * blk + m - 1
                is_q = (t % (2 * m)) >= m
                expo = np.where(is_q, (u > mid) & (u <= t), (u > t) & (u <= mid))
            else:
                mid = 2 * m * blk + m
                is_q = (t % (2 * m)) < m
                expo = np.where(is_q, (u >= t) & (u < mid), (u >= mid) & (u < t))
            blocks.append(expo)
            roles.append(np.broadcast_to(is_q, (c, 2 * HG_DK)))
            ms.append((blk == blk.T) & is_q & ~is_q.T)
        ms.append(t == u)
        full = np.concatenate(blocks, axis=0).astype(np.float32)
        cm.append(np.concatenate([full, full], axis=1))
        role.append(np.stack(roles).astype(np.float32))
        masks.append(np.stack([np.concatenate([x, x], axis=1) for x in ms]).astype(np.float32))
    return jnp.asarray(np.stack(cm), BF16), jnp.asarray(np.stack(role), BF16), jnp.asarray(np.stack(masks))


def _block_diag(x, zeros):
    return jnp.concatenate([jnp.concatenate([x[:, :HG_DK], zeros], axis=1),
                            jnp.concatenate([zeros, x[:, HG_DK:]], axis=1)], axis=0)


def _hg_exponents(gh, gl, cmat_ref, di):
    return jnp.dot(cmat_ref[di], jnp.concatenate([gh, gl], axis=0), preferred_element_type=F32)


def _hg_intra(q, k, dall, role_ref, masks_ref, di):
    c = HG_CHUNK
    nl = len(HG_LEVELS)
    nt = (((1,), (1,)), ((), ()))
    b = dall[0:c]
    bend = b[c - 1:c] if di == 0 else b[0:1]
    qe = q * jnp.exp2(b).astype(BF16)
    ke = k * jnp.exp2(bend - b).astype(BF16)
    zeros = jnp.zeros((c, HG_DK), BF16)
    a = None
    for li in range(nl):
        e = jnp.exp2(dall[(li + 1) * c:(li + 2) * c]).astype(BF16)
        x = jnp.where(role_ref[di, li] > 0.5, q, k) * e
        r = lax.dot_general(x, _block_diag(x, zeros), nt, preferred_element_type=F32) * masks_ref[di, li]
        a = r if a is None else a + r
    qk = q.astype(F32) * k.astype(F32)
    lane = lax.broadcasted_iota(jnp.int32, a.shape, 1)
    d0 = jnp.sum(qk[:, :HG_DK], axis=-1, keepdims=True)
    d1 = jnp.sum(qk[:, HG_DK:], axis=-1, keepdims=True)
    a = a + jnp.where(lane < c, d0, d1) * masks_ref[di, nl]
    return a.astype(BF16), qe, ke, bend


def _hg_outputs(a, qe, ke, bend, v, st_ref, di, pair):
    nt = (((1,), (1,)), ((), ()))
    tn = (((0,), (0,)), ((), ()))
    o = jnp.dot(a, _block_diag(v, jnp.zeros((HG_CHUNK, HG_DK), BF16)), preferred_element_type=F32)
    outs = []
    for hh in range(2):
        hs = slice(hh * HG_DK, (hh + 1) * HG_DK)
        st = st_ref[di, 2 * pair + hh]
        outs.append(o[:, hs] + lax.dot_general(qe[:, hs], st.astype(BF16), nt, preferred_element_type=F32))
        upd = lax.dot_general(v[:, hs], ke[:, hs], tn, preferred_element_type=F32)
        st_ref[di, 2 * pair + hh] = jnp.exp2(bend[:, hs]) * st + upd
    return outs


def _hgrn_kernel(sq_ref, hi_ref, kf_ref, gfh_ref, gfl_ref, kb_ref, gbh_ref, gbl_ref, hg_ref,
                 ci_ref, ckf_ref, cgfh_ref, cgfl_ref, ckb_ref, cgbh_ref, cgbl_ref, gain_ref,
                 cmat_ref, role_ref, masks_ref, after_ref, before_ref, o_ref, acc_ref, st_ref, *, n_chunks):
    c = HG_CHUNK
    tn = (((0,), (0,)), ((), ()))

    ctx_dirs = ((ckf_ref, cgfh_ref, cgfl_ref, after_ref), (ckb_ref, cgbh_ref, cgbl_ref, before_ref))
    for di, (k_ref, gh_ref, gl_ref, cm_ref) in enumerate(ctx_dirs):
        decay = jnp.dot(cm_ref[...], jnp.concatenate([gh_ref[0], gl_ref[0]], axis=0), preferred_element_type=F32)
        ke = k_ref[0] * jnp.exp2(decay).astype(BF16)
        for h in range(HG_HEADS):
            hs = slice(h * HG_DK, (h + 1) * HG_DK)
            st_ref[di, h] = lax.dot_general(ci_ref[0, :, hs], ke[:, hs], tn, preferred_element_type=F32)

    def step(i, second):
        units = [(sub, pair, di) for sub in range(HG_CHUNKS_PER_STEP)
                 for pair in range(HG_HEADS // 2) for di in range(2)]

        def window(u):
            sub, pair, di = units[u]
            ci = i * HG_CHUNKS_PER_STEP + sub
            first = ci * c if di == 0 else (n_chunks - 1 - ci) * c
            return pl.ds(pl.multiple_of(first, c), c), slice(2 * pair * HG_DK, 2 * (pair + 1) * HG_DK)

        def exponents(u):
            rr, cols = window(u)
            gh_ref, gl_ref = (gfh_ref, gfl_ref) if units[u][2] == 0 else (gbh_ref, gbl_ref)
            return _hg_exponents(gh_ref[0, rr, cols], gl_ref[0, rr, cols], cmat_ref, units[u][2])

        def intra(u, dall):
            rr, cols = window(u)
            k_ref = kf_ref if units[u][2] == 0 else kb_ref
            return _hg_intra(sq_ref[0, rr, cols], k_ref[0, rr, cols], dall, role_ref, masks_ref, units[u][2])

        def finish(u, parts):
            _, pair, di = units[u]
            rr, cols = window(u)
            outs = _hg_outputs(*parts, hi_ref[0, rr, cols], st_ref, di, pair)
            for hh, o in enumerate(outs):
                hs = slice((2 * pair + hh) * HG_DK, (2 * pair + hh + 1) * HG_DK)
                if not second:
                    acc_ref[rr, hs] = o
                else:
                    ot = acc_ref[rr, hs] + o
                    y = ot * lax.rsqrt(jnp.mean(ot * ot, axis=-1, keepdims=True) + EPS) * gain_ref[:, hs]
                    o_ref[0, rr, hs] = (y * _silu(hg_ref[0, rr, hs].astype(F32))).astype(BF16)

        n = len(units)
        dalls = [exponents(u) for u in range(n)]
        parts = [intra(u, dalls[u]) for u in range(n)]
        for u in range(n):
            finish(u, parts[u])

    def first_half(i, carry):
        step(i, False)
        return carry

    def second_half(i, carry):
        step(i, True)
        return carry

    n_steps = n_chunks // HG_CHUNKS_PER_STEP
    lax.fori_loop(0, n_steps // 2, first_half, 0)
    lax.fori_loop(n_steps // 2, n_steps, second_half, 0)


def _hgrn(proj, cproj, gain):
    B, S, _ = proj.shape
    L = cproj.shape[1]
    c = HG_CHUNK
    nl = len(HG_LEVELS)
    before = np.tril(np.ones((L, L), np.float32), -1)
    stacked = lambda m: jnp.asarray(np.concatenate([m, m], axis=1), BF16)
    consts = list(_hg_consts()) + [stacked(before.T), stacked(before)]
    seq = lambda j: pl.BlockSpec((1, S, SEG), lambda b: (b, 0, j))
    cseq = lambda j: pl.BlockSpec((1, L, SEG), lambda b: (b, 0, j))
    full2 = lambda n, m: pl.BlockSpec((n, m), lambda b: (0, 0))
    return pl.pallas_call(
        functools.partial(_hgrn_kernel, n_chunks=S // c),
        grid=(B,),
        in_specs=[seq(P_HQ), seq(P_HI), seq(P_KF), seq(P_GFH), seq(P_GFL), seq(P_KB), seq(P_GBH), seq(P_GBL),
                  seq(P_HG), cseq(C_I), cseq(C_KF), cseq(C_GFH), cseq(C_GFL), cseq(C_KB), cseq(C_GBH), cseq(C_GBL),
                  full2(1, HG_WIDTH),
                  pl.BlockSpec(consts[0].shape, lambda b: (0, 0, 0)),
                  pl.BlockSpec((2, nl, c, 2 * HG_DK), lambda b: (0, 0, 0, 0)),
                  pl.BlockSpec((2, nl + 1, c, 2 * c), lambda b: (0, 0, 0, 0)),
                  full2(L, 2 * L), full2(L, 2 * L)],
        out_specs=pl.BlockSpec((1, S, HG_WIDTH), lambda b: (b, 0, 0)),
        out_shape=jax.ShapeDtypeStruct((B, S, HG_WIDTH), BF16),
        scratch_shapes=[pltpu.VMEM((S, HG_WIDTH), F32),
                        pltpu.VMEM((2, HG_HEADS, HG_DK, HG_DK), F32)],
        compiler_params=_cparams(("arbitrary",)),
        name="hgrn",
    )(*([proj] * 9), *([cproj] * 7), gain, *consts)


ROUTE_E_ROW = 8


def _route_t(lt):
    tm = lt.shape[1]
    row8 = lax.broadcasted_iota(jnp.int32, (SUBLANES, tm), 0).astype(F32)
    gl = jnp.where(row8 < N_GROUPS, lt[0:SUBLANES], -jnp.inf)
    gmax = jnp.max(gl, axis=0, keepdims=True)
    g_sel = jnp.min(jnp.where(gl == gmax, row8, float(SUBLANES)), axis=0, keepdims=True)
    g_w = 1.0 / jnp.sum(jnp.exp(gl - gmax), axis=0, keepdims=True)
    row16 = lax.broadcasted_iota(jnp.int32, (N_EXPERTS, tm), 0).astype(F32)
    first = g_sel * EXPERTS_PER_GROUP
    in_grp = (row16 >= first) & (row16 < first + EXPERTS_PER_GROUP)
    e1 = jnp.where(in_grp, lt[ROUTE_E_ROW:ROUTE_E_ROW + N_EXPERTS], -jnp.inf)
    v1 = jnp.max(e1, axis=0, keepdims=True)
    i1 = jnp.min(jnp.where(e1 == v1, row16, float(N_EXPERTS)), axis=0, keepdims=True)
    e2 = jnp.where(row16 == i1, -jnp.inf, e1)
    v2 = jnp.max(e2, axis=0, keepdims=True)
    i2 = jnp.min(jnp.where(e2 == v2, row16, float(N_EXPERTS)), axis=0, keepdims=True)
    t = jnp.exp(v2 - v1)
    w1 = g_w / (1.0 + t)
    w2 = g_w * t / (1.0 + t)
    onehot = jnp.where(row8 == g_sel, 1.0, 0.0)
    gates = jnp.where(row8 == i1 - first, w1, jnp.where(row8 == i2 - first, w2, 0.0))
    return onehot, gates


def _outproj_kernel(x_ref, na_ref, hg_ref, w_ref, ga_ref, g_ref, sc_ref, sh_ref, wrh_ref, wrl_ref, br_ref,
                    x1_ref, hx_ref, oh_ref, cnt_ref):
    tm, d = x_ref.shape
    nt = (((1,), (1,)), ((), ()))
    mix = (jnp.dot(na_ref[...], w_ref[0:NA_WIDTH, :], preferred_element_type=F32)
           + jnp.dot(hg_ref[...], w_ref[NA_WIDTH:, :], preferred_element_type=F32))
    x1 = x_ref[...] + ga_ref[0] * mix
    x1_ref[...] = x1
    h2 = _norm_mod(x1, g_ref[...], sc_ref[0], sh_ref[0])
    h_hi = h2.astype(BF16)
    h_lo = (h2 - h_hi.astype(F32)).astype(BF16)
    hx_ref[:, 0:d] = h_hi
    lt = ((lax.dot_general(wrl_ref[...], h_hi, nt, preferred_element_type=F32)
           + lax.dot_general(wrh_ref[...], h_lo, nt, preferred_element_type=F32))
          + lax.dot_general(wrh_ref[...], h_hi, nt, preferred_element_type=F32)) + br_ref[...]
    onehot_t, gates_t = _route_t(lt)
    tok = jnp.concatenate([onehot_t, gates_t, jnp.zeros((LANES - 2 * SUBLANES, tm), F32)], axis=0).T
    lane = lax.broadcasted_iota(jnp.int32, tok.shape, 1)
    onehot = jnp.where(lane < ROUTE_E_ROW, tok, 0.0)
    gates = jnp.where(lane >= ROUTE_E_ROW, tok, 0.0)
    for j, piece in enumerate(_split3(gates)):
        hx_ref[:, d + j * LANES:d + (j + 1) * LANES] = piece
    oh_ref[...] = onehot
    cnt_ref[0] = jnp.broadcast_to(jnp.sum(onehot, axis=0, keepdims=True), (SUBLANES, LANES))


def _outproj(x2d, na2d, hg2d, w_out_bf, ga, g, sc, sh, wr_hi, wr_lo, br, S):
    T, D = x2d.shape
    tm = MOE_TM
    per = S // tm
    tok = lambda w: pl.BlockSpec((tm, w), lambda i: (i, 0))
    bat = pl.BlockSpec((1, 1, D), lambda i: (i // per, 0, 0))
    return pl.pallas_call(
        _outproj_kernel,
        grid=(T // tm,),
        in_specs=[tok(D), tok(NA_WIDTH), tok(HG_WIDTH),
                  pl.BlockSpec((NA_WIDTH + HG_WIDTH, D), lambda i: (0, 0)),
                  bat, pl.BlockSpec((1, D), lambda i: (0, 0)), bat, bat,
                  pl.BlockSpec((LANES, D), lambda i: (0, 0)),
                  pl.BlockSpec((LANES, D), lambda i: (0, 0)),
                  pl.BlockSpec((LANES, 1), lambda i: (0, 0))],
        out_specs=[tok(D), tok(MOE_XW), tok(LANES),
                   pl.BlockSpec((1, SUBLANES, LANES), lambda i: (i, 0, 0))],
        out_shape=[jax.ShapeDtypeStruct((T, D), F32), jax.ShapeDtypeStruct((T, MOE_XW), BF16),
                   jax.ShapeDtypeStruct((T, LANES), F32),
                   jax.ShapeDtypeStruct((T // tm, SUBLANES, LANES), F32)],
        compiler_params=_cparams(("arbitrary",)),
        name="outproj",
    )(x2d, na2d, hg2d, w_out_bf, ga, g, sc, sh, wr_hi, wr_lo, br)


MOE_TM = 512
MOE_ALIGN = 16
MOE_LOC = 640
MOE_XW = D_MODEL + 3 * LANES
MOE_TE = 512
MOE_BITS = tuple(1 << b for b in range(9, 3, -1))


def _moe_steps(T):
    worst = T + (T // MOE_TM) * N_GROUPS * (MOE_ALIGN - 1)
    return -(-worst // MOE_TE) + N_GROUPS


def _local_positions(oh_ref, lstrict_ref, cnt_ref, k):
    onehot = oh_ref[...]
    ranks = jnp.dot(lstrict_ref[...], onehot.astype(BF16), preferred_element_type=F32)
    lane = lax.broadcasted_iota(jnp.int32, (1, LANES), 1)
    base = jnp.zeros((1, LANES), F32)
    o = jnp.int32(0)
    for g in range(N_GROUPS):
        base = jnp.where(lane == g, o.astype(F32), base)
        o = o + cnt_ref[k * N_GROUPS + g]
    return jnp.sum(onehot * (ranks + base), axis=-1, keepdims=True)


def _sort_matrix(lpos):
    col = lax.broadcasted_iota(jnp.int32, (MOE_TM, MOE_LOC), 1).astype(F32)
    return jnp.where(lpos == col, 1.0, 0.0).astype(BF16)


def _block_copies(n_rows, src_row, dst_row, bits, make_copy, action):
    for bit in bits:
        part = n_rows & (-2 * bit)

        @pl.when((n_rows & bit) != 0)
        def _(part=part, bit=bit):
            action(make_copy(pl.multiple_of(src_row + part, MOE_ALIGN), pl.multiple_of(dst_row + part, MOE_ALIGN), bit))


def _run_copies(off_ref, cnt_ref, k, make_copy, action):
    o = jnp.int32(0)
    for g in range(N_GROUPS):
        c = cnt_ref[k * N_GROUPS + g]
        _block_copies(c, o, off_ref[k * N_GROUPS + g], MOE_BITS, make_copy, action)
        o = o + c


def _dispatch_kernel(off_ref, cnt_ref, tail_ref, hx_ref, oh_ref, lstrict_ref, hs_ref, buf_ref, zero_ref, sem,
                     *, n_tiles, n_steps):
    k = pl.program_id(0)
    slot = k % 2

    def copies(kk, sl, action):
        def make(src_row, dst_row, n):
            return pltpu.make_async_copy(buf_ref.at[sl, pl.ds(src_row, n)], hs_ref.at[pl.ds(dst_row, n)], sem.at[sl])
        _run_copies(off_ref, cnt_ref, kk, make, action)

    @pl.when(k >= 2)
    def _():
        copies(k - 2, slot, lambda cp: cp.wait())

    pt = _sort_matrix(_local_positions(oh_ref, lstrict_ref, cnt_ref, k))
    srt = lax.dot_general(pt, hx_ref[...], (((0,), (0,)), ((), ())), preferred_element_type=F32)
    buf_ref[slot] = srt.astype(BF16)
    copies(k, slot, lambda cp: cp.start())

    @pl.when(k == n_tiles - 1)
    def _():
        zero_ref[...] = jnp.zeros_like(zero_ref)
        n_used = tail_ref[2 * N_GROUPS]

        def zero_copy(src_row, dst_row, n):
            return pltpu.make_async_copy(zero_ref.at[pl.ds(src_row, n)], hs_ref.at[pl.ds(dst_row, n)], sem.at[2])

        def tile_copy(i):
            return zero_copy(0, pl.multiple_of(i * MOE_TE, MOE_TE), MOE_TE)

        def fills(action):
            for g in range(N_GROUPS):
                _block_copies(tail_ref[N_GROUPS + g], jnp.int32(0), tail_ref[g], MOE_BITS[1:], zero_copy, action)

        fills(lambda cp: cp.start())
        lax.fori_loop(n_used, n_steps, lambda i, c: (tile_copy(i).start(), c)[1], 0)
        if n_tiles >= 2:
            copies(k - 1, 1 - slot, lambda cp: cp.wait())
        copies(k, slot, lambda cp: cp.wait())
        fills(lambda cp: cp.wait())
        lax.fori_loop(n_used, n_steps, lambda i, c: (tile_copy(i).wait(), c)[1], 0)


def _dispatch(off, cnt, tail, hx, onehot, lstrict, n_steps):
    T = hx.shape[0]
    n_tiles = T // MOE_TM
    return pl.pallas_call(
        functools.partial(_dispatch_kernel, n_tiles=n_tiles, n_steps=n_steps),
        grid_spec=pltpu.PrefetchScalarGridSpec(
            num_scalar_prefetch=3,
            grid=(n_tiles,),
            in_specs=[pl.BlockSpec((MOE_TM, MOE_XW), lambda k, *_: (k, 0)),
                      pl.BlockSpec((MOE_TM, LANES), lambda k, *_: (k, 0)),
                      pl.BlockSpec((MOE_TM, MOE_TM), lambda k, *_: (0, 0))],
            out_specs=pl.BlockSpec(memory_space=pl.ANY),
            scratch_shapes=[pltpu.VMEM((2, MOE_LOC, MOE_XW), BF16),
                            pltpu.VMEM((MOE_TE, MOE_XW), BF16),
                            pltpu.SemaphoreType.DMA((3,))]),
        out_shape=jax.ShapeDtypeStruct((n_steps * MOE_TE, MOE_XW), BF16),
        compiler_params=_cparams(("arbitrary",)),
        name="dispatch",
    )(off, cnt, tail, hx, onehot, lstrict)


def _experts_kernel(grp_ref, used_ref, hs_ref, w1_ref, w3_ref, w2_ref, ys_ref):
    i = pl.program_id(0)
    d = ys_ref.shape[1]
    ne = EXPERTS_PER_GROUP

    @pl.when(i < used_ref[0])
    def _():
        t = hs_ref[:, 0:d]
        gates = ((hs_ref[:, d + 2 * LANES:d + 3 * LANES].astype(F32) + hs_ref[:, d + LANES:d + 2 * LANES].astype(F32))
                 + hs_ref[:, d:d + LANES].astype(F32))
        lane = lax.broadcasted_iota(jnp.int32, gates.shape, 1)
        acc = None
        for j in range(ne):
            a = jnp.dot(t, w1_ref[0, j], preferred_element_type=F32)
            b = jnp.dot(t, w3_ref[0, j], preferred_element_type=F32)
            gj = jnp.sum(jnp.where(lane == ROUTE_E_ROW + j, gates, 0.0), axis=-1, keepdims=True)
            y = gj * jnp.dot((_silu(a) * b).astype(BF16), w2_ref[0, j], preferred_element_type=F32)
            acc = y if acc is None else acc + y
        ys_ref[...] = acc.astype(BF16)

    @pl.when(i >= used_ref[0])
    def _():
        ys_ref[...] = jnp.zeros_like(ys_ref)


def _experts(grp, used, hs, w1_bf, w3_bf, w2_bf):
    rows, _ = hs.shape
    D = w2_bf.shape[3]
    wspec = lambda w: pl.BlockSpec((1,) + w.shape[1:], lambda i, grp, used: (grp[i], 0, 0, 0))
    return pl.pallas_call(
        _experts_kernel,
        grid_spec=pltpu.PrefetchScalarGridSpec(
            num_scalar_prefetch=2,
            grid=(rows // MOE_TE,),
            in_specs=[pl.BlockSpec((MOE_TE, MOE_XW), lambda i, grp, used: (i, 0)),
                      wspec(w1_bf), wspec(w3_bf), wspec(w2_bf)],
            out_specs=pl.BlockSpec((MOE_TE, D), lambda i, grp, used: (i, 0))),
        out_shape=jax.ShapeDtypeStruct((rows, D), BF16),
        compiler_params=_cparams(("arbitrary",)),
        name="experts",
    )(grp, used, hs, w1_bf, w3_bf, w2_bf)


def _combine_kernel(off_ref, cnt_ref, x1_ref, oh_ref, lstrict_ref, ga_ref, gf_ref, ys_ref, o_ref, buf_ref, sem,
                    *, n_tiles):
    k = pl.program_id(0)
    slot = k % 2

    def copies(kk, sl, action):
        def make(loc_row, seg_row, n):
            return pltpu.make_async_copy(ys_ref.at[pl.ds(seg_row, n)], buf_ref.at[sl, pl.ds(loc_row, n)], sem.at[sl])
        _run_copies(off_ref, cnt_ref, kk, make, action)

    @pl.when(k == 0)
    def _():
        buf_ref[...] = jnp.zeros_like(buf_ref)
        copies(k, slot, lambda cp: cp.start())

    @pl.when(k + 1 < n_tiles)
    def _():
        copies(k + 1, 1 - slot, lambda cp: cp.start())

    pt = _sort_matrix(_local_positions(oh_ref, lstrict_ref, cnt_ref, k))
    copies(k, slot, lambda cp: cp.wait())
    y = jnp.dot(pt, buf_ref[slot], preferred_element_type=F32)
    x2 = x1_ref[...] + ga_ref[0] * y
    o_ref[...] = x2 * lax.rsqrt(jnp.mean(x2 * x2, axis=-1, keepdims=True) + EPS) * gf_ref[...]


def _combine(off, cnt, x1, onehot, lstrict, ga, gfin, ys, S):
    T, D = x1.shape
    n_tiles = T // MOE_TM
    per = S // MOE_TM
    return pl.pallas_call(
        functools.partial(_combine_kernel, n_tiles=n_tiles),
        grid_spec=pltpu.PrefetchScalarGridSpec(
            num_scalar_prefetch=2,
            grid=(n_tiles,),
            in_specs=[pl.BlockSpec((MOE_TM, D), lambda k, off, cnt: (k, 0)),
                      pl.BlockSpec((MOE_TM, LANES), lambda k, off, cnt: (k, 0)),
                      pl.BlockSpec((MOE_TM, MOE_TM), lambda k, off, cnt: (0, 0)),
                      pl.BlockSpec((1, 1, D), lambda k, off, cnt: (k // per, 0, 0)),
                      pl.BlockSpec((1, D), lambda k, off, cnt: (0, 0)),
                      pl.BlockSpec(memory_space=pl.ANY)],
            out_specs=pl.BlockSpec((MOE_TM, D), lambda k, off, cnt: (k, 0)),
            scratch_shapes=[pltpu.VMEM((2, MOE_LOC, D), BF16),
                            pltpu.SemaphoreType.DMA((2,))]),
        out_shape=jax.ShapeDtypeStruct((T, D), F32),
        compiler_params=_cparams(("arbitrary",)),
        name="combine",
    )(off, cnt, x1, onehot, lstrict, ga, gfin, ys)


def _moe_schedule(cnt_tiles, n_steps):
    cnt = ((cnt_tiles + (MOE_ALIGN - 1)) // MOE_ALIGN) * MOE_ALIGN
    ends = jnp.cumsum(cnt, axis=0)
    total = ends[-1]
    ntile = (total + MOE_TE - 1) // MOE_TE
    cum = jnp.cumsum(ntile)
    base = (cum - ntile) * MOE_TE
    off = base[None, :] + ends - cnt
    i = jnp.arange(n_steps, dtype=jnp.int32)
    grp = jnp.minimum(jnp.sum(i[:, None] >= cum[None, :], axis=1), N_GROUPS - 1)
    grp = jnp.where(i < cum[-1], grp, grp[jnp.maximum(cum[-1] - 1, 0)])
    tail = jnp.concatenate([base + total, ntile * MOE_TE - total, cum[-1:]])
    as_i32 = lambda a: a.reshape(-1).astype(jnp.int32)
    return as_i32(off), as_i32(cnt), as_i32(tail), as_i32(grp), as_i32(cum[-1:])


def _rope_tables(S):
    t = jnp.arange(S)
    pos = jnp.stack([t // GRID_W, t % GRID_W], axis=-1).astype(F32)
    inv = ROPE_BASE ** (-jnp.arange(0, ROPE_AXIS_DIM, 2, dtype=F32) / ROPE_AXIS_DIM)
    ang = pos[:, :, None] * inv
    cos, sin = jnp.cos(ang), jnp.sin(ang)
    cos_h = jnp.concatenate([cos, cos], axis=-1).reshape(S, NA_HEAD_DIM)
    sin_h = jnp.concatenate([-sin, sin], axis=-1).reshape(S, NA_HEAD_DIM)
    return jnp.tile(cos_h, (1, NA_HEADS)), jnp.tile(sin_h, (1, NA_HEADS))


def kernel(x, c, ctx, c_ctx, w_mod, b_mod, norm_mix, norm_ffn, w_in, w_out, na_rpb, hg_lb, hg_norm,
           w_grp, b_grp, w_exp, b_exp, w1, w3, w2, norm_final):
    B, S, D = x.shape
    T = B * S
    assert w_mod.shape[0] == 1, "single-layer kernel"

    rows = -(-(B + 1) // SUBLANES) * SUBLANES
    cc = jnp.zeros((rows, D), F32).at[:B].set(c).at[B].set(c_ctx)
    mod = _modulation(cc, w_mod[0], b_mod[0])
    sh_a, sc_a, ga_a, sh_f, sc_f, ga_f = [m.reshape(B, 1, D) for m in jnp.split(mod[:B], 6, axis=-1)]
    csh_a, csc_a = [m.reshape(1, D) for m in jnp.split(mod[B], 6)[:2]]

    w_in_bf = w_in[0].astype(BF16)
    w_ctx_bf = jnp.concatenate([w_in_bf[:, i * SEG:(i + 1) * SEG] for i in (1, 2, 4, 5, 6)], axis=1)
    cos_t, sin_t = _rope_tables(S)
    g_mix = norm_mix[0].reshape(1, D)

    lb = jnp.cumsum(jax.nn.softmax(hg_lb.astype(F32), axis=0), axis=0)[0]
    proj = _project(x, g_mix, sc_a, sh_a, w_in_bf, cos_t, sin_t, lb, tm=512)
    cproj = _project_ctx(ctx, g_mix, csc_a, csh_a, w_ctx_bf, lb)

    na_out = _neighbourhood_attention(proj, cproj, _na_bias_table(na_rpb[0]))

    gain = jnp.tile(hg_norm[0].astype(F32), HG_HEADS).reshape(1, HG_WIDTH)
    hg_out = _hgrn(proj, cproj, gain)

    e_rows = slice(ROUTE_E_ROW, ROUTE_E_ROW + N_EXPERTS)
    wr = jnp.zeros((LANES, D), F32).at[:N_GROUPS].set(w_grp[0].T).at[e_rows].set(w_exp[0].T)
    br = jnp.zeros((LANES, 1), F32).at[:N_GROUPS, 0].set(b_grp[0]).at[e_rows, 0].set(b_exp[0])
    hi_f32 = lax.bitcast_convert_type(lax.bitcast_convert_type(wr, jnp.uint32) & jnp.uint32(0xFFFF0000), F32)
    wr_hi = hi_f32.astype(BF16)
    wr_lo = (wr - hi_f32).astype(BF16)
    x1, hx, onehot, cnt_tiles = _outproj(x.reshape(T, D), na_out.reshape(T, NA_WIDTH), hg_out.reshape(T, HG_WIDTH),
                                         w_out[0].astype(BF16), ga_a, norm_ffn[0].reshape(1, D), sc_f, sh_f,
                                         wr_hi, wr_lo, br, S=S)

    n_steps = _moe_steps(T)
    off, cnt, tail, grp, used = _moe_schedule(cnt_tiles[:, 0, :N_GROUPS].astype(jnp.int32), n_steps)
    lstrict = jnp.asarray(np.tril(np.ones((MOE_TM, MOE_TM), np.float32), -1), BF16)
    by_group = lambda w: w[0].astype(BF16).reshape(N_GROUPS, EXPERTS_PER_GROUP, *w.shape[2:])
    hs = _dispatch(off, cnt, tail, hx, onehot, lstrict, n_steps)
    ys = _experts(grp, used, hs, by_group(w1), by_group(w3), by_group(w2))
    out = _combine(off, cnt, x1, onehot, lstrict, ga_f, norm_final.reshape(1, D), ys, S)
    return out.reshape(B, S, D)
```

```python
import functools

import numpy as np
import jax
import jax.numpy as jnp
from jax import lax
from jax.experimental import pallas as pl
from jax.experimental.pallas import tpu as pltpu

F32 = jnp.float32
BF16 = jnp.bfloat16
HIGHEST = lax.Precision.HIGHEST

D_MODEL = 1024
GRID_W = 64
NA_HEADS = 8
NA_HEAD_DIM = 64
NA_WIDTH = NA_HEADS * NA_HEAD_DIM
NA_KH = 8
NA_KW = 16
ROPE_AXIS_DIM = NA_HEAD_DIM // 2
ROPE_BASE = 10000.0
HG_HEADS = 4
HG_DK = 128
HG_WIDTH = HG_HEADS * HG_DK
HG_CHUNK = 64
SEG = 512
N_GROUPS = 4
EXPERTS_PER_GROUP = 4
N_EXPERTS = N_GROUPS * EXPERTS_PER_GROUP
D_EXPERT = 512
EPS = 1e-6
NEG = -1e30
LOG2E = 1.4426950408889634
LANES = 128
SUBLANES = 8
VMEM_LIMIT = 56 * 1024 * 1024

HG_LEVELS = (32, 16, 8, 4, 2, 1)
HG_CHUNKS_PER_STEP = 4


def _cparams(sem):
    return pltpu.CompilerParams(dimension_semantics=sem, vmem_limit_bytes=VMEM_LIMIT)


def _sigmoid(x):
    return 1.0 / (1.0 + jnp.exp(-x))


def _silu(x):
    return x * _sigmoid(x)


def _mod_kernel(c_ref, w_ref, b_ref, o_ref):
    s = _silu(c_ref[...])
    o_ref[...] = jnp.dot(s, w_ref[...], precision=HIGHEST, preferred_element_type=F32) + b_ref[...]


def _modulation(cc, w_mod, b_mod):
    rows, d = cc.shape
    n = w_mod.shape[1]
    tn = 1024
    return pl.pallas_call(
        _mod_kernel,
        grid=(n // tn,),
        in_specs=[pl.BlockSpec((rows, d), lambda j: (0, 0)),
                  pl.BlockSpec((d, tn), lambda j: (0, j)),
                  pl.BlockSpec((1, tn), lambda j: (0, j))],
        out_specs=pl.BlockSpec((rows, tn), lambda j: (0, j)),
        out_shape=jax.ShapeDtypeStruct((rows, n), F32),
        compiler_params=_cparams(("arbitrary",)),
        name="mod",
    )(cc, w_mod, b_mod.reshape(1, n))


def _norm_mod(x, g, sc, sh):
    y = x * lax.rsqrt(jnp.mean(x * x, axis=-1, keepdims=True) + EPS)
    return (y * g) * (1.0 + sc) + sh


def _rope(a, cos, sin):
    lane = lax.broadcasted_iota(jnp.int32, a.shape, 1)
    first = (lane % ROPE_AXIS_DIM) < (ROPE_AXIS_DIM // 2)
    up = pltpu.roll(a, LANES - ROPE_AXIS_DIM // 2, axis=1)
    dn = pltpu.roll(a, ROPE_AXIS_DIM // 2, axis=1)
    return a * cos + jnp.where(first, up, dn) * sin


def _forget_gate(pre, lb):
    f = lb + (1.0 - lb) * _sigmoid(pre)
    g2 = jnp.log2(f)
    hi = g2.astype(BF16)
    return 1.0 - f, hi, g2 - hi.astype(F32)


P_QRAW, P_QROT, P_KROT, P_V, P_HQ, P_HI, P_KF, P_GFH, P_GFL, P_KB, P_GBH, P_GBL, P_HG = range(13)
P_SEGS = 13


def _proj_kernel(x_ref, g_ref, sc_ref, sh_ref, w_ref, cos_ref, sin_ref, lb_ref, o_ref):
    h = _norm_mod(x_ref[0], g_ref[...], sc_ref[0], sh_ref[0]).astype(BF16)
    scale = NA_HEAD_DIM ** -0.5 * LOG2E

    def put(seg, val):
        o_ref[0, :, seg * SEG:(seg + 1) * SEG] = val.astype(BF16)

    for j in range(8):
        acc = jnp.dot(h, w_ref[:, j * SEG:(j + 1) * SEG], preferred_element_type=F32)
        if j <= 1:
            rot = jnp.concatenate(
                [_rope(acc[:, p * LANES:(p + 1) * LANES], cos_ref[:, p * LANES:(p + 1) * LANES],
                       sin_ref[:, p * LANES:(p + 1) * LANES]) for p in range(SEG // LANES)], axis=1)
            if j == 0:
                put(P_QRAW, acc * scale)
                put(P_QROT, rot * scale)
            else:
                put(P_KROT, rot)
        elif j == 2:
            put(P_V, acc)
        elif j == 3:
            put(P_HQ, _silu(acc))
        elif j == 4:
            put(P_HI, acc)
        elif j in (5, 6):
            base = P_KF if j == 5 else P_KB
            for i, val in enumerate(_forget_gate(acc, lb_ref[j - 5:j - 4, :])):
                put(base + i, val)
        else:
            put(P_HG, acc)


def _project(x, g, sc, sh, w_bf, cos_t, sin_t, lb, tm):
    B, S, D = x.shape
    return pl.pallas_call(
        _proj_kernel,
        grid=(S // tm, B),
        in_specs=[pl.BlockSpec((1, tm, D), lambda s, b: (b, s, 0)),
                  pl.BlockSpec((1, D), lambda s, b: (0, 0)),
                  pl.BlockSpec((1, 1, D), lambda s, b: (b, 0, 0)),
                  pl.BlockSpec((1, 1, D), lambda s, b: (b, 0, 0)),
                  pl.BlockSpec((D, 8 * SEG), lambda s, b: (0, 0)),
                  pl.BlockSpec((tm, SEG), lambda s, b: (s, 0)),
                  pl.BlockSpec((tm, SEG), lambda s, b: (s, 0)),
                  pl.BlockSpec((2, HG_WIDTH), lambda s, b: (0, 0))],
        out_specs=pl.BlockSpec((1, tm, P_SEGS * SEG), lambda s, b: (b, s, 0)),
        out_shape=jax.ShapeDtypeStruct((B, S, P_SEGS * SEG), BF16),
        compiler_params=_cparams(("arbitrary", "arbitrary")),
        name="proj",
    )(x, g, sc, sh, w_bf, cos_t, sin_t, lb)


C_K, C_V, C_I, C_KF, C_GFH, C_GFL, C_KB, C_GBH, C_GBL = range(9)
C_SEGS = 9


def _ctxproj_kernel(x_ref, g_ref, sc_ref, sh_ref, w_ref, lb_ref, o_ref):
    h = _norm_mod(x_ref[0], g_ref[...], sc_ref[...], sh_ref[...]).astype(BF16)

    def put(seg, val):
        o_ref[0, :, seg * SEG:(seg + 1) * SEG] = val.astype(BF16)

    for j in range(5):
        acc = jnp.dot(h, w_ref[:, j * SEG:(j + 1) * SEG], preferred_element_type=F32)
        if j < 3:
            put(j, acc)
        else:
            for i, val in enumerate(_forget_gate(acc, lb_ref[j - 3:j - 2, :])):
                put(C_KF + 3 * (j - 3) + i, val)


def _project_ctx(ctx, g, sc, sh, w_bf, lb):
    B, L, D = ctx.shape
    return pl.pallas_call(
        _ctxproj_kernel,
        grid=(B,),
        in_specs=[pl.BlockSpec((1, L, D), lambda b: (b, 0, 0)),
                  pl.BlockSpec((1, D), lambda b: (0, 0)),
                  pl.BlockSpec((1, D), lambda b: (0, 0)),
                  pl.BlockSpec((1, D), lambda b: (0, 0)),
                  pl.BlockSpec((D, 5 * SEG), lambda b: (0, 0)),
                  pl.BlockSpec((2, HG_WIDTH), lambda b: (0, 0))],
        out_specs=pl.BlockSpec((1, L, C_SEGS * SEG), lambda b: (b, 0, 0)),
        out_shape=jax.ShapeDtypeStruct((B, L, C_SEGS * SEG), BF16),
        compiler_params=_cparams(("arbitrary",)),
        name="ctxproj",
    )(ctx, g, sc, sh, w_bf, lb)


NA_ROWS_PER_STEP = 8


def _na_kernel(qraw_ref, qrot_ref, k_ref, v_ref, ck_ref, cv_ref, bias_ref, o_ref, *, rows):
    nk = NA_KH * GRID_W
    lane = lax.broadcasted_iota(jnp.int32, (GRID_W, LANES), 1)
    sel0 = lane < NA_HEAD_DIM
    nt = (((1,), (1,)), ((), ()))
    n_pairs = NA_WIDTH // LANES
    ones = jnp.ones((nk + ck_ref.shape[1], LANES), BF16)
    units = [(j, p) for j in range(NA_ROWS_PER_STEP) for p in range(n_pairs)]

    def window(j):
        r = pl.program_id(1) * NA_ROWS_PER_STEP + j
        rs = jnp.clip(r - NA_KH // 2, 0, rows - NA_KH)
        return pl.multiple_of(rs * GRID_W, GRID_W), pl.multiple_of((rs - r + NA_KH - 1) * GRID_W, GRID_W)

    def query_slots(ref, j, cols):
        q = ref[0, j * GRID_W:(j + 1) * GRID_W, cols]
        zero = jnp.zeros_like(q)
        return jnp.concatenate([jnp.where(sel0, q, zero), jnp.where(sel0, zero, q)], axis=0)

    def scores(j, p):
        cols = slice(p * LANES, (p + 1) * LANES)
        start, bias_row = window(j)
        s_loc = lax.dot_general(k_ref[0, pl.ds(start, nk), cols], query_slots(qrot_ref, j, cols), nt,
                                preferred_element_type=F32) + bias_ref[p, pl.ds(bias_row, nk), :]
        s_ctx = lax.dot_general(ck_ref[0, :, cols], query_slots(qraw_ref, j, cols), nt,
                                preferred_element_type=F32)
        return s_loc, s_ctx

    def softmax(s_loc, s_ctx):
        m = jnp.maximum(jnp.max(s_loc, axis=0, keepdims=True), jnp.max(s_ctx, axis=0, keepdims=True))
        return jnp.concatenate([jnp.exp2(s_loc - m).T, jnp.exp2(s_ctx - m).T], axis=1).astype(BF16)

    def values(j, p, probs):
        cols = slice(p * LANES, (p + 1) * LANES)
        start, _ = window(j)
        vals = jnp.concatenate([v_ref[0, pl.ds(start, nk), cols], cv_ref[0, :, cols]], axis=0)
        oe = jnp.dot(probs, jnp.concatenate([vals, ones], axis=1), preferred_element_type=F32)
        res = oe[:, :LANES] / oe[:, LANES:]
        o_ref[0, j * GRID_W:(j + 1) * GRID_W, cols] = jnp.where(sel0, res[0:GRID_W],
                                                                res[GRID_W:2 * GRID_W]).astype(BF16)

    s_all = [scores(j, p) for j, p in units]
    probs = [softmax(*s) for s in s_all]
    for (j, p), pr in zip(units, probs):
        values(j, p, pr)


def _na_bias_table(rpb):
    qc = np.arange(GRID_W)[None, :]
    kc = np.arange(GRID_W)[:, None]
    cs = np.clip(qc - NA_KW // 2, 0, GRID_W - NA_KW)
    valid = (kc >= cs) & (kc < cs + NA_KW)
    col_off = np.clip(kc - qc + NA_KW - 1, 0, 2 * NA_KW - 2)
    t = (rpb.astype(F32) * LOG2E)[:, :, col_off]
    t = jnp.where(jnp.asarray(valid)[None, None], t, NEG)
    n_pairs = NA_HEADS // 2
    n_ro = 2 * NA_KH - 1
    t = t.reshape(n_pairs, 2, n_ro, GRID_W, GRID_W).transpose(0, 2, 3, 1, 4)
    return t.reshape(n_pairs, n_ro * GRID_W, 2 * GRID_W)


def _neighbourhood_attention(proj, cproj, bias_tbl):
    B, S, _ = proj.shape
    L = cproj.shape[1]
    rows = S // GRID_W
    tq = NA_ROWS_PER_STEP * GRID_W
    return pl.pallas_call(
        functools.partial(_na_kernel, rows=rows),
        grid=(B, rows // NA_ROWS_PER_STEP),
        in_specs=[pl.BlockSpec((1, tq, SEG), lambda b, r: (b, r, P_QRAW)),
                  pl.BlockSpec((1, tq, SEG), lambda b, r: (b, r, P_QROT)),
                  pl.BlockSpec((1, S, SEG), lambda b, r: (b, 0, P_KROT)),
                  pl.BlockSpec((1, S, SEG), lambda b, r: (b, 0, P_V)),
                  pl.BlockSpec((1, L, SEG), lambda b, r: (b, 0, C_K)),
                  pl.BlockSpec((1, L, SEG), lambda b, r: (b, 0, C_V)),
                  pl.BlockSpec(bias_tbl.shape, lambda b, r: (0, 0, 0))],
        out_specs=pl.BlockSpec((1, tq, SEG), lambda b, r: (b, r, 0)),
        out_shape=jax.ShapeDtypeStruct((B, S, NA_WIDTH), BF16),
        compiler_params=_cparams(("arbitrary", "arbitrary")),
        name="na",
    )(proj, proj, proj, proj, cproj, cproj, bias_tbl)


def _split3(g):
    g1 = g.astype(BF16)
    r1 = g - g1.astype(F32)
    g2 = r1.astype(BF16)
    g3 = (r1 - g2.astype(F32)).astype(BF16)
    return g1, g2, g3


def _hg_consts():
    c = HG_CHUNK
    t = np.arange(c)[:, None]
    u = np.arange(c)[None, :]
    cm, role, masks = [], [], []
    for fwd in (True, False):
        blocks = [(u <= t) if fwd else (u >= t)]
        roles, ms = [], []
        for m in HG_LEVELS:
            blk = t // (2 * m)
            if fwd:
                mid = 2 * m * blk + m - 1
                is_q = (t % (2 * m)) >= m
                expo = np.where(is_q, (u > mid) & (u <= t), (u > t) & (u <= mid))
            else:
                mid = 2 * m * blk + m
                is_q = (t % (2 * m)) < m
                expo = np.where(is_q, (u >= t) & (u < mid), (u >= mid) & (u < t))
            blocks.append(expo)
            roles.append(np.broadcast_to(is_q, (c, 2 * HG_DK)))
            ms.append((blk == blk.T) & is_q & ~is_q.T)
        ms.append(t == u)
        full = np.concatenate(blocks, axis=0).astype(np.float32)
        cm.append(np.concatenate([full, full], axis=1))
        role.append(np.stack(roles).astype(np.float32))
        masks.append(np.stack([np.concatenate([x, x], axis=1) for x in ms]).astype(np.float32))
    return jnp.asarray(np.stack(cm), BF16), jnp.asarray(np.stack(role), BF16), jnp.asarray(np.stack(masks))


def _block_diag(x, zeros):
    return jnp.concatenate([jnp.concatenate([x[:, :HG_DK], zeros], axis=1),
                            jnp.concatenate([zeros, x[:, HG_DK:]], axis=1)], axis=0)


def _hg_exponents(gh, gl, cmat_ref, di):
    return jnp.dot(cmat_ref[di], jnp.concatenate([gh, gl], axis=0), preferred_element_type=F32)


def _hg_intra(q, k, dall, role_ref, masks_ref, di):
    c = HG_CHUNK
    nl = len(HG_LEVELS)
    nt = (((1,), (1,)), ((), ()))
    b = dall[0:c]
    bend = b[c - 1:c] if di == 0 else b[0:1]
    qe = q * jnp.exp2(b).astype(BF16)
    ke = k * jnp.exp2(bend - b).astype(BF16)
    zeros = jnp.zeros((c, HG_DK), BF16)
    a = None
    for li in range(nl):
        e = jnp.exp2(dall[(li + 1) * c:(li + 2) * c]).astype(BF16)
        x = jnp.where(role_ref[di, li] > 0.5, q, k) * e
        r = lax.dot_general(x, _block_diag(x, zeros), nt, preferred_element_type=F32) * masks_ref[di, li]
        a = r if a is None else a + r
    qk = q.astype(F32) * k.astype(F32)
    lane = lax.broadcasted_iota(jnp.int32, a.shape, 1)
    d0 = jnp.sum(qk[:, :HG_DK], axis=-1, keepdims=True)
    d1 = jnp.sum(qk[:, HG_DK:], axis=-1, keepdims=True)
    a = a + jnp.where(lane < c, d0, d1) * masks_ref[di, nl]
    return a.astype(BF16), qe, ke, bend


def _hg_outputs(a, qe, ke, bend, v, st_ref, di, pair):
    nt = (((1,), (1,)), ((), ()))
    tn = (((0,), (0,)), ((), ()))
    o = jnp.dot(a, _block_diag(v, jnp.zeros((HG_CHUNK, HG_DK), BF16)), preferred_element_type=F32)
    outs = []
    for hh in range(2):
        hs = slice(hh * HG_DK, (hh + 1) * HG_DK)
        st = st_ref[di, 2 * pair + hh]
        outs.append(o[:, hs] + lax.dot_general(qe[:, hs], st.astype(BF16), nt, preferred_element_type=F32))
        upd = lax.dot_general(v[:, hs], ke[:, hs], tn, preferred_element_type=F32)
        st_ref[di, 2 * pair + hh] = jnp.exp2(bend[:, hs]) * st + upd
    return outs


def _hgrn_kernel(sq_ref, hi_ref, kf_ref, gfh_ref, gfl_ref, kb_ref, gbh_ref, gbl_ref, hg_ref,
                 ci_ref, ckf_ref, cgfh_ref, cgfl_ref, ckb_ref, cgbh_ref, cgbl_ref, gain_ref,
                 cmat_ref, role_ref, masks_ref, after_ref, before_ref, o_ref, acc_ref, st_ref, *, n_chunks):
    c = HG_CHUNK
    tn = (((0,), (0,)), ((), ()))

    ctx_dirs = ((ckf_ref, cgfh_ref, cgfl_ref, after_ref), (ckb_ref, cgbh_ref, cgbl_ref, before_ref))
    for di, (k_ref, gh_ref, gl_ref, cm_ref) in enumerate(ctx_dirs):
        decay = jnp.dot(cm_ref[...], jnp.concatenate([gh_ref[0], gl_ref[0]], axis=0), preferred_element_type=F32)
        ke = k_ref[0] * jnp.exp2(decay).astype(BF16)
        for h in range(HG_HEADS):
            hs = slice(h * HG_DK, (h + 1) * HG_DK)
            st_ref[di, h] = lax.dot_general(ci_ref[0, :, hs], ke[:, hs], tn, preferred_element_type=F32)

    def step(i, second):
        units = [(sub, pair, di) for sub in range(HG_CHUNKS_PER_STEP)
                 for pair in range(HG_HEADS // 2) for di in range(2)]

        def window(u):
            sub, pair, di = units[u]
            ci = i * HG_CHUNKS_PER_STEP + sub
            first = ci * c if di == 0 else (n_chunks - 1 - ci) * c
            return pl.ds(pl.multiple_of(first, c), c), slice(2 * pair * HG_DK, 2 * (pair + 1) * HG_DK)

        def exponents(u):
            rr, cols = window(u)
            gh_ref, gl_ref = (gfh_ref, gfl_ref) if units[u][2] == 0 else (gbh_ref, gbl_ref)
            return _hg_exponents(gh_ref[0, rr, cols], gl_ref[0, rr, cols], cmat_ref, units[u][2])

        def intra(u, dall):
            rr, cols = window(u)
            k_ref = kf_ref if units[u][2] == 0 else kb_ref
            return _hg_intra(sq_ref[0, rr, cols], k_ref[0, rr, cols], dall, role_ref, masks_ref, units[u][2])

        def finish(u, parts):
            _, pair, di = units[u]
            rr, cols = window(u)
            outs = _hg_outputs(*parts, hi_ref[0, rr, cols], st_ref, di, pair)
            for hh, o in enumerate(outs):
                hs = slice((2 * pair + hh) * HG_DK, (2 * pair + hh + 1) * HG_DK)
                if not second:
                    acc_ref[rr, hs] = o
                else:
                    ot = acc_ref[rr, hs] + o
                    y = ot * lax.rsqrt(jnp.mean(ot * ot, axis=-1, keepdims=True) + EPS) * gain_ref[:, hs]
                    o_ref[0, rr, hs] = (y * _silu(hg_ref[0, rr, hs].astype(F32))).astype(BF16)

        n = len(units)
        dalls = [exponents(u) for u in range(n)]
        parts = [intra(u, dalls[u]) for u in range(n)]
        for u in range(n):
            finish(u, parts[u])

    def first_half(i, carry):
        step(i, False)
        return carry

    def second_half(i, carry):
        step(i, True)
        return carry

    n_steps = n_chunks // HG_CHUNKS_PER_STEP
    lax.fori_loop(0, n_steps // 2, first_half, 0)
    lax.fori_loop(n_steps // 2, n_steps, second_half, 0)


def _hgrn(proj, cproj, gain):
    B, S, _ = proj.shape
    L = cproj.shape[1]
    c = HG_CHUNK
    nl = len(HG_LEVELS)
    before = np.tril(np.ones((L, L), np.float32), -1)
    stacked = lambda m: jnp.asarray(np.concatenate([m, m], axis=1), BF16)
    consts = list(_hg_consts()) + [stacked(before.T), stacked(before)]
    seq = lambda j: pl.BlockSpec((1, S, SEG), lambda b: (b, 0, j))
    cseq = lambda j: pl.BlockSpec((1, L, SEG), lambda b: (b, 0, j))
    full2 = lambda n, m: pl.BlockSpec((n, m), lambda b: (0, 0))
    return pl.pallas_call(
        functools.partial(_hgrn_kernel, n_chunks=S // c),
        grid=(B,),
        in_specs=[seq(P_HQ), seq(P_HI), seq(P_KF), seq(P_GFH), seq(P_GFL), seq(P_KB), seq(P_GBH), seq(P_GBL),
                  seq(P_HG), cseq(C_I), cseq(C_KF), cseq(C_GFH), cseq(C_GFL), cseq(C_KB), cseq(C_GBH), cseq(C_GBL),
                  full2(1, HG_WIDTH),
                  pl.BlockSpec(consts[0].shape, lambda b: (0, 0, 0)),
                  pl.BlockSpec((2, nl, c, 2 * HG_DK), lambda b: (0, 0, 0, 0)),
                  pl.BlockSpec((2, nl + 1, c, 2 * c), lambda b: (0, 0, 0, 0)),
                  full2(L, 2 * L), full2(L, 2 * L)],
        out_specs=pl.BlockSpec((1, S, HG_WIDTH), lambda b: (b, 0, 0)),
        out_shape=jax.ShapeDtypeStruct((B, S, HG_WIDTH), BF16),
        scratch_shapes=[pltpu.VMEM((S, HG_WIDTH), F32),
                        pltpu.VMEM((2, HG_HEADS, HG_DK, HG_DK), F32)],
        compiler_params=_cparams(("arbitrary",)),
        name="hgrn",
    )(*([proj] * 9), *([cproj] * 7), gain, *consts)


ROUTE_E_ROW = 8


def _route_t(lt):
    tm = lt.shape[1]
    row8 = lax.broadcasted_iota(jnp.int32, (SUBLANES, tm), 0).astype(F32)
    gl = jnp.where(row8 < N_GROUPS, lt[0:SUBLANES], -jnp.inf)
    gmax = jnp.max(gl, axis=0, keepdims=True)
    g_sel = jnp.min(jnp.where(gl == gmax, row8, float(SUBLANES)), axis=0, keepdims=True)
    g_w = 1.0 / jnp.sum(jnp.exp(gl - gmax), axis=0, keepdims=True)
    row16 = lax.broadcasted_iota(jnp.int32, (N_EXPERTS, tm), 0).astype(F32)
    first = g_sel * EXPERTS_PER_GROUP
    in_grp = (row16 >= first) & (row16 < first + EXPERTS_PER_GROUP)
    e1 = jnp.where(in_grp, lt[ROUTE_E_ROW:ROUTE_E_ROW + N_EXPERTS], -jnp.inf)
    v1 = jnp.max(e1, axis=0, keepdims=True)
    i1 = jnp.min(jnp.where(e1 == v1, row16, float(N_EXPERTS)), axis=0, keepdims=True)
    e2 = jnp.where(row16 == i1, -jnp.inf, e1)
    v2 = jnp.max(e2, axis=0, keepdims=True)
    i2 = jnp.min(jnp.where(e2 == v2, row16, float(N_EXPERTS)), axis=0, keepdims=True)
    t = jnp.exp(v2 - v1)
    w1 = g_w / (1.0 + t)
    w2 = g_w * t / (1.0 + t)
    onehot = jnp.where(row8 == g_sel, 1.0, 0.0)
    gates = jnp.where(row8 == i1 - first, w1, jnp.where(row8 == i2 - first, w2, 0.0))
    return onehot, gates


def _outproj_kernel(x_ref, na_ref, hg_ref, w_ref, ga_ref, g_ref, sc_ref, sh_ref, wrh_ref, wrl_ref, br_ref,
                    x1_ref, hx_ref, oh_ref, cnt_ref):
    tm, d = x_ref.shape
    nt = (((1,), (1,)), ((), ()))
    mix = (jnp.dot(na_ref[...], w_ref[0:NA_WIDTH, :], preferred_element_type=F32)
           + jnp.dot(hg_ref[...], w_ref[NA_WIDTH:, :], preferred_element_type=F32))
    x1 = x_ref[...] + ga_ref[0] * mix
    x1_ref[...] = x1
    h2 = _norm_mod(x1, g_ref[...], sc_ref[0], sh_ref[0])
    h_hi = h2.astype(BF16)
    h_lo = (h2 - h_hi.astype(F32)).astype(BF16)
    hx_ref[:, 0:d] = h_hi
    lt = ((lax.dot_general(wrl_ref[...], h_hi, nt, preferred_element_type=F32)
           + lax.dot_general(wrh_ref[...], h_lo, nt, preferred_element_type=F32))
          + lax.dot_general(wrh_ref[...], h_hi, nt, preferred_element_type=F32)) + br_ref[...]
    onehot_t, gates_t = _route_t(lt)
    pieces_t = [p.astype(F32) for p in _split3(gates_t)]
    pad = jnp.zeros((LANES - (1 + len(pieces_t)) * SUBLANES, tm), F32)
    tok = jnp.concatenate([onehot_t] + pieces_t + [pad], axis=0).T
    lane = lax.broadcasted_iota(jnp.int32, tok.shape, 1)
    onehot = jnp.where(lane < ROUTE_E_ROW, tok, 0.0)
    hx_ref[:, d:d + LANES] = jnp.where(lane >= ROUTE_E_ROW, tok, 0.0).astype(BF16)
    oh_ref[...] = onehot
    cnt_ref[0] = jnp.broadcast_to(jnp.sum(onehot, axis=0, keepdims=True), (SUBLANES, LANES))


def _outproj(x2d, na2d, hg2d, w_out_bf, ga, g, sc, sh, wr_hi, wr_lo, br, S):
    T, D = x2d.shape
    tm = MOE_TM
    per = S // tm
    tok = lambda w: pl.BlockSpec((tm, w), lambda i: (i, 0))
    bat = pl.BlockSpec((1, 1, D), lambda i: (i // per, 0, 0))
    return pl.pallas_call(
        _outproj_kernel,
        grid=(T // tm,),
        in_specs=[tok(D), tok(NA_WIDTH), tok(HG_WIDTH),
                  pl.BlockSpec((NA_WIDTH + HG_WIDTH, D), lambda i: (0, 0)),
                  bat, pl.BlockSpec((1, D), lambda i: (0, 0)), bat, bat,
                  pl.BlockSpec((LANES, D), lambda i: (0, 0)),
                  pl.BlockSpec((LANES, D), lambda i: (0, 0)),
                  pl.BlockSpec((LANES, 1), lambda i: (0, 0))],
        out_specs=[tok(D), tok(MOE_XW), tok(LANES),
                   pl.BlockSpec((1, SUBLANES, LANES), lambda i: (i, 0, 0))],
        out_shape=[jax.ShapeDtypeStruct((T, D), F32), jax.ShapeDtypeStruct((T, MOE_XW), BF16),
                   jax.ShapeDtypeStruct((T, LANES), F32),
                   jax.ShapeDtypeStruct((T // tm, SUBLANES, LANES), F32)],
        compiler_params=_cparams(("arbitrary",)),
        name="outproj",
    )(x2d, na2d, hg2d, w_out_bf, ga, g, sc, sh, wr_hi, wr_lo, br)


MOE_TM = 512
MOE_ALIGN = 16
MOE_LOC = 640
MOE_XW = D_MODEL + LANES
MOE_TE = 512
MOE_BITS = tuple(1 << b for b in range(9, 3, -1))


def _moe_steps(T):
    worst = T + (T // MOE_TM) * N_GROUPS * (MOE_ALIGN - 1)
    return -(-worst // MOE_TE) + N_GROUPS


def _local_positions(oh_ref, lstrict_ref, cnt_ref, k):
    onehot = oh_ref[...]
    ranks = jnp.dot(lstrict_ref[...], onehot.astype(BF16), preferred_element_type=F32)
    lane = lax.broadcasted_iota(jnp.int32, (1, LANES), 1)
    base = jnp.zeros((1, LANES), F32)
    o = jnp.int32(0)
    for g in range(N_GROUPS):
        base = jnp.where(lane == g, o.astype(F32), base)
        o = o + cnt_ref[k * N_GROUPS + g]
    return jnp.sum(onehot * (ranks + base), axis=-1, keepdims=True)


def _sort_matrix(lpos):
    col = lax.broadcasted_iota(jnp.int32, (MOE_TM, MOE_LOC), 1).astype(F32)
    return jnp.where(lpos == col, 1.0, 0.0).astype(BF16)


def _block_copies(n_rows, src_row, dst_row, bits, make_copy, action):
    for bit in bits:
        part = n_rows & (-2 * bit)

        @pl.when((n_rows & bit) != 0)
        def _(part=part, bit=bit):
            action(make_copy(pl.multiple_of(src_row + part, MOE_ALIGN), pl.multiple_of(dst_row + part, MOE_ALIGN), bit))


def _run_copies(off_ref, cnt_ref, k, make_copy, action):
    o = jnp.int32(0)
    for g in range(N_GROUPS):
        c = cnt_ref[k * N_GROUPS + g]
        _block_copies(c, o, off_ref[k * N_GROUPS + g], MOE_BITS, make_copy, action)
        o = o + c


def _dispatch_kernel(off_ref, cnt_ref, tail_ref, hx_ref, oh_ref, lstrict_ref, hs_ref, buf_ref, zero_ref, sem,
                     *, n_tiles, n_steps):
    k = pl.program_id(0)
    slot = k % 2

    def copies(kk, sl, action):
        def make(src_row, dst_row, n):
            return pltpu.make_async_copy(buf_ref.at[sl, pl.ds(src_row, n)], hs_ref.at[pl.ds(dst_row, n)], sem.at[sl])
        _run_copies(off_ref, cnt_ref, kk, make, action)

    @pl.when(k >= 2)
    def _():
        copies(k - 2, slot, lambda cp: cp.wait())

    pt = _sort_matrix(_local_positions(oh_ref, lstrict_ref, cnt_ref, k))
    srt = lax.dot_general(pt, hx_ref[...], (((0,), (0,)), ((), ())), preferred_element_type=F32)
    buf_ref[slot] = srt.astype(BF16)
    copies(k, slot, lambda cp: cp.start())

    @pl.when(k == n_tiles - 1)
    def _():
        zero_ref[...] = jnp.zeros_like(zero_ref)
        n_used = tail_ref[2 * N_GROUPS]

        def zero_copy(src_row, dst_row, n):
            return pltpu.make_async_copy(zero_ref.at[pl.ds(src_row, n)], hs_ref.at[pl.ds(dst_row, n)], sem.at[2])

        def tile_copy(i):
            return zero_copy(0, pl.multiple_of(i * MOE_TE, MOE_TE), MOE_TE)

        def fills(action):
            for g in range(N_GROUPS):
                _block_copies(tail_ref[N_GROUPS + g], jnp.int32(0), tail_ref[g], MOE_BITS[1:], zero_copy, action)

        fills(lambda cp: cp.start())
        lax.fori_loop(n_used, n_steps, lambda i, c: (tile_copy(i).start(), c)[1], 0)
        if n_tiles >= 2:
            copies(k - 1, 1 - slot, lambda cp: cp.wait())
        copies(k, slot, lambda cp: cp.wait())
        fills(lambda cp: cp.wait())
        lax.fori_loop(n_used, n_steps, lambda i, c: (tile_copy(i).wait(), c)[1], 0)


def _dispatch(off, cnt, tail, hx, onehot, lstrict, n_steps):
    T = hx.shape[0]
    n_tiles = T // MOE_TM
    return pl.pallas_call(
        functools.partial(_dispatch_kernel, n_tiles=n_tiles, n_steps=n_steps),
        grid_spec=pltpu.PrefetchScalarGridSpec(
            num_scalar_prefetch=3,
            grid=(n_tiles,),
            in_specs=[pl.BlockSpec((MOE_TM, MOE_XW), lambda k, *_: (k, 0)),
                      pl.BlockSpec((MOE_TM, LANES), lambda k, *_: (k, 0)),
                      pl.BlockSpec((MOE_TM, MOE_TM), lambda k, *_: (0, 0))],
            out_specs=pl.BlockSpec(memory_space=pl.ANY),
            scratch_shapes=[pltpu.VMEM((2, MOE_LOC, MOE_XW), BF16),
                            pltpu.VMEM((MOE_TE, MOE_XW), BF16),
                            pltpu.SemaphoreType.DMA((3,))]),
        out_shape=jax.ShapeDtypeStruct((n_steps * MOE_TE, MOE_XW), BF16),
        compiler_params=_cparams(("arbitrary",)),
        name="dispatch",
    )(off, cnt, tail, hx, onehot, lstrict)


def _experts_kernel(grp_ref, used_ref, hs_ref, w1_ref, w3_ref, w2_ref, ys_ref):
    i = pl.program_id(0)
    d = ys_ref.shape[1]
    ne = EXPERTS_PER_GROUP

    @pl.when(i < used_ref[0])
    def _():
        t = hs_ref[:, 0:d]
        extra = hs_ref[:, d:d + LANES].astype(F32)
        lane = lax.broadcasted_iota(jnp.int32, extra.shape, 1)
        acc = None
        for j in range(ne):
            a = jnp.dot(t, w1_ref[0, j], preferred_element_type=F32)
            b = jnp.dot(t, w3_ref[0, j], preferred_element_type=F32)
            mine = (lane >= ROUTE_E_ROW) & (lane < ROUTE_E_ROW + 3 * SUBLANES) & (lane % SUBLANES == j)
            gj = jnp.sum(jnp.where(mine, extra, 0.0), axis=-1, keepdims=True)
            y = gj * jnp.dot((_silu(a) * b).astype(BF16), w2_ref[0, j], preferred_element_type=F32)
            acc = y if acc is None else acc + y
        ys_ref[...] = acc.astype(BF16)

    @pl.when(i >= used_ref[0])
    def _():
        ys_ref[...] = jnp.zeros_like(ys_ref)


def _experts(grp, used, hs, w1_bf, w3_bf, w2_bf):
    rows, _ = hs.shape
    D = w2_bf.shape[3]
    wspec = lambda w: pl.BlockSpec((1,) + w.shape[1:], lambda i, grp, used: (grp[i], 0, 0, 0))
    return pl.pallas_call(
        _experts_kernel,
        grid_spec=pltpu.PrefetchScalarGridSpec(
            num_scalar_prefetch=2,
            grid=(rows // MOE_TE,),
            in_specs=[pl.BlockSpec((MOE_TE, MOE_XW), lambda i, grp, used: (i, 0)),
                      wspec(w1_bf), wspec(w3_bf), wspec(w2_bf)],
            out_specs=pl.BlockSpec((MOE_TE, D), lambda i, grp, used: (i, 0))),
        out_shape=jax.ShapeDtypeStruct((rows, D), BF16),
        compiler_params=_cparams(("arbitrary",)),
        name="experts",
    )(grp, used, hs, w1_bf, w3_bf, w2_bf)


def _combine_kernel(off_ref, cnt_ref, x1_ref, oh_ref, lstrict_ref, ga_ref, gf_ref, ys_ref, o_ref, buf_ref, sem,
                    *, n_tiles):
    k = pl.program_id(0)
    slot = k % 2

    def copies(kk, sl, action):
        def make(loc_row, seg_row, n):
            return pltpu.make_async_copy(ys_ref.at[pl.ds(seg_row, n)], buf_ref.at[sl, pl.ds(loc_row, n)], sem.at[sl])
        _run_copies(off_ref, cnt_ref, kk, make, action)

    @pl.when(k == 0)
    def _():
        buf_ref[...] = jnp.zeros_like(buf_ref)
        copies(k, slot, lambda cp: cp.start())

    @pl.when(k + 1 < n_tiles)
    def _():
        copies(k + 1, 1 - slot, lambda cp: cp.start())

    pt = _sort_matrix(_local_positions(oh_ref, lstrict_ref, cnt_ref, k))
    copies(k, slot, lambda cp: cp.wait())
    y = jnp.dot(pt, buf_ref[slot], preferred_element_type=F32)
    x2 = x1_ref[...] + ga_ref[0] * y
    o_ref[...] = x2 * lax.rsqrt(jnp.mean(x2 * x2, axis=-1, keepdims=True) + EPS) * gf_ref[...]


def _combine(off, cnt, x1, onehot, lstrict, ga, gfin, ys, S):
    T, D = x1.shape
    n_tiles = T // MOE_TM
    per = S // MOE_TM
    return pl.pallas_call(
        functools.partial(_combine_kernel, n_tiles=n_tiles),
        grid_spec=pltpu.PrefetchScalarGridSpec(
            num_scalar_prefetch=2,
            grid=(n_tiles,),
            in_specs=[pl.BlockSpec((MOE_TM, D), lambda k, off, cnt: (k, 0)),
                      pl.BlockSpec((MOE_TM, LANES), lambda k, off, cnt: (k, 0)),
                      pl.BlockSpec((MOE_TM, MOE_TM), lambda k, off, cnt: (0, 0)),
                      pl.BlockSpec((1, 1, D), lambda k, off, cnt: (k // per, 0, 0)),
                      pl.BlockSpec((1, D), lambda k, off, cnt: (0, 0)),
                      pl.BlockSpec(memory_space=pl.ANY)],
            out_specs=pl.BlockSpec((MOE_TM, D), lambda k, off, cnt: (k, 0)),
            scratch_shapes=[pltpu.VMEM((2, MOE_LOC, D), BF16),
                            pltpu.SemaphoreType.DMA((2,))]),
        out_shape=jax.ShapeDtypeStruct((T, D), F32),
        compiler_params=_cparams(("arbitrary",)),
        name="combine",
    )(off, cnt, x1, onehot, lstrict, ga, gfin, ys)


def _moe_schedule(cnt_tiles, n_steps):
    cnt = ((cnt_tiles + (MOE_ALIGN - 1)) // MOE_ALIGN) * MOE_ALIGN
    ends = jnp.cumsum(cnt, axis=0)
    total = ends[-1]
    ntile = (total + MOE_TE - 1) // MOE_TE
    cum = jnp.cumsum(ntile)
    base = (cum - ntile) * MOE_TE
    off = base[None, :] + ends - cnt
    i = jnp.arange(n_steps, dtype=jnp.int32)
    grp = jnp.minimum(jnp.sum(i[:, None] >= cum[None, :], axis=1), N_GROUPS - 1)
    grp = jnp.where(i < cum[-1], grp, grp[jnp.maximum(cum[-1] - 1, 0)])
    tail = jnp.concatenate([base + total, ntile * MOE_TE - total, cum[-1:]])
    as_i32 = lambda a: a.reshape(-1).astype(jnp.int32)
    return as_i32(off), as_i32(cnt), as_i32(tail), as_i32(grp), as_i32(cum[-1:])


def _rope_tables(S):
    t = jnp.arange(S)
    pos = jnp.stack([t // GRID_W, t % GRID_W], axis=-1).astype(F32)
    inv = ROPE_BASE ** (-jnp.arange(0, ROPE_AXIS_DIM, 2, dtype=F32) / ROPE_AXIS_DIM)
    ang = pos[:, :, None] * inv
    cos, sin = jnp.cos(ang), jnp.sin(ang)
    cos_h = jnp.concatenate([cos, cos], axis=-1).reshape(S, NA_HEAD_DIM)
    sin_h = jnp.concatenate([-sin, sin], axis=-1).reshape(S, NA_HEAD_DIM)
    return jnp.tile(cos_h, (1, NA_HEADS)), jnp.tile(sin_h, (1, NA_HEADS))


def kernel(x, c, ctx, c_ctx, w_mod, b_mod, norm_mix, norm_ffn, w_in, w_out, na_rpb, hg_lb, hg_norm,
           w_grp, b_grp, w_exp, b_exp, w1, w3, w2, norm_final):
    B, S, D = x.shape
    T = B * S
    assert w_mod.shape[0] == 1, "single-layer kernel"

    rows = -(-(B + 1) // SUBLANES) * SUBLANES
    cc = jnp.zeros((rows, D), F32).at[:B].set(c).at[B].set(c_ctx)
    mod = _modulation(cc, w_mod[0], b_mod[0])
    sh_a, sc_a, ga_a, sh_f, sc_f, ga_f = [m.reshape(B, 1, D) for m in jnp.split(mod[:B], 6, axis=-1)]
    csh_a, csc_a = [m.reshape(1, D) for m in jnp.split(mod[B], 6)[:2]]

    w_in_bf = w_in[0].astype(BF16)
    w_ctx_bf = jnp.concatenate([w_in_bf[:, i * SEG:(i + 1) * SEG] for i in (1, 2, 4, 5, 6)], axis=1)
    cos_t, sin_t = _rope_tables(S)
    g_mix = norm_mix[0].reshape(1, D)

    lb = jnp.cumsum(jax.nn.softmax(hg_lb.astype(F32), axis=0), axis=0)[0]
    proj = _project(x, g_mix, sc_a, sh_a, w_in_bf, cos_t, sin_t, lb, tm=512)
    cproj = _project_ctx(ctx, g_mix, csc_a, csh_a, w_ctx_bf, lb)

    na_out = _neighbourhood_attention(proj, cproj, _na_bias_table(na_rpb[0]))

    gain = jnp.tile(hg_norm[0].astype(F32), HG_HEADS).reshape(1, HG_WIDTH)
    hg_out = _hgrn(proj, cproj, gain)

    e_rows = slice(ROUTE_E_ROW, ROUTE_E_ROW + N_EXPERTS)
    wr = jnp.zeros((LANES, D), F32).at[:N_GROUPS].set(w_grp[0].T).at[e_rows].set(w_exp[0].T)
    br = jnp.zeros((LANES, 1), F32).at[:N_GROUPS, 0].set(b_grp[0]).at[e_rows, 0].set(b_exp[0])
    hi_f32 = lax.bitcast_convert_type(lax.bitcast_convert_type(wr, jnp.uint32) & jnp.uint32(0xFFFF0000), F32)
    wr_hi = hi_f32.astype(BF16)
    wr_lo = (wr - hi_f32).astype(BF16)
    x1, hx, onehot, cnt_tiles = _outproj(x.reshape(T, D), na_out.reshape(T, NA_WIDTH), hg_out.reshape(T, HG_WIDTH),
                                         w_out[0].astype(BF16), ga_a, norm_ffn[0].reshape(1, D), sc_f, sh_f,
                                         wr_hi, wr_lo, br, S=S)

    n_steps = _moe_steps(T)
    off, cnt, tail, grp, used = _moe_schedule(cnt_tiles[:, 0, :N_GROUPS].astype(jnp.int32), n_steps)
    lstrict = jnp.asarray(np.tril(np.ones((MOE_TM, MOE_TM), np.float32), -1), BF16)
    by_group = lambda w: w[0].astype(BF16).reshape(N_GROUPS, EXPERTS_PER_GROUP, *w.shape[2:])
    hs = _dispatch(off, cnt, tail, hx, onehot, lstrict, n_steps)
    ys = _experts(grp, used, hs, by_group(w1), by_group(w3), by_group(w2))
    out = _combine(off, cnt, x1, onehot, lstrict, ga_f, norm_final.reshape(1, D), ys, S)
    return out.reshape(B, S, D)
```

```python
import functools

import numpy as np
import jax
import jax.numpy as jnp
from jax import lax
from jax.experimental import pallas as pl
from jax.experimental.pallas import tpu as pltpu

F32 = jnp.float32
BF16 = jnp.bfloat16
HIGHEST = lax.Precision.HIGHEST

D_MODEL = 1024
GRID_W = 64
NA_HEADS = 8
NA_HEAD_DIM = 64
NA_WIDTH = NA_HEADS * NA_HEAD_DIM
NA_KH = 8
NA_KW = 16
ROPE_AXIS_DIM = NA_HEAD_DIM // 2
ROPE_BASE = 10000.0
HG_HEADS = 4
HG_DK = 128
HG_WIDTH = HG_HEADS * HG_DK
HG_CHUNK = 64
SEG = 512
N_GROUPS = 4
EXPERTS_PER_GROUP = 4
N_EXPERTS = N_GROUPS * EXPERTS_PER_GROUP
D_EXPERT = 512
EPS = 1e-6
NEG = -1e30
LOG2E = 1.4426950408889634
LANES = 128
SUBLANES = 8
VMEM_LIMIT = 56 * 1024 * 1024

HG_LEVELS = (32, 16, 8, 4, 2, 1)
HG_CHUNKS_PER_STEP = 4


def _cparams(sem):
    return pltpu.CompilerParams(dimension_semantics=sem, vmem_limit_bytes=VMEM_LIMIT)


def _sigmoid(x):
    return 1.0 / (1.0 + jnp.exp(-x))


def _silu(x):
    return x * _sigmoid(x)


def _mod_kernel(c_ref, w_ref, b_ref, o_ref):
    s = _silu(c_ref[...])
    o_ref[...] = jnp.dot(s, w_ref[...], precision=HIGHEST, preferred_element_type=F32) + b_ref[...]


def _modulation(cc, w_mod, b_mod):
    rows, d = cc.shape
    n = w_mod.shape[1]
    tn = 1024
    return pl.pallas_call(
        _mod_kernel,
        grid=(n // tn,),
        in_specs=[pl.BlockSpec((rows, d), lambda j: (0, 0)),
                  pl.BlockSpec((d, tn), lambda j: (0, j)),
                  pl.BlockSpec((1, tn), lambda j: (0, j))],
        out_specs=pl.BlockSpec((rows, tn), lambda j: (0, j)),
        out_shape=jax.ShapeDtypeStruct((rows, n), F32),
        compiler_params=_cparams(("arbitrary",)),
        name="mod",
    )(cc, w_mod, b_mod.reshape(1, n))


def _norm_mod(x, g, sc, sh):
    y = x * lax.rsqrt(jnp.mean(x * x, axis=-1, keepdims=True) + EPS)
    return (y * g) * (1.0 + sc) + sh


def _rope(a, cos, sin):
    lane = lax.broadcasted_iota(jnp.int32, a.shape, 1)
    first = (lane % ROPE_AXIS_DIM) < (ROPE_AXIS_DIM // 2)
    up = pltpu.roll(a, LANES - ROPE_AXIS_DIM // 2, axis=1)
    dn = pltpu.roll(a, ROPE_AXIS_DIM // 2, axis=1)
    return a * cos + jnp.where(first, up, dn) * sin


def _forget_gate(pre, lb):
    f = lb + (1.0 - lb) * _sigmoid(pre)
    g2 = jnp.log2(f)
    hi = g2.astype(BF16)
    return 1.0 - f, hi, g2 - hi.astype(F32)


P_QRAW, P_QROT, P_KROT, P_V, P_HQ, P_HI, P_KF, P_GFH, P_GFL, P_KB, P_GBH, P_GBL, P_HG = range(13)
P_SEGS = 13


def _proj_kernel(x_ref, g_ref, sc_ref, sh_ref, w_ref, cos_ref, sin_ref, lb_ref, o_ref):
    h = _norm_mod(x_ref[0], g_ref[...], sc_ref[0], sh_ref[0]).astype(BF16)
    scale = NA_HEAD_DIM ** -0.5 * LOG2E

    def put(seg, val):
        o_ref[0, :, seg * SEG:(seg + 1) * SEG] = val.astype(BF16)

    for j in range(8):
        acc = jnp.dot(h, w_ref[:, j * SEG:(j + 1) * SEG], preferred_element_type=F32)
        if j <= 1:
            rot = jnp.concatenate(
                [_rope(acc[:, p * LANES:(p + 1) * LANES], cos_ref[:, p * LANES:(p + 1) * LANES],
                       sin_ref[:, p * LANES:(p + 1) * LANES]) for p in range(SEG // LANES)], axis=1)
            if j == 0:
                put(P_QRAW, acc * scale)
                put(P_QROT, rot * scale)
            else:
                put(P_KROT, rot)
        elif j == 2:
            put(P_V, acc)
        elif j == 3:
            put(P_HQ, _silu(acc))
        elif j == 4:
            put(P_HI, acc)
        elif j in (5, 6):
            base = P_KF if j == 5 else P_KB
            for i, val in enumerate(_forget_gate(acc, lb_ref[j - 5:j - 4, :])):
                put(base + i, val)
        else:
            put(P_HG, acc)


def _project(x, g, sc, sh, w_bf, cos_t, sin_t, lb, tm):
    B, S, D = x.shape
    return pl.pallas_call(
        _proj_kernel,
        grid=(S // tm, B),
        in_specs=[pl.BlockSpec((1, tm, D), lambda s, b: (b, s, 0)),
                  pl.BlockSpec((1, D), lambda s, b: (0, 0)),
                  pl.BlockSpec((1, 1, D), lambda s, b: (b, 0, 0)),
                  pl.BlockSpec((1, 1, D), lambda s, b: (b, 0, 0)),
                  pl.BlockSpec((D, 8 * SEG), lambda s, b: (0, 0)),
                  pl.BlockSpec((tm, SEG), lambda s, b: (s, 0)),
                  pl.BlockSpec((tm, SEG), lambda s, b: (s, 0)),
                  pl.BlockSpec((2, HG_WIDTH), lambda s, b: (0, 0))],
        out_specs=pl.BlockSpec((1, tm, P_SEGS * SEG), lambda s, b: (b, s, 0)),
        out_shape=jax.ShapeDtypeStruct((B, S, P_SEGS * SEG), BF16),
        compiler_params=_cparams(("arbitrary", "arbitrary")),
        name="proj",
    )(x, g, sc, sh, w_bf, cos_t, sin_t, lb)


C_K, C_V, C_I, C_KF, C_GFH, C_GFL, C_KB, C_GBH, C_GBL = range(9)
C_SEGS = 9


CTX_W_SEGS = (1, 2, 4, 5, 6)


def _ctxproj_kernel(x_ref, g_ref, sc_ref, sh_ref, *rest):
    w_refs, (lb_ref, o_ref) = rest[:len(CTX_W_SEGS)], rest[len(CTX_W_SEGS):]
    h = _norm_mod(x_ref[0], g_ref[...], sc_ref[...], sh_ref[...]).astype(BF16)

    def put(seg, val):
        o_ref[0, :, seg * SEG:(seg + 1) * SEG] = val.astype(BF16)

    for j, w_ref in enumerate(w_refs):
        acc = jnp.dot(h, w_ref[...], preferred_element_type=F32)
        if j < 3:
            put(j, acc)
        else:
            for i, val in enumerate(_forget_gate(acc, lb_ref[j - 3:j - 2, :])):
                put(C_KF + 3 * (j - 3) + i, val)


def _project_ctx(ctx, g, sc, sh, w_bf, lb):
    B, L, D = ctx.shape
    return pl.pallas_call(
        _ctxproj_kernel,
        grid=(B,),
        in_specs=[pl.BlockSpec((1, L, D), lambda b: (b, 0, 0)),
                  pl.BlockSpec((1, D), lambda b: (0, 0)),
                  pl.BlockSpec((1, D), lambda b: (0, 0)),
                  pl.BlockSpec((1, D), lambda b: (0, 0))]
                 + [pl.BlockSpec((D, SEG), lambda b, j=j: (0, j)) for j in CTX_W_SEGS]
                 + [pl.BlockSpec((2, HG_WIDTH), lambda b: (0, 0))],
        out_specs=pl.BlockSpec((1, L, C_SEGS * SEG), lambda b: (b, 0, 0)),
        out_shape=jax.ShapeDtypeStruct((B, L, C_SEGS * SEG), BF16),
        compiler_params=_cparams(("arbitrary",)),
        name="ctxproj",
    )(ctx, g, sc, sh, *([w_bf] * len(CTX_W_SEGS)), lb)


NA_ROWS_PER_STEP = 8


def _na_kernel(qraw_ref, qrot_ref, k_ref, v_ref, ck_ref, cv_ref, bias_ref, o_ref, *, rows):
    nk = NA_KH * GRID_W
    lane = lax.broadcasted_iota(jnp.int32, (GRID_W, LANES), 1)
    sel0 = lane < NA_HEAD_DIM
    nt = (((1,), (1,)), ((), ()))
    n_pairs = NA_WIDTH // LANES
    ones = jnp.ones((nk + ck_ref.shape[1], LANES), BF16)
    units = [(j, p) for j in range(NA_ROWS_PER_STEP) for p in range(n_pairs)]

    def window(j):
        r = pl.program_id(1) * NA_ROWS_PER_STEP + j
        rs = jnp.clip(r - NA_KH // 2, 0, rows - NA_KH)
        return pl.multiple_of(rs * GRID_W, GRID_W), pl.multiple_of((rs - r + NA_KH - 1) * GRID_W, GRID_W)

    def query_slots(ref, j, cols):
        q = ref[0, j * GRID_W:(j + 1) * GRID_W, cols]
        zero = jnp.zeros_like(q)
        return jnp.concatenate([jnp.where(sel0, q, zero), jnp.where(sel0, zero, q)], axis=0)

    def scores(j, p):
        cols = slice(p * LANES, (p + 1) * LANES)
        start, bias_row = window(j)
        s_loc = lax.dot_general(k_ref[0, pl.ds(start, nk), cols], query_slots(qrot_ref, j, cols), nt,
                                preferred_element_type=F32) + bias_ref[p, pl.ds(bias_row, nk), :]
        s_ctx = lax.dot_general(ck_ref[0, :, cols], query_slots(qraw_ref, j, cols), nt,
                                preferred_element_type=F32)
        return s_loc, s_ctx

    def softmax(s_loc, s_ctx):
        m = jnp.maximum(jnp.max(s_loc, axis=0, keepdims=True), jnp.max(s_ctx, axis=0, keepdims=True))
        return jnp.concatenate([jnp.exp2(s_loc - m).T, jnp.exp2(s_ctx - m).T], axis=1).astype(BF16)

    def values(j, p, probs):
        cols = slice(p * LANES, (p + 1) * LANES)
        start, _ = window(j)
        vals = jnp.concatenate([v_ref[0, pl.ds(start, nk), cols], cv_ref[0, :, cols]], axis=0)
        oe = jnp.dot(probs, jnp.concatenate([vals, ones], axis=1), preferred_element_type=F32)
        res = oe[:, :LANES] / oe[:, LANES:]
        o_ref[0, j * GRID_W:(j + 1) * GRID_W, cols] = jnp.where(sel0, res[0:GRID_W],
                                                                res[GRID_W:2 * GRID_W]).astype(BF16)

    s_all = [scores(j, p) for j, p in units]
    probs = [softmax(*s) for s in s_all]
    for (j, p), pr in zip(units, probs):
        values(j, p, pr)


def _na_bias_table(rpb):
    qc = np.arange(GRID_W)[None, :]
    kc = np.arange(GRID_W)[:, None]
    cs = np.clip(qc - NA_KW // 2, 0, GRID_W - NA_KW)
    valid = (kc >= cs) & (kc < cs + NA_KW)
    r = rpb.astype(F32) * LOG2E
    period = GRID_W + 2 * NA_KW - 1
    gap = jnp.zeros(r.shape[:-1] + (GRID_W,), F32)
    w = jnp.concatenate([r[..., NA_KW - 1::-1], gap, r[..., :NA_KW - 1:-1]], axis=-1)
    t = jnp.tile(w, (1, 1, GRID_W))[..., :GRID_W * (period - 1)]
    t = t.reshape(r.shape[:-1] + (GRID_W, period - 1))[..., :GRID_W]
    t = jnp.where(jnp.asarray(valid)[None, None], t, NEG)
    n_pairs = NA_HEADS // 2
    n_ro = 2 * NA_KH - 1
    t = t.reshape(n_pairs, 2, n_ro, GRID_W, GRID_W).transpose(0, 2, 3, 1, 4)
    return t.reshape(n_pairs, n_ro * GRID_W, 2 * GRID_W)


def _neighbourhood_attention(proj, cproj, bias_tbl):
    B, S, _ = proj.shape
    L = cproj.shape[1]
    rows = S // GRID_W
    tq = NA_ROWS_PER_STEP * GRID_W
    return pl.pallas_call(
        functools.partial(_na_kernel, rows=rows),
        grid=(B, rows // NA_ROWS_PER_STEP),
        in_specs=[pl.BlockSpec((1, tq, SEG), lambda b, r: (b, r, P_QRAW)),
                  pl.BlockSpec((1, tq, SEG), lambda b, r: (b, r, P_QROT)),
                  pl.BlockSpec((1, S, SEG), lambda b, r: (b, 0, P_KROT)),
                  pl.BlockSpec((1, S, SEG), lambda b, r: (b, 0, P_V)),
                  pl.BlockSpec((1, L, SEG), lambda b, r: (b, 0, C_K)),
                  pl.BlockSpec((1, L, SEG), lambda b, r: (b, 0, C_V)),
                  pl.BlockSpec(bias_tbl.shape, lambda b, r: (0, 0, 0))],
        out_specs=pl.BlockSpec((1, tq, SEG), lambda b, r: (b, r, 0)),
        out_shape=jax.ShapeDtypeStruct((B, S, NA_WIDTH), BF16),
        compiler_params=_cparams(("arbitrary", "arbitrary")),
        name="na",
    )(proj, proj, proj, proj, cproj, cproj, bias_tbl)


def _split3(g):
    g1 = g.astype(BF16)
    r1 = g - g1.astype(F32)
    g2 = r1.astype(BF16)
    g3 = (r1 - g2.astype(F32)).astype(BF16)
    return g1, g2, g3


def _hg_consts():
    c = HG_CHUNK
    t = np.arange(c)[:, None]
    u = np.arange(c)[None, :]
    cm, role, masks = [], [], []
    for fwd in (True, False):
        blocks = [(u <= t) if fwd else (u >= t)]
        roles, ms = [], []
        for m in HG_LEVELS:
            blk = t // (2 * m)
            if fwd:
                mid = 2 * m * blk + m - 1
                is_q = (t % (2 * m)) >= m
                expo = np.where(is_q, (u > mid) & (u <= t), (u > t) & (u <= mid))
            else:
                mid = 2 * m * blk + m
                is_q = (t % (2 * m)) < m
                expo = np.where(is_q, (u >= t) & (u < mid), (u >= mid) & (u < t))
            blocks.append(expo)
            roles.append(np.broadcast_to(is_q, (c, 2 * HG_DK)))
            ms.append((blk == blk.T) & is_q & ~is_q.T)
        ms.append(t == u)
        full = np.concatenate(blocks, axis=0).astype(np.float32)
        cm.append(np.concatenate([full, full], axis=1))
        role.append(np.stack(roles).astype(np.float32))
        masks.append(np.stack([np.concatenate([x, x], axis=1) for x in ms]).astype(np.float32))
    return jnp.asarray(np.stack(cm), BF16), jnp.asarray(np.stack(role), BF16), jnp.asarray(np.stack(masks))


def _block_diag(x, zeros):
    return jnp.concatenate([jnp.concatenate([x[:, :HG_DK], zeros], axis=1),
                            jnp.concatenate([zeros, x[:, HG_DK:]], axis=1)], axis=0)


def _hg_exponents(gh, gl, cmat_ref, di):
    return jnp.dot(cmat_ref[di], jnp.concatenate([gh, gl], axis=0), preferred_element_type=F32)


def _hg_intra(q, k, dall, role_ref, masks_ref, di):
    c = HG_CHUNK
    nl = len(HG_LEVELS)
    nt = (((1,), (1,)), ((), ()))
    b = dall[0:c]
    bend = b[c - 1:c] if di == 0 else b[0:1]
    qe = q * jnp.exp2(b).astype(BF16)
    ke = k * jnp.exp2(bend - b).astype(BF16)
    zeros = jnp.zeros((c, HG_DK), BF16)
    a = None
    for li in range(nl):
        e = jnp.exp2(dall[(li + 1) * c:(li + 2) * c]).astype(BF16)
        x = jnp.where(role_ref[di, li] > 0.5, q, k) * e
        r = lax.dot_general(x, _block_diag(x, zeros), nt, preferred_element_type=F32) * masks_ref[di, li]
        a = r if a is None else a + r
    qk = q.astype(F32) * k.astype(F32)
    lane = lax.broadcasted_iota(jnp.int32, a.shape, 1)
    d0 = jnp.sum(qk[:, :HG_DK], axis=-1, keepdims=True)
    d1 = jnp.sum(qk[:, HG_DK:], axis=-1, keepdims=True)
    a = a + jnp.where(lane < c, d0, d1) * masks_ref[di, nl]
    return a.astype(BF16), qe, ke, bend


def _hg_outputs(a, qe, ke, bend, v, st_ref, di, pair):
    nt = (((1,), (1,)), ((), ()))
    tn = (((0,), (0,)), ((), ()))
    o = jnp.dot(a, _block_diag(v, jnp.zeros((HG_CHUNK, HG_DK), BF16)), preferred_element_type=F32)
    outs = []
    for hh in range(2):
        hs = slice(hh * HG_DK, (hh + 1) * HG_DK)
        st = st_ref[di, 2 * pair + hh]
        outs.append(o[:, hs] + lax.dot_general(qe[:, hs], st.astype(BF16), nt, preferred_element_type=F32))
        upd = lax.dot_general(v[:, hs], ke[:, hs], tn, preferred_element_type=F32)
        st_ref[di, 2 * pair + hh] = jnp.exp2(bend[:, hs]) * st + upd
    return outs


def _hgrn_kernel(sq_ref, hi_ref, kf_ref, gfh_ref, gfl_ref, kb_ref, gbh_ref, gbl_ref, hg_ref,
                 ci_ref, ckf_ref, cgfh_ref, cgfl_ref, ckb_ref, cgbh_ref, cgbl_ref, gain_ref,
                 cmat_ref, role_ref, masks_ref, after_ref, before_ref, o_ref, acc_ref, st_ref, *, n_chunks):
    c = HG_CHUNK
    tn = (((0,), (0,)), ((), ()))

    ctx_dirs = ((ckf_ref, cgfh_ref, cgfl_ref, after_ref), (ckb_ref, cgbh_ref, cgbl_ref, before_ref))
    for di, (k_ref, gh_ref, gl_ref, cm_ref) in enumerate(ctx_dirs):
        decay = jnp.dot(cm_ref[...], jnp.concatenate([gh_ref[0], gl_ref[0]], axis=0), preferred_element_type=F32)
        ke = k_ref[0] * jnp.exp2(decay).astype(BF16)
        for h in range(HG_HEADS):
            hs = slice(h * HG_DK, (h + 1) * HG_DK)
            st_ref[di, h] = lax.dot_general(ci_ref[0, :, hs], ke[:, hs], tn, preferred_element_type=F32)

    def step(i, second):
        units = [(sub, pair, di) for sub in range(HG_CHUNKS_PER_STEP)
                 for pair in range(HG_HEADS // 2) for di in range(2)]

        def window(u):
            sub, pair, di = units[u]
            ci = i * HG_CHUNKS_PER_STEP + sub
            first = ci * c if di == 0 else (n_chunks - 1 - ci) * c
            return pl.ds(pl.multiple_of(first, c), c), slice(2 * pair * HG_DK, 2 * (pair + 1) * HG_DK)

        def exponents(u):
            rr, cols = window(u)
            gh_ref, gl_ref = (gfh_ref, gfl_ref) if units[u][2] == 0 else (gbh_ref, gbl_ref)
            return _hg_exponents(gh_ref[0, rr, cols], gl_ref[0, rr, cols], cmat_ref, units[u][2])

        def intra(u, dall):
            rr, cols = window(u)
            k_ref = kf_ref if units[u][2] == 0 else kb_ref
            return _hg_intra(sq_ref[0, rr, cols], k_ref[0, rr, cols], dall, role_ref, masks_ref, units[u][2])

        def finish(u, parts):
            _, pair, di = units[u]
            rr, cols = window(u)
            outs = _hg_outputs(*parts, hi_ref[0, rr, cols], st_ref, di, pair)
            for hh, o in enumerate(outs):
                hs = slice((2 * pair + hh) * HG_DK, (2 * pair + hh + 1) * HG_DK)
                if not second:
                    acc_ref[rr, hs] = o
                else:
                    ot = acc_ref[rr, hs] + o
                    y = ot * lax.rsqrt(jnp.mean(ot * ot, axis=-1, keepdims=True) + EPS) * gain_ref[:, hs]
                    o_ref[0, rr, hs] = (y * _silu(hg_ref[0, rr, hs].astype(F32))).astype(BF16)

        n = len(units)
        dalls = [exponents(u) for u in range(n)]
        parts = [intra(u, dalls[u]) for u in range(n)]
        for u in range(n):
            finish(u, parts[u])

    def first_half(i, carry):
        step(i, False)
        return carry

    def second_half(i, carry):
        step(i, True)
        return carry

    n_steps = n_chunks // HG_CHUNKS_PER_STEP
    lax.fori_loop(0, n_steps // 2, first_half, 0)
    lax.fori_loop(n_steps // 2, n_steps, second_half, 0)


def _hgrn(proj, cproj, gain):
    B, S, _ = proj.shape
    L = cproj.shape[1]
    c = HG_CHUNK
    nl = len(HG_LEVELS)
    before = np.tril(np.ones((L, L), np.float32), -1)
    stacked = lambda m: jnp.asarray(np.concatenate([m, m], axis=1), BF16)
    consts = list(_hg_consts()) + [stacked(before.T), stacked(before)]
    seq = lambda j: pl.BlockSpec((1, S, SEG), lambda b: (b, 0, j))
    cseq = lambda j: pl.BlockSpec((1, L, SEG), lambda b: (b, 0, j))
    full2 = lambda n, m: pl.BlockSpec((n, m), lambda b: (0, 0))
    return pl.pallas_call(
        functools.partial(_hgrn_kernel, n_chunks=S // c),
        grid=(B,),
        in_specs=[seq(P_HQ), seq(P_HI), seq(P_KF), seq(P_GFH), seq(P_GFL), seq(P_KB), seq(P_GBH), seq(P_GBL),
                  seq(P_HG), cseq(C_I), cseq(C_KF), cseq(C_GFH), cseq(C_GFL), cseq(C_KB), cseq(C_GBH), cseq(C_GBL),
                  full2(1, HG_WIDTH),
                  pl.BlockSpec(consts[0].shape, lambda b: (0, 0, 0)),
                  pl.BlockSpec((2, nl, c, 2 * HG_DK), lambda b: (0, 0, 0, 0)),
                  pl.BlockSpec((2, nl + 1, c, 2 * c), lambda b: (0, 0, 0, 0)),
                  full2(L, 2 * L), full2(L, 2 * L)],
        out_specs=pl.BlockSpec((1, S, HG_WIDTH), lambda b: (b, 0, 0)),
        out_shape=jax.ShapeDtypeStruct((B, S, HG_WIDTH), BF16),
        scratch_shapes=[pltpu.VMEM((S, HG_WIDTH), F32),
                        pltpu.VMEM((2, HG_HEADS, HG_DK, HG_DK), F32)],
        compiler_params=_cparams(("arbitrary",)),
        name="hgrn",
    )(*([proj] * 9), *([cproj] * 7), gain, *consts)


ROUTE_E_ROW = 8


def _route_t(lt):
    tm = lt.shape[1]
    row8 = lax.broadcasted_iota(jnp.int32, (SUBLANES, tm), 0).astype(F32)
    gl = jnp.where(row8 < N_GROUPS, lt[0:SUBLANES], -jnp.inf)
    gmax = jnp.max(gl, axis=0, keepdims=True)
    g_sel = jnp.min(jnp.where(gl == gmax, row8, float(SUBLANES)), axis=0, keepdims=True)
    g_w = 1.0 / jnp.sum(jnp.exp(gl - gmax), axis=0, keepdims=True)
    row16 = lax.broadcasted_iota(jnp.int32, (N_EXPERTS, tm), 0).astype(F32)
    first = g_sel * EXPERTS_PER_GROUP
    in_grp = (row16 >= first) & (row16 < first + EXPERTS_PER_GROUP)
    e1 = jnp.where(in_grp, lt[ROUTE_E_ROW:ROUTE_E_ROW + N_EXPERTS], -jnp.inf)
    v1 = jnp.max(e1, axis=0, keepdims=True)
    i1 = jnp.min(jnp.where(e1 == v1, row16, float(N_EXPERTS)), axis=0, keepdims=True)
    e2 = jnp.where(row16 == i1, -jnp.inf, e1)
    v2 = jnp.max(e2, axis=0, keepdims=True)
    i2 = jnp.min(jnp.where(e2 == v2, row16, float(N_EXPERTS)), axis=0, keepdims=True)
    t = jnp.exp(v2 - v1)
    w1 = g_w / (1.0 + t)
    w2 = g_w * t / (1.0 + t)
    onehot = jnp.where(row8 == g_sel, 1.0, 0.0)
    gates = jnp.where(row8 == i1 - first, w1, jnp.where(row8 == i2 - first, w2, 0.0))
    return onehot, gates


def _outproj_kernel(x_ref, na_ref, hg_ref, w_ref, ga_ref, g_ref, sc_ref, sh_ref, wrh_ref, wrl_ref, br_ref,
                    x1_ref, hx_ref, oh_ref, cnt_ref):
    tm, d = x_ref.shape
    nt = (((1,), (1,)), ((), ()))
    mix = (jnp.dot(na_ref[...], w_ref[0:NA_WIDTH, :], preferred_element_type=F32)
           + jnp.dot(hg_ref[...], w_ref[NA_WIDTH:, :], preferred_element_type=F32))
    x1 = x_ref[...] + ga_ref[0] * mix
    x1_ref[...] = x1
    h2 = _norm_mod(x1, g_ref[...], sc_ref[0], sh_ref[0])
    h_hi = h2.astype(BF16)
    h_lo = (h2 - h_hi.astype(F32)).astype(BF16)
    hx_ref[:, 0:d] = h_hi
    lt = ((lax.dot_general(wrl_ref[...], h_hi, nt, preferred_element_type=F32)
           + lax.dot_general(wrh_ref[...], h_lo, nt, preferred_element_type=F32))
          + lax.dot_general(wrh_ref[...], h_hi, nt, preferred_element_type=F32)) + br_ref[...]
    onehot_t, gates_t = _route_t(lt)
    pieces_t = [p.astype(F32) for p in _split3(gates_t)]
    pad = jnp.zeros((LANES - (1 + len(pieces_t)) * SUBLANES, tm), F32)
    tok = jnp.concatenate([onehot_t] + pieces_t + [pad], axis=0).T
    lane = lax.broadcasted_iota(jnp.int32, tok.shape, 1)
    onehot = jnp.where(lane < ROUTE_E_ROW, tok, 0.0)
    hx_ref[:, d:d + LANES] = jnp.where(lane >= ROUTE_E_ROW, tok, 0.0).astype(BF16)
    oh_ref[...] = onehot
    cnt_ref[0] = jnp.broadcast_to(jnp.sum(onehot, axis=0, keepdims=True), (SUBLANES, LANES))


def _outproj(x2d, na2d, hg2d, w_out_bf, ga, g, sc, sh, wr_hi, wr_lo, br, S):
    T, D = x2d.shape
    tm = MOE_TM
    per = S // tm
    tok = lambda w: pl.BlockSpec((tm, w), lambda i: (i, 0))
    bat = pl.BlockSpec((1, 1, D), lambda i: (i // per, 0, 0))
    return pl.pallas_call(
        _outproj_kernel,
        grid=(T // tm,),
        in_specs=[tok(D), tok(NA_WIDTH), tok(HG_WIDTH),
                  pl.BlockSpec((NA_WIDTH + HG_WIDTH, D), lambda i: (0, 0)),
                  bat, pl.BlockSpec((1, D), lambda i: (0, 0)), bat, bat,
                  pl.BlockSpec((LANES, D), lambda i: (0, 0)),
                  pl.BlockSpec((LANES, D), lambda i: (0, 0)),
                  pl.BlockSpec((LANES, 1), lambda i: (0, 0))],
        out_specs=[tok(D), tok(MOE_XW), tok(LANES),
                   pl.BlockSpec((1, SUBLANES, LANES), lambda i: (i, 0, 0))],
        out_shape=[jax.ShapeDtypeStruct((T, D), F32), jax.ShapeDtypeStruct((T, MOE_XW), BF16),
                   jax.ShapeDtypeStruct((T, LANES), F32),
                   jax.ShapeDtypeStruct((T // tm, SUBLANES, LANES), F32)],
        compiler_params=_cparams(("arbitrary",)),
        name="outproj",
    )(x2d, na2d, hg2d, w_out_bf, ga, g, sc, sh, wr_hi, wr_lo, br)


MOE_TM = 512
MOE_ALIGN = 16
MOE_LOC = 640
MOE_XW = D_MODEL + LANES
MOE_TE = 512
MOE_BITS = tuple(1 << b for b in range(9, 3, -1))


def _moe_steps(T):
    worst = T + (T // MOE_TM) * N_GROUPS * (MOE_ALIGN - 1)
    return -(-worst // MOE_TE) + N_GROUPS


def _local_positions(oh_ref, lstrict_ref, cnt_ref, k):
    onehot = oh_ref[...]
    ranks = jnp.dot(lstrict_ref[...], onehot.astype(BF16), preferred_element_type=F32)
    lane = lax.broadcasted_iota(jnp.int32, (1, LANES), 1)
    base = jnp.zeros((1, LANES), F32)
    o = jnp.int32(0)
    for g in range(N_GROUPS):
        base = jnp.where(lane == g, o.astype(F32), base)
        o = o + cnt_ref[k * N_GROUPS + g]
    return jnp.sum(onehot * (ranks + base), axis=-1, keepdims=True)


def _sort_matrix(lpos):
    col = lax.broadcasted_iota(jnp.int32, (MOE_TM, MOE_LOC), 1).astype(F32)
    return jnp.where(lpos == col, 1.0, 0.0).astype(BF16)


def _block_copies(n_rows, src_row, dst_row, bits, make_copy, action):
    for bit in bits:
        part = n_rows & (-2 * bit)

        @pl.when((n_rows & bit) != 0)
        def _(part=part, bit=bit):
            action(make_copy(pl.multiple_of(src_row + part, MOE_ALIGN), pl.multiple_of(dst_row + part, MOE_ALIGN), bit))


def _run_copies(off_ref, cnt_ref, k, make_copy, action):
    o = jnp.int32(0)
    for g in range(N_GROUPS):
        c = cnt_ref[k * N_GROUPS + g]
        _block_copies(c, o, off_ref[k * N_GROUPS + g], MOE_BITS, make_copy, action)
        o = o + c


def _dispatch_kernel(off_ref, cnt_ref, tail_ref, hx_ref, oh_ref, lstrict_ref, hs_ref, buf_ref, zero_ref, sem,
                     *, n_tiles, n_steps):
    k = pl.program_id(0)
    slot = k % 2

    def copies(kk, sl, action):
        def make(src_row, dst_row, n):
            return pltpu.make_async_copy(buf_ref.at[sl, pl.ds(src_row, n)], hs_ref.at[pl.ds(dst_row, n)], sem.at[sl])
        _run_copies(off_ref, cnt_ref, kk, make, action)

    @pl.when(k >= 2)
    def _():
        copies(k - 2, slot, lambda cp: cp.wait())

    pt = _sort_matrix(_local_positions(oh_ref, lstrict_ref, cnt_ref, k))
    srt = lax.dot_general(pt, hx_ref[...], (((0,), (0,)), ((), ())), preferred_element_type=F32)
    buf_ref[slot] = srt.astype(BF16)
    copies(k, slot, lambda cp: cp.start())

    @pl.when(k == n_tiles - 1)
    def _():
        zero_ref[...] = jnp.zeros_like(zero_ref)
        n_used = tail_ref[2 * N_GROUPS]

        def zero_copy(src_row, dst_row, n):
            return pltpu.make_async_copy(zero_ref.at[pl.ds(src_row, n)], hs_ref.at[pl.ds(dst_row, n)], sem.at[2])

        def tile_copy(i):
            return zero_copy(0, pl.multiple_of(i * MOE_TE, MOE_TE), MOE_TE)

        def fills(action):
            for g in range(N_GROUPS):
                _block_copies(tail_ref[N_GROUPS + g], jnp.int32(0), tail_ref[g], MOE_BITS[1:], zero_copy, action)

        fills(lambda cp: cp.start())
        lax.fori_loop(n_used, n_steps, lambda i, c: (tile_copy(i).start(), c)[1], 0)
        if n_tiles >= 2:
            copies(k - 1, 1 - slot, lambda cp: cp.wait())
        copies(k, slot, lambda cp: cp.wait())
        fills(lambda cp: cp.wait())
        lax.fori_loop(n_used, n_steps, lambda i, c: (tile_copy(i).wait(), c)[1], 0)


def _dispatch(off, cnt, tail, hx, onehot, lstrict, n_steps):
    T = hx.shape[0]
    n_tiles = T // MOE_TM
    return pl.pallas_call(
        functools.partial(_dispatch_kernel, n_tiles=n_tiles, n_steps=n_steps),
        grid_spec=pltpu.PrefetchScalarGridSpec(
            num_scalar_prefetch=3,
            grid=(n_tiles,),
            in_specs=[pl.BlockSpec((MOE_TM, MOE_XW), lambda k, *_: (k, 0)),
                      pl.BlockSpec((MOE_TM, LANES), lambda k, *_: (k, 0)),
                      pl.BlockSpec((MOE_TM, MOE_TM), lambda k, *_: (0, 0))],
            out_specs=pl.BlockSpec(memory_space=pl.ANY),
            scratch_shapes=[pltpu.VMEM((2, MOE_LOC, MOE_XW), BF16),
                            pltpu.VMEM((MOE_TE, MOE_XW), BF16),
                            pltpu.SemaphoreType.DMA((3,))]),
        out_shape=jax.ShapeDtypeStruct((n_steps * MOE_TE, MOE_XW), BF16),
        compiler_params=_cparams(("arbitrary",)),
        name="dispatch",
    )(off, cnt, tail, hx, onehot, lstrict)


def _experts_kernel(grp_ref, used_ref, hs_ref, w1_ref, w3_ref, w2_ref, ys_ref):
    i = pl.program_id(0)
    d = ys_ref.shape[1]
    ne = EXPERTS_PER_GROUP

    @pl.when(i < used_ref[0])
    def _():
        t = hs_ref[:, 0:d]
        extra = hs_ref[:, d:d + LANES].astype(F32)
        lane = lax.broadcasted_iota(jnp.int32, extra.shape, 1)
        acc = None
        for j in range(ne):
            a = jnp.dot(t, w1_ref[0, j], preferred_element_type=F32)
            b = jnp.dot(t, w3_ref[0, j], preferred_element_type=F32)
            mine = (lane >= ROUTE_E_ROW) & (lane < ROUTE_E_ROW + 3 * SUBLANES) & (lane % SUBLANES == j)
            gj = jnp.sum(jnp.where(mine, extra, 0.0), axis=-1, keepdims=True)
            y = gj * jnp.dot((_silu(a) * b).astype(BF16), w2_ref[0, j], preferred_element_type=F32)
            acc = y if acc is None else acc + y
        ys_ref[...] = acc.astype(BF16)

    @pl.when(i >= used_ref[0])
    def _():
        ys_ref[...] = jnp.zeros_like(ys_ref)


def _experts(grp, used, hs, w1_bf, w3_bf, w2_bf):
    rows, _ = hs.shape
    D = w2_bf.shape[3]
    wspec = lambda w: pl.BlockSpec((1,) + w.shape[1:], lambda i, grp, used: (grp[i], 0, 0, 0))
    return pl.pallas_call(
        _experts_kernel,
        grid_spec=pltpu.PrefetchScalarGridSpec(
            num_scalar_prefetch=2,
            grid=(rows // MOE_TE,),
            in_specs=[pl.BlockSpec((MOE_TE, MOE_XW), lambda i, grp, used: (i, 0)),
                      wspec(w1_bf), wspec(w3_bf), wspec(w2_bf)],
            out_specs=pl.BlockSpec((MOE_TE, D), lambda i, grp, used: (i, 0))),
        out_shape=jax.ShapeDtypeStruct((rows, D), BF16),
        compiler_params=_cparams(("arbitrary",)),
        name="experts",
    )(grp, used, hs, w1_bf, w3_bf, w2_bf)


def _combine_kernel(off_ref, cnt_ref, x1_ref, oh_ref, lstrict_ref, ga_ref, gf_ref, ys_ref, o_ref, buf_ref, sem,
                    *, n_tiles):
    k = pl.program_id(0)
    slot = k % 2

    def copies(kk, sl, action):
        def make(loc_row, seg_row, n):
            return pltpu.make_async_copy(ys_ref.at[pl.ds(seg_row, n)], buf_ref.at[sl, pl.ds(loc_row, n)], sem.at[sl])
        _run_copies(off_ref, cnt_ref, kk, make, action)

    @pl.when(k == 0)
    def _():
        buf_ref[...] = jnp.zeros_like(buf_ref)
        copies(k, slot, lambda cp: cp.start())

    @pl.when(k + 1 < n_tiles)
    def _():
        copies(k + 1, 1 - slot, lambda cp: cp.start())

    pt = _sort_matrix(_local_positions(oh_ref, lstrict_ref, cnt_ref, k))
    copies(k, slot, lambda cp: cp.wait())
    y = jnp.dot(pt, buf_ref[slot], preferred_element_type=F32)
    x2 = x1_ref[...] + ga_ref[0] * y
    o_ref[...] = x2 * lax.rsqrt(jnp.mean(x2 * x2, axis=-1, keepdims=True) + EPS) * gf_ref[...]


def _combine(off, cnt, x1, onehot, lstrict, ga, gfin, ys, S):
    T, D = x1.shape
    n_tiles = T // MOE_TM
    per = S // MOE_TM
    return pl.pallas_call(
        functools.partial(_combine_kernel, n_tiles=n_tiles),
        grid_spec=pltpu.PrefetchScalarGridSpec(
            num_scalar_prefetch=2,
            grid=(n_tiles,),
            in_specs=[pl.BlockSpec((MOE_TM, D), lambda k, off, cnt: (k, 0)),
                      pl.BlockSpec((MOE_TM, LANES), lambda k, off, cnt: (k, 0)),
                      pl.BlockSpec((MOE_TM, MOE_TM), lambda k, off, cnt: (0, 0)),
                      pl.BlockSpec((1, 1, D), lambda k, off, cnt: (k // per, 0, 0)),
                      pl.BlockSpec((1, D), lambda k, off, cnt: (0, 0)),
                      pl.BlockSpec(memory_space=pl.ANY)],
            out_specs=pl.BlockSpec((MOE_TM, D), lambda k, off, cnt: (k, 0)),
            scratch_shapes=[pltpu.VMEM((2, MOE_LOC, D), BF16),
                            pltpu.SemaphoreType.DMA((2,))]),
        out_shape=jax.ShapeDtypeStruct((T, D), F32),
        compiler_params=_cparams(("arbitrary",)),
        name="combine",
    )(off, cnt, x1, onehot, lstrict, ga, gfin, ys)


def _moe_schedule(cnt_tiles, n_steps):
    cnt = ((cnt_tiles + (MOE_ALIGN - 1)) // MOE_ALIGN) * MOE_ALIGN
    ends = jnp.cumsum(cnt, axis=0)
    total = ends[-1]
    ntile = (total + MOE_TE - 1) // MOE_TE
    cum = jnp.cumsum(ntile)
    base = (cum - ntile) * MOE_TE
    off = base[None, :] + ends - cnt
    i = jnp.arange(n_steps, dtype=jnp.int32)
    grp = jnp.minimum(jnp.sum(i[:, None] >= cum[None, :], axis=1), N_GROUPS - 1)
    grp = jnp.where(i < cum[-1], grp, grp[jnp.maximum(cum[-1] - 1, 0)])
    tail = jnp.concatenate([base + total, ntile * MOE_TE - total, cum[-1:]])
    as_i32 = lambda a: a.reshape(-1).astype(jnp.int32)
    return as_i32(off), as_i32(cnt), as_i32(tail), as_i32(grp), as_i32(cum[-1:])


def _rope_tables(S):
    t = jnp.arange(S)
    pos = jnp.stack([t // GRID_W, t % GRID_W], axis=-1).astype(F32)
    inv = ROPE_BASE ** (-jnp.arange(0, ROPE_AXIS_DIM, 2, dtype=F32) / ROPE_AXIS_DIM)
    ang = pos[:, :, None] * inv
    cos, sin = jnp.cos(ang), jnp.sin(ang)
    cos_h = jnp.concatenate([cos, cos], axis=-1).reshape(S, NA_HEAD_DIM)
    sin_h = jnp.concatenate([-sin, sin], axis=-1).reshape(S, NA_HEAD_DIM)
    return jnp.tile(cos_h, (1, NA_HEADS)), jnp.tile(sin_h, (1, NA_HEADS))


def kernel(x, c, ctx, c_ctx, w_mod, b_mod, norm_mix, norm_ffn, w_in, w_out, na_rpb, hg_lb, hg_norm,
           w_grp, b_grp, w_exp, b_exp, w1, w3, w2, norm_final):
    B, S, D = x.shape
    T = B * S
    assert w_mod.shape[0] == 1, "single-layer kernel"

    rows = -(-(B + 1) // SUBLANES) * SUBLANES
    cc = jnp.zeros((rows, D), F32).at[:B].set(c).at[B].set(c_ctx)
    mod = _modulation(cc, w_mod[0], b_mod[0])
    sh_a, sc_a, ga_a, sh_f, sc_f, ga_f = [m.reshape(B, 1, D) for m in jnp.split(mod[:B], 6, axis=-1)]
    csh_a, csc_a = [m.reshape(1, D) for m in jnp.split(mod[B], 6)[:2]]

    w_in_bf = w_in[0].astype(BF16)
    cos_t, sin_t = _rope_tables(S)
    g_mix = norm_mix[0].reshape(1, D)

    lb = jnp.cumsum(jax.nn.softmax(hg_lb.astype(F32), axis=0), axis=0)[0]
    proj = _project(x, g_mix, sc_a, sh_a, w_in_bf, cos_t, sin_t, lb, tm=512)
    cproj = _project_ctx(ctx, g_mix, csc_a, csh_a, w_in_bf, lb)

    na_out = _neighbourhood_attention(proj, cproj, _na_bias_table(na_rpb[0]))

    gain = jnp.tile(hg_norm[0].astype(F32), HG_HEADS).reshape(1, HG_WIDTH)
    hg_out = _hgrn(proj, cproj, gain)

    e_rows = slice(ROUTE_E_ROW, ROUTE_E_ROW + N_EXPERTS)
    wr = jnp.zeros((LANES, D), F32).at[:N_GROUPS].set(w_grp[0].T).at[e_rows].set(w_exp[0].T)
    br = jnp.zeros((LANES, 1), F32).at[:N_GROUPS, 0].set(b_grp[0]).at[e_rows, 0].set(b_exp[0])
    hi_f32 = lax.bitcast_convert_type(lax.bitcast_convert_type(wr, jnp.uint32) & jnp.uint32(0xFFFF0000), F32)
    wr_hi = hi_f32.astype(BF16)
    wr_lo = (wr - hi_f32).astype(BF16)
    x1, hx, onehot, cnt_tiles = _outproj(x.reshape(T, D), na_out.reshape(T, NA_WIDTH), hg_out.reshape(T, HG_WIDTH),
                                         w_out[0].astype(BF16), ga_a, norm_ffn[0].reshape(1, D), sc_f, sh_f,
                                         wr_hi, wr_lo, br, S=S)

    n_steps = _moe_steps(T)
    off, cnt, tail, grp, used = _moe_schedule(cnt_tiles[:, 0, :N_GROUPS].astype(jnp.int32), n_steps)
    lstrict = jnp.asarray(np.tril(np.ones((MOE_TM, MOE_TM), np.float32), -1), BF16)
    by_group = lambda w: w[0].astype(BF16).reshape(N_GROUPS, EXPERTS_PER_GROUP, *w.shape[2:])
    hs = _dispatch(off, cnt, tail, hx, onehot, lstrict, n_steps)
    ys = _experts(grp, used, hs, by_group(w1), by_group(w3), by_group(w2))
    out = _combine(off, cnt, x1, onehot, lstrict, ga_f, norm_final.reshape(1, D), ys, S)
    return out.reshape(B, S, D)
```

```python
import functools

import numpy as np
import jax
import jax.numpy as jnp
from jax import lax
from jax.experimental import pallas as pl
from jax.experimental.pallas import tpu as pltpu

F32 = jnp.float32
BF16 = jnp.bfloat16
HIGHEST = lax.Precision.HIGHEST

D_MODEL = 1024
GRID_W = 64
NA_HEADS = 8
NA_HEAD_DIM = 64
NA_WIDTH = NA_HEADS * NA_HEAD_DIM
NA_KH = 8
NA_KW = 16
ROPE_AXIS_DIM = NA_HEAD_DIM // 2
ROPE_BASE = 10000.0
HG_HEADS = 4
HG_DK = 128
HG_WIDTH = HG_HEADS * HG_DK
HG_CHUNK = 64
SEG = 512
N_GROUPS = 4
EXPERTS_PER_GROUP = 4
N_EXPERTS = N_GROUPS * EXPERTS_PER_GROUP
D_EXPERT = 512
EPS = 1e-6
NEG = -1e30
LOG2E = 1.4426950408889634
LANES = 128
SUBLANES = 8
VMEM_LIMIT = 56 * 1024 * 1024

HG_LEVELS = (32, 16, 8, 4, 2, 1)
HG_CHUNKS_PER_STEP = 4


def _cparams(sem):
    return pltpu.CompilerParams(dimension_semantics=sem, vmem_limit_bytes=VMEM_LIMIT)


def _sigmoid(x):
    return 1.0 / (1.0 + jnp.exp(-x))


def _silu(x):
    return x * _sigmoid(x)


def _mod_kernel(c_ref, w_ref, b_ref, o_ref):
    s = _silu(c_ref[...])
    o_ref[...] = jnp.dot(s, w_ref[...], precision=HIGHEST, preferred_element_type=F32) + b_ref[...]


def _modulation(cc, w_mod, b_mod):
    rows, d = cc.shape
    n = w_mod.shape[1]
    tn = 1024
    return pl.pallas_call(
        _mod_kernel,
        grid=(n // tn,),
        in_specs=[pl.BlockSpec((rows, d), lambda j: (0, 0)),
                  pl.BlockSpec((d, tn), lambda j: (0, j)),
                  pl.BlockSpec((1, tn), lambda j: (0, j))],
        out_specs=pl.BlockSpec((rows, tn), lambda j: (0, j)),
        out_shape=jax.ShapeDtypeStruct((rows, n), F32),
        compiler_params=_cparams(("arbitrary",)),
        name="mod",
    )(cc, w_mod, b_mod.reshape(1, n))


def _norm_mod(x, g, sc, sh):
    y = x * lax.rsqrt(jnp.mean(x * x, axis=-1, keepdims=True) + EPS)
    return (y * g) * (1.0 + sc) + sh


def _rope(a, cos, sin):
    lane = lax.broadcasted_iota(jnp.int32, a.shape, 1)
    first = (lane % ROPE_AXIS_DIM) < (ROPE_AXIS_DIM // 2)
    up = pltpu.roll(a, LANES - ROPE_AXIS_DIM // 2, axis=1)
    dn = pltpu.roll(a, ROPE_AXIS_DIM // 2, axis=1)
    return a * cos + jnp.where(first, up, dn) * sin


def _forget_gate(pre, lb):
    f = lb + (1.0 - lb) * _sigmoid(pre)
    g2 = jnp.log2(f)
    hi = g2.astype(BF16)
    return 1.0 - f, hi, g2 - hi.astype(F32)


P_QRAW, P_QROT, P_KROT, P_V, P_HQ, P_HI, P_KF, P_GFH, P_GFL, P_KB, P_GBH, P_GBL, P_HG = range(13)
P_SEGS = 13


def _proj_kernel(x_ref, g_ref, sc_ref, sh_ref, w_ref, cos_ref, sin_ref, lb_ref, o_ref):
    h = _norm_mod(x_ref[0], g_ref[...], sc_ref[0], sh_ref[0]).astype(BF16)
    scale = NA_HEAD_DIM ** -0.5 * LOG2E

    def put(seg, val):
        o_ref[0, :, seg * SEG:(seg + 1) * SEG] = val.astype(BF16)

    for j in range(8):
        acc = jnp.dot(h, w_ref[:, j * SEG:(j + 1) * SEG], preferred_element_type=F32)
        if j <= 1:
            rot = jnp.concatenate(
                [_rope(acc[:, p * LANES:(p + 1) * LANES], cos_ref[:, p * LANES:(p + 1) * LANES],
                       sin_ref[:, p * LANES:(p + 1) * LANES]) for p in range(SEG // LANES)], axis=1)
            if j == 0:
                put(P_QRAW, acc * scale)
                put(P_QROT, rot * scale)
            else:
                put(P_KROT, rot)
        elif j == 2:
            put(P_V, acc)
        elif j == 3:
            put(P_HQ, _silu(acc))
        elif j == 4:
            put(P_HI, acc)
        elif j in (5, 6):
            base = P_KF if j == 5 else P_KB
            for i, val in enumerate(_forget_gate(acc, lb_ref[j - 5:j - 4, :])):
                put(base + i, val)
        else:
            put(P_HG, acc)


def _project(x, g, sc, sh, w_bf, cos_t, sin_t, lb, tm):
    B, S, D = x.shape
    return pl.pallas_call(
        _proj_kernel,
        grid=(S // tm, B),
        in_specs=[pl.BlockSpec((1, tm, D), lambda s, b: (b, s, 0)),
                  pl.BlockSpec((1, D), lambda s, b: (0, 0)),
                  pl.BlockSpec((1, 1, D), lambda s, b: (b, 0, 0)),
                  pl.BlockSpec((1, 1, D), lambda s, b: (b, 0, 0)),
                  pl.BlockSpec((D, 8 * SEG), lambda s, b: (0, 0)),
                  pl.BlockSpec((tm, SEG), lambda s, b: (s, 0)),
                  pl.BlockSpec((tm, SEG), lambda s, b: (s, 0)),
                  pl.BlockSpec((2, HG_WIDTH), lambda s, b: (0, 0))],
        out_specs=pl.BlockSpec((1, tm, P_SEGS * SEG), lambda s, b: (b, s, 0)),
        out_shape=jax.ShapeDtypeStruct((B, S, P_SEGS * SEG), BF16),
        compiler_params=_cparams(("arbitrary", "arbitrary")),
        name="proj",
    )(x, g, sc, sh, w_bf, cos_t, sin_t, lb)


C_K, C_V, C_I, C_KF, C_GFH, C_GFL, C_KB, C_GBH, C_GBL = range(9)
C_SEGS = 9


CTX_W_SEGS = (1, 2, 4, 5, 6)


def _ctxproj_kernel(x_ref, g_ref, sc_ref, sh_ref, *rest):
    w_refs, (lb_ref, o_ref) = rest[:len(CTX_W_SEGS)], rest[len(CTX_W_SEGS):]
    h = _norm_mod(x_ref[0], g_ref[...], sc_ref[...], sh_ref[...]).astype(BF16)

    def put(seg, val):
        o_ref[0, :, seg * SEG:(seg + 1) * SEG] = val.astype(BF16)

    for j, w_ref in enumerate(w_refs):
        acc = jnp.dot(h, w_ref[...], preferred_element_type=F32)
        if j < 3:
            put(j, acc)
        else:
            for i, val in enumerate(_forget_gate(acc, lb_ref[j - 3:j - 2, :])):
                put(C_KF + 3 * (j - 3) + i, val)


def _project_ctx(ctx, g, sc, sh, w_bf, lb):
    B, L, D = ctx.shape
    return pl.pallas_call(
        _ctxproj_kernel,
        grid=(B,),
        in_specs=[pl.BlockSpec((1, L, D), lambda b: (b, 0, 0)),
                  pl.BlockSpec((1, D), lambda b: (0, 0)),
                  pl.BlockSpec((1, D), lambda b: (0, 0)),
                  pl.BlockSpec((1, D), lambda b: (0, 0))]
                 + [pl.BlockSpec((D, SEG), lambda b, j=j: (0, j)) for j in CTX_W_SEGS]
                 + [pl.BlockSpec((2, HG_WIDTH), lambda b: (0, 0))],
        out_specs=pl.BlockSpec((1, L, C_SEGS * SEG), lambda b: (b, 0, 0)),
        out_shape=jax.ShapeDtypeStruct((B, L, C_SEGS * SEG), BF16),
        compiler_params=_cparams(("arbitrary",)),
        name="ctxproj",
    )(ctx, g, sc, sh, *([w_bf] * len(CTX_W_SEGS)), lb)


NA_ROWS_PER_STEP = 8


def _na_kernel(qraw_ref, qrot_ref, k_ref, v_ref, ck_ref, cv_ref, bias_ref, o_ref, *, rows):
    nk = NA_KH * GRID_W
    lane = lax.broadcasted_iota(jnp.int32, (GRID_W, LANES), 1)
    sel0 = lane < NA_HEAD_DIM
    nt = (((1,), (1,)), ((), ()))
    n_pairs = NA_WIDTH // LANES
    ones = jnp.ones((nk + ck_ref.shape[1], LANES), BF16)
    units = [(j, p) for j in range(NA_ROWS_PER_STEP) for p in range(n_pairs)]

    def window(j):
        r = pl.program_id(1) * NA_ROWS_PER_STEP + j
        rs = jnp.clip(r - NA_KH // 2, 0, rows - NA_KH)
        return pl.multiple_of(rs * GRID_W, GRID_W), pl.multiple_of((rs - r + NA_KH - 1) * GRID_W, GRID_W)

    def query_slots(ref, j, cols):
        q = ref[0, j * GRID_W:(j + 1) * GRID_W, cols]
        zero = jnp.zeros_like(q)
        return jnp.concatenate([jnp.where(sel0, q, zero), jnp.where(sel0, zero, q)], axis=0)

    def scores(j, p):
        cols = slice(p * LANES, (p + 1) * LANES)
        start, bias_row = window(j)
        s_loc = lax.dot_general(k_ref[0, pl.ds(start, nk), cols], query_slots(qrot_ref, j, cols), nt,
                                preferred_element_type=F32) + bias_ref[p, pl.ds(bias_row, nk), :]
        s_ctx = lax.dot_general(ck_ref[0, :, cols], query_slots(qraw_ref, j, cols), nt,
                                preferred_element_type=F32)
        return s_loc, s_ctx

    def softmax(s_loc, s_ctx):
        m = jnp.maximum(jnp.max(s_loc, axis=0, keepdims=True), jnp.max(s_ctx, axis=0, keepdims=True))
        return jnp.concatenate([jnp.exp2(s_loc - m).T, jnp.exp2(s_ctx - m).T], axis=1).astype(BF16)

    def values(j, p, probs):
        cols = slice(p * LANES, (p + 1) * LANES)
        start, _ = window(j)
        vals = jnp.concatenate([v_ref[0, pl.ds(start, nk), cols], cv_ref[0, :, cols]], axis=0)
        oe = jnp.dot(probs, jnp.concatenate([vals, ones], axis=1), preferred_element_type=F32)
        res = oe[:, :LANES] / oe[:, LANES:]
        o_ref[0, j * GRID_W:(j + 1) * GRID_W, cols] = jnp.where(sel0, res[0:GRID_W],
                                                                res[GRID_W:2 * GRID_W]).astype(BF16)

    s_all = [scores(j, p) for j, p in units]
    probs = [softmax(*s) for s in s_all]
    for (j, p), pr in zip(units, probs):
        values(j, p, pr)


def _na_bias_table(rpb):
    qc = np.arange(GRID_W)[None, :]
    kc = np.arange(GRID_W)[:, None]
    cs = np.clip(qc - NA_KW // 2, 0, GRID_W - NA_KW)
    valid = (kc >= cs) & (kc < cs + NA_KW)
    r = rpb.astype(F32) * LOG2E
    period = GRID_W + 2 * NA_KW - 1
    gap = jnp.zeros(r.shape[:-1] + (GRID_W,), F32)
    w = jnp.concatenate([r[..., NA_KW - 1::-1], gap, r[..., :NA_KW - 1:-1]], axis=-1)
    t = jnp.tile(w, (1, 1, GRID_W))[..., :GRID_W * (period - 1)]
    t = t.reshape(r.shape[:-1] + (GRID_W, period - 1))[..., :GRID_W]
    t = jnp.where(jnp.asarray(valid)[None, None], t, NEG)
    n_pairs = NA_HEADS // 2
    n_ro = 2 * NA_KH - 1
    t = t.reshape(n_pairs, 2, n_ro, GRID_W, GRID_W).transpose(0, 2, 3, 1, 4)
    return t.reshape(n_pairs, n_ro * GRID_W, 2 * GRID_W)


def _neighbourhood_attention(proj, cproj, bias_tbl):
    B, S, _ = proj.shape
    L = cproj.shape[1]
    rows = S // GRID_W
    tq = NA_ROWS_PER_STEP * GRID_W
    return pl.pallas_call(
        functools.partial(_na_kernel, rows=rows),
        grid=(B, rows // NA_ROWS_PER_STEP),
        in_specs=[pl.BlockSpec((1, tq, SEG), lambda b, r: (b, r, P_QRAW)),
                  pl.BlockSpec((1, tq, SEG), lambda b, r: (b, r, P_QROT)),
                  pl.BlockSpec((1, S, SEG), lambda b, r: (b, 0, P_KROT)),
                  pl.BlockSpec((1, S, SEG), lambda b, r: (b, 0, P_V)),
                  pl.BlockSpec((1, L, SEG), lambda b, r: (b, 0, C_K)),
                  pl.BlockSpec((1, L, SEG), lambda b, r: (b, 0, C_V)),
                  pl.BlockSpec(bias_tbl.shape, lambda b, r: (0, 0, 0))],
        out_specs=pl.BlockSpec((1, tq, SEG), lambda b, r: (b, r, 0)),
        out_shape=jax.ShapeDtypeStruct((B, S, NA_WIDTH), BF16),
        compiler_params=_cparams(("arbitrary", "arbitrary")),
        name="na",
    )(proj, proj, proj, proj, cproj, cproj, bias_tbl)


def _split3(g):
    g1 = g.astype(BF16)
    r1 = g - g1.astype(F32)
    g2 = r1.astype(BF16)
    g3 = (r1 - g2.astype(F32)).astype(BF16)
    return g1, g2, g3


def _hg_consts():
    c = HG_CHUNK
    t = np.arange(c)[:, None]
    u = np.arange(c)[None, :]
    cm, role, masks = [], [], []
    for fwd in (True, False):
        blocks = [(u <= t) if fwd else (u >= t)]
        roles, ms = [], []
        for m in HG_LEVELS:
            blk = t // (2 * m)
            if fwd:
                mid = 2 * m * blk + m - 1
                is_q = (t % (2 * m)) >= m
                expo = np.where(is_q, (u > mid) & (u <= t), (u > t) & (u <= mid))
            else:
                mid = 2 * m * blk + m
                is_q = (t % (2 * m)) < m
                expo = np.where(is_q, (u >= t) & (u < mid), (u >= mid) & (u < t))
            blocks.append(expo)
            roles.append(np.broadcast_to(is_q, (c, 2 * HG_DK)))
            ms.append((blk == blk.T) & is_q & ~is_q.T)
        ms.append(t == u)
        full = np.concatenate(blocks, axis=0).astype(np.float32)
        cm.append(np.concatenate([full, full], axis=1))
        role.append(np.stack(roles).astype(np.float32))
        masks.append(np.stack([np.concatenate([x, x], axis=1) for x in ms]).astype(np.float32))
    return jnp.asarray(np.stack(cm), BF16), jnp.asarray(np.stack(role), BF16), jnp.asarray(np.stack(masks))


def _block_diag(x, zeros):
    return jnp.concatenate([jnp.concatenate([x[:, :HG_DK], zeros], axis=1),
                            jnp.concatenate([zeros, x[:, HG_DK:]], axis=1)], axis=0)


def _hg_exponents(gh, gl, cmat_ref, di):
    return jnp.dot(cmat_ref[di], jnp.concatenate([gh, gl], axis=0), preferred_element_type=F32)


def _hg_intra(q, k, dall, role_ref, masks_ref, di):
    c = HG_CHUNK
    nl = len(HG_LEVELS)
    nt = (((1,), (1,)), ((), ()))
    b = dall[0:c]
    bend = b[c - 1:c] if di == 0 else b[0:1]
    qe = q * jnp.exp2(b).astype(BF16)
    ke = k * jnp.exp2(bend - b).astype(BF16)
    zeros = jnp.zeros((c, HG_DK), BF16)
    a = None
    for li in range(nl):
        e = jnp.exp2(dall[(li + 1) * c:(li + 2) * c]).astype(BF16)
        x = jnp.where(role_ref[di, li] > 0.5, q, k) * e
        r = lax.dot_general(x, _block_diag(x, zeros), nt, preferred_element_type=F32) * masks_ref[di, li]
        a = r if a is None else a + r
    qk = q.astype(F32) * k.astype(F32)
    lane = lax.broadcasted_iota(jnp.int32, a.shape, 1)
    d0 = jnp.sum(qk[:, :HG_DK], axis=-1, keepdims=True)
    d1 = jnp.sum(qk[:, HG_DK:], axis=-1, keepdims=True)
    a = a + jnp.where(lane < c, d0, d1) * masks_ref[di, nl]
    return a.astype(BF16), qe, ke, bend


def _hg_outputs(a, qe, ke, bend, v, st_ref, di, pair):
    nt = (((1,), (1,)), ((), ()))
    tn = (((0,), (0,)), ((), ()))
    o = jnp.dot(a, _block_diag(v, jnp.zeros((HG_CHUNK, HG_DK), BF16)), preferred_element_type=F32)
    outs = []
    for hh in range(2):
        hs = slice(hh * HG_DK, (hh + 1) * HG_DK)
        st = st_ref[di, 2 * pair + hh]
        outs.append(o[:, hs] + lax.dot_general(qe[:, hs], st.astype(BF16), nt, preferred_element_type=F32))
        upd = lax.dot_general(v[:, hs], ke[:, hs], tn, preferred_element_type=F32)
        st_ref[di, 2 * pair + hh] = jnp.exp2(bend[:, hs]) * st + upd
    return outs


def _hgrn_kernel(sq_ref, hi_ref, kf_ref, gfh_ref, gfl_ref, kb_ref, gbh_ref, gbl_ref, hg_ref,
                 ci_ref, ckf_ref, cgfh_ref, cgfl_ref, ckb_ref, cgbh_ref, cgbl_ref, gain_ref,
                 cmat_ref, role_ref, masks_ref, after_ref, before_ref, o_ref, acc_ref, st_ref, *, n_chunks):
    c = HG_CHUNK
    tn = (((0,), (0,)), ((), ()))

    ctx_dirs = ((ckf_ref, cgfh_ref, cgfl_ref, after_ref), (ckb_ref, cgbh_ref, cgbl_ref, before_ref))
    for di, (k_ref, gh_ref, gl_ref, cm_ref) in enumerate(ctx_dirs):
        decay = jnp.dot(cm_ref[...], jnp.concatenate([gh_ref[0], gl_ref[0]], axis=0), preferred_element_type=F32)
        ke = k_ref[0] * jnp.exp2(decay).astype(BF16)
        for h in range(HG_HEADS):
            hs = slice(h * HG_DK, (h + 1) * HG_DK)
            st_ref[di, h] = lax.dot_general(ci_ref[0, :, hs], ke[:, hs], tn, preferred_element_type=F32)

    def step(i, second):
        units = [(sub, pair, di) for sub in range(HG_CHUNKS_PER_STEP)
                 for pair in range(HG_HEADS // 2) for di in range(2)]

        def window(u):
            sub, pair, di = units[u]
            ci = i * HG_CHUNKS_PER_STEP + sub
            first = ci * c if di == 0 else (n_chunks - 1 - ci) * c
            return pl.ds(pl.multiple_of(first, c), c), slice(2 * pair * HG_DK, 2 * (pair + 1) * HG_DK)

        def exponents(u):
            rr, cols = window(u)
            gh_ref, gl_ref = (gfh_ref, gfl_ref) if units[u][2] == 0 else (gbh_ref, gbl_ref)
            return _hg_exponents(gh_ref[0, rr, cols], gl_ref[0, rr, cols], cmat_ref, units[u][2])

        def intra(u, dall):
            rr, cols = window(u)
            k_ref = kf_ref if units[u][2] == 0 else kb_ref
            return _hg_intra(sq_ref[0, rr, cols], k_ref[0, rr, cols], dall, role_ref, masks_ref, units[u][2])

        def finish(u, parts):
            _, pair, di = units[u]
            rr, cols = window(u)
            outs = _hg_outputs(*parts, hi_ref[0, rr, cols], st_ref, di, pair)
            for hh, o in enumerate(outs):
                hs = slice((2 * pair + hh) * HG_DK, (2 * pair + hh + 1) * HG_DK)
                if not second:
                    acc_ref[rr, hs] = o
                else:
                    ot = acc_ref[rr, hs] + o
                    y = ot * lax.rsqrt(jnp.mean(ot * ot, axis=-1, keepdims=True) + EPS) * gain_ref[:, hs]
                    o_ref[0, rr, hs] = (y * _silu(hg_ref[0, rr, hs].astype(F32))).astype(BF16)

        n = len(units)
        dalls = [exponents(u) for u in range(n)]
        parts = [intra(u, dalls[u]) for u in range(n)]
        for u in range(n):
            finish(u, parts[u])

    def first_half(i, carry):
        step(i, False)
        return carry

    def second_half(i, carry):
        step(i, True)
        return carry

    n_steps = n_chunks // HG_CHUNKS_PER_STEP
    lax.fori_loop(0, n_steps // 2, first_half, 0)
    lax.fori_loop(n_steps // 2, n_steps, second_half, 0)


def _hgrn(proj, cproj, gain):
    B, S, _ = proj.shape
    L = cproj.shape[1]
    c = HG_CHUNK
    nl = len(HG_LEVELS)
    before = np.tril(np.ones((L, L), np.float32), -1)
    stacked = lambda m: jnp.asarray(np.concatenate([m, m], axis=1), BF16)
    consts = list(_hg_consts()) + [stacked(before.T), stacked(before)]
    seq = lambda j: pl.BlockSpec((1, S, SEG), lambda b: (b, 0, j))
    cseq = lambda j: pl.BlockSpec((1, L, SEG), lambda b: (b, 0, j))
    full2 = lambda n, m: pl.BlockSpec((n, m), lambda b: (0, 0))
    return pl.pallas_call(
        functools.partial(_hgrn_kernel, n_chunks=S // c),
        grid=(B,),
        in_specs=[seq(P_HQ), seq(P_HI), seq(P_KF), seq(P_GFH), seq(P_GFL), seq(P_KB), seq(P_GBH), seq(P_GBL),
                  seq(P_HG), cseq(C_I), cseq(C_KF), cseq(C_GFH), cseq(C_GFL), cseq(C_KB), cseq(C_GBH), cseq(C_GBL),
                  full2(1, HG_WIDTH),
                  pl.BlockSpec(consts[0].shape, lambda b: (0, 0, 0)),
                  pl.BlockSpec((2, nl, c, 2 * HG_DK), lambda b: (0, 0, 0, 0)),
                  pl.BlockSpec((2, nl + 1, c, 2 * c), lambda b: (0, 0, 0, 0)),
                  full2(L, 2 * L), full2(L, 2 * L)],
        out_specs=pl.BlockSpec((1, S, HG_WIDTH), lambda b: (b, 0, 0)),
        out_shape=jax.ShapeDtypeStruct((B, S, HG_WIDTH), BF16),
        scratch_shapes=[pltpu.VMEM((S, HG_WIDTH), F32),
                        pltpu.VMEM((2, HG_HEADS, HG_DK, HG_DK), F32)],
        compiler_params=_cparams(("arbitrary",)),
        name="hgrn",
    )(*([proj] * 9), *([cproj] * 7), gain, *consts)


ROUTE_E_ROW = 8


def _route_t(lt):
    tm = lt.shape[1]
    row8 = lax.broadcasted_iota(jnp.int32, (SUBLANES, tm), 0).astype(F32)
    gl = jnp.where(row8 < N_GROUPS, lt[0:SUBLANES], -jnp.inf)
    gmax = jnp.max(gl, axis=0, keepdims=True)
    g_sel = jnp.min(jnp.where(gl == gmax, row8, float(SUBLANES)), axis=0, keepdims=True)
    g_w = 1.0 / jnp.sum(jnp.exp(gl - gmax), axis=0, keepdims=True)
    row16 = lax.broadcasted_iota(jnp.int32, (N_EXPERTS, tm), 0).astype(F32)
    first = g_sel * EXPERTS_PER_GROUP
    in_grp = (row16 >= first) & (row16 < first + EXPERTS_PER_GROUP)
    e1 = jnp.where(in_grp, lt[ROUTE_E_ROW:ROUTE_E_ROW + N_EXPERTS], -jnp.inf)
    v1 = jnp.max(e1, axis=0, keepdims=True)
    i1 = jnp.min(jnp.where(e1 == v1, row16, float(N_EXPERTS)), axis=0, keepdims=True)
    e2 = jnp.where(row16 == i1, -jnp.inf, e1)
    v2 = jnp.max(e2, axis=0, keepdims=True)
    i2 = jnp.min(jnp.where(e2 == v2, row16, float(N_EXPERTS)), axis=0, keepdims=True)
    t = jnp.exp(v2 - v1)
    w1 = g_w / (1.0 + t)
    w2 = g_w * t / (1.0 + t)
    onehot = jnp.where(row8 == g_sel, 1.0, 0.0)
    gates = jnp.where(row8 == i1 - first, w1, jnp.where(row8 == i2 - first, w2, 0.0))
    return onehot, gates


def _outproj_kernel(x_ref, na_ref, hg_ref, w_ref, ga_ref, g_ref, sc_ref, sh_ref, wrh_ref, wrl_ref, br_ref,
                    x1_ref, hx_ref, oh_ref, cnt_ref):
    tm, d = x_ref.shape
    nt = (((1,), (1,)), ((), ()))
    mix = (jnp.dot(na_ref[...], w_ref[0:NA_WIDTH, :], preferred_element_type=F32)
           + jnp.dot(hg_ref[...], w_ref[NA_WIDTH:, :], preferred_element_type=F32))
    x1 = x_ref[...] + ga_ref[0] * mix
    x1_ref[...] = x1
    h2 = _norm_mod(x1, g_ref[...], sc_ref[0], sh_ref[0])
    h_hi = h2.astype(BF16)
    h_lo = (h2 - h_hi.astype(F32)).astype(BF16)
    hx_ref[:, 0:d] = h_hi
    both = lax.dot_general(jnp.concatenate([wrh_ref[...], wrl_ref[...]], axis=0), h_hi, nt,
                           preferred_element_type=F32)
    lt = ((both[LANES:] + lax.dot_general(wrh_ref[...], h_lo, nt, preferred_element_type=F32))
          + both[:LANES]) + br_ref[...]
    onehot_t, gates_t = _route_t(lt)
    pieces_t = [p.astype(F32) for p in _split3(gates_t)]
    pad = jnp.zeros((LANES - (1 + len(pieces_t)) * SUBLANES, tm), F32)
    tok = jnp.concatenate([onehot_t] + pieces_t + [pad], axis=0).T
    lane = lax.broadcasted_iota(jnp.int32, tok.shape, 1)
    onehot = jnp.where(lane < ROUTE_E_ROW, tok, 0.0)
    hx_ref[:, d:d + LANES] = jnp.where(lane >= ROUTE_E_ROW, tok, 0.0).astype(BF16)
    oh_ref[...] = onehot
    cnt_ref[0] = jnp.broadcast_to(jnp.sum(onehot, axis=0, keepdims=True), (SUBLANES, LANES))


def _outproj(x2d, na2d, hg2d, w_out_bf, ga, g, sc, sh, wr_hi, wr_lo, br, S):
    T, D = x2d.shape
    tm = MOE_TM
    per = S // tm
    tok = lambda w: pl.BlockSpec((tm, w), lambda i: (i, 0))
    bat = pl.BlockSpec((1, 1, D), lambda i: (i // per, 0, 0))
    return pl.pallas_call(
        _outproj_kernel,
        grid=(T // tm,),
        in_specs=[tok(D), tok(NA_WIDTH), tok(HG_WIDTH),
                  pl.BlockSpec((NA_WIDTH + HG_WIDTH, D), lambda i: (0, 0)),
                  bat, pl.BlockSpec((1, D), lambda i: (0, 0)), bat, bat,
                  pl.BlockSpec((LANES, D), lambda i: (0, 0)),
                  pl.BlockSpec((LANES, D), lambda i: (0, 0)),
                  pl.BlockSpec((LANES, 1), lambda i: (0, 0))],
        out_specs=[tok(D), tok(MOE_XW), tok(LANES),
                   pl.BlockSpec((1, SUBLANES, LANES), lambda i: (i, 0, 0))],
        out_shape=[jax.ShapeDtypeStruct((T, D), F32), jax.ShapeDtypeStruct((T, MOE_XW), BF16),
                   jax.ShapeDtypeStruct((T, LANES), F32),
                   jax.ShapeDtypeStruct((T // tm, SUBLANES, LANES), F32)],
        compiler_params=_cparams(("arbitrary",)),
        name="outproj",
    )(x2d, na2d, hg2d, w_out_bf, ga, g, sc, sh, wr_hi, wr_lo, br)


MOE_TM = 512
MOE_ALIGN = 16
MOE_LOC = 640
MOE_XW = D_MODEL + LANES
MOE_TE = 512
MOE_BITS = tuple(1 << b for b in range(9, 3, -1))


def _moe_steps(T):
    worst = T + (T // MOE_TM) * N_GROUPS * (MOE_ALIGN - 1)
    return -(-worst // MOE_TE) + N_GROUPS


def _local_positions(oh_ref, lstrict_ref, cnt_ref, k):
    onehot = oh_ref[...]
    ranks = jnp.dot(lstrict_ref[...], onehot.astype(BF16), preferred_element_type=F32)
    lane = lax.broadcasted_iota(jnp.int32, (1, LANES), 1)
    base = jnp.zeros((1, LANES), F32)
    o = jnp.int32(0)
    for g in range(N_GROUPS):
        base = jnp.where(lane == g, o.astype(F32), base)
        o = o + cnt_ref[k * N_GROUPS + g]
    return jnp.sum(onehot * (ranks + base), axis=-1, keepdims=True)


def _sort_matrix(lpos):
    col = lax.broadcasted_iota(jnp.int32, (MOE_TM, MOE_LOC), 1).astype(F32)
    return jnp.where(lpos == col, 1.0, 0.0).astype(BF16)


def _block_copies(n_rows, src_row, dst_row, bits, make_copy, action):
    for bit in bits:
        part = n_rows & (-2 * bit)

        @pl.when((n_rows & bit) != 0)
        def _(part=part, bit=bit):
            action(make_copy(pl.multiple_of(src_row + part, MOE_ALIGN), pl.multiple_of(dst_row + part, MOE_ALIGN), bit))


def _run_copies(off_ref, cnt_ref, k, make_copy, action):
    o = jnp.int32(0)
    for g in range(N_GROUPS):
        c = cnt_ref[k * N_GROUPS + g]
        _block_copies(c, o, off_ref[k * N_GROUPS + g], MOE_BITS, make_copy, action)
        o = o + c


def _dispatch_kernel(off_ref, cnt_ref, tail_ref, hx_ref, oh_ref, lstrict_ref, hs_ref, buf_ref, zero_ref, sem,
                     *, n_tiles, n_steps):
    k = pl.program_id(0)
    slot = k % 2

    def copies(kk, sl, action):
        def make(src_row, dst_row, n):
            return pltpu.make_async_copy(buf_ref.at[sl, pl.ds(src_row, n)], hs_ref.at[pl.ds(dst_row, n)], sem.at[sl])
        _run_copies(off_ref, cnt_ref, kk, make, action)

    @pl.when(k >= 2)
    def _():
        copies(k - 2, slot, lambda cp: cp.wait())

    pt = _sort_matrix(_local_positions(oh_ref, lstrict_ref, cnt_ref, k))
    srt = lax.dot_general(pt, hx_ref[...], (((0,), (0,)), ((), ())), preferred_element_type=F32)
    buf_ref[slot] = srt.astype(BF16)
    copies(k, slot, lambda cp: cp.start())

    @pl.when(k == n_tiles - 1)
    def _():
        zero_ref[...] = jnp.zeros_like(zero_ref)
        n_used = tail_ref[2 * N_GROUPS]

        def zero_copy(src_row, dst_row, n):
            return pltpu.make_async_copy(zero_ref.at[pl.ds(src_row, n)], hs_ref.at[pl.ds(dst_row, n)], sem.at[2])

        def tile_copy(i):
            return zero_copy(0, pl.multiple_of(i * MOE_TE, MOE_TE), MOE_TE)

        def fills(action):
            for g in range(N_GROUPS):
                _block_copies(tail_ref[N_GROUPS + g], jnp.int32(0), tail_ref[g], MOE_BITS[1:], zero_copy, action)

        fills(lambda cp: cp.start())
        lax.fori_loop(n_used, n_steps, lambda i, c: (tile_copy(i).start(), c)[1], 0)
        if n_tiles >= 2:
            copies(k - 1, 1 - slot, lambda cp: cp.wait())
        copies(k, slot, lambda cp: cp.wait())
        fills(lambda cp: cp.wait())
        lax.fori_loop(n_used, n_steps, lambda i, c: (tile_copy(i).wait(), c)[1], 0)


def _dispatch(off, cnt, tail, hx, onehot, lstrict, n_steps):
    T = hx.shape[0]
    n_tiles = T // MOE_TM
    return pl.pallas_call(
        functools.partial(_dispatch_kernel, n_tiles=n_tiles, n_steps=n_steps),
        grid_spec=pltpu.PrefetchScalarGridSpec(
            num_scalar_prefetch=3,
            grid=(n_tiles,),
            in_specs=[pl.BlockSpec((MOE_TM, MOE_XW), lambda k, *_: (k, 0)),
                      pl.BlockSpec((MOE_TM, LANES), lambda k, *_: (k, 0)),
                      pl.BlockSpec((MOE_TM, MOE_TM), lambda k, *_: (0, 0))],
            out_specs=pl.BlockSpec(memory_space=pl.ANY),
            scratch_shapes=[pltpu.VMEM((2, MOE_LOC, MOE_XW), BF16),
                            pltpu.VMEM((MOE_TE, MOE_XW), BF16),
                            pltpu.SemaphoreType.DMA((3,))]),
        out_shape=jax.ShapeDtypeStruct((n_steps * MOE_TE, MOE_XW), BF16),
        compiler_params=_cparams(("arbitrary",)),
        name="dispatch",
    )(off, cnt, tail, hx, onehot, lstrict)


def _experts_kernel(grp_ref, used_ref, hs_ref, w1_ref, w3_ref, w2_ref, ys_ref):
    i = pl.program_id(0)
    d = ys_ref.shape[1]
    ne = EXPERTS_PER_GROUP

    @pl.when(i < used_ref[0])
    def _():
        t = hs_ref[:, 0:d]
        extra = hs_ref[:, d:d + LANES].astype(F32)
        lane = lax.broadcasted_iota(jnp.int32, extra.shape, 1)
        acc = None
        for j in range(ne):
            a = jnp.dot(t, w1_ref[0, j], preferred_element_type=F32)
            b = jnp.dot(t, w3_ref[0, j], preferred_element_type=F32)
            mine = (lane >= ROUTE_E_ROW) & (lane < ROUTE_E_ROW + 3 * SUBLANES) & (lane % SUBLANES == j)
            gj = jnp.sum(jnp.where(mine, extra, 0.0), axis=-1, keepdims=True)
            y = gj * jnp.dot((_silu(a) * b).astype(BF16), w2_ref[0, j], preferred_element_type=F32)
            acc = y if acc is None else acc + y
        ys_ref[...] = acc.astype(BF16)

    @pl.when(i >= used_ref[0])
    def _():
        ys_ref[...] = jnp.zeros_like(ys_ref)


def _experts(grp, used, hs, w1_bf, w3_bf, w2_bf):
    rows, _ = hs.shape
    D = w2_bf.shape[3]
    wspec = lambda w: pl.BlockSpec((1,) + w.shape[1:], lambda i, grp, used: (grp[i], 0, 0, 0))
    return pl.pallas_call(
        _experts_kernel,
        grid_spec=pltpu.PrefetchScalarGridSpec(
            num_scalar_prefetch=2,
            grid=(rows // MOE_TE,),
            in_specs=[pl.BlockSpec((MOE_TE, MOE_XW), lambda i, grp, used: (i, 0)),
                      wspec(w1_bf), wspec(w3_bf), wspec(w2_bf)],
            out_specs=pl.BlockSpec((MOE_TE, D), lambda i, grp, used: (i, 0))),
        out_shape=jax.ShapeDtypeStruct((rows, D), BF16),
        compiler_params=_cparams(("arbitrary",)),
        name="experts",
    )(grp, used, hs, w1_bf, w3_bf, w2_bf)


def _combine_kernel(off_ref, cnt_ref, x1_ref, oh_ref, lstrict_ref, ga_ref, gf_ref, ys_ref, o_ref, buf_ref, sem,
                    *, n_tiles):
    k = pl.program_id(0)
    slot = k % 2

    def copies(kk, sl, action):
        def make(loc_row, seg_row, n):
            return pltpu.make_async_copy(ys_ref.at[pl.ds(seg_row, n)], buf_ref.at[sl, pl.ds(loc_row, n)], sem.at[sl])
        _run_copies(off_ref, cnt_ref, kk, make, action)

    @pl.when(k == 0)
    def _():
        buf_ref[...] = jnp.zeros_like(buf_ref)
        copies(k, slot, lambda cp: cp.start())

    @pl.when(k + 1 < n_tiles)
    def _():
        copies(k + 1, 1 - slot, lambda cp: cp.start())

    pt = _sort_matrix(_local_positions(oh_ref, lstrict_ref, cnt_ref, k))
    copies(k, slot, lambda cp: cp.wait())
    y = jnp.dot(pt, buf_ref[slot], preferred_element_type=F32)
    x2 = x1_ref[...] + ga_ref[0] * y
    o_ref[...] = x2 * lax.rsqrt(jnp.mean(x2 * x2, axis=-1, keepdims=True) + EPS) * gf_ref[...]


def _combine(off, cnt, x1, onehot, lstrict, ga, gfin, ys, S):
    T, D = x1.shape
    n_tiles = T // MOE_TM
    per = S // MOE_TM
    return pl.pallas_call(
        functools.partial(_combine_kernel, n_tiles=n_tiles),
        grid_spec=pltpu.PrefetchScalarGridSpec(
            num_scalar_prefetch=2,
            grid=(n_tiles,),
            in_specs=[pl.BlockSpec((MOE_TM, D), lambda k, off, cnt: (k, 0)),
                      pl.BlockSpec((MOE_TM, LANES), lambda k, off, cnt: (k, 0)),
                      pl.BlockSpec((MOE_TM, MOE_TM), lambda k, off, cnt: (0, 0)),
                      pl.BlockSpec((1, 1, D), lambda k, off, cnt: (k // per, 0, 0)),
                      pl.BlockSpec((1, D), lambda k, off, cnt: (0, 0)),
                      pl.BlockSpec(memory_space=pl.ANY)],
            out_specs=pl.BlockSpec((MOE_TM, D), lambda k, off, cnt: (k, 0)),
            scratch_shapes=[pltpu.VMEM((2, MOE_LOC, D), BF16),
                            pltpu.SemaphoreType.DMA((2,))]),
        out_shape=jax.ShapeDtypeStruct((T, D), F32),
        compiler_params=_cparams(("arbitrary",)),
        name="combine",
    )(off, cnt, x1, onehot, lstrict, ga, gfin, ys)


def _moe_schedule(cnt_tiles, n_steps):
    cnt = ((cnt_tiles + (MOE_ALIGN - 1)) // MOE_ALIGN) * MOE_ALIGN
    ends = jnp.cumsum(cnt, axis=0)
    total = ends[-1]
    ntile = (total + MOE_TE - 1) // MOE_TE
    cum = jnp.cumsum(ntile)
    base = (cum - ntile) * MOE_TE
    off = base[None, :] + ends - cnt
    i = jnp.arange(n_steps, dtype=jnp.int32)
    grp = jnp.minimum(jnp.sum(i[:, None] >= cum[None, :], axis=1), N_GROUPS - 1)
    grp = jnp.where(i < cum[-1], grp, grp[jnp.maximum(cum[-1] - 1, 0)])
    tail = jnp.concatenate([base + total, ntile * MOE_TE - total, cum[-1:]])
    as_i32 = lambda a: a.reshape(-1).astype(jnp.int32)
    return as_i32(off), as_i32(cnt), as_i32(tail), as_i32(grp), as_i32(cum[-1:])


def _rope_tables(S):
    t = jnp.arange(S)
    pos = jnp.stack([t // GRID_W, t % GRID_W], axis=-1).astype(F32)
    inv = ROPE_BASE ** (-jnp.arange(0, ROPE_AXIS_DIM, 2, dtype=F32) / ROPE_AXIS_DIM)
    ang = pos[:, :, None] * inv
    cos, sin = jnp.cos(ang), jnp.sin(ang)
    cos_h = jnp.concatenate([cos, cos], axis=-1).reshape(S, NA_HEAD_DIM)
    sin_h = jnp.concatenate([-sin, sin], axis=-1).reshape(S, NA_HEAD_DIM)
    return jnp.tile(cos_h, (1, NA_HEADS)), jnp.tile(sin_h, (1, NA_HEADS))


def kernel(x, c, ctx, c_ctx, w_mod, b_mod, norm_mix, norm_ffn, w_in, w_out, na_rpb, hg_lb, hg_norm,
           w_grp, b_grp, w_exp, b_exp, w1, w3, w2, norm_final):
    B, S, D = x.shape
    T = B * S
    assert w_mod.shape[0] == 1, "single-layer kernel"

    rows = -(-(B + 1) // SUBLANES) * SUBLANES
    cc = jnp.zeros((rows, D), F32).at[:B].set(c).at[B].set(c_ctx)
    mod = _modulation(cc, w_mod[0], b_mod[0])
    sh_a, sc_a, ga_a, sh_f, sc_f, ga_f = [m.reshape(B, 1, D) for m in jnp.split(mod[:B], 6, axis=-1)]
    csh_a, csc_a = [m.reshape(1, D) for m in jnp.split(mod[B], 6)[:2]]

    w_in_bf = w_in[0].astype(BF16)
    cos_t, sin_t = _rope_tables(S)
    g_mix = norm_mix[0].reshape(1, D)

    lb = jnp.cumsum(jax.nn.softmax(hg_lb.astype(F32), axis=0), axis=0)[0]
    proj = _project(x, g_mix, sc_a, sh_a, w_in_bf, cos_t, sin_t, lb, tm=512)
    cproj = _project_ctx(ctx, g_mix, csc_a, csh_a, w_in_bf, lb)

    na_out = _neighbourhood_attention(proj, cproj, _na_bias_table(na_rpb[0]))

    gain = jnp.tile(hg_norm[0].astype(F32), HG_HEADS).reshape(1, HG_WIDTH)
    hg_out = _hgrn(proj, cproj, gain)

    e_rows = slice(ROUTE_E_ROW, ROUTE_E_ROW + N_EXPERTS)
    wr = jnp.zeros((LANES, D), F32).at[:N_GROUPS].set(w_grp[0].T).at[e_rows].set(w_exp[0].T)
    br = jnp.zeros((LANES, 1), F32).at[:N_GROUPS, 0].set(b_grp[0]).at[e_rows, 0].set(b_exp[0])
    hi_f32 = lax.bitcast_convert_type(lax.bitcast_convert_type(wr, jnp.uint32) & jnp.uint32(0xFFFF0000), F32)
    wr_hi = hi_f32.astype(BF16)
    wr_lo = (wr - hi_f32).astype(BF16)
    x1, hx, onehot, cnt_tiles = _outproj(x.reshape(T, D), na_out.reshape(T, NA_WIDTH), hg_out.reshape(T, HG_WIDTH),
                                         w_out[0].astype(BF16), ga_a, norm_ffn[0].reshape(1, D), sc_f, sh_f,
                                         wr_hi, wr_lo, br, S=S)

    n_steps = _moe_steps(T)
    off, cnt, tail, grp, used = _moe_schedule(cnt_tiles[:, 0, :N_GROUPS].astype(jnp.int32), n_steps)
    lstrict = jnp.asarray(np.tril(np.ones((MOE_TM, MOE_TM), np.float32), -1), BF16)
    by_group = lambda w: w[0].astype(BF16).reshape(N_GROUPS, EXPERTS_PER_GROUP, *w.shape[2:])
    hs = _dispatch(off, cnt, tail, hx, onehot, lstrict, n_steps)
    ys = _experts(grp, used, hs, by_group(w1), by_group(w3), by_group(w2))
    out = _combine(off, cnt, x1, onehot, lstrict, ga_f, norm_final.reshape(1, D), ys, S)
    return out.reshape(B, S, D)
```

```python
import functools

import numpy as np
import jax
import jax.numpy as jnp
from jax import lax
from jax.experimental import pallas as pl
from jax.experimental.pallas import tpu as pltpu

F32 = jnp.float32
BF16 = jnp.bfloat16
HIGHEST = lax.Precision.HIGHEST

D_MODEL = 1024
GRID_W = 64
NA_HEADS = 8
NA_HEAD_DIM = 64
NA_WIDTH = NA_HEADS * NA_HEAD_DIM
NA_KH = 8
NA_KW = 16
ROPE_AXIS_DIM = NA_HEAD_DIM // 2
ROPE_BASE = 10000.0
HG_HEADS = 4
HG_DK = 128
HG_WIDTH = HG_HEADS * HG_DK
HG_CHUNK = 64
SEG = 512
N_GROUPS = 4
EXPERTS_PER_GROUP = 4
N_EXPERTS = N_GROUPS * EXPERTS_PER_GROUP
D_EXPERT = 512
EPS = 1e-6
NEG = -1e30
LOG2E = 1.4426950408889634
LANES = 128
SUBLANES = 8
VMEM_LIMIT = 56 * 1024 * 1024

HG_LEVELS = (32, 16, 8, 4, 2, 1)
HG_CHUNKS_PER_STEP = 4


def _cparams(sem):
    return pltpu.CompilerParams(dimension_semantics=sem, vmem_limit_bytes=VMEM_LIMIT)


def _sigmoid(x):
    return 1.0 / (1.0 + jnp.exp(-x))


def _silu(x):
    return x * _sigmoid(x)


def _mod_kernel(c_ref, w_ref, b_ref, o_ref):
    s = _silu(c_ref[...])
    o_ref[...] = jnp.dot(s, w_ref[...], precision=HIGHEST, preferred_element_type=F32) + b_ref[...]


def _modulation(cc, w_mod, b_mod):
    rows, d = cc.shape
    n = w_mod.shape[1]
    tn = 1024
    return pl.pallas_call(
        _mod_kernel,
        grid=(n // tn,),
        in_specs=[pl.BlockSpec((rows, d), lambda j: (0, 0)),
                  pl.BlockSpec((d, tn), lambda j: (0, j)),
                  pl.BlockSpec((1, tn), lambda j: (0, j))],
        out_specs=pl.BlockSpec((rows, tn), lambda j: (0, j)),
        out_shape=jax.ShapeDtypeStruct((rows, n), F32),
        compiler_params=_cparams(("arbitrary",)),
        name="mod",
    )(cc, w_mod, b_mod.reshape(1, n))


def _norm_mod(x, g, sc, sh):
    y = x * lax.rsqrt(jnp.mean(x * x, axis=-1, keepdims=True) + EPS)
    return (y * g) * (1.0 + sc) + sh


def _rope(a, cos, sin):
    lane = lax.broadcasted_iota(jnp.int32, a.shape, 1)
    first = (lane % ROPE_AXIS_DIM) < (ROPE_AXIS_DIM // 2)
    up = pltpu.roll(a, LANES - ROPE_AXIS_DIM // 2, axis=1)
    dn = pltpu.roll(a, ROPE_AXIS_DIM // 2, axis=1)
    return a * cos + jnp.where(first, up, dn) * sin


def _forget_gate(pre, lb):
    f = lb + (1.0 - lb) * _sigmoid(pre)
    g2 = jnp.log2(f)
    hi = g2.astype(BF16)
    return 1.0 - f, hi, g2 - hi.astype(F32)


P_QRAW, P_QROT, P_KROT, P_V, P_HQ, P_HI, P_KF, P_GFH, P_GFL, P_KB, P_GBH, P_GBL, P_HG = range(13)
P_SEGS = 13


def _proj_kernel(x_ref, g_ref, sc_ref, sh_ref, w_ref, cos_ref, sin_ref, lb_ref, o_ref):
    h = _norm_mod(x_ref[0], g_ref[...], sc_ref[0], sh_ref[0]).astype(BF16)
    scale = NA_HEAD_DIM ** -0.5 * LOG2E

    def put(seg, val):
        o_ref[0, :, seg * SEG:(seg + 1) * SEG] = val.astype(BF16)

    for j in range(8):
        acc = jnp.dot(h, w_ref[:, j * SEG:(j + 1) * SEG], preferred_element_type=F32)
        if j <= 1:
            rot = jnp.concatenate(
                [_rope(acc[:, p * LANES:(p + 1) * LANES], cos_ref[:, p * LANES:(p + 1) * LANES],
                       sin_ref[:, p * LANES:(p + 1) * LANES]) for p in range(SEG // LANES)], axis=1)
            if j == 0:
                put(P_QRAW, acc * scale)
                put(P_QROT, rot * scale)
            else:
                put(P_KROT, rot)
        elif j == 2:
            put(P_V, acc)
        elif j == 3:
            put(P_HQ, _silu(acc))
        elif j == 4:
            put(P_HI, acc)
        elif j in (5, 6):
            base = P_KF if j == 5 else P_KB
            for i, val in enumerate(_forget_gate(acc, lb_ref[j - 5:j - 4, :])):
                put(base + i, val)
        else:
            put(P_HG, acc)


def _project(x, g, sc, sh, w_bf, cos_t, sin_t, lb, tm):
    B, S, D = x.shape
    return pl.pallas_call(
        _proj_kernel,
        grid=(S // tm, B),
        in_specs=[pl.BlockSpec((1, tm, D), lambda s, b: (b, s, 0)),
                  pl.BlockSpec((1, D), lambda s, b: (0, 0)),
                  pl.BlockSpec((1, 1, D), lambda s, b: (b, 0, 0)),
                  pl.BlockSpec((1, 1, D), lambda s, b: (b, 0, 0)),
                  pl.BlockSpec((D, 8 * SEG), lambda s, b: (0, 0)),
                  pl.BlockSpec((tm, SEG), lambda s, b: (s, 0)),
                  pl.BlockSpec((tm, SEG), lambda s, b: (s, 0)),
                  pl.BlockSpec((2, HG_WIDTH), lambda s, b: (0, 0))],
        out_specs=pl.BlockSpec((1, tm, P_SEGS * SEG), lambda s, b: (b, s, 0)),
        out_shape=jax.ShapeDtypeStruct((B, S, P_SEGS * SEG), BF16),
        compiler_params=_cparams(("arbitrary", "arbitrary")),
        name="proj",
    )(x, g, sc, sh, w_bf, cos_t, sin_t, lb)


C_K, C_V, C_I, C_KF, C_GFH, C_GFL, C_KB, C_GBH, C_GBL = range(9)
C_SEGS = 9


CTX_W_SEGS = (1, 2, 4, 5, 6)


def _ctxproj_kernel(x_ref, g_ref, sc_ref, sh_ref, *rest):
    w_refs, (lb_ref, o_ref) = rest[:len(CTX_W_SEGS)], rest[len(CTX_W_SEGS):]
    h = _norm_mod(x_ref[0], g_ref[...], sc_ref[...], sh_ref[...]).astype(BF16)

    def put(seg, val):
        o_ref[0, :, seg * SEG:(seg + 1) * SEG] = val.astype(BF16)

    for j, w_ref in enumerate(w_refs):
        acc = jnp.dot(h, w_ref[...], preferred_element_type=F32)
        if j < 3:
            put(j, acc)
        else:
            for i, val in enumerate(_forget_gate(acc, lb_ref[j - 3:j - 2, :])):
                put(C_KF + 3 * (j - 3) + i, val)


def _project_ctx(ctx, g, sc, sh, w_bf, lb):
    B, L, D = ctx.shape
    return pl.pallas_call(
        _ctxproj_kernel,
        grid=(B,),
        in_specs=[pl.BlockSpec((1, L, D), lambda b: (b, 0, 0)),
                  pl.BlockSpec((1, D), lambda b: (0, 0)),
                  pl.BlockSpec((1, D), lambda b: (0, 0)),
                  pl.BlockSpec((1, D), lambda b: (0, 0))]
                 + [pl.BlockSpec((D, SEG), lambda b, j=j: (0, j)) for j in CTX_W_SEGS]
                 + [pl.BlockSpec((2, HG_WIDTH), lambda b: (0, 0))],
        out_specs=pl.BlockSpec((1, L, C_SEGS * SEG), lambda b: (b, 0, 0)),
        out_shape=jax.ShapeDtypeStruct((B, L, C_SEGS * SEG), BF16),
        compiler_params=_cparams(("arbitrary",)),
        name="ctxproj",
    )(ctx, g, sc, sh, *([w_bf] * len(CTX_W_SEGS)), lb)


NA_ROWS_PER_STEP = 16


def _na_kernel(qraw_ref, qrot_ref, k_ref, v_ref, ck_ref, cv_ref, bias_ref, o_ref, *, rows):
    nk = NA_KH * GRID_W
    lane = lax.broadcasted_iota(jnp.int32, (GRID_W, LANES), 1)
    sel0 = lane < NA_HEAD_DIM
    nt = (((1,), (1,)), ((), ()))
    n_pairs = NA_WIDTH // LANES
    ones = jnp.ones((nk + ck_ref.shape[1], LANES), BF16)
    units = [(j, p) for j in range(NA_ROWS_PER_STEP) for p in range(n_pairs)]

    def window(j):
        r = pl.program_id(1) * NA_ROWS_PER_STEP + j
        rs = jnp.clip(r - NA_KH // 2, 0, rows - NA_KH)
        return pl.multiple_of(rs * GRID_W, GRID_W), pl.multiple_of((rs - r + NA_KH - 1) * GRID_W, GRID_W)

    def query_slots(ref, j, cols):
        q = ref[0, j * GRID_W:(j + 1) * GRID_W, cols]
        zero = jnp.zeros_like(q)
        return jnp.concatenate([jnp.where(sel0, q, zero), jnp.where(sel0, zero, q)], axis=0)

    def scores(j, p):
        cols = slice(p * LANES, (p + 1) * LANES)
        start, bias_row = window(j)
        s_loc = lax.dot_general(k_ref[0, pl.ds(start, nk), cols], query_slots(qrot_ref, j, cols), nt,
                                preferred_element_type=F32) + bias_ref[p, pl.ds(bias_row, nk), :]
        s_ctx = lax.dot_general(ck_ref[0, :, cols], query_slots(qraw_ref, j, cols), nt,
                                preferred_element_type=F32)
        return s_loc, s_ctx

    def softmax(s_loc, s_ctx):
        m = jnp.maximum(jnp.max(s_loc, axis=0, keepdims=True), jnp.max(s_ctx, axis=0, keepdims=True))
        return jnp.concatenate([jnp.exp2(s_loc - m).T, jnp.exp2(s_ctx - m).T], axis=1).astype(BF16)

    def values(j, p, probs):
        cols = slice(p * LANES, (p + 1) * LANES)
        start, _ = window(j)
        vals = jnp.concatenate([v_ref[0, pl.ds(start, nk), cols], cv_ref[0, :, cols]], axis=0)
        oe = jnp.dot(probs, jnp.concatenate([vals, ones], axis=1), preferred_element_type=F32)
        res = oe[:, :LANES] / oe[:, LANES:]
        o_ref[0, j * GRID_W:(j + 1) * GRID_W, cols] = jnp.where(sel0, res[0:GRID_W],
                                                                res[GRID_W:2 * GRID_W]).astype(BF16)

    s_all = [scores(j, p) for j, p in units]
    probs = [softmax(*s) for s in s_all]
    for (j, p), pr in zip(units, probs):
        values(j, p, pr)


def _na_bias_table(rpb):
    qc = np.arange(GRID_W)[None, :]
    kc = np.arange(GRID_W)[:, None]
    cs = np.clip(qc - NA_KW // 2, 0, GRID_W - NA_KW)
    valid = (kc >= cs) & (kc < cs + NA_KW)
    r = rpb.astype(F32) * LOG2E
    period = GRID_W + 2 * NA_KW - 1
    gap = jnp.zeros(r.shape[:-1] + (GRID_W,), F32)
    w = jnp.concatenate([r[..., NA_KW - 1::-1], gap, r[..., :NA_KW - 1:-1]], axis=-1)
    t = jnp.tile(w, (1, 1, GRID_W))[..., :GRID_W * (period - 1)]
    t = t.reshape(r.shape[:-1] + (GRID_W, period - 1))[..., :GRID_W]
    t = jnp.where(jnp.asarray(valid)[None, None], t, NEG)
    n_pairs = NA_HEADS // 2
    n_ro = 2 * NA_KH - 1
    t = t.reshape(n_pairs, 2, n_ro, GRID_W, GRID_W).transpose(0, 2, 3, 1, 4)
    return t.reshape(n_pairs, n_ro * GRID_W, 2 * GRID_W)


def _neighbourhood_attention(proj, cproj, bias_tbl):
    B, S, _ = proj.shape
    L = cproj.shape[1]
    rows = S // GRID_W
    tq = NA_ROWS_PER_STEP * GRID_W
    return pl.pallas_call(
        functools.partial(_na_kernel, rows=rows),
        grid=(B, rows // NA_ROWS_PER_STEP),
        in_specs=[pl.BlockSpec((1, tq, SEG), lambda b, r: (b, r, P_QRAW)),
                  pl.BlockSpec((1, tq, SEG), lambda b, r: (b, r, P_QROT)),
                  pl.BlockSpec((1, S, SEG), lambda b, r: (b, 0, P_KROT)),
                  pl.BlockSpec((1, S, SEG), lambda b, r: (b, 0, P_V)),
                  pl.BlockSpec((1, L, SEG), lambda b, r: (b, 0, C_K)),
                  pl.BlockSpec((1, L, SEG), lambda b, r: (b, 0, C_V)),
                  pl.BlockSpec(bias_tbl.shape, lambda b, r: (0, 0, 0))],
        out_specs=pl.BlockSpec((1, tq, SEG), lambda b, r: (b, r, 0)),
        out_shape=jax.ShapeDtypeStruct((B, S, NA_WIDTH), BF16),
        compiler_params=_cparams(("arbitrary", "arbitrary")),
        name="na",
    )(proj, proj, proj, proj, cproj, cproj, bias_tbl)


def _split3(g):
    g1 = g.astype(BF16)
    r1 = g - g1.astype(F32)
    g2 = r1.astype(BF16)
    g3 = (r1 - g2.astype(F32)).astype(BF16)
    return g1, g2, g3


def _hg_consts():
    c = HG_CHUNK
    t = np.arange(c)[:, None]
    u = np.arange(c)[None, :]
    cm, role, masks = [], [], []
    for fwd in (True, False):
        blocks = [(u <= t) if fwd else (u >= t)]
        roles, ms = [], []
        for m in HG_LEVELS:
            blk = t // (2 * m)
            if fwd:
                mid = 2 * m * blk + m - 1
                is_q = (t % (2 * m)) >= m
                expo = np.where(is_q, (u > mid) & (u <= t), (u > t) & (u <= mid))
            else:
                mid = 2 * m * blk + m
                is_q = (t % (2 * m)) < m
                expo = np.where(is_q, (u >= t) & (u < mid), (u >= mid) & (u < t))
            blocks.append(expo)
            roles.append(np.broadcast_to(is_q, (c, 2 * HG_DK)))
            ms.append((blk == blk.T) & is_q & ~is_q.T)
        ms.append(t == u)
        full = np.concatenate(blocks, axis=0).astype(np.float32)
        cm.append(np.concatenate([full, full], axis=1))
        role.append(np.stack(roles).astype(np.float32))
        masks.append(np.stack([np.concatenate([x, x], axis=1) for x in ms]).astype(np.float32))
    return jnp.asarray(np.stack(cm), BF16), jnp.asarray(np.stack(role), BF16), jnp.asarray(np.stack(masks))


def _block_diag(x, zeros):
    return jnp.concatenate([jnp.concatenate([x[:, :HG_DK], zeros], axis=1),
                            jnp.concatenate([zeros, x[:, HG_DK:]], axis=1)], axis=0)


def _hg_exponents(gh, gl, cmat_ref, di):
    return jnp.dot(cmat_ref[di], jnp.concatenate([gh, gl], axis=0), preferred_element_type=F32)


def _hg_intra(q, k, dall, role_ref, masks_ref, di):
    c = HG_CHUNK
    nl = len(HG_LEVELS)
    nt = (((1,), (1,)), ((), ()))
    b = dall[0:c]
    bend = b[c - 1:c] if di == 0 else b[0:1]
    qe = q * jnp.exp2(b).astype(BF16)
    ke = k * jnp.exp2(bend - b).astype(BF16)
    zeros = jnp.zeros((c, HG_DK), BF16)
    a = None
    for li in range(nl):
        e = jnp.exp2(dall[(li + 1) * c:(li + 2) * c]).astype(BF16)
        x = jnp.where(role_ref[di, li] > 0.5, q, k) * e
        r = lax.dot_general(x, _block_diag(x, zeros), nt, preferred_element_type=F32) * masks_ref[di, li]
        a = r if a is None else a + r
    qk = q.astype(F32) * k.astype(F32)
    lane = lax.broadcasted_iota(jnp.int32, a.shape, 1)
    d0 = jnp.sum(qk[:, :HG_DK], axis=-1, keepdims=True)
    d1 = jnp.sum(qk[:, HG_DK:], axis=-1, keepdims=True)
    a = a + jnp.where(lane < c, d0, d1) * masks_ref[di, nl]
    return a.astype(BF16), qe, ke, bend


def _hg_outputs(a, qe, ke, bend, v, st_ref, di, pair):
    nt = (((1,), (1,)), ((), ()))
    tn = (((0,), (0,)), ((), ()))
    o = jnp.dot(a, _block_diag(v, jnp.zeros((HG_CHUNK, HG_DK), BF16)), preferred_element_type=F32)
    outs = []
    for hh in range(2):
        hs = slice(hh * HG_DK, (hh + 1) * HG_DK)
        st = st_ref[di, 2 * pair + hh]
        outs.append(o[:, hs] + lax.dot_general(qe[:, hs], st.astype(BF16), nt, preferred_element_type=F32))
        upd = lax.dot_general(v[:, hs], ke[:, hs], tn, preferred_element_type=F32)
        st_ref[di, 2 * pair + hh] = jnp.exp2(bend[:, hs]) * st + upd
    return outs


def _hgrn_kernel(sq_ref, hi_ref, kf_ref, gfh_ref, gfl_ref, kb_ref, gbh_ref, gbl_ref, hg_ref,
                 ci_ref, ckf_ref, cgfh_ref, cgfl_ref, ckb_ref, cgbh_ref, cgbl_ref, gain_ref,
                 cmat_ref, role_ref, masks_ref, after_ref, before_ref, o_ref, acc_ref, st_ref, *, n_chunks):
    c = HG_CHUNK
    tn = (((0,), (0,)), ((), ()))

    ctx_dirs = ((ckf_ref, cgfh_ref, cgfl_ref, after_ref), (ckb_ref, cgbh_ref, cgbl_ref, before_ref))
    for di, (k_ref, gh_ref, gl_ref, cm_ref) in enumerate(ctx_dirs):
        decay = jnp.dot(cm_ref[...], jnp.concatenate([gh_ref[0], gl_ref[0]], axis=0), preferred_element_type=F32)
        ke = k_ref[0] * jnp.exp2(decay).astype(BF16)
        for h in range(HG_HEADS):
            hs = slice(h * HG_DK, (h + 1) * HG_DK)
            st_ref[di, h] = lax.dot_general(ci_ref[0, :, hs], ke[:, hs], tn, preferred_element_type=F32)

    def step(i, second):
        units = [(sub, pair, di) for sub in range(HG_CHUNKS_PER_STEP)
                 for pair in range(HG_HEADS // 2) for di in range(2)]

        def window(u):
            sub, pair, di = units[u]
            ci = i * HG_CHUNKS_PER_STEP + sub
            first = ci * c if di == 0 else (n_chunks - 1 - ci) * c
            return pl.ds(pl.multiple_of(first, c), c), slice(2 * pair * HG_DK, 2 * (pair + 1) * HG_DK)

        def exponents(u):
            rr, cols = window(u)
            gh_ref, gl_ref = (gfh_ref, gfl_ref) if units[u][2] == 0 else (gbh_ref, gbl_ref)
            return _hg_exponents(gh_ref[0, rr, cols], gl_ref[0, rr, cols], cmat_ref, units[u][2])

        def intra(u, dall):
            rr, cols = window(u)
            k_ref = kf_ref if units[u][2] == 0 else kb_ref
            return _hg_intra(sq_ref[0, rr, cols], k_ref[0, rr, cols], dall, role_ref, masks_ref, units[u][2])

        def finish(u, parts):
            _, pair, di = units[u]
            rr, cols = window(u)
            outs = _hg_outputs(*parts, hi_ref[0, rr, cols], st_ref, di, pair)
            for hh, o in enumerate(outs):
                hs = slice((2 * pair + hh) * HG_DK, (2 * pair + hh + 1) * HG_DK)
                if not second:
                    acc_ref[rr, hs] = o
                else:
                    ot = acc_ref[rr, hs] + o
                    y = ot * lax.rsqrt(jnp.mean(ot * ot, axis=-1, keepdims=True) + EPS) * gain_ref[:, hs]
                    o_ref[0, rr, hs] = (y * _silu(hg_ref[0, rr, hs].astype(F32))).astype(BF16)

        n = len(units)
        dalls = [exponents(u) for u in range(n)]
        parts = [intra(u, dalls[u]) for u in range(n)]
        for u in range(n):
            finish(u, parts[u])

    def first_half(i, carry):
        step(i, False)
        return carry

    def second_half(i, carry):
        step(i, True)
        return carry

    n_steps = n_chunks // HG_CHUNKS_PER_STEP
    lax.fori_loop(0, n_steps // 2, first_half, 0)
    lax.fori_loop(n_steps // 2, n_steps, second_half, 0)


def _hgrn(proj, cproj, gain):
    B, S, _ = proj.shape
    L = cproj.shape[1]
    c = HG_CHUNK
    nl = len(HG_LEVELS)
    before = np.tril(np.ones((L, L), np.float32), -1)
    stacked = lambda m: jnp.asarray(np.concatenate([m, m], axis=1), BF16)
    consts = list(_hg_consts()) + [stacked(before.T), stacked(before)]
    seq = lambda j: pl.BlockSpec((1, S, SEG), lambda b: (b, 0, j))
    cseq = lambda j: pl.BlockSpec((1, L, SEG), lambda b: (b, 0, j))
    full2 = lambda n, m: pl.BlockSpec((n, m), lambda b: (0, 0))
    return pl.pallas_call(
        functools.partial(_hgrn_kernel, n_chunks=S // c),
        grid=(B,),
        in_specs=[seq(P_HQ), seq(P_HI), seq(P_KF), seq(P_GFH), seq(P_GFL), seq(P_KB), seq(P_GBH), seq(P_GBL),
                  seq(P_HG), cseq(C_I), cseq(C_KF), cseq(C_GFH), cseq(C_GFL), cseq(C_KB), cseq(C_GBH), cseq(C_GBL),
                  full2(1, HG_WIDTH),
                  pl.BlockSpec(consts[0].shape, lambda b: (0, 0, 0)),
                  pl.BlockSpec((2, nl, c, 2 * HG_DK), lambda b: (0, 0, 0, 0)),
                  pl.BlockSpec((2, nl + 1, c, 2 * c), lambda b: (0, 0, 0, 0)),
                  full2(L, 2 * L), full2(L, 2 * L)],
        out_specs=pl.BlockSpec((1, S, HG_WIDTH), lambda b: (b, 0, 0)),
        out_shape=jax.ShapeDtypeStruct((B, S, HG_WIDTH), BF16),
        scratch_shapes=[pltpu.VMEM((S, HG_WIDTH), F32),
                        pltpu.VMEM((2, HG_HEADS, HG_DK, HG_DK), F32)],
        compiler_params=_cparams(("arbitrary",)),
        name="hgrn",
    )(*([proj] * 9), *([cproj] * 7), gain, *consts)


ROUTE_E_ROW = 8


def _route_t(lt):
    tm = lt.shape[1]
    row8 = lax.broadcasted_iota(jnp.int32, (SUBLANES, tm), 0).astype(F32)
    gl = jnp.where(row8 < N_GROUPS, lt[0:SUBLANES], -jnp.inf)
    gmax = jnp.max(gl, axis=0, keepdims=True)
    g_sel = jnp.min(jnp.where(gl == gmax, row8, float(SUBLANES)), axis=0, keepdims=True)
    g_w = 1.0 / jnp.sum(jnp.exp(gl - gmax), axis=0, keepdims=True)
    row16 = lax.broadcasted_iota(jnp.int32, (N_EXPERTS, tm), 0).astype(F32)
    first = g_sel * EXPERTS_PER_GROUP
    in_grp = (row16 >= first) & (row16 < first + EXPERTS_PER_GROUP)
    e1 = jnp.where(in_grp, lt[ROUTE_E_ROW:ROUTE_E_ROW + N_EXPERTS], -jnp.inf)
    v1 = jnp.max(e1, axis=0, keepdims=True)
    i1 = jnp.min(jnp.where(e1 == v1, row16, float(N_EXPERTS)), axis=0, keepdims=True)
    e2 = jnp.where(row16 == i1, -jnp.inf, e1)
    v2 = jnp.max(e2, axis=0, keepdims=True)
    i2 = jnp.min(jnp.where(e2 == v2, row16, float(N_EXPERTS)), axis=0, keepdims=True)
    t = jnp.exp(v2 - v1)
    w1 = g_w / (1.0 + t)
    w2 = g_w * t / (1.0 + t)
    onehot = jnp.where(row8 == g_sel, 1.0, 0.0)
    gates = jnp.where(row8 == i1 - first, w1, jnp.where(row8 == i2 - first, w2, 0.0))
    return onehot, gates


def _outproj_kernel(x_ref, na_ref, hg_ref, w_ref, ga_ref, g_ref, sc_ref, sh_ref, wrh_ref, wrl_ref, br_ref,
                    x1_ref, hx_ref, oh_ref, cnt_ref):
    tm, d = x_ref.shape
    nt = (((1,), (1,)), ((), ()))
    mix = (jnp.dot(na_ref[...], w_ref[0:NA_WIDTH, :], preferred_element_type=F32)
           + jnp.dot(hg_ref[...], w_ref[NA_WIDTH:, :], preferred_element_type=F32))
    x1 = x_ref[...] + ga_ref[0] * mix
    x1_ref[...] = x1
    h2 = _norm_mod(x1, g_ref[...], sc_ref[0], sh_ref[0])
    h_hi = h2.astype(BF16)
    h_lo = (h2 - h_hi.astype(F32)).astype(BF16)
    hx_ref[:, 0:d] = h_hi
    both = lax.dot_general(jnp.concatenate([wrh_ref[...], wrl_ref[...]], axis=0), h_hi, nt,
                           preferred_element_type=F32)
    lt = ((both[LANES:] + lax.dot_general(wrh_ref[...], h_lo, nt, preferred_element_type=F32))
          + both[:LANES]) + br_ref[...]
    onehot_t, gates_t = _route_t(lt)
    pieces_t = [p.astype(F32) for p in _split3(gates_t)]
    pad = jnp.zeros((LANES - (1 + len(pieces_t)) * SUBLANES, tm), F32)
    tok = jnp.concatenate([onehot_t] + pieces_t + [pad], axis=0).T
    lane = lax.broadcasted_iota(jnp.int32, tok.shape, 1)
    onehot = jnp.where(lane < ROUTE_E_ROW, tok, 0.0)
    hx_ref[:, d:d + LANES] = jnp.where(lane >= ROUTE_E_ROW, tok, 0.0).astype(BF16)
    oh_ref[...] = onehot
    cnt_ref[0] = jnp.broadcast_to(jnp.sum(onehot, axis=0, keepdims=True), (SUBLANES, LANES))


def _outproj(x2d, na2d, hg2d, w_out_bf, ga, g, sc, sh, wr_hi, wr_lo, br, S):
    T, D = x2d.shape
    tm = MOE_TM
    per = S // tm
    tok = lambda w: pl.BlockSpec((tm, w), lambda i: (i, 0))
    bat = pl.BlockSpec((1, 1, D), lambda i: (i // per, 0, 0))
    return pl.pallas_call(
        _outproj_kernel,
        grid=(T // tm,),
        in_specs=[tok(D), tok(NA_WIDTH), tok(HG_WIDTH),
                  pl.BlockSpec((NA_WIDTH + HG_WIDTH, D), lambda i: (0, 0)),
                  bat, pl.BlockSpec((1, D), lambda i: (0, 0)), bat, bat,
                  pl.BlockSpec((LANES, D), lambda i: (0, 0)),
                  pl.BlockSpec((LANES, D), lambda i: (0, 0)),
                  pl.BlockSpec((LANES, 1), lambda i: (0, 0))],
        out_specs=[tok(D), tok(MOE_XW), tok(LANES),
                   pl.BlockSpec((1, SUBLANES, LANES), lambda i: (i, 0, 0))],
        out_shape=[jax.ShapeDtypeStruct((T, D), F32), jax.ShapeDtypeStruct((T, MOE_XW), BF16),
                   jax.ShapeDtypeStruct((T, LANES), F32),
                   jax.ShapeDtypeStruct((T // tm, SUBLANES, LANES), F32)],
        compiler_params=_cparams(("arbitrary",)),
        name="outproj",
    )(x2d, na2d, hg2d, w_out_bf, ga, g, sc, sh, wr_hi, wr_lo, br)


MOE_TM = 512
MOE_ALIGN = 16
MOE_LOC = 640
MOE_XW = D_MODEL + LANES
MOE_TE = 512
MOE_BITS = tuple(1 << b for b in range(9, 3, -1))


def _moe_steps(T):
    worst = T + (T // MOE_TM) * N_GROUPS * (MOE_ALIGN - 1)
    return -(-worst // MOE_TE) + N_GROUPS


def _local_positions(oh_ref, lstrict_ref, cnt_ref, k):
    onehot = oh_ref[...]
    ranks = jnp.dot(lstrict_ref[...], onehot.astype(BF16), preferred_element_type=F32)
    lane = lax.broadcasted_iota(jnp.int32, (1, LANES), 1)
    base = jnp.zeros((1, LANES), F32)
    o = jnp.int32(0)
    for g in range(N_GROUPS):
        base = jnp.where(lane == g, o.astype(F32), base)
        o = o + cnt_ref[k * N_GROUPS + g]
    return jnp.sum(onehot * (ranks + base), axis=-1, keepdims=True)


def _sort_matrix(lpos):
    col = lax.broadcasted_iota(jnp.int32, (MOE_TM, MOE_LOC), 1).astype(F32)
    return jnp.where(lpos == col, 1.0, 0.0).astype(BF16)


def _block_copies(n_rows, src_row, dst_row, bits, make_copy, action):
    for bit in bits:
        part = n_rows & (-2 * bit)

        @pl.when((n_rows & bit) != 0)
        def _(part=part, bit=bit):
            action(make_copy(pl.multiple_of(src_row + part, MOE_ALIGN), pl.multiple_of(dst_row + part, MOE_ALIGN), bit))


def _run_copies(off_ref, cnt_ref, k, make_copy, action):
    o = jnp.int32(0)
    for g in range(N_GROUPS):
        c = cnt_ref[k * N_GROUPS + g]
        _block_copies(c, o, off_ref[k * N_GROUPS + g], MOE_BITS, make_copy, action)
        o = o + c


def _dispatch_kernel(off_ref, cnt_ref, tail_ref, hx_ref, oh_ref, lstrict_ref, hs_ref, buf_ref, zero_ref, sem,
                     *, n_tiles, n_steps):
    k = pl.program_id(0)
    slot = k % 2

    def copies(kk, sl, action):
        def make(src_row, dst_row, n):
            return pltpu.make_async_copy(buf_ref.at[sl, pl.ds(src_row, n)], hs_ref.at[pl.ds(dst_row, n)], sem.at[sl])
        _run_copies(off_ref, cnt_ref, kk, make, action)

    @pl.when(k >= 2)
    def _():
        copies(k - 2, slot, lambda cp: cp.wait())

    pt = _sort_matrix(_local_positions(oh_ref, lstrict_ref, cnt_ref, k))
    srt = lax.dot_general(pt, hx_ref[...], (((0,), (0,)), ((), ())), preferred_element_type=F32)
    buf_ref[slot] = srt.astype(BF16)
    copies(k, slot, lambda cp: cp.start())

    @pl.when(k == n_tiles - 1)
    def _():
        zero_ref[...] = jnp.zeros_like(zero_ref)
        n_used = tail_ref[2 * N_GROUPS]

        def zero_copy(src_row, dst_row, n):
            return pltpu.make_async_copy(zero_ref.at[pl.ds(src_row, n)], hs_ref.at[pl.ds(dst_row, n)], sem.at[2])

        def tile_copy(i):
            return zero_copy(0, pl.multiple_of(i * MOE_TE, MOE_TE), MOE_TE)

        def fills(action):
            for g in range(N_GROUPS):
                _block_copies(tail_ref[N_GROUPS + g], jnp.int32(0), tail_ref[g], MOE_BITS[1:], zero_copy, action)

        fills(lambda cp: cp.start())
        lax.fori_loop(n_used, n_steps, lambda i, c: (tile_copy(i).start(), c)[1], 0)
        if n_tiles >= 2:
            copies(k - 1, 1 - slot, lambda cp: cp.wait())
        copies(k, slot, lambda cp: cp.wait())
        fills(lambda cp: cp.wait())
        lax.fori_loop(n_used, n_steps, lambda i, c: (tile_copy(i).wait(), c)[1], 0)


def _dispatch(off, cnt, tail, hx, onehot, lstrict, n_steps):
    T = hx.shape[0]
    n_tiles = T // MOE_TM
    return pl.pallas_call(
        functools.partial(_dispatch_kernel, n_tiles=n_tiles, n_steps=n_steps),
        grid_spec=pltpu.PrefetchScalarGridSpec(
            num_scalar_prefetch=3,
            grid=(n_tiles,),
            in_specs=[pl.BlockSpec((MOE_TM, MOE_XW), lambda k, *_: (k, 0)),
                      pl.BlockSpec((MOE_TM, LANES), lambda k, *_: (k, 0)),
                      pl.BlockSpec((MOE_TM, MOE_TM), lambda k, *_: (0, 0))],
            out_specs=pl.BlockSpec(memory_space=pl.ANY),
            scratch_shapes=[pltpu.VMEM((2, MOE_LOC, MOE_XW), BF16),
                            pltpu.VMEM((MOE_TE, MOE_XW), BF16),
                            pltpu.SemaphoreType.DMA((3,))]),
        out_shape=jax.ShapeDtypeStruct((n_steps * MOE_TE, MOE_XW), BF16),
        compiler_params=_cparams(("arbitrary",)),
        name="dispatch",
    )(off, cnt, tail, hx, onehot, lstrict)


def _experts_kernel(grp_ref, used_ref, hs_ref, w1_ref, w3_ref, w2_ref, ys_ref):
    i = pl.program_id(0)
    d = ys_ref.shape[1]
    ne = EXPERTS_PER_GROUP

    @pl.when(i < used_ref[0])
    def _():
        t = hs_ref[:, 0:d]
        extra = hs_ref[:, d:d + LANES].astype(F32)
        lane = lax.broadcasted_iota(jnp.int32, extra.shape, 1)
        acc = None
        for j in range(ne):
            a = jnp.dot(t, w1_ref[0, j], preferred_element_type=F32)
            b = jnp.dot(t, w3_ref[0, j], preferred_element_type=F32)
            mine = (lane >= ROUTE_E_ROW) & (lane < ROUTE_E_ROW + 3 * SUBLANES) & (lane % SUBLANES == j)
            gj = jnp.sum(jnp.where(mine, extra, 0.0), axis=-1, keepdims=True)
            y = gj * jnp.dot((_silu(a) * b).astype(BF16), w2_ref[0, j], preferred_element_type=F32)
            acc = y if acc is None else acc + y
        ys_ref[...] = acc.astype(BF16)

    @pl.when(i >= used_ref[0])
    def _():
        ys_ref[...] = jnp.zeros_like(ys_ref)


def _experts(grp, used, hs, w1_bf, w3_bf, w2_bf):
    rows, _ = hs.shape
    D = w2_bf.shape[3]
    wspec = lambda w: pl.BlockSpec((1,) + w.shape[1:], lambda i, grp, used: (grp[i], 0, 0, 0))
    return pl.pallas_call(
        _experts_kernel,
        grid_spec=pltpu.PrefetchScalarGridSpec(
            num_scalar_prefetch=2,
            grid=(rows // MOE_TE,),
            in_specs=[pl.BlockSpec((MOE_TE, MOE_XW), lambda i, grp, used: (i, 0)),
                      wspec(w1_bf), wspec(w3_bf), wspec(w2_bf)],
            out_specs=pl.BlockSpec((MOE_TE, D), lambda i, grp, used: (i, 0))),
        out_shape=jax.ShapeDtypeStruct((rows, D), BF16),
        compiler_params=_cparams(("arbitrary",)),
        name="experts",
    )(grp, used, hs, w1_bf, w3_bf, w2_bf)


def _combine_kernel(off_ref, cnt_ref, x1_ref, oh_ref, lstrict_ref, ga_ref, gf_ref, ys_ref, o_ref, buf_ref, sem,
                    *, n_tiles):
    k = pl.program_id(0)
    slot = k % 2

    def copies(kk, sl, action):
        def make(loc_row, seg_row, n):
            return pltpu.make_async_copy(ys_ref.at[pl.ds(seg_row, n)], buf_ref.at[sl, pl.ds(loc_row, n)], sem.at[sl])
        _run_copies(off_ref, cnt_ref, kk, make, action)

    @pl.when(k == 0)
    def _():
        buf_ref[...] = jnp.zeros_like(buf_ref)
        copies(k, slot, lambda cp: cp.start())

    @pl.when(k + 1 < n_tiles)
    def _():
        copies(k + 1, 1 - slot, lambda cp: cp.start())

    pt = _sort_matrix(_local_positions(oh_ref, lstrict_ref, cnt_ref, k))
    copies(k, slot, lambda cp: cp.wait())
    y = jnp.dot(pt, buf_ref[slot], preferred_element_type=F32)
    x2 = x1_ref[...] + ga_ref[0] * y
    o_ref[...] = x2 * lax.rsqrt(jnp.mean(x2 * x2, axis=-1, keepdims=True) + EPS) * gf_ref[...]


def _combine(off, cnt, x1, onehot, lstrict, ga, gfin, ys, S):
    T, D = x1.shape
    n_tiles = T // MOE_TM
    per = S // MOE_TM
    return pl.pallas_call(
        functools.partial(_combine_kernel, n_tiles=n_tiles),
        grid_spec=pltpu.PrefetchScalarGridSpec(
            num_scalar_prefetch=2,
            grid=(n_tiles,),
            in_specs=[pl.BlockSpec((MOE_TM, D), lambda k, off, cnt: (k, 0)),
                      pl.BlockSpec((MOE_TM, LANES), lambda k, off, cnt: (k, 0)),
                      pl.BlockSpec((MOE_TM, MOE_TM), lambda k, off, cnt: (0, 0)),
                      pl.BlockSpec((1, 1, D), lambda k, off, cnt: (k // per, 0, 0)),
                      pl.BlockSpec((1, D), lambda k, off, cnt: (0, 0)),
                      pl.BlockSpec(memory_space=pl.ANY)],
            out_specs=pl.BlockSpec((MOE_TM, D), lambda k, off, cnt: (k, 0)),
            scratch_shapes=[pltpu.VMEM((2, MOE_LOC, D), BF16),
                            pltpu.SemaphoreType.DMA((2,))]),
        out_shape=jax.ShapeDtypeStruct((T, D), F32),
        compiler_params=_cparams(("arbitrary",)),
        name="combine",
    )(off, cnt, x1, onehot, lstrict, ga, gfin, ys)


def _moe_schedule(cnt_tiles, n_steps):
    cnt = ((cnt_tiles + (MOE_ALIGN - 1)) // MOE_ALIGN) * MOE_ALIGN
    ends = jnp.cumsum(cnt, axis=0)
    total = ends[-1]
    ntile = (total + MOE_TE - 1) // MOE_TE
    cum = jnp.cumsum(ntile)
    base = (cum - ntile) * MOE_TE
    off = base[None, :] + ends - cnt
    i = jnp.arange(n_steps, dtype=jnp.int32)
    grp = jnp.minimum(jnp.sum(i[:, None] >= cum[None, :], axis=1), N_GROUPS - 1)
    grp = jnp.where(i < cum[-1], grp, grp[jnp.maximum(cum[-1] - 1, 0)])
    tail = jnp.concatenate([base + total, ntile * MOE_TE - total, cum[-1:]])
    as_i32 = lambda a: a.reshape(-1).astype(jnp.int32)
    return as_i32(off), as_i32(cnt), as_i32(tail), as_i32(grp), as_i32(cum[-1:])


def _rope_tables(S):
    t = jnp.arange(S)
    pos = jnp.stack([t // GRID_W, t % GRID_W], axis=-1).astype(F32)
    inv = ROPE_BASE ** (-jnp.arange(0, ROPE_AXIS_DIM, 2, dtype=F32) / ROPE_AXIS_DIM)
    ang = pos[:, :, None] * inv
    cos, sin = jnp.cos(ang), jnp.sin(ang)
    cos_h = jnp.concatenate([cos, cos], axis=-1).reshape(S, NA_HEAD_DIM)
    sin_h = jnp.concatenate([-sin, sin], axis=-1).reshape(S, NA_HEAD_DIM)
    return jnp.tile(cos_h, (1, NA_HEADS)), jnp.tile(sin_h, (1, NA_HEADS))


def kernel(x, c, ctx, c_ctx, w_mod, b_mod, norm_mix, norm_ffn, w_in, w_out, na_rpb, hg_lb, hg_norm,
           w_grp, b_grp, w_exp, b_exp, w1, w3, w2, norm_final):
    B, S, D = x.shape
    T = B * S
    assert w_mod.shape[0] == 1, "single-layer kernel"

    rows = -(-(B + 1) // SUBLANES) * SUBLANES
    cc = jnp.zeros((rows, D), F32).at[:B].set(c).at[B].set(c_ctx)
    mod = _modulation(cc, w_mod[0], b_mod[0])
    sh_a, sc_a, ga_a, sh_f, sc_f, ga_f = [m.reshape(B, 1, D) for m in jnp.split(mod[:B], 6, axis=-1)]
    csh_a, csc_a = [m.reshape(1, D) for m in jnp.split(mod[B], 6)[:2]]

    w_in_bf = w_in[0].astype(BF16)
    cos_t, sin_t = _rope_tables(S)
    g_mix = norm_mix[0].reshape(1, D)

    lb = jnp.cumsum(jax.nn.softmax(hg_lb.astype(F32), axis=0), axis=0)[0]
    proj = _project(x, g_mix, sc_a, sh_a, w_in_bf, cos_t, sin_t, lb, tm=512)
    cproj = _project_ctx(ctx, g_mix, csc_a, csh_a, w_in_bf, lb)

    na_out = _neighbourhood_attention(proj, cproj, _na_bias_table(na_rpb[0]))

    gain = jnp.tile(hg_norm[0].astype(F32), HG_HEADS).reshape(1, HG_WIDTH)
    hg_out = _hgrn(proj, cproj, gain)

    e_rows = slice(ROUTE_E_ROW, ROUTE_E_ROW + N_EXPERTS)
    wr = jnp.zeros((LANES, D), F32).at[:N_GROUPS].set(w_grp[0].T).at[e_rows].set(w_exp[0].T)
    br = jnp.zeros((LANES, 1), F32).at[:N_GROUPS, 0].set(b_grp[0]).at[e_rows, 0].set(b_exp[0])
    hi_f32 = lax.bitcast_convert_type(lax.bitcast_convert_type(wr, jnp.uint32) & jnp.uint32(0xFFFF0000), F32)
    wr_hi = hi_f32.astype(BF16)
    wr_lo = (wr - hi_f32).astype(BF16)
    x1, hx, onehot, cnt_tiles = _outproj(x.reshape(T, D), na_out.reshape(T, NA_WIDTH), hg_out.reshape(T, HG_WIDTH),
                                         w_out[0].astype(BF16), ga_a, norm_ffn[0].reshape(1, D), sc_f, sh_f,
                                         wr_hi, wr_lo, br, S=S)

    n_steps = _moe_steps(T)
    off, cnt, tail, grp, used = _moe_schedule(cnt_tiles[:, 0, :N_GROUPS].astype(jnp.int32), n_steps)
    lstrict = jnp.asarray(np.tril(np.ones((MOE_TM, MOE_TM), np.float32), -1), BF16)
    by_group = lambda w: w[0].astype(BF16).reshape(N_GROUPS, EXPERTS_PER_GROUP, *w.shape[2:])
    hs = _dispatch(off, cnt, tail, hx, onehot, lstrict, n_steps)
    ys = _experts(grp, used, hs, by_group(w1), by_group(w3), by_group(w2))
    out = _combine(off, cnt, x1, onehot, lstrict, ga_f, norm_final.reshape(1, D), ys, S)
    return out.reshape(B, S, D)
```
